```python
import math
import jax, jax.numpy as jnp
from jax import lax
import numpy as np


D_MODEL = 1024
BATCH = 8
SEQ = 2048
DEPTH = 2
DEC_BATCH = 128
DEC_SEQ = 1
PAST_LEN = 16384
PAGE_SIZE = 128

D_MIX = D_MODEL
W_A = D_MIX // 2
W_B = D_MIX - W_A
CHUNK = 128
H_A = 4
P_A = W_A // H_A
GROUP_B = 16
G_B = W_B // GROUP_B
N_STATE = 64
D_IN = 2 * W_A + W_B
N_EXPERTS = 16
N_EXPERT_GROUPS = 4
EXPERTS_PER_GROUP = N_EXPERTS // N_EXPERT_GROUPS
TOP_K = 2
D_FF_EXPERT = D_MODEL // 4
DEEPNORM_ALPHA = (2 * DEPTH) ** 0.25
DEEPNORM_BETA = (8 * DEPTH) ** -0.25
LN_EPS = 1e-5
DT_MIN = 1e-3
DT_MAX = 1e-1

kernel_name = "hymba_gmlp_s5_sharedrouter_moe_step"


def layer_norm(x, g, b):
    xf = x.astype(jnp.float32)
    mu = xf.mean(-1, keepdims=True)
    var = jnp.square(xf - mu).mean(-1, keepdims=True)
    y = (xf - mu) * lax.rsqrt(var + LN_EPS) * g.astype(jnp.float32) + b.astype(jnp.float32)
    return y.astype(x.dtype)


def cmul(ar, ai, br, bi):
    return ar * br - ai * bi, ar * bi + ai * br


def chunk_spatial_gate(v, w_s, b_s):
    bt, s = v.shape[:2]
    l = min(s, CHUNK)
    vc = v.reshape(bt, s // l, l, H_A, P_A)
    w = jnp.tril(w_s[:, :l, :l])
    out = jnp.einsum('hlm,bcmhp->bclhp', w, vc) + b_s[:, :l].T[:, :, None]
    return out.reshape(bt, s, H_A, P_A)


def s5_mixer(xs, h0_re, h0_im, a_re, a_im, log_dt, b_re, b_im, c_re, c_im, d_skip, w_glu, b_glu):
    f32 = jnp.float32
    xf = xs.astype(f32)
    a_re = a_re.astype(f32)
    a_im = a_im.astype(f32)
    dt = jnp.exp(log_dt.astype(f32))[:, None]
    decay = jnp.exp(a_re * dt)
    lb_re, lb_im = decay * jnp.cos(a_im * dt), decay * jnp.sin(a_im * dt)
    den = a_re * a_re + a_im * a_im
    nr, ni = lb_re - 1.0, lb_im
    zr = (nr * a_re + ni * a_im) / den
    zi = (ni * a_re - nr * a_im) / den
    bb_re, bb_im = cmul(zr[..., None], zi[..., None], b_re.astype(f32), b_im.astype(f32))
    bu_re = jnp.einsum('bsgp,gnp->bsgn', xf, bb_re)
    bu_im = jnp.einsum('bsgp,gnp->bsgn', xf, bb_im)
    s = xs.shape[1]
    la_re = jnp.broadcast_to(lb_re, (1, s) + lb_re.shape)
    la_im = jnp.broadcast_to(lb_im, (1, s) + lb_im.shape)

    def combine(e1, e2):
        a1r, a1i, b1r, b1i = e1
        a2r, a2i, b2r, b2i = e2
        ar, ai = cmul(a2r, a2i, a1r, a1i)
        br, bi = cmul(a2r, a2i, b1r, b1i)
        return ar, ai, br + b2r, bi + b2i

    ac_re, ac_im, h_re, h_im = lax.associative_scan(combine, (la_re, la_im, bu_re, bu_im), axis=1)
    ir, ii = cmul(ac_re, ac_im, h0_re.astype(f32)[:, None], h0_im.astype(f32)[:, None])
    h_re = h_re + ir
    h_im = h_im + ii
    y = (jnp.einsum('bsgn,gpn->bsgp', h_re, c_re.astype(f32))
         - jnp.einsum('bsgn,gpn->bsgp', h_im, c_im.astype(f32))
         + d_skip.astype(f32) * xf)
    y = jax.nn.gelu(y)
    z = y * jax.nn.sigmoid(jnp.einsum('bsgp,gpq->bsgq', y, w_glu.astype(f32)) + b_glu.astype(f32))
    return z.astype(xs.dtype), h_re[:, -1], h_im[:, -1]


def grouped_moe(x, w_router, router_bias, w_gate, w_up, w_down):
    f32 = jnp.float32
    bt, s, d = x.shape
    xt = x.reshape(bt * s, d)
    scores = jax.nn.sigmoid((xt @ w_router).astype(f32))
    biased = scores + router_bias.astype(f32)
    grouped = biased.reshape(-1, N_EXPERT_GROUPS, EXPERTS_PER_GROUP)
    group_score = lax.top_k(grouped, TOP_K)[0].sum(-1)
    sel_group = jnp.argmax(group_score, axis=-1)
    expert_group = jnp.arange(N_EXPERTS) // EXPERTS_PER_GROUP
    masked = jnp.where(expert_group[None, :] == sel_group[:, None], biased, -jnp.inf)
    _, top_idx = lax.top_k(masked, TOP_K)
    top_s = jnp.take_along_axis(scores, top_idx, axis=-1)
    gates = top_s / top_s.sum(-1, keepdims=True)
    comb = (jax.nn.one_hot(top_idx, N_EXPERTS, dtype=f32) * gates[..., None]).sum(1)
    out = jnp.zeros(xt.shape, f32)
    for e in range(N_EXPERTS):
        h = jax.nn.silu(xt @ w_gate[e]) * (xt @ w_up[e])
        out = out + comb[:, e:e + 1] * (h @ w_down[e]).astype(f32)
    return out.astype(x.dtype).reshape(bt, s, d)


def hybrid_layer(x, h0_re, h0_im, w_in, w_out, ln_v_g, ln_v_b, w_s, b_s, ssm_a_re, ssm_a_im,
                 ssm_log_dt, ssm_b_re, ssm_b_im, ssm_c_re, ssm_c_im, ssm_d, w_glu, b_glu,
                 ln1_g, ln1_b, ln2_g, ln2_b, w_router, router_bias, w_gate, w_up, w_down):
    bt, s, _ = x.shape
    proj = x @ w_in
    u = jax.nn.gelu(proj[..., :W_A])
    v = layer_norm(jax.nn.gelu(proj[..., W_A:2 * W_A]), ln_v_g, ln_v_b)
    xs = proj[..., 2 * W_A:].reshape(bt, s, G_B, GROUP_B)
    y_a = u * chunk_spatial_gate(v.reshape(bt, s, H_A, P_A), w_s, b_s).reshape(bt, s, W_A)
    y_b, h_re, h_im = s5_mixer(xs, h0_re, h0_im, ssm_a_re, ssm_a_im, ssm_log_dt, ssm_b_re, ssm_b_im,
                               ssm_c_re, ssm_c_im, ssm_d, w_glu, b_glu)
    mix = jnp.concatenate([y_a, y_b.reshape(bt, s, W_B)], axis=-1) @ w_out
    x = layer_norm(DEEPNORM_ALPHA * x + mix, ln1_g, ln1_b)
    x = layer_norm(DEEPNORM_ALPHA * x + grouped_moe(x, w_router, router_bias, w_gate, w_up, w_down), ln2_g, ln2_b)
    return x, h_re, h_im, v


def setup_inputs(seed: int = 0) -> dict:
    key = jax.random.key(seed)
    ks = jax.random.split(key, 32)
    nrm = jax.random.normal
    f32 = jnp.float32
    x_prompt = nrm(ks[0], (BATCH, SEQ, D_MODEL), f32)
    x_sample = nrm(ks[1], (DEC_BATCH, DEC_SEQ, D_MODEL), f32)
    state_ssm_re = 0.5 * nrm(ks[2], (DEPTH, DEC_BATCH, G_B, N_STATE), f32)
    state_ssm_im = 0.5 * nrm(ks[3], (DEPTH, DEC_BATCH, G_B, N_STATE), f32)
    w_in = nrm(ks[4], (DEPTH, D_MODEL, D_IN), f32) * D_MODEL ** -0.5
    w_out = nrm(ks[5], (DEPTH, D_MIX, D_MODEL), f32) * (D_MIX ** -0.5 * DEEPNORM_BETA)
    ln_v_g = 1.0 + 0.05 * nrm(ks[6], (DEPTH, W_A), f32)
    ln_v_b = 0.02 * nrm(ks[7], (DEPTH, W_A), f32)
    w_s = nrm(ks[8], (DEPTH, H_A, CHUNK, CHUNK), f32) * CHUNK ** -0.5
    b_s = 1.0 + 0.1 * nrm(ks[9], (DEPTH, H_A, CHUNK), f32)
    ssm_a_re = -0.5 + 0.01 * nrm(ks[10], (DEPTH, G_B, N_STATE), f32)
    ssm_a_im = (jnp.pi * jnp.arange(N_STATE, dtype=f32))[None, None, :] + 0.01 * nrm(ks[11], (DEPTH, G_B, N_STATE), f32)
    ssm_log_dt = math.log(DT_MIN) + jax.random.uniform(ks[12], (DEPTH, G_B), f32) * (math.log(DT_MAX) - math.log(DT_MIN))
    ssm_b_re = nrm(ks[13], (DEPTH, G_B, N_STATE, GROUP_B), f32) * (2 * GROUP_B) ** -0.5
    ssm_b_im = nrm(ks[14], (DEPTH, G_B, N_STATE, GROUP_B), f32) * (2 * GROUP_B) ** -0.5
    ssm_c_re = nrm(ks[15], (DEPTH, G_B, GROUP_B, N_STATE), f32) * (2 * N_STATE) ** -0.5
    ssm_c_im = nrm(ks[16], (DEPTH, G_B, GROUP_B, N_STATE), f32) * (2 * N_STATE) ** -0.5
    ssm_d = nrm(ks[17], (DEPTH, G_B, GROUP_B), f32)
    w_glu = nrm(ks[18], (DEPTH, G_B, GROUP_B, GROUP_B), f32) * GROUP_B ** -0.5
    b_glu = 0.02 * nrm(ks[19], (DEPTH, G_B, GROUP_B), f32)
    ln1_g = 1.0 + 0.05 * nrm(ks[20], (DEPTH, D_MODEL), f32)
    ln1_b = 0.02 * nrm(ks[21], (DEPTH, D_MODEL), f32)
    ln2_g = 1.0 + 0.05 * nrm(ks[22], (DEPTH, D_MODEL), f32)
    ln2_b = 0.02 * nrm(ks[23], (DEPTH, D_MODEL), f32)
    w_router = nrm(ks[24], (D_MODEL, N_EXPERTS), f32) * D_MODEL ** -0.5
    router_bias = 0.01 * nrm(ks[25], (N_EXPERTS,), f32)
    w_gate = nrm(ks[26], (DEPTH, N_EXPERTS, D_MODEL, D_FF_EXPERT), f32) * D_MODEL ** -0.5
    w_up = nrm(ks[27], (DEPTH, N_EXPERTS, D_MODEL, D_FF_EXPERT), f32) * D_MODEL ** -0.5
    w_down = nrm(ks[28], (DEPTH, N_EXPERTS, D_FF_EXPERT, D_MODEL), f32) * (D_FF_EXPERT ** -0.5 * DEEPNORM_BETA)
    return {"x_prompt": x_prompt, "x_sample": x_sample, "state_ssm_re": state_ssm_re,
            "state_ssm_im": state_ssm_im, "w_in": w_in, "w_out": w_out, "ln_v_g": ln_v_g,
            "ln_v_b": ln_v_b, "w_s": w_s, "b_s": b_s, "ssm_a_re": ssm_a_re, "ssm_a_im": ssm_a_im,
            "ssm_log_dt": ssm_log_dt, "ssm_b_re": ssm_b_re, "ssm_b_im": ssm_b_im,
            "ssm_c_re": ssm_c_re, "ssm_c_im": ssm_c_im, "ssm_d": ssm_d, "w_glu": w_glu,
            "b_glu": b_glu, "ln1_g": ln1_g, "ln1_b": ln1_b, "ln2_g": ln2_g, "ln2_b": ln2_b,
            "w_router": w_router, "router_bias": router_bias, "w_gate": w_gate, "w_up": w_up,
            "w_down": w_down}


def reference(x_prompt, x_sample, state_ssm_re, state_ssm_im, w_in, w_out, ln_v_g, ln_v_b, w_s, b_s,
              ssm_a_re, ssm_a_im, ssm_log_dt, ssm_b_re, ssm_b_im, ssm_c_re, ssm_c_im, ssm_d,
              w_glu, b_glu, ln1_g, ln1_b, ln2_g, ln2_b, w_router, router_bias, w_gate, w_up, w_down):
    h0_zero = jnp.zeros((x_prompt.shape[0], G_B, N_STATE), jnp.float32)
    xp = x_prompt
    xsm = x_sample
    pr_re, pr_im, sm_re, sm_im, sm_v = [], [], [], [], []
    for l in range(DEPTH):
        p = dict(w_in=w_in[l], w_out=w_out[l], ln_v_g=ln_v_g[l], ln_v_b=ln_v_b[l], w_s=w_s[l],
                 b_s=b_s[l], ssm_a_re=ssm_a_re[l], ssm_a_im=ssm_a_im[l], ssm_log_dt=ssm_log_dt[l],
                 ssm_b_re=ssm_b_re[l], ssm_b_im=ssm_b_im[l], ssm_c_re=ssm_c_re[l],
                 ssm_c_im=ssm_c_im[l], ssm_d=ssm_d[l], w_glu=w_glu[l], b_glu=b_glu[l],
                 ln1_g=ln1_g[l], ln1_b=ln1_b[l], ln2_g=ln2_g[l], ln2_b=ln2_b[l],
                 w_router=w_router, router_bias=router_bias, w_gate=w_gate[l], w_up=w_up[l],
                 w_down=w_down[l])
        xp, hr, hi, _ = hybrid_layer(xp, h0_zero, h0_zero, **p)
        pr_re.append(hr)
        pr_im.append(hi)
        xsm, hr, hi, v_new = hybrid_layer(xsm, state_ssm_re[l], state_ssm_im[l], **p)
        sm_re.append(hr)
        sm_im.append(hi)
        sm_v.append(v_new)
    return (xp, xsm, jnp.stack(pr_re), jnp.stack(pr_im), jnp.stack(sm_re), jnp.stack(sm_im), jnp.stack(sm_v))
```

```python
import functools

import jax
import jax.numpy as jnp
from jax import lax
from jax.experimental import pallas as pl
from jax.experimental.pallas import tpu as pltpu

D_MODEL = 1024
W_A = 512
W_B = 512
CHUNK = 128
H_A = 4
P_A = W_A // H_A
GROUP_B = 16
G_B = W_B // GROUP_B
N_STATE = 64
N_EXPERTS = 16
N_EXPERT_GROUPS = 4
EXPERTS_PER_GROUP = N_EXPERTS // N_EXPERT_GROUPS
D_FF_EXPERT = D_MODEL // 4
LN_EPS = 1e-5

LANES = 128
SUBLANES = 8
HALF_GROUPS = 16
HALF_W = HALF_GROUPS * GROUP_B
HALF_STATE = HALF_GROUPS * N_STATE
VMEM_LIMIT = 56 * 1024 * 1024

F32 = jnp.float32
BF16 = jnp.bfloat16


def _layer_norm(x, g, b):
    mu = jnp.mean(x, axis=-1, keepdims=True)
    xc = x - mu
    var = jnp.mean(xc * xc, axis=-1, keepdims=True)
    return xc * lax.rsqrt(var + LN_EPS) * g + b


def _dot(a, b):
    return jnp.dot(a, b, preferred_element_type=F32)


def _mixer_kernel(alpha, nb, x_ref, win_ref, wout_ref, lnvg_ref, lnvb_ref, wtril_ref, bsb_ref,
                  bdb_ref, bdc_ref, lr_ref, li_ref, dskip_ref, glu_ref, bglu_ref,
                  ln1g_ref, ln1b_ref,
                  x1_ref, hfin_ref,
                  xb_ref, xs_slab, xs_tb, bu_ref, y_tb, z_slab, mix_ref, hstate_ref):
    lt = CHUNK
    rows = nb * lt
    pitch = lt + SUBLANES
    step = pl.program_id(0)

    @pl.when(step == 0)
    def _():
        hstate_ref[...] = jnp.zeros_like(hstate_ref)

    x = x_ref[...].reshape(rows, D_MODEL)
    xb_ref[...] = x.astype(BF16)

    xs = _dot(xb_ref[...], win_ref[:, 2 * W_A:])
    for j in range(W_B // LANES):
        for b in range(nb):
            xs_slab[j, b * pitch:b * pitch + lt, :] = xs[b * lt:(b + 1) * lt, j * LANES:(j + 1) * LANES]

    def to_tb(t, c):
        r0 = pl.multiple_of(t * nb, SUBLANES)
        for j in range(W_B // LANES):
            xs_tb[pl.ds(r0, nb), j * LANES:(j + 1) * LANES] = xs_slab[j, pl.ds(t, nb, stride=pitch), :]
        return c
    lax.fori_loop(0, lt, to_tb, 0, unroll=4)

    nblk = HALF_STATE // LANES
    for k in range(2):
        bu_ref[...] = _dot(xs_tb[:, k * HALF_W:(k + 1) * HALF_W].astype(BF16), bdb_ref[k])
        lr = [jnp.broadcast_to(lr_ref[k, :, c * LANES:(c + 1) * LANES], (nb, LANES)) for c in range(nblk)]
        li = [jnp.broadcast_to(li_ref[k, :, c * LANES:(c + 1) * LANES], (nb, LANES)) for c in range(nblk)]
        h0 = tuple(hstate_ref[k, :, c * LANES:(c + 1) * LANES] for c in range(2 * nblk))

        def scan(t, h):
            r0 = pl.multiple_of(t * nb, SUBLANES)
            out = [None] * (2 * nblk)
            for c in range(nblk):
                re_sl = slice(c * LANES, (c + 1) * LANES)
                im_sl = slice(HALF_STATE + c * LANES, HALF_STATE + (c + 1) * LANES)
                hr, hi = h[c], h[nblk + c]
                nr = lr[c] * hr - li[c] * hi + bu_ref[pl.ds(r0, nb), re_sl]
                ni = lr[c] * hi + li[c] * hr + bu_ref[pl.ds(r0, nb), im_sl]
                bu_ref[pl.ds(r0, nb), re_sl] = nr
                bu_ref[pl.ds(r0, nb), im_sl] = ni
                out[c], out[nblk + c] = nr, ni
            return tuple(out)
        hf = lax.fori_loop(0, lt, scan, h0)
        for c in range(2 * nblk):
            hstate_ref[k, :, c * LANES:(c + 1) * LANES] = hf[c]
        y_tb[:, k * HALF_W:(k + 1) * HALF_W] = _dot(bu_ref[...].astype(BF16), bdc_ref[k])
    hfin_ref[...] = hstate_ref[...]

    y = jax.nn.gelu(y_tb[...] + dskip_ref[...] * xs_tb[...])
    yb = y.astype(BF16)
    for k in range(2):
        sl = slice(k * HALF_W, (k + 1) * HALF_W)
        gl = _dot(yb[:, sl], glu_ref[k]) + bglu_ref[:, sl]
        z = y[:, sl] * jax.nn.sigmoid(gl)
        for j in range(HALF_W // LANES):
            z_slab[k * (HALF_W // LANES) + j] = z[:, j * LANES:(j + 1) * LANES]

    def to_bt(t8, c):
        for b in range(nb):
            dst = pl.multiple_of(b * lt + t8 * SUBLANES, SUBLANES)
            for j in range(W_B // LANES):
                mix_ref[pl.ds(dst, SUBLANES), W_A + j * LANES:W_A + (j + 1) * LANES] = (
                    z_slab[j, pl.ds(t8 * SUBLANES * nb + b, SUBLANES, stride=nb), :])
        return c
    lax.fori_loop(0, lt // SUBLANES, to_bt, 0)

    vg = jax.nn.gelu(_dot(xb_ref[...], win_ref[:, W_A:2 * W_A]))
    v = _layer_norm(vg, lnvg_ref[...], lnvb_ref[...]).astype(BF16)
    u = jax.nn.gelu(_dot(xb_ref[...], win_ref[:, :W_A]))
    for h in range(H_A):
        hs = slice(h * P_A, (h + 1) * P_A)
        vcat = jnp.concatenate([v[b * lt:(b + 1) * lt, hs] for b in range(nb)], axis=1)
        o = _dot(wtril_ref[h], vcat)
        for b in range(nb):
            rs = slice(b * lt, (b + 1) * lt)
            mix_ref[rs, hs] = u[rs, hs] * (o[:, b * LANES:(b + 1) * LANES] + bsb_ref[h])

    mix = _dot(mix_ref[...].astype(BF16), wout_ref[...])
    x1 = _layer_norm(alpha * x_ref[...].reshape(rows, D_MODEL) + mix, ln1g_ref[...], ln1b_ref[...])
    x1_ref[...] = x1.reshape(nb, lt, D_MODEL)


def _const_spec(shape):
    nd = len(shape)
    return pl.BlockSpec(shape, lambda *_: (0,) * nd, pipeline_mode=pl.Buffered(1))


def _mixer_prompt(x, lw, alpha):
    nb, seq, _ = x.shape
    lt = CHUNK
    rows = nb * lt
    weights = (lw["win"], lw["wout"], lw["lnvg"], lw["lnvb"], lw["wtril"], lw["bsb"], lw["bdb"],
               lw["bdc"], lw["lr"], lw["li"], lw["dskip"], lw["glu"], lw["bglu"], lw["ln1g"], lw["ln1b"])
    return pl.pallas_call(
        functools.partial(_mixer_kernel, alpha, nb),
        grid=(seq // lt,),
        in_specs=[pl.BlockSpec((nb, lt, D_MODEL), lambda i: (0, i, 0))]
        + [_const_spec(w.shape) for w in weights],
        out_specs=[pl.BlockSpec((nb, lt, D_MODEL), lambda i: (0, i, 0)),
                   pl.BlockSpec((2, nb, 2 * HALF_STATE), lambda i: (0, 0, 0))],
        out_shape=[jax.ShapeDtypeStruct((nb, seq, D_MODEL), F32),
                   jax.ShapeDtypeStruct((2, nb, 2 * HALF_STATE), F32)],
        scratch_shapes=[
            pltpu.VMEM((rows, D_MODEL), BF16),
            pltpu.VMEM((W_B // LANES, nb * (lt + SUBLANES), LANES), F32),
            pltpu.VMEM((rows, W_B), F32),
            pltpu.VMEM((rows, 2 * HALF_STATE), F32),
            pltpu.VMEM((rows, W_B), F32),
            pltpu.VMEM((W_B // LANES, rows, LANES), F32),
            pltpu.VMEM((rows, D_MODEL), F32),
            pltpu.VMEM((2, nb, 2 * HALF_STATE), F32),
        ],
        compiler_params=pltpu.CompilerParams(dimension_semantics=("arbitrary",),
                                             vmem_limit_bytes=VMEM_LIMIT),
    )(x, *weights)


def _mixer_sample_kernel(alpha, x_ref, h0_ref, win_ref, wout_ref, lnvg_ref, lnvb_ref, ws0_ref, bs0_ref,
                         bdb_ref, bdc_ref, lr_ref, li_ref, dskip_ref, glu_ref, bglu_ref,
                         ln1g_ref, ln1b_ref,
                         x1_ref, hnew_ref, v_ref):
    x = x_ref[...]
    proj = _dot(x.astype(BF16), win_ref[...])
    u = jax.nn.gelu(proj[:, :W_A])
    v = _layer_norm(jax.nn.gelu(proj[:, W_A:2 * W_A]), lnvg_ref[...], lnvb_ref[...])
    v_ref[...] = v
    y_a = u * (ws0_ref[...] * v + bs0_ref[...])
    xs = proj[:, 2 * W_A:]
    zs = []
    for k in range(2):
        sl = slice(k * HALF_W, (k + 1) * HALF_W)
        bu = _dot(xs[:, sl].astype(BF16), bdb_ref[k])
        h0r = h0_ref[k, :, :HALF_STATE]
        h0i = h0_ref[k, :, HALF_STATE:]
        lr = lr_ref[k]
        li = li_ref[k]
        hr = lr * h0r - li * h0i + bu[:, :HALF_STATE]
        hi = lr * h0i + li * h0r + bu[:, HALF_STATE:]
        hnew_ref[k, :, :HALF_STATE] = hr
        hnew_ref[k, :, HALF_STATE:] = hi
        hcat = jnp.concatenate([hr, hi], axis=1).astype(BF16)
        y = jax.nn.gelu(_dot(hcat, bdc_ref[k]) + dskip_ref[:, sl] * xs[:, sl])
        gl = _dot(y.astype(BF16), glu_ref[k]) + bglu_ref[:, sl]
        zs.append(y * jax.nn.sigmoid(gl))
    cat = jnp.concatenate([y_a] + zs, axis=1).astype(BF16)
    mix = _dot(cat, wout_ref[...])
    x1_ref[...] = _layer_norm(alpha * x + mix, ln1g_ref[...], ln1b_ref[...])


def _mixer_sample(x, h0, lw, alpha):
    n = x.shape[0]
    return pl.pallas_call(
        functools.partial(_mixer_sample_kernel, alpha),
        out_shape=[jax.ShapeDtypeStruct((n, D_MODEL), F32),
                   jax.ShapeDtypeStruct((2, n, 2 * HALF_STATE), F32),
                   jax.ShapeDtypeStruct((n, W_A), F32)],
        compiler_params=pltpu.CompilerParams(vmem_limit_bytes=VMEM_LIMIT),
    )(x, h0, lw["win"], lw["wout"], lw["lnvg"], lw["lnvb"], lw["ws0"], lw["bs0"], lw["bdb"],
      lw["bdc"], lw["lr"], lw["li"], lw["dskip"], lw["glu"], lw["bglu"], lw["ln1g"], lw["ln1b"])


def _route(x, wr, rbias):
    logits = jnp.dot(x, wr, preferred_element_type=F32, precision=lax.Precision.HIGHEST)
    scores = jax.nn.sigmoid(logits)
    biased = scores + rbias
    lane = lax.broadcasted_iota(jnp.int32, biased.shape, 1)
    grp = lane // EXPERTS_PER_GROUP
    neg = jnp.float32(-jnp.inf)

    def top2(vals):
        m1 = jnp.max(vals, axis=-1, keepdims=True)
        i1 = jnp.min(jnp.where(vals == m1, lane, N_EXPERTS), axis=-1, keepdims=True)
        rest = jnp.where(lane == i1, neg, vals)
        m2 = jnp.max(rest, axis=-1, keepdims=True)
        i2 = jnp.min(jnp.where(rest == m2, lane, N_EXPERTS), axis=-1, keepdims=True)
        return m1, i1, m2, i2

    best = sel = None
    for g in range(N_EXPERT_GROUPS):
        m1, _, m2, _ = top2(jnp.where(grp == g, biased, neg))
        gs = m1 + m2
        if g == 0:
            best, sel = gs, jnp.zeros(gs.shape, jnp.int32)
        else:
            upd = gs > best
            sel = jnp.where(upd, g, sel)
            best = jnp.where(upd, gs, best)
    _, i1, _, i2 = top2(jnp.where(grp == sel, biased, neg))
    s1 = jnp.sum(jnp.where(lane == i1, scores, 0.0), axis=-1, keepdims=True)
    s2 = jnp.sum(jnp.where(lane == i2, scores, 0.0), axis=-1, keepdims=True)
    tot = s1 + s2
    return jnp.where(lane == i1, s1 / tot, 0.0) + jnp.where(lane == i2, s2 / tot, 0.0)


def _moe_kernel(alpha, x_ref, wr_ref, rb_ref, wg_ref, wu_ref, wd_ref, ln2g_ref, ln2b_ref,
                out_ref, xb_ref, comb_ref, acc_ref):
    e = pl.program_id(1)

    @pl.when(e == 0)
    def _():
        x = x_ref[...]
        xb_ref[...] = x.astype(BF16)
        comb_ref[...] = _route(x, wr_ref[...], rb_ref[...])
        acc_ref[...] = jnp.zeros_like(acc_ref)

    xb = xb_ref[...]
    g = _dot(xb, wg_ref[0].astype(BF16))
    u = _dot(xb, wu_ref[0].astype(BF16))
    comb = comb_ref[...]
    lane = lax.broadcasted_iota(jnp.int32, comb.shape, 1)
    ce = jnp.sum(jnp.where(lane == e, comb, 0.0), axis=-1, keepdims=True)
    h = (jax.nn.silu(g) * u * ce).astype(BF16)
    acc_ref[...] += _dot(h, wd_ref[0].astype(BF16))

    @pl.when(e == N_EXPERTS - 1)
    def _():
        out_ref[...] = _layer_norm(alpha * x_ref[...] + acc_ref[...], ln2g_ref[...], ln2b_ref[...])


def _moe(x, wr, rb, wg, wu, wd, ln2g, ln2b, alpha, tm):
    t = x.shape[0]
    return pl.pallas_call(
        functools.partial(_moe_kernel, alpha),
        grid=(t // tm, N_EXPERTS),
        in_specs=[pl.BlockSpec((tm, D_MODEL), lambda i, e: (i, 0)),
                  pl.BlockSpec(wr.shape, lambda i, e: (0, 0)),
                  pl.BlockSpec(rb.shape, lambda i, e: (0, 0)),
                  pl.BlockSpec((1, D_MODEL, D_FF_EXPERT), lambda i, e: (e, 0, 0)),
                  pl.BlockSpec((1, D_MODEL, D_FF_EXPERT), lambda i, e: (e, 0, 0)),
                  pl.BlockSpec((1, D_FF_EXPERT, D_MODEL), lambda i, e: (e, 0, 0)),
                  pl.BlockSpec(ln2g.shape, lambda i, e: (0, 0)),
                  pl.BlockSpec(ln2b.shape, lambda i, e: (0, 0))],
        out_specs=pl.BlockSpec((tm, D_MODEL), lambda i, e: (i, 0)),
        out_shape=jax.ShapeDtypeStruct((t, D_MODEL), F32),
        scratch_shapes=[pltpu.VMEM((tm, D_MODEL), BF16),
                        pltpu.VMEM((tm, N_EXPERTS), F32),
                        pltpu.VMEM((tm, D_MODEL), F32)],
        compiler_params=pltpu.CompilerParams(dimension_semantics=("arbitrary", "arbitrary"),
                                             vmem_limit_bytes=VMEM_LIMIT),
    )(x, wr, rb, wg, wu, wd, ln2g, ln2b)


def _block_diag(blocks):
    eye = jnp.eye(HALF_GROUPS, dtype=blocks.dtype)
    k, g, a, b = blocks.shape
    return jnp.einsum("kgab,gh->kgahb", blocks, eye).reshape(k, g * a, g * b)


def _prep_layer(l, w_in, w_out, ln_v_g, ln_v_b, w_s, b_s, a_re, a_im, log_dt, b_re, b_im, c_re, c_im,
                d_skip, w_glu, b_glu, ln1_g, ln1_b):
    a_re, a_im = a_re[l], a_im[l]
    dt = jnp.exp(log_dt[l])[:, None]
    decay = jnp.exp(a_re * dt)
    lb_re, lb_im = decay * jnp.cos(a_im * dt), decay * jnp.sin(a_im * dt)
    den = a_re * a_re + a_im * a_im
    nr, ni = lb_re - 1.0, lb_im
    zr = (nr * a_re + ni * a_im) / den
    zi = (ni * a_re - nr * a_im) / den
    bb_re = zr[..., None] * b_re[l] - zi[..., None] * b_im[l]
    bb_im = zr[..., None] * b_im[l] + zi[..., None] * b_re[l]

    def halves(a):
        return a.reshape((2, HALF_GROUPS) + a.shape[1:])

    bdb = jnp.concatenate([_block_diag(halves(bb_re.transpose(0, 2, 1))),
                           _block_diag(halves(bb_im.transpose(0, 2, 1)))], axis=2)
    bdc = jnp.concatenate([_block_diag(halves(c_re[l].transpose(0, 2, 1))),
                           _block_diag(halves(-c_im[l].transpose(0, 2, 1)))], axis=1)
    return dict(
        win=w_in[l].astype(BF16), wout=w_out[l].astype(BF16),
        lnvg=ln_v_g[l][None], lnvb=ln_v_b[l][None],
        wtril=jnp.tril(w_s[l]).astype(BF16),
        bsb=jnp.broadcast_to(b_s[l][:, :, None], (H_A, CHUNK, LANES)),
        ws0=jnp.repeat(w_s[l][:, 0, 0], P_A)[None], bs0=jnp.repeat(b_s[l][:, 0], P_A)[None],
        bdb=bdb.astype(BF16), bdc=bdc.astype(BF16),
        lr=lb_re.reshape(2, 1, HALF_STATE), li=lb_im.reshape(2, 1, HALF_STATE),
        dskip=d_skip[l].reshape(1, W_B), glu=_block_diag(halves(w_glu[l])).astype(BF16),
        bglu=b_glu[l].reshape(1, W_B), ln1g=ln1_g[l][None], ln1b=ln1_b[l][None])


def _state_to_cols(h_re, h_im):
    b = h_re.shape[0]
    re = h_re.reshape(b, 2, HALF_STATE)
    im = h_im.reshape(b, 2, HALF_STATE)
    return jnp.concatenate([re, im], axis=2).transpose(1, 0, 2)


def _cols_to_state(h):
    b = h.shape[1]
    re = h[:, :, :HALF_STATE].transpose(1, 0, 2).reshape(b, G_B, N_STATE)
    im = h[:, :, HALF_STATE:].transpose(1, 0, 2).reshape(b, G_B, N_STATE)
    return re, im


def kernel(x_prompt, x_sample, state_ssm_re, state_ssm_im, w_in, w_out, ln_v_g, ln_v_b, w_s, b_s, ssm_a_re, ssm_a_im, ssm_log_dt, ssm_b_re, ssm_b_im, ssm_c_re, ssm_c_im, ssm_d, w_glu, b_glu, ln1_g, ln1_b, ln2_g, ln2_b, w_router, router_bias, w_gate, w_up, w_down):
    depth = w_in.shape[0]
    alpha = float((2 * depth) ** 0.25)
    nb, seq, _ = x_prompt.shape
    ns = x_sample.shape[0]
    rb = router_bias[None]
    xp = x_prompt
    xs = x_sample.reshape(ns, D_MODEL)
    pr_re, pr_im, sm_re, sm_im, sm_v = [], [], [], [], []
    for l in range(depth):
        lw = _prep_layer(l, w_in, w_out, ln_v_g, ln_v_b, w_s, b_s, ssm_a_re, ssm_a_im, ssm_log_dt,
                         ssm_b_re, ssm_b_im, ssm_c_re, ssm_c_im, ssm_d, w_glu, b_glu, ln1_g, ln1_b)
        ln2g, ln2b = ln2_g[l][None], ln2_b[l][None]

        x1, hfin = _mixer_prompt(xp, lw, alpha)
        xp = _moe(x1.reshape(nb * seq, D_MODEL), w_router, rb, w_gate[l], w_up[l], w_down[l],
                  ln2g, ln2b, alpha, tm=1024).reshape(nb, seq, D_MODEL)
        hr, hi = _cols_to_state(hfin)
        pr_re.append(hr)
        pr_im.append(hi)

        x1s, hnew, v_new = _mixer_sample(xs, _state_to_cols(state_ssm_re[l], state_ssm_im[l]), lw, alpha)
        xs = _moe(x1s, w_router, rb, w_gate[l], w_up[l], w_down[l], ln2g, ln2b, alpha, tm=ns)
        hr, hi = _cols_to_state(hnew)
        sm_re.append(hr)
        sm_im.append(hi)
        sm_v.append(v_new.reshape(ns, 1, W_A))
    return (xp, xs.reshape(ns, 1, D_MODEL), jnp.stack(pr_re), jnp.stack(pr_im),
            jnp.stack(sm_re), jnp.stack(sm_im), jnp.stack(sm_v))
```

```python
import functools

import jax
import jax.numpy as jnp
import numpy as np
from jax import lax
from jax.experimental import pallas as pl
from jax.experimental.pallas import tpu as pltpu

D_MODEL = 1024
W_A = 512
W_B = 512
CHUNK = 128
H_A = 4
P_A = W_A // H_A
GROUP_B = 16
G_B = W_B // GROUP_B
N_STATE = 64
N_EXPERTS = 16
N_EXPERT_GROUPS = 4
EXPERTS_PER_GROUP = N_EXPERTS // N_EXPERT_GROUPS
D_FF_EXPERT = D_MODEL // 4
LN_EPS = 1e-5

LANES = 128
SUBLANES = 8
HALF_GROUPS = 16
HALF_W = HALF_GROUPS * GROUP_B
HALF_STATE = HALF_GROUPS * N_STATE
VMEM_LIMIT = 56 * 1024 * 1024

PAIRS = ((0, 1), (0, 2), (0, 3), (1, 2), (1, 3), (2, 3))
N_CLASSES = N_EXPERT_GROUPS * len(PAIRS)
CLASS_ROWS = 32
TILE_M = 256
EA_TABLE = np.array([EXPERTS_PER_GROUP * g + a for g in range(N_EXPERT_GROUPS) for a, _ in PAIRS], np.int32)
EB_TABLE = np.array([EXPERTS_PER_GROUP * g + b for g in range(N_EXPERT_GROUPS) for _, b in PAIRS], np.int32)

F32 = jnp.float32
BF16 = jnp.bfloat16
I32 = jnp.int32
HIGHEST = lax.Precision.HIGHEST


def _layer_norm(x, g, b):
    mu = jnp.mean(x, axis=-1, keepdims=True)
    xc = x - mu
    var = jnp.mean(xc * xc, axis=-1, keepdims=True)
    return xc * lax.rsqrt(var + LN_EPS) * g + b


def _dot(a, b):
    return jnp.dot(a, b, preferred_element_type=F32)


def _route_classes(x1, wrt, rbcol):
    logits_t = lax.dot_general(wrt, x1, (((1,), (1,)), ((), ())),
                               preferred_element_type=F32, precision=HIGHEST)
    biased = jax.nn.sigmoid(logits_t) + rbcol
    rows = [biased[e:e + 1, :] for e in range(N_EXPERTS)]
    n = EXPERTS_PER_GROUP

    best = sel = None
    for g in range(N_EXPERT_GROUPS):
        v = rows[n * g:n * (g + 1)]
        gs = None
        for a, b in PAIRS:
            s = v[a] + v[b]
            gs = s if gs is None else jnp.maximum(gs, s)
        if g == 0:
            best, sel = gs, jnp.zeros(gs.shape, I32)
        else:
            upd = gs > best
            sel = jnp.where(upd, g, sel)
            best = jnp.where(upd, gs, best)

    cls = jnp.zeros(sel.shape, I32)
    for g in range(N_EXPERT_GROUPS):
        v = rows[n * g:n * (g + 1)]
        lo = jnp.full(sel.shape, n, I32)
        hi = jnp.full(sel.shape, -1, I32)
        for i in range(n):
            before = jnp.zeros(sel.shape, I32)
            for j in range(n):
                if j < i:
                    before = before + (v[j] >= v[i]).astype(I32)
                elif j > i:
                    before = before + (v[j] > v[i]).astype(I32)
            member = before < 2
            lo = jnp.where(member, jnp.minimum(lo, i), lo)
            hi = jnp.where(member, jnp.maximum(hi, i), hi)
        base = jnp.where(lo == 0, 0, jnp.where(lo == 1, 3, 5))
        cls = jnp.where(sel == g, g * len(PAIRS) + base + (hi - lo - 1), cls)
    return cls


def _gather_copy(src_hbm, pos_ref, base, dst, sem, rows):
    def body(r, c):
        p = pos_ref[base + r]
        pltpu.make_async_copy(src_hbm.at[pl.ds(p, 1), :], dst.at[pl.ds(r, 1), :], sem).start()
        return c
    lax.fori_loop(0, rows, body, 0, unroll=8)


def _gather_wait(src_hbm, dst, sem, rows):
    for q in range(rows // CHUNK):
        pltpu.make_async_copy(src_hbm.at[pl.ds(0, CHUNK), :], dst.at[pl.ds(q * CHUNK, CHUNK), :], sem).wait()


def _mixer_kernel(alpha, nb, gather_in, *refs):
    if gather_in:
        pos_ref, zs_hbm = refs[:2]
        refs = refs[2:]
    else:
        x_ref = refs[0]
        refs = refs[1:]
    (win_ref, wout_ref, lnvg_ref, lnvb_ref, wtril_ref, bsb_ref, bdb_ref, bdc_ref, lr_ref, li_ref,
     dskip_ref, glu_ref, bglu_ref, ln1g_ref, ln1b_ref, wrt_ref, rbcol_ref, tri_ref,
     x1_ref, hfin_ref, cls_ref, rank_ref, cnt_ref,
     xb_ref, xs_slab, xs_tb, bu_ref, y_tb, z_slab, mix_ref, hstate_ref, carry_ref) = refs[:32]
    lt = CHUNK
    rows = nb * lt
    pitch = lt + SUBLANES
    step = pl.program_id(0)
    nsteps = pl.num_programs(0)

    @pl.when(step == 0)
    def _():
        hstate_ref[...] = jnp.zeros_like(hstate_ref)
        carry_ref[...] = jnp.zeros_like(carry_ref)

    if gather_in:
        xbuf, gsem = refs[32:34]
        slot = lax.rem(step, 2)

        @pl.when(step == 0)
        def _():
            _gather_copy(zs_hbm, pos_ref, 0, xbuf.at[0], gsem.at[0], rows)

        for s in range(2):
            @pl.when(slot == s)
            def _(s=s):
                _gather_wait(zs_hbm, xbuf.at[s], gsem.at[s], rows)

                @pl.when(step + 1 < nsteps)
                def _():
                    _gather_copy(zs_hbm, pos_ref, (step + 1) * rows, xbuf.at[1 - s], gsem.at[1 - s], rows)

        def load_x():
            return xbuf[slot]
    else:
        def load_x():
            return x_ref[...].reshape(rows, D_MODEL)

    xb_ref[...] = load_x().astype(BF16)

    xs = _dot(xb_ref[...], win_ref[:, 2 * W_A:])
    for j in range(W_B // LANES):
        for b in range(nb):
            xs_slab[j, b * pitch:b * pitch + lt, :] = xs[b * lt:(b + 1) * lt, j * LANES:(j + 1) * LANES]

    def to_tb(t, c):
        r0 = pl.multiple_of(t * nb, SUBLANES)
        for j in range(W_B // LANES):
            xs_tb[pl.ds(r0, nb), j * LANES:(j + 1) * LANES] = xs_slab[j, pl.ds(t, nb, stride=pitch), :]
        return c
    lax.fori_loop(0, lt, to_tb, 0, unroll=4)

    nblk = HALF_STATE // LANES
    for k in range(2):
        bu_ref[...] = _dot(xs_tb[:, k * HALF_W:(k + 1) * HALF_W].astype(BF16), bdb_ref[k])
        lr = [jnp.broadcast_to(lr_ref[k, :, c * LANES:(c + 1) * LANES], (nb, LANES)) for c in range(nblk)]
        li = [jnp.broadcast_to(li_ref[k, :, c * LANES:(c + 1) * LANES], (nb, LANES)) for c in range(nblk)]
        h0 = tuple(hstate_ref[k, :, c * LANES:(c + 1) * LANES] for c in range(2 * nblk))

        def scan(t, h):
            r0 = pl.multiple_of(t * nb, SUBLANES)
            out = [None] * (2 * nblk)
            for c in range(nblk):
                re_sl = slice(c * LANES, (c + 1) * LANES)
                im_sl = slice(HALF_STATE + c * LANES, HALF_STATE + (c + 1) * LANES)
                hr, hi = h[c], h[nblk + c]
                nr = lr[c] * hr - li[c] * hi + bu_ref[pl.ds(r0, nb), re_sl]
                ni = lr[c] * hi + li[c] * hr + bu_ref[pl.ds(r0, nb), im_sl]
                bu_ref[pl.ds(r0, nb), re_sl] = nr
                bu_ref[pl.ds(r0, nb), im_sl] = ni
                out[c], out[nblk + c] = nr, ni
            return tuple(out)
        hf = lax.fori_loop(0, lt, scan, h0)
        for c in range(2 * nblk):
            hstate_ref[k, :, c * LANES:(c + 1) * LANES] = hf[c]
        y_tb[:, k * HALF_W:(k + 1) * HALF_W] = _dot(bu_ref[...].astype(BF16), bdc_ref[k])
    hfin_ref[...] = hstate_ref[...]

    y = jax.nn.gelu(y_tb[...] + dskip_ref[...] * xs_tb[...])
    yb = y.astype(BF16)
    for k in range(2):
        sl = slice(k * HALF_W, (k + 1) * HALF_W)
        gl = _dot(yb[:, sl], glu_ref[k]) + bglu_ref[:, sl]
        z = y[:, sl] * jax.nn.sigmoid(gl)
        for j in range(HALF_W // LANES):
            z_slab[k * (HALF_W // LANES) + j] = z[:, j * LANES:(j + 1) * LANES]

    def to_bt(t8, c):
        for b in range(nb):
            dst = pl.multiple_of(b * lt + t8 * SUBLANES, SUBLANES)
            for j in range(W_B // LANES):
                mix_ref[pl.ds(dst, SUBLANES), W_A + j * LANES:W_A + (j + 1) * LANES] = (
                    z_slab[j, pl.ds(t8 * SUBLANES * nb + b, SUBLANES, stride=nb), :])
        return c
    lax.fori_loop(0, lt // SUBLANES, to_bt, 0)

    vg = jax.nn.gelu(_dot(xb_ref[...], win_ref[:, W_A:2 * W_A]))
    v = _layer_norm(vg, lnvg_ref[...], lnvb_ref[...]).astype(BF16)
    u = jax.nn.gelu(_dot(xb_ref[...], win_ref[:, :W_A]))
    for h in range(H_A):
        hs = slice(h * P_A, (h + 1) * P_A)
        vcat = jnp.concatenate([v[b * lt:(b + 1) * lt, hs] for b in range(nb)], axis=1)
        o = _dot(wtril_ref[h], vcat)
        for b in range(nb):
            rs = slice(b * lt, (b + 1) * lt)
            mix_ref[rs, hs] = u[rs, hs] * (o[:, b * LANES:(b + 1) * LANES] + bsb_ref[h])

    mix = _dot(mix_ref[...].astype(BF16), wout_ref[...])
    x1 = _layer_norm(alpha * load_x() + mix, ln1g_ref[...], ln1b_ref[...])
    x1_ref[...] = x1.reshape(nb, lt, D_MODEL)

    cls = _route_classes(x1, wrt_ref[...], rbcol_ref[...])
    crow = lax.broadcasted_iota(I32, (CLASS_ROWS, rows), 0)
    onehot = jnp.where(crow == cls, 1.0, 0.0)
    prefix = _dot(onehot.astype(BF16), tri_ref[...])
    carry = carry_ref[:, 0:1]
    rank = jnp.sum(onehot * (prefix + carry), axis=0, keepdims=True)
    cls_ref[...] = cls.reshape(1, 1, rows)
    rank_ref[...] = rank.astype(I32).reshape(1, 1, rows)
    carry_ref[...] = carry_ref[...] + jnp.sum(onehot, axis=1, keepdims=True)
    cnt_ref[...] = carry_ref[...]


def _const_spec(shape, nprefetch=0):
    nd = len(shape)
    return pl.BlockSpec(shape, lambda *_: (0,) * nd, pipeline_mode=pl.Buffered(1))


def _mixer_prompt(x, lw, alpha, nb, seq, pos_prev=None):
    lt = CHUNK
    rows = nb * lt
    nsteps = seq // lt
    gather_in = pos_prev is not None
    weights = (lw["win"], lw["wout"], lw["lnvg"], lw["lnvb"], lw["wtril"], lw["bsb"], lw["bdb"],
               lw["bdc"], lw["lr"], lw["li"], lw["dskip"], lw["glu"], lw["bglu"], lw["ln1g"], lw["ln1b"],
               lw["wrt"], lw["rbcol"], lw["tri"])
    if gather_in:
        x_spec = pl.BlockSpec(memory_space=pl.ANY)
    else:
        x_spec = pl.BlockSpec((nb, lt, D_MODEL), lambda i, *_: (0, i, 0))
    scratch = [
        pltpu.VMEM((rows, D_MODEL), BF16),
        pltpu.VMEM((W_B // LANES, nb * (lt + SUBLANES), LANES), F32),
        pltpu.VMEM((rows, W_B), F32),
        pltpu.VMEM((rows, 2 * HALF_STATE), F32),
        pltpu.VMEM((rows, W_B), F32),
        pltpu.VMEM((W_B // LANES, rows, LANES), F32),
        pltpu.VMEM((rows, D_MODEL), F32),
        pltpu.VMEM((2, nb, 2 * HALF_STATE), F32),
        pltpu.VMEM((CLASS_ROWS, LANES), F32),
    ]
    if gather_in:
        scratch += [pltpu.VMEM((2, rows, D_MODEL), F32), pltpu.SemaphoreType.DMA((2,))]
    grid_spec = pltpu.PrefetchScalarGridSpec(
        num_scalar_prefetch=1 if gather_in else 0,
        grid=(nsteps,),
        in_specs=[x_spec] + [_const_spec(w.shape) for w in weights],
        out_specs=[pl.BlockSpec((nb, lt, D_MODEL), lambda i, *_: (0, i, 0)),
                   pl.BlockSpec((2, nb, 2 * HALF_STATE), lambda i, *_: (0, 0, 0)),
                   pl.BlockSpec((1, 1, rows), lambda i, *_: (i, 0, 0)),
                   pl.BlockSpec((1, 1, rows), lambda i, *_: (i, 0, 0)),
                   pl.BlockSpec((CLASS_ROWS, LANES), lambda i, *_: (0, 0))],
        scratch_shapes=scratch)
    args = ((pos_prev, x) if gather_in else (x,)) + weights
    return pl.pallas_call(
        functools.partial(_mixer_kernel, alpha, nb, gather_in),
        grid_spec=grid_spec,
        out_shape=[jax.ShapeDtypeStruct((nb, seq, D_MODEL), F32),
                   jax.ShapeDtypeStruct((2, nb, 2 * HALF_STATE), F32),
                   jax.ShapeDtypeStruct((nsteps, 1, rows), I32),
                   jax.ShapeDtypeStruct((nsteps, 1, rows), I32),
                   jax.ShapeDtypeStruct((CLASS_ROWS, LANES), F32)],
        compiler_params=pltpu.CompilerParams(dimension_semantics=("arbitrary",),
                                             vmem_limit_bytes=VMEM_LIMIT),
    )(*args)


def _dispatch_kernel(nb, pos_ref, zstart_ref, zlen_ref, tail_ref, x_ref, xs_hbm, zero_ref, sem, zsem):
    lt = CHUNK
    rows = nb * lt
    step = pl.program_id(0)

    @pl.when(step == 0)
    def _():
        zero_ref[...] = jnp.zeros_like(zero_ref)
        pieces = []
        for c in range(N_CLASSES):
            start = zstart_ref[c]
            zlen = zlen_ref[c]
            head = lax.bitwise_and(-start, SUBLANES - 1)
            for i in range(SUBLANES - 1):
                pieces.append((i < head, pltpu.make_async_copy(
                    zero_ref.at[pl.ds(0, 1), :], xs_hbm.at[pl.ds(start + i, 1), :], zsem)))
            start = start + head
            zlen = zlen - head
            p = TILE_M // 2
            while p >= SUBLANES:
                hit = (zlen & p) != 0
                pieces.append((hit, pltpu.make_async_copy(
                    zero_ref.at[pl.ds(0, p), :], xs_hbm.at[pl.ds(pl.multiple_of(start, SUBLANES), p), :], zsem)))
                start = start + jnp.where(hit, p, 0)
                p //= 2
        for hit, cp in pieces:
            pl.when(hit)(cp.start)
        for hit, cp in pieces:
            pl.when(hit)(cp.wait)

        zrows = TILE_M // 2
        first = lax.shift_right_logical(tail_ref[0], 7)

        def tail_copy(q):
            return pltpu.make_async_copy(
                zero_ref, xs_hbm.at[pl.ds(pl.multiple_of(q * zrows, zrows), zrows), :], zsem)

        def tail_start(q, c):
            tail_copy(q).start()
            return c

        def tail_wait(q, c):
            tail_copy(q).wait()
            return c
        lax.fori_loop(first, xs_hbm.shape[0] // zrows, tail_start, 0)
        lax.fori_loop(first, xs_hbm.shape[0] // zrows, tail_wait, 0)

    base = step * rows

    def body(r, c):
        p = pos_ref[base + r]
        b = lax.shift_right_logical(r, 7)
        tl = lax.bitwise_and(r, lt - 1)
        pltpu.make_async_copy(x_ref.at[b, pl.ds(tl, 1), :], xs_hbm.at[pl.ds(p, 1), :], sem).start()
        return c
    lax.fori_loop(0, rows, body, 0, unroll=8)
    for b in range(nb):
        pltpu.make_async_copy(x_ref.at[b], xs_hbm.at[pl.ds(0, lt), :], sem).wait()


def _dispatch(x1, pos, zstart, zlen, tail, ns_rows):
    nb, seq, _ = x1.shape
    lt = CHUNK
    assert lt == 128
    return pl.pallas_call(
        functools.partial(_dispatch_kernel, nb),
        grid_spec=pltpu.PrefetchScalarGridSpec(
            num_scalar_prefetch=4,
            grid=(seq // lt,),
            in_specs=[pl.BlockSpec((nb, lt, D_MODEL), lambda i, *_: (0, i, 0))],
            out_specs=pl.BlockSpec(memory_space=pl.ANY),
            scratch_shapes=[pltpu.VMEM((TILE_M // 2, D_MODEL), F32),
                            pltpu.SemaphoreType.DMA(()), pltpu.SemaphoreType.DMA(())]),
        out_shape=jax.ShapeDtypeStruct((ns_rows, D_MODEL), F32),
        compiler_params=pltpu.CompilerParams(dimension_semantics=("arbitrary",),
                                             vmem_limit_bytes=VMEM_LIMIT),
    )(pos, zstart, zlen, tail, x1)


def _pair_kernel(alpha, tidx_ref, tcls_ref, ea_ref, eb_ref, nused_ref,
                 x_ref, wr_ref, rb_ref, wga_ref, wua_ref, wda_ref, wgb_ref, wub_ref, wdb_ref,
                 ln2g_ref, ln2b_ref, z_ref, w1_ref, w2_ref):
    j = pl.program_id(0)
    f = D_FF_EXPERT

    @pl.when(j >= nused_ref[0])
    def _():
        z_ref[...] = jnp.zeros_like(z_ref)

    @pl.when(j < nused_ref[0])
    def _():
        changed = jnp.logical_or(j == 0, tcls_ref[j] != tcls_ref[jnp.maximum(j - 1, 0)])

        @pl.when(changed)
        def _():
            w1_ref[:, 0 * f:1 * f] = wga_ref[...].astype(BF16)
            w1_ref[:, 1 * f:2 * f] = wua_ref[...].astype(BF16)
            w1_ref[:, 2 * f:3 * f] = wgb_ref[...].astype(BF16)
            w1_ref[:, 3 * f:4 * f] = wub_ref[...].astype(BF16)
            w2_ref[0:f, :] = wda_ref[...].astype(BF16)
            w2_ref[f:2 * f, :] = wdb_ref[...].astype(BF16)

        x = x_ref[...]
        scores = jax.nn.sigmoid(jnp.dot(x, wr_ref[...], preferred_element_type=F32, precision=HIGHEST))
        lane = lax.broadcasted_iota(I32, scores.shape, 1)
        sa = jnp.sum(jnp.where(lane == ea_ref[j], scores, 0.0), axis=-1, keepdims=True)
        sb = jnp.sum(jnp.where(lane == eb_ref[j], scores, 0.0), axis=-1, keepdims=True)
        tot = sa + sb
        gu = _dot(x.astype(BF16), w1_ref[...])
        ha = jax.nn.silu(gu[:, 0 * f:1 * f]) * gu[:, 1 * f:2 * f] * (sa / tot)
        hb = jax.nn.silu(gu[:, 2 * f:3 * f]) * gu[:, 3 * f:4 * f] * (sb / tot)
        moe = _dot(jnp.concatenate([ha, hb], axis=1).astype(BF16), w2_ref[...])
        z_ref[...] = _layer_norm(alpha * x + moe, ln2g_ref[...], ln2b_ref[...])


def _pair_experts(xs, l, tidx, tcls, ea, eb, nused, wr, rb, w_gate, w_up, w_down, ln2g, ln2b, alpha):
    ns_rows = xs.shape[0]
    nt = tidx.shape[0]

    def wspec(shape, table):
        return pl.BlockSpec((None, None) + shape, lambda j, ti, tc, a, b, nu: (l, (a, b)[table][j], 0, 0))
    up = (D_MODEL, D_FF_EXPERT)
    dn = (D_FF_EXPERT, D_MODEL)
    cst = lambda j, *_: (0, 0)
    return pl.pallas_call(
        functools.partial(_pair_kernel, alpha),
        grid_spec=pltpu.PrefetchScalarGridSpec(
            num_scalar_prefetch=5,
            grid=(nt,),
            in_specs=[pl.BlockSpec((TILE_M, D_MODEL), lambda j, ti, *_: (ti[j], 0)),
                      pl.BlockSpec(wr.shape, cst), pl.BlockSpec(rb.shape, cst),
                      wspec(up, 0), wspec(up, 0), wspec(dn, 0), wspec(up, 1), wspec(up, 1), wspec(dn, 1),
                      pl.BlockSpec(ln2g.shape, cst), pl.BlockSpec(ln2b.shape, cst)],
            out_specs=pl.BlockSpec((TILE_M, D_MODEL), lambda j, *_: (j, 0)),
            scratch_shapes=[pltpu.VMEM((D_MODEL, 4 * D_FF_EXPERT), BF16),
                            pltpu.VMEM((2 * D_FF_EXPERT, D_MODEL), BF16)]),
        out_shape=jax.ShapeDtypeStruct((ns_rows, D_MODEL), F32),
        compiler_params=pltpu.CompilerParams(dimension_semantics=("arbitrary",),
                                             vmem_limit_bytes=VMEM_LIMIT),
    )(tidx, tcls, ea, eb, nused, xs, wr, rb, w_gate, w_up, w_down, w_gate, w_up, w_down, ln2g, ln2b)


def _ungather_kernel(nb, seq, pos_ref, zs_hbm, out_hbm, sem):
    lt = CHUNK
    rows = nb * lt
    step = pl.program_id(0)
    base = step * rows

    def body(r, c):
        p = pos_ref[base + r]
        b = lax.shift_right_logical(r, 7)
        tl = lax.bitwise_and(r, lt - 1)
        dst = b * seq + step * lt + tl
        pltpu.make_async_copy(zs_hbm.at[pl.ds(p, 1), :], out_hbm.at[pl.ds(dst, 1), :], sem).start()
        return c
    lax.fori_loop(0, rows, body, 0, unroll=8)
    for q in range(rows // lt):
        pltpu.make_async_copy(zs_hbm.at[pl.ds(0, lt), :], out_hbm.at[pl.ds(0, lt), :], sem).wait()


def _ungather(zs, pos, nb, seq):
    return pl.pallas_call(
        functools.partial(_ungather_kernel, nb, seq),
        grid_spec=pltpu.PrefetchScalarGridSpec(
            num_scalar_prefetch=1,
            grid=(seq // CHUNK,),
            in_specs=[pl.BlockSpec(memory_space=pl.ANY)],
            out_specs=pl.BlockSpec(memory_space=pl.ANY),
            scratch_shapes=[pltpu.SemaphoreType.DMA(())]),
        out_shape=jax.ShapeDtypeStruct((nb * seq, D_MODEL), F32),
        compiler_params=pltpu.CompilerParams(dimension_semantics=("arbitrary",)),
    )(pos, zs)


def _plan(cls, rank, counts, nt):
    cnt = counts[:N_CLASSES, 0].astype(I32)
    ntile = (cnt + TILE_M - 1) // TILE_M
    padded = ntile * TILE_M
    off = jnp.cumsum(padded) - padded
    pos = (off[cls.reshape(-1)] + rank.reshape(-1)).astype(I32)
    tile_end = jnp.cumsum(ntile)
    nused = tile_end[-1:]
    tidx = jnp.minimum(jnp.arange(nt, dtype=I32), nused - 1)
    tcls = jnp.sum((tile_end[None, :] <= tidx[:, None]).astype(I32), axis=1)
    ea = jnp.asarray(EA_TABLE)[tcls]
    eb = jnp.asarray(EB_TABLE)[tcls]
    return pos, off + cnt, padded - cnt, tidx, tcls, ea, eb, nused.astype(I32)


def _mixer_sample_kernel(alpha, x_ref, h0_ref, win_ref, wout_ref, lnvg_ref, lnvb_ref, ws0_ref, bs0_ref,
                         bdb_ref, bdc_ref, lr_ref, li_ref, dskip_ref, glu_ref, bglu_ref,
                         ln1g_ref, ln1b_ref,
                         x1_ref, hnew_ref, v_ref):
    x = x_ref[...]
    proj = _dot(x.astype(BF16), win_ref[...])
    u = jax.nn.gelu(proj[:, :W_A])
    v = _layer_norm(jax.nn.gelu(proj[:, W_A:2 * W_A]), lnvg_ref[...], lnvb_ref[...])
    v_ref[...] = v
    y_a = u * (ws0_ref[...] * v + bs0_ref[...])
    xs = proj[:, 2 * W_A:]
    zs = []
    for k in range(2):
        sl = slice(k * HALF_W, (k + 1) * HALF_W)
        bu = _dot(xs[:, sl].astype(BF16), bdb_ref[k])
        h0r = h0_ref[k, :, :HALF_STATE]
        h0i = h0_ref[k, :, HALF_STATE:]
        lr = lr_ref[k]
        li = li_ref[k]
        hr = lr * h0r - li * h0i + bu[:, :HALF_STATE]
        hi = lr * h0i + li * h0r + bu[:, HALF_STATE:]
        hnew_ref[k, :, :HALF_STATE] = hr
        hnew_ref[k, :, HALF_STATE:] = hi
        hcat = jnp.concatenate([hr, hi], axis=1).astype(BF16)
        y = jax.nn.gelu(_dot(hcat, bdc_ref[k]) + dskip_ref[:, sl] * xs[:, sl])
        gl = _dot(y.astype(BF16), glu_ref[k]) + bglu_ref[:, sl]
        zs.append(y * jax.nn.sigmoid(gl))
    cat = jnp.concatenate([y_a] + zs, axis=1).astype(BF16)
    mix = _dot(cat, wout_ref[...])
    x1_ref[...] = _layer_norm(alpha * x + mix, ln1g_ref[...], ln1b_ref[...])


def _mixer_sample(x, h0, lw, alpha):
    n = x.shape[0]
    return pl.pallas_call(
        functools.partial(_mixer_sample_kernel, alpha),
        out_shape=[jax.ShapeDtypeStruct((n, D_MODEL), F32),
                   jax.ShapeDtypeStruct((2, n, 2 * HALF_STATE), F32),
                   jax.ShapeDtypeStruct((n, W_A), F32)],
        compiler_params=pltpu.CompilerParams(vmem_limit_bytes=VMEM_LIMIT),
    )(x, h0, lw["win"], lw["wout"], lw["lnvg"], lw["lnvb"], lw["ws0"], lw["bs0"], lw["bdb"],
      lw["bdc"], lw["lr"], lw["li"], lw["dskip"], lw["glu"], lw["bglu"], lw["ln1g"], lw["ln1b"])


def _route(x, wr, rbias):
    logits = jnp.dot(x, wr, preferred_element_type=F32, precision=HIGHEST)
    scores = jax.nn.sigmoid(logits)
    biased = scores + rbias
    lane = lax.broadcasted_iota(I32, biased.shape, 1)
    grp = lane // EXPERTS_PER_GROUP
    neg = jnp.float32(-jnp.inf)

    def top2(vals):
        m1 = jnp.max(vals, axis=-1, keepdims=True)
        i1 = jnp.min(jnp.where(vals == m1, lane, N_EXPERTS), axis=-1, keepdims=True)
        rest = jnp.where(lane == i1, neg, vals)
        m2 = jnp.max(rest, axis=-1, keepdims=True)
        i2 = jnp.min(jnp.where(rest == m2, lane, N_EXPERTS), axis=-1, keepdims=True)
        return m1, i1, m2, i2

    best = sel = None
    for g in range(N_EXPERT_GROUPS):
        m1, _, m2, _ = top2(jnp.where(grp == g, biased, neg))
        gs = m1 + m2
        if g == 0:
            best, sel = gs, jnp.zeros(gs.shape, I32)
        else:
            upd = gs > best
            sel = jnp.where(upd, g, sel)
            best = jnp.where(upd, gs, best)
    _, i1, _, i2 = top2(jnp.where(grp == sel, biased, neg))
    s1 = jnp.sum(jnp.where(lane == i1, scores, 0.0), axis=-1, keepdims=True)
    s2 = jnp.sum(jnp.where(lane == i2, scores, 0.0), axis=-1, keepdims=True)
    tot = s1 + s2
    return jnp.where(lane == i1, s1 / tot, 0.0) + jnp.where(lane == i2, s2 / tot, 0.0)


def _moe_kernel(alpha, x_ref, wr_ref, rb_ref, wg_ref, wu_ref, wd_ref, ln2g_ref, ln2b_ref,
                out_ref, xb_ref, comb_ref, acc_ref):
    e = pl.program_id(1)

    @pl.when(e == 0)
    def _():
        x = x_ref[...]
        xb_ref[...] = x.astype(BF16)
        comb_ref[...] = _route(x, wr_ref[...], rb_ref[...])
        acc_ref[...] = jnp.zeros_like(acc_ref)

    xb = xb_ref[...]
    g = _dot(xb, wg_ref[...].astype(BF16))
    u = _dot(xb, wu_ref[...].astype(BF16))
    comb = comb_ref[...]
    lane = lax.broadcasted_iota(I32, comb.shape, 1)
    ce = jnp.sum(jnp.where(lane == e, comb, 0.0), axis=-1, keepdims=True)
    h = (jax.nn.silu(g) * u * ce).astype(BF16)
    acc_ref[...] += _dot(h, wd_ref[...].astype(BF16))

    @pl.when(e == N_EXPERTS - 1)
    def _():
        out_ref[...] = _layer_norm(alpha * x_ref[...] + acc_ref[...], ln2g_ref[...], ln2b_ref[...])


def _moe_dense(x, l, wr, rb, w_gate, w_up, w_down, ln2g, ln2b, alpha, tm):
    t = x.shape[0]
    cst = lambda i, e: (0, 0)
    wsel = lambda i, e: (l, e, 0, 0)
    return pl.pallas_call(
        functools.partial(_moe_kernel, alpha),
        grid=(t // tm, N_EXPERTS),
        in_specs=[pl.BlockSpec((tm, D_MODEL), lambda i, e: (i, 0)),
                  pl.BlockSpec(wr.shape, cst), pl.BlockSpec(rb.shape, cst),
                  pl.BlockSpec((None, None, D_MODEL, D_FF_EXPERT), wsel),
                  pl.BlockSpec((None, None, D_MODEL, D_FF_EXPERT), wsel),
                  pl.BlockSpec((None, None, D_FF_EXPERT, D_MODEL), wsel),
                  pl.BlockSpec(ln2g.shape, cst), pl.BlockSpec(ln2b.shape, cst)],
        out_specs=pl.BlockSpec((tm, D_MODEL), lambda i, e: (i, 0)),
        out_shape=jax.ShapeDtypeStruct((t, D_MODEL), F32),
        scratch_shapes=[pltpu.VMEM((tm, D_MODEL), BF16),
                        pltpu.VMEM((tm, N_EXPERTS), F32),
                        pltpu.VMEM((tm, D_MODEL), F32)],
        compiler_params=pltpu.CompilerParams(dimension_semantics=("arbitrary", "arbitrary"),
                                             vmem_limit_bytes=VMEM_LIMIT),
    )(x, wr, rb, w_gate, w_up, w_down, ln2g, ln2b)


def _block_diag(blocks):
    eye = jnp.eye(HALF_GROUPS, dtype=blocks.dtype)
    k, g, a, b = blocks.shape
    return jnp.einsum("kgab,gh->kgahb", blocks, eye).reshape(k, g * a, g * b)


def _prep_layer(l, w_in, w_out, ln_v_g, ln_v_b, w_s, b_s, a_re, a_im, log_dt, b_re, b_im, c_re, c_im,
                d_skip, w_glu, b_glu, ln1_g, ln1_b):
    a_re, a_im = a_re[l], a_im[l]
    dt = jnp.exp(log_dt[l])[:, None]
    decay = jnp.exp(a_re * dt)
    lb_re, lb_im = decay * jnp.cos(a_im * dt), decay * jnp.sin(a_im * dt)
    den = a_re * a_re + a_im * a_im
    nr, ni = lb_re - 1.0, lb_im
    zr = (nr * a_re + ni * a_im) / den
    zi = (ni * a_re - nr * a_im) / den
    bb_re = zr[..., None] * b_re[l] - zi[..., None] * b_im[l]
    bb_im = zr[..., None] * b_im[l] + zi[..., None] * b_re[l]

    def halves(a):
        return a.reshape((2, HALF_GROUPS) + a.shape[1:])

    bdb = jnp.concatenate([_block_diag(halves(bb_re.transpose(0, 2, 1))),
                           _block_diag(halves(bb_im.transpose(0, 2, 1)))], axis=2)
    bdc = jnp.concatenate([_block_diag(halves(c_re[l].transpose(0, 2, 1))),
                           _block_diag(halves(-c_im[l].transpose(0, 2, 1)))], axis=1)
    return dict(
        win=w_in[l].astype(BF16), wout=w_out[l].astype(BF16),
        lnvg=ln_v_g[l][None], lnvb=ln_v_b[l][None],
        wtril=jnp.tril(w_s[l]).astype(BF16),
        bsb=jnp.broadcast_to(b_s[l][:, :, None], (H_A, CHUNK, LANES)),
        ws0=jnp.repeat(w_s[l][:, 0, 0], P_A)[None], bs0=jnp.repeat(b_s[l][:, 0], P_A)[None],
        bdb=bdb.astype(BF16), bdc=bdc.astype(BF16),
        lr=lb_re.reshape(2, 1, HALF_STATE), li=lb_im.reshape(2, 1, HALF_STATE),
        dskip=d_skip[l].reshape(1, W_B), glu=_block_diag(halves(w_glu[l])).astype(BF16),
        bglu=b_glu[l].reshape(1, W_B), ln1g=ln1_g[l][None], ln1b=ln1_b[l][None])


def _state_to_cols(h_re, h_im):
    b = h_re.shape[0]
    re = h_re.reshape(b, 2, HALF_STATE)
    im = h_im.reshape(b, 2, HALF_STATE)
    return jnp.concatenate([re, im], axis=2).transpose(1, 0, 2)


def _cols_to_state(h):
    b = h.shape[1]
    re = h[:, :, :HALF_STATE].transpose(1, 0, 2).reshape(b, G_B, N_STATE)
    im = h[:, :, HALF_STATE:].transpose(1, 0, 2).reshape(b, G_B, N_STATE)
    return re, im


def kernel(x_prompt, x_sample, state_ssm_re, state_ssm_im, w_in, w_out, ln_v_g, ln_v_b, w_s, b_s, ssm_a_re, ssm_a_im, ssm_log_dt, ssm_b_re, ssm_b_im, ssm_c_re, ssm_c_im, ssm_d, w_glu, b_glu, ln1_g, ln1_b, ln2_g, ln2_b, w_router, router_bias, w_gate, w_up, w_down):
    depth = w_in.shape[0]
    alpha = float((2 * depth) ** 0.25)
    nb, seq, _ = x_prompt.shape
    ns = x_sample.shape[0]
    tokens = nb * seq
    nt = tokens // TILE_M + N_CLASSES
    ns_rows = nt * TILE_M
    rb = router_bias[None]
    wrt = w_router.T
    rbcol = router_bias[:, None]
    tok = np.arange(nb * CHUNK)
    tri = jnp.asarray(tok[:, None] < tok[None, :], BF16)
    xp = x_prompt
    pos = None
    xs = x_sample.reshape(ns, D_MODEL)
    pr_re, pr_im, sm_re, sm_im, sm_v = [], [], [], [], []
    for l in range(depth):
        lw = _prep_layer(l, w_in, w_out, ln_v_g, ln_v_b, w_s, b_s, ssm_a_re, ssm_a_im, ssm_log_dt,
                         ssm_b_re, ssm_b_im, ssm_c_re, ssm_c_im, ssm_d, w_glu, b_glu, ln1_g, ln1_b)
        lw["wrt"], lw["rbcol"], lw["tri"] = wrt, rbcol, tri
        ln2g, ln2b = ln2_g[l][None], ln2_b[l][None]

        x1, hfin, cls, rank, counts = _mixer_prompt(xp, lw, alpha, nb, seq, pos)
        pos, zstart, zlen, tidx, tcls, ea, eb, nused = _plan(cls, rank, counts, nt)
        x_sorted = _dispatch(x1, pos, zstart, zlen, nused * TILE_M, ns_rows)
        xp = _pair_experts(x_sorted, l, tidx, tcls, ea, eb, nused, w_router, rb, w_gate, w_up, w_down,
                           ln2g, ln2b, alpha)
        hr, hi = _cols_to_state(hfin)
        pr_re.append(hr)
        pr_im.append(hi)

        x1s, hnew, v_new = _mixer_sample(xs, _state_to_cols(state_ssm_re[l], state_ssm_im[l]), lw, alpha)
        xs = _moe_dense(x1s, l, w_router, rb, w_gate, w_up, w_down, ln2g, ln2b, alpha, tm=ns)
        hr, hi = _cols_to_state(hnew)
        sm_re.append(hr)
        sm_im.append(hi)
        sm_v.append(v_new.reshape(ns, 1, W_A))
    y_prompt = _ungather(xp, pos, nb, seq).reshape(nb, seq, D_MODEL)
    return (y_prompt, xs.reshape(ns, 1, D_MODEL), jnp.stack(pr_re), jnp.stack(pr_im),
            jnp.stack(sm_re), jnp.stack(sm_im), jnp.stack(sm_v))
```

```python
import functools

import jax
import jax.numpy as jnp
import numpy as np
from jax import lax
from jax.experimental import pallas as pl
from jax.experimental.pallas import tpu as pltpu

D_MODEL = 1024
W_A = 512
W_B = 512
CHUNK = 128
H_A = 4
P_A = W_A // H_A
GROUP_B = 16
G_B = W_B // GROUP_B
N_STATE = 64
N_EXPERTS = 16
N_EXPERT_GROUPS = 4
EXPERTS_PER_GROUP = N_EXPERTS // N_EXPERT_GROUPS
D_FF_EXPERT = D_MODEL // 4
LN_EPS = 1e-5

LANES = 128
SUBLANES = 8
HALF_GROUPS = 16
HALF_W = HALF_GROUPS * GROUP_B
HALF_STATE = HALF_GROUPS * N_STATE
VMEM_LIMIT = 56 * 1024 * 1024

PAIRS = ((0, 1), (0, 2), (0, 3), (1, 2), (1, 3), (2, 3))
N_CLASSES = N_EXPERT_GROUPS * len(PAIRS)
CLASS_ROWS = 32
TILE_M = 256
PAIR_SUBTILES = 2
TOK_ROWS = D_MODEL // LANES
ZERO_TOKENS = TILE_M // 2
EA_TABLE = np.array([EXPERTS_PER_GROUP * g + a for g in range(N_EXPERT_GROUPS) for a, _ in PAIRS], np.int32)
EB_TABLE = np.array([EXPERTS_PER_GROUP * g + b for g in range(N_EXPERT_GROUPS) for _, b in PAIRS], np.int32)

F32 = jnp.float32
BF16 = jnp.bfloat16
I32 = jnp.int32
HIGHEST = lax.Precision.HIGHEST


def _layer_norm(x, g, b):
    mu = jnp.mean(x, axis=-1, keepdims=True)
    xc = x - mu
    var = jnp.mean(xc * xc, axis=-1, keepdims=True)
    return xc * lax.rsqrt(var + LN_EPS) * g + b


def _dot(a, b):
    return jnp.dot(a, b, preferred_element_type=F32)


def _route_classes(x1, wrt, rbcol):
    logits_t = lax.dot_general(wrt, x1, (((1,), (1,)), ((), ())),
                               preferred_element_type=F32, precision=HIGHEST)
    biased = jax.nn.sigmoid(logits_t) + rbcol
    rows = [biased[e:e + 1, :] for e in range(N_EXPERTS)]
    n = EXPERTS_PER_GROUP

    best = sel = None
    for g in range(N_EXPERT_GROUPS):
        v = rows[n * g:n * (g + 1)]
        gs = None
        for a, b in PAIRS:
            s = v[a] + v[b]
            gs = s if gs is None else jnp.maximum(gs, s)
        if g == 0:
            best, sel = gs, jnp.zeros(gs.shape, I32)
        else:
            upd = gs > best
            sel = jnp.where(upd, g, sel)
            best = jnp.where(upd, gs, best)

    cls = jnp.zeros(sel.shape, I32)
    for g in range(N_EXPERT_GROUPS):
        v = rows[n * g:n * (g + 1)]
        lo = jnp.full(sel.shape, n, I32)
        hi = jnp.full(sel.shape, -1, I32)
        for i in range(n):
            before = jnp.zeros(sel.shape, I32)
            for j in range(n):
                if j < i:
                    before = before + (v[j] >= v[i]).astype(I32)
                elif j > i:
                    before = before + (v[j] > v[i]).astype(I32)
            member = before < 2
            lo = jnp.where(member, jnp.minimum(lo, i), lo)
            hi = jnp.where(member, jnp.maximum(hi, i), hi)
        base = jnp.where(lo == 0, 0, jnp.where(lo == 1, 3, 5))
        cls = jnp.where(sel == g, g * len(PAIRS) + base + (hi - lo - 1), cls)
    return cls


def _to_token_tiles(ref, row0, x):
    n = x.shape[0]
    for c in range(TOK_ROWS):
        ref[pl.ds(row0 * TOK_ROWS + c, n, stride=TOK_ROWS), :] = x[:, c * LANES:(c + 1) * LANES]


def _from_token_tiles(ref, row0, n):
    return jnp.concatenate(
        [ref[pl.ds(row0 * TOK_ROWS + c, n, stride=TOK_ROWS), :] for c in range(TOK_ROWS)], axis=1)


def _gathered_tokens(step, nsteps, pos8_ref, src_hbm, bufs, sems, rows, consume):
    def start(s, base):
        def body(r, c):
            p8 = pl.multiple_of(pos8_ref[base + r], TOK_ROWS)
            pltpu.make_async_copy(src_hbm.at[pl.ds(p8, TOK_ROWS), :],
                                  bufs[s].at[pl.ds(pl.multiple_of(r * TOK_ROWS, TOK_ROWS), TOK_ROWS), :],
                                  sems.at[s]).start()
            return c
        lax.fori_loop(0, rows, body, 0, unroll=8)

    @pl.when(step == 0)
    def _():
        start(0, 0)

    for s in range(2):
        @pl.when(lax.rem(step, 2) == s)
        def _(s=s):
            pltpu.make_async_copy(src_hbm.at[pl.ds(0, rows * TOK_ROWS), :], bufs[s], sems.at[s]).wait()

            @pl.when(step + 1 < nsteps)
            def _():
                start(1 - s, (step + 1) * rows)
            consume(bufs[s])


def _mixer_kernel(alpha, nb, gather_in, *refs):
    if gather_in:
        pos8_ref, zs_hbm = refs[:2]
        refs = refs[2:]
    else:
        x_ref = refs[0]
        refs = refs[1:]
    (win_ref, wout_ref, lnvg_ref, lnvb_ref, wtril_ref, bsb_ref, bdb_ref, bdc_ref, lr_ref, li_ref,
     dskip_ref, glu_ref, bglu_ref, ln1g_ref, ln1b_ref, wrt_ref, rbcol_ref, tri_ref,
     x1t_ref, hfin_ref, cls_ref, rank_ref, cnt_ref,
     xb_ref, xs_slab, xs_tb, bu_ref, mix_ref, hstate_ref, carry_ref) = refs[:30]
    lt = CHUNK
    rows = nb * lt
    pitch = lt + SUBLANES
    step = pl.program_id(0)
    nsteps = pl.num_programs(0)

    @pl.when(step == 0)
    def _():
        hstate_ref[...] = jnp.zeros_like(hstate_ref)
        carry_ref[...] = jnp.zeros_like(carry_ref)

    if gather_in:
        xin_ref, xbuf0, xbuf1, gsem = refs[30:34]

        def consume(buf):
            xin_ref[...] = _from_token_tiles(buf, 0, rows)
        _gathered_tokens(step, nsteps, pos8_ref, zs_hbm, (xbuf0, xbuf1), gsem, rows, consume)

        def load_x():
            return xin_ref[...]
    else:
        def load_x():
            return x_ref[...].reshape(rows, D_MODEL)

    xb_ref[...] = load_x().astype(BF16)

    xs = _dot(xb_ref[...], win_ref[:, 2 * W_A:])
    for j in range(W_B // LANES):
        for b in range(nb):
            xs_slab[j, b * pitch:b * pitch + lt, :] = xs[b * lt:(b + 1) * lt, j * LANES:(j + 1) * LANES]

    def to_tb(t, c):
        r0 = pl.multiple_of(t * nb, SUBLANES)
        for j in range(W_B // LANES):
            xs_tb[pl.ds(r0, nb), j * LANES:(j + 1) * LANES] = xs_slab[j, pl.ds(t, nb, stride=pitch), :]
        return c
    lax.fori_loop(0, lt, to_tb, 0, unroll=4)

    nblk = HALF_STATE // LANES
    for k in range(2):
        bu_ref[...] = _dot(xs_tb[:, k * HALF_W:(k + 1) * HALF_W].astype(BF16), bdb_ref[k])
        lr = [jnp.broadcast_to(lr_ref[k, :, c * LANES:(c + 1) * LANES], (nb, LANES)) for c in range(nblk)]
        li = [jnp.broadcast_to(li_ref[k, :, c * LANES:(c + 1) * LANES], (nb, LANES)) for c in range(nblk)]
        h0 = tuple(hstate_ref[k, :, c * LANES:(c + 1) * LANES] for c in range(2 * nblk))

        def scan(t, h):
            r0 = pl.multiple_of(t * nb, SUBLANES)
            out = [None] * (2 * nblk)
            for c in range(nblk):
                re_sl = slice(c * LANES, (c + 1) * LANES)
                im_sl = slice(HALF_STATE + c * LANES, HALF_STATE + (c + 1) * LANES)
                hr, hi = h[c], h[nblk + c]
                nr = lr[c] * hr - li[c] * hi + bu_ref[pl.ds(r0, nb), re_sl]
                ni = lr[c] * hi + li[c] * hr + bu_ref[pl.ds(r0, nb), im_sl]
                bu_ref[pl.ds(r0, nb), re_sl] = nr
                bu_ref[pl.ds(r0, nb), im_sl] = ni
                out[c], out[nblk + c] = nr, ni
            return tuple(out)
        hf = lax.fori_loop(0, lt, scan, h0)
        for c in range(2 * nblk):
            hstate_ref[k, :, c * LANES:(c + 1) * LANES] = hf[c]
        mix_ref[:, k * HALF_W:(k + 1) * HALF_W] = _dot(bu_ref[...].astype(BF16), bdc_ref[k])
    hfin_ref[...] = hstate_ref[...]

    y = jax.nn.gelu(mix_ref[:, :W_B] + dskip_ref[...] * xs_tb[...])
    yb = y.astype(BF16)
    for k in range(2):
        sl = slice(k * HALF_W, (k + 1) * HALF_W)
        gl = _dot(yb[:, sl], glu_ref[k]) + bglu_ref[:, sl]
        z = y[:, sl] * jax.nn.sigmoid(gl)
        for j in range(HALF_W // LANES):
            xs_slab[k * (HALF_W // LANES) + j, 0:rows, :] = z[:, j * LANES:(j + 1) * LANES]

    def to_bt(t8, c):
        for b in range(nb):
            dst = pl.multiple_of(b * lt + t8 * SUBLANES, SUBLANES)
            for j in range(W_B // LANES):
                mix_ref[pl.ds(dst, SUBLANES), W_A + j * LANES:W_A + (j + 1) * LANES] = (
                    xs_slab[j, pl.ds(t8 * SUBLANES * nb + b, SUBLANES, stride=nb), :])
        return c
    lax.fori_loop(0, lt // SUBLANES, to_bt, 0)

    vg = jax.nn.gelu(_dot(xb_ref[...], win_ref[:, W_A:2 * W_A]))
    v = _layer_norm(vg, lnvg_ref[...], lnvb_ref[...]).astype(BF16)
    u = jax.nn.gelu(_dot(xb_ref[...], win_ref[:, :W_A]))
    for h in range(H_A):
        hs = slice(h * P_A, (h + 1) * P_A)
        vcat = jnp.concatenate([v[b * lt:(b + 1) * lt, hs] for b in range(nb)], axis=1)
        o = _dot(wtril_ref[h], vcat)
        for b in range(nb):
            rs = slice(b * lt, (b + 1) * lt)
            mix_ref[rs, hs] = u[rs, hs] * (o[:, b * LANES:(b + 1) * LANES] + bsb_ref[h])

    mix = _dot(mix_ref[...].astype(BF16), wout_ref[...])
    x1 = _layer_norm(alpha * load_x() + mix, ln1g_ref[...], ln1b_ref[...])
    _to_token_tiles(x1t_ref, 0, x1)

    cls = _route_classes(x1, wrt_ref[...], rbcol_ref[...])
    crow = lax.broadcasted_iota(I32, (CLASS_ROWS, rows), 0)
    onehot = jnp.where(crow == cls, 1.0, 0.0)
    prefix = _dot(onehot.astype(BF16), tri_ref[...])
    carry = carry_ref[:, 0:1]
    rank = jnp.sum(onehot * (prefix + carry), axis=0, keepdims=True)
    cls_ref[...] = cls.reshape(1, 1, rows)
    rank_ref[...] = rank.astype(I32).reshape(1, 1, rows)
    carry_ref[...] = carry_ref[...] + jnp.sum(onehot, axis=1, keepdims=True)
    cnt_ref[...] = carry_ref[...]


def _const_spec(shape, nprefetch=0):
    nd = len(shape)
    return pl.BlockSpec(shape, lambda *_: (0,) * nd, pipeline_mode=pl.Buffered(1))


def _mixer_prompt(x, lw, alpha, nb, seq, pos_prev=None):
    lt = CHUNK
    rows = nb * lt
    nsteps = seq // lt
    gather_in = pos_prev is not None
    weights = (lw["win"], lw["wout"], lw["lnvg"], lw["lnvb"], lw["wtril"], lw["bsb"], lw["bdb"],
               lw["bdc"], lw["lr"], lw["li"], lw["dskip"], lw["glu"], lw["bglu"], lw["ln1g"], lw["ln1b"],
               lw["wrt"], lw["rbcol"], lw["tri"])
    if gather_in:
        x_spec = pl.BlockSpec(memory_space=pl.ANY)
    else:
        x_spec = pl.BlockSpec((nb, lt, D_MODEL), lambda i, *_: (0, i, 0))
    scratch = [
        pltpu.VMEM((rows, D_MODEL), BF16),
        pltpu.VMEM((W_B // LANES, nb * (lt + SUBLANES), LANES), F32),
        pltpu.VMEM((rows, W_B), F32),
        pltpu.VMEM((rows, 2 * HALF_STATE), F32),
        pltpu.VMEM((rows, D_MODEL), F32),
        pltpu.VMEM((2, nb, 2 * HALF_STATE), F32),
        pltpu.VMEM((CLASS_ROWS, LANES), F32),
    ]
    if gather_in:
        scratch += [pltpu.VMEM((rows, D_MODEL), F32),
                    pltpu.VMEM((rows * TOK_ROWS, LANES), F32), pltpu.VMEM((rows * TOK_ROWS, LANES), F32),
                    pltpu.SemaphoreType.DMA((2,))]
    grid_spec = pltpu.PrefetchScalarGridSpec(
        num_scalar_prefetch=1 if gather_in else 0,
        grid=(nsteps,),
        in_specs=[x_spec] + [_const_spec(w.shape) for w in weights],
        out_specs=[pl.BlockSpec((rows * TOK_ROWS, LANES), lambda i, *_: (i, 0)),
                   pl.BlockSpec((2, nb, 2 * HALF_STATE), lambda i, *_: (0, 0, 0)),
                   pl.BlockSpec((1, 1, rows), lambda i, *_: (i, 0, 0)),
                   pl.BlockSpec((1, 1, rows), lambda i, *_: (i, 0, 0)),
                   pl.BlockSpec((CLASS_ROWS, LANES), lambda i, *_: (0, 0))],
        scratch_shapes=scratch)
    args = ((pos_prev, x) if gather_in else (x,)) + weights
    return pl.pallas_call(
        functools.partial(_mixer_kernel, alpha, nb, gather_in),
        grid_spec=grid_spec,
        out_shape=[jax.ShapeDtypeStruct((nb * seq * TOK_ROWS, LANES), F32),
                   jax.ShapeDtypeStruct((2, nb, 2 * HALF_STATE), F32),
                   jax.ShapeDtypeStruct((nsteps, 1, rows), I32),
                   jax.ShapeDtypeStruct((nsteps, 1, rows), I32),
                   jax.ShapeDtypeStruct((CLASS_ROWS, LANES), F32)],
        compiler_params=pltpu.CompilerParams(dimension_semantics=("arbitrary",),
                                             vmem_limit_bytes=VMEM_LIMIT),
    )(*args)


def _dispatch_kernel(rows, pos8_ref, zstart_ref, zlen_ref, tail_ref, x_ref, xs_hbm, zero_ref, sem, zsem):
    step = pl.program_id(0)
    ztok = ZERO_TOKENS

    @pl.when(step == 0)
    def _():
        zero_ref[...] = jnp.zeros_like(zero_ref)
        pieces = []
        for c in range(N_CLASSES):
            start = zstart_ref[c]
            zlen = zlen_ref[c]
            p = TILE_M // 2
            while p >= 1:
                hit = (zlen & p) != 0
                pieces.append((hit, pltpu.make_async_copy(
                    zero_ref.at[pl.ds(0, p * TOK_ROWS), :],
                    xs_hbm.at[pl.ds(pl.multiple_of(start * TOK_ROWS, TOK_ROWS), p * TOK_ROWS), :], zsem)))
                start = start + jnp.where(hit, p, 0)
                p //= 2
        for hit, cp in pieces:
            pl.when(hit)(cp.start)
        for hit, cp in pieces:
            pl.when(hit)(cp.wait)

        zrows = ztok * TOK_ROWS
        first = tail_ref[0] // ztok

        def tail_copy(q):
            return pltpu.make_async_copy(
                zero_ref, xs_hbm.at[pl.ds(pl.multiple_of(q * zrows, zrows), zrows), :], zsem)

        def tail_start(q, c):
            tail_copy(q).start()
            return c

        def tail_wait(q, c):
            tail_copy(q).wait()
            return c
        lax.fori_loop(first, xs_hbm.shape[0] // zrows, tail_start, 0)
        lax.fori_loop(first, xs_hbm.shape[0] // zrows, tail_wait, 0)

    base = step * rows

    def body(r, c):
        p8 = pl.multiple_of(pos8_ref[base + r], TOK_ROWS)
        pltpu.make_async_copy(x_ref.at[pl.ds(pl.multiple_of(r * TOK_ROWS, TOK_ROWS), TOK_ROWS), :],
                              xs_hbm.at[pl.ds(p8, TOK_ROWS), :], sem).start()
        return c
    lax.fori_loop(0, rows, body, 0, unroll=8)
    pltpu.make_async_copy(x_ref, xs_hbm.at[pl.ds(0, rows * TOK_ROWS), :], sem).wait()


def _dispatch(x1t, pos8, zstart, zlen, tail, rows, ns_tokens):
    return pl.pallas_call(
        functools.partial(_dispatch_kernel, rows),
        grid_spec=pltpu.PrefetchScalarGridSpec(
            num_scalar_prefetch=4,
            grid=(x1t.shape[0] // (rows * TOK_ROWS),),
            in_specs=[pl.BlockSpec((rows * TOK_ROWS, LANES), lambda i, *_: (i, 0))],
            out_specs=pl.BlockSpec(memory_space=pl.ANY),
            scratch_shapes=[pltpu.VMEM((ZERO_TOKENS * TOK_ROWS, LANES), F32),
                            pltpu.SemaphoreType.DMA(()), pltpu.SemaphoreType.DMA(())]),
        out_shape=jax.ShapeDtypeStruct((ns_tokens * TOK_ROWS, LANES), F32),
        compiler_params=pltpu.CompilerParams(dimension_semantics=("arbitrary",),
                                             vmem_limit_bytes=VMEM_LIMIT),
    )(pos8, zstart, zlen, tail, x1t)


def _pair_kernel(alpha, tidx_ref, tcls_ref, ea_ref, eb_ref, nused_ref,
                 x_ref, wrp_ref, wga_ref, wua_ref, wda_ref, wgb_ref, wub_ref, wdb_ref,
                 ln2g_ref, ln2b_ref, z_ref, w1_ref, w2_ref):
    j = pl.program_id(0)
    f = D_FF_EXPERT
    m = TILE_M // PAIR_SUBTILES

    @pl.when(j >= nused_ref[0])
    def _():
        z_ref[...] = jnp.zeros_like(z_ref)

    @pl.when(j < nused_ref[0])
    def _():
        changed = jnp.logical_or(j == 0, tcls_ref[j] != tcls_ref[jnp.maximum(j - 1, 0)])

        @pl.when(changed)
        def _():
            w1_ref[:, 0 * f:1 * f] = wga_ref[...].astype(BF16)
            w1_ref[:, 1 * f:2 * f] = wua_ref[...].astype(BF16)
            w1_ref[:, 2 * f:3 * f] = wgb_ref[...].astype(BF16)
            w1_ref[:, 3 * f:4 * f] = wub_ref[...].astype(BF16)
            w1_ref[:, 4 * f:] = wrp_ref[...].astype(BF16)
            w2_ref[0:f, :] = wda_ref[...].astype(BF16)
            w2_ref[f:2 * f, :] = wdb_ref[...].astype(BF16)

        for s in range(PAIR_SUBTILES):
            x = _from_token_tiles(x_ref, s * m, m)
            gu = _dot(x.astype(BF16), w1_ref[...])
            scores = jax.nn.sigmoid(gu[:, 4 * f:])
            lane = lax.broadcasted_iota(I32, scores.shape, 1)
            sa = jnp.sum(jnp.where(lane == ea_ref[j], scores, 0.0), axis=-1, keepdims=True)
            sb = jnp.sum(jnp.where(lane == eb_ref[j], scores, 0.0), axis=-1, keepdims=True)
            tot = sa + sb
            ha = jax.nn.silu(gu[:, 0 * f:1 * f]) * gu[:, 1 * f:2 * f] * (sa / tot)
            hb = jax.nn.silu(gu[:, 2 * f:3 * f]) * gu[:, 3 * f:4 * f] * (sb / tot)
            moe = _dot(jnp.concatenate([ha, hb], axis=1).astype(BF16), w2_ref[...])
            _to_token_tiles(z_ref, s * m, _layer_norm(alpha * x + moe, ln2g_ref[...], ln2b_ref[...]))


def _pair_experts(xs, l, tidx, tcls, ea, eb, nused, wrp, w_gate, w_up, w_down, ln2g, ln2b, alpha):
    nt = tidx.shape[0]

    def wspec(shape, table):
        return pl.BlockSpec((None, None) + shape, lambda j, ti, tc, a, b, nu: (l, (a, b)[table][j], 0, 0))
    up = (D_MODEL, D_FF_EXPERT)
    dn = (D_FF_EXPERT, D_MODEL)
    cst = lambda j, *_: (0, 0)
    return pl.pallas_call(
        functools.partial(_pair_kernel, alpha),
        grid_spec=pltpu.PrefetchScalarGridSpec(
            num_scalar_prefetch=5,
            grid=(nt,),
            in_specs=[pl.BlockSpec((TILE_M * TOK_ROWS, LANES), lambda j, ti, *_: (ti[j], 0)),
                      pl.BlockSpec(wrp.shape, cst),
                      wspec(up, 0), wspec(up, 0), wspec(dn, 0), wspec(up, 1), wspec(up, 1), wspec(dn, 1),
                      pl.BlockSpec(ln2g.shape, cst), pl.BlockSpec(ln2b.shape, cst)],
            out_specs=pl.BlockSpec((TILE_M * TOK_ROWS, LANES), lambda j, *_: (j, 0)),
            scratch_shapes=[pltpu.VMEM((D_MODEL, 4 * D_FF_EXPERT + LANES), BF16),
                            pltpu.VMEM((2 * D_FF_EXPERT, D_MODEL), BF16)]),
        out_shape=jax.ShapeDtypeStruct(xs.shape, F32),
        compiler_params=pltpu.CompilerParams(dimension_semantics=("arbitrary",),
                                             vmem_limit_bytes=VMEM_LIMIT),
    )(tidx, tcls, ea, eb, nused, xs, wrp, w_gate, w_up, w_down, w_gate, w_up, w_down, ln2g, ln2b)


def _ungather_kernel(nb, pos8_ref, zs_hbm, out_ref, buf0, buf1, sems):
    rows = nb * CHUNK

    def consume(buf):
        out_ref[...] = _from_token_tiles(buf, 0, rows).reshape(nb, CHUNK, D_MODEL)
    _gathered_tokens(pl.program_id(0), pl.num_programs(0), pos8_ref, zs_hbm, (buf0, buf1), sems, rows, consume)


def _ungather(zs, pos8, nb, seq):
    rows = nb * CHUNK
    return pl.pallas_call(
        functools.partial(_ungather_kernel, nb),
        grid_spec=pltpu.PrefetchScalarGridSpec(
            num_scalar_prefetch=1,
            grid=(seq // CHUNK,),
            in_specs=[pl.BlockSpec(memory_space=pl.ANY)],
            out_specs=pl.BlockSpec((nb, CHUNK, D_MODEL), lambda i, *_: (0, i, 0)),
            scratch_shapes=[pltpu.VMEM((rows * TOK_ROWS, LANES), F32), pltpu.VMEM((rows * TOK_ROWS, LANES), F32),
                            pltpu.SemaphoreType.DMA((2,))]),
        out_shape=jax.ShapeDtypeStruct((nb, seq, D_MODEL), F32),
        compiler_params=pltpu.CompilerParams(dimension_semantics=("arbitrary",),
                                             vmem_limit_bytes=VMEM_LIMIT),
    )(pos8, zs)


def _plan(cls, rank, counts, nt):
    cnt = counts[:N_CLASSES, 0].astype(I32)
    ntile = (cnt + TILE_M - 1) // TILE_M
    padded = ntile * TILE_M
    off = jnp.cumsum(padded) - padded
    pos = (off[cls.reshape(-1)] + rank.reshape(-1)).astype(I32)
    tile_end = jnp.cumsum(ntile)
    nused = tile_end[-1:].astype(I32)
    tidx = jnp.minimum(jnp.arange(nt, dtype=I32), nused - 1)
    tcls = jnp.sum((tile_end[None, :] <= tidx[:, None]).astype(I32), axis=1)
    ea = jnp.asarray(EA_TABLE)[tcls]
    eb = jnp.asarray(EB_TABLE)[tcls]
    return pos * TOK_ROWS, off + cnt, padded - cnt, nused * TILE_M, tidx, tcls, ea, eb, nused


def _mixer_sample_kernel(alpha, x_ref, h0_ref, win_ref, wout_ref, lnvg_ref, lnvb_ref, ws0_ref, bs0_ref,
                         bdb_ref, bdc_ref, lr_ref, li_ref, dskip_ref, glu_ref, bglu_ref,
                         ln1g_ref, ln1b_ref,
                         x1_ref, hnew_ref, v_ref):
    x = x_ref[...]
    proj = _dot(x.astype(BF16), win_ref[...])
    u = jax.nn.gelu(proj[:, :W_A])
    v = _layer_norm(jax.nn.gelu(proj[:, W_A:2 * W_A]), lnvg_ref[...], lnvb_ref[...])
    v_ref[...] = v
    y_a = u * (ws0_ref[...] * v + bs0_ref[...])
    xs = proj[:, 2 * W_A:]
    zs = []
    for k in range(2):
        sl = slice(k * HALF_W, (k + 1) * HALF_W)
        bu = _dot(xs[:, sl].astype(BF16), bdb_ref[k])
        h0r = h0_ref[k, :, :HALF_STATE]
        h0i = h0_ref[k, :, HALF_STATE:]
        lr = lr_ref[k]
        li = li_ref[k]
        hr = lr * h0r - li * h0i + bu[:, :HALF_STATE]
        hi = lr * h0i + li * h0r + bu[:, HALF_STATE:]
        hnew_ref[k, :, :HALF_STATE] = hr
        hnew_ref[k, :, HALF_STATE:] = hi
        hcat = jnp.concatenate([hr, hi], axis=1).astype(BF16)
        y = jax.nn.gelu(_dot(hcat, bdc_ref[k]) + dskip_ref[:, sl] * xs[:, sl])
        gl = _dot(y.astype(BF16), glu_ref[k]) + bglu_ref[:, sl]
        zs.append(y * jax.nn.sigmoid(gl))
    cat = jnp.concatenate([y_a] + zs, axis=1).astype(BF16)
    mix = _dot(cat, wout_ref[...])
    x1_ref[...] = _layer_norm(alpha * x + mix, ln1g_ref[...], ln1b_ref[...])


def _mixer_sample(x, h0, lw, alpha):
    n = x.shape[0]
    return pl.pallas_call(
        functools.partial(_mixer_sample_kernel, alpha),
        out_shape=[jax.ShapeDtypeStruct((n, D_MODEL), F32),
                   jax.ShapeDtypeStruct((2, n, 2 * HALF_STATE), F32),
                   jax.ShapeDtypeStruct((n, W_A), F32)],
        compiler_params=pltpu.CompilerParams(vmem_limit_bytes=VMEM_LIMIT),
    )(x, h0, lw["win"], lw["wout"], lw["lnvg"], lw["lnvb"], lw["ws0"], lw["bs0"], lw["bdb"],
      lw["bdc"], lw["lr"], lw["li"], lw["dskip"], lw["glu"], lw["bglu"], lw["ln1g"], lw["ln1b"])


def _route(x, wr, rbias):
    logits = jnp.dot(x, wr, preferred_element_type=F32, precision=HIGHEST)
    scores = jax.nn.sigmoid(logits)
    biased = scores + rbias
    lane = lax.broadcasted_iota(I32, biased.shape, 1)
    grp = lane // EXPERTS_PER_GROUP
    neg = jnp.float32(-jnp.inf)

    def top2(vals):
        m1 = jnp.max(vals, axis=-1, keepdims=True)
        i1 = jnp.min(jnp.where(vals == m1, lane, N_EXPERTS), axis=-1, keepdims=True)
        rest = jnp.where(lane == i1, neg, vals)
        m2 = jnp.max(rest, axis=-1, keepdims=True)
        i2 = jnp.min(jnp.where(rest == m2, lane, N_EXPERTS), axis=-1, keepdims=True)
        return m1, i1, m2, i2

    best = sel = None
    for g in range(N_EXPERT_GROUPS):
        m1, _, m2, _ = top2(jnp.where(grp == g, biased, neg))
        gs = m1 + m2
        if g == 0:
            best, sel = gs, jnp.zeros(gs.shape, I32)
        else:
            upd = gs > best
            sel = jnp.where(upd, g, sel)
            best = jnp.where(upd, gs, best)
    _, i1, _, i2 = top2(jnp.where(grp == sel, biased, neg))
    s1 = jnp.sum(jnp.where(lane == i1, scores, 0.0), axis=-1, keepdims=True)
    s2 = jnp.sum(jnp.where(lane == i2, scores, 0.0), axis=-1, keepdims=True)
    tot = s1 + s2
    return jnp.where(lane == i1, s1 / tot, 0.0) + jnp.where(lane == i2, s2 / tot, 0.0)


def _moe_kernel(alpha, x_ref, wr_ref, rb_ref, wg_ref, wu_ref, wd_ref, ln2g_ref, ln2b_ref,
                out_ref, xb_ref, comb_ref, acc_ref):
    e = pl.program_id(1)

    @pl.when(e == 0)
    def _():
        x = x_ref[...]
        xb_ref[...] = x.astype(BF16)
        comb_ref[...] = _route(x, wr_ref[...], rb_ref[...])
        acc_ref[...] = jnp.zeros_like(acc_ref)

    xb = xb_ref[...]
    g = _dot(xb, wg_ref[...].astype(BF16))
    u = _dot(xb, wu_ref[...].astype(BF16))
    comb = comb_ref[...]
    lane = lax.broadcasted_iota(I32, comb.shape, 1)
    ce = jnp.sum(jnp.where(lane == e, comb, 0.0), axis=-1, keepdims=True)
    h = (jax.nn.silu(g) * u * ce).astype(BF16)
    acc_ref[...] += _dot(h, wd_ref[...].astype(BF16))

    @pl.when(e == N_EXPERTS - 1)
    def _():
        out_ref[...] = _layer_norm(alpha * x_ref[...] + acc_ref[...], ln2g_ref[...], ln2b_ref[...])


def _moe_dense(x, l, wr, rb, w_gate, w_up, w_down, ln2g, ln2b, alpha, tm):
    t = x.shape[0]
    cst = lambda i, e: (0, 0)
    wsel = lambda i, e: (l, e, 0, 0)
    return pl.pallas_call(
        functools.partial(_moe_kernel, alpha),
        grid=(t // tm, N_EXPERTS),
        in_specs=[pl.BlockSpec((tm, D_MODEL), lambda i, e: (i, 0)),
                  pl.BlockSpec(wr.shape, cst), pl.BlockSpec(rb.shape, cst),
                  pl.BlockSpec((None, None, D_MODEL, D_FF_EXPERT), wsel),
                  pl.BlockSpec((None, None, D_MODEL, D_FF_EXPERT), wsel),
                  pl.BlockSpec((None, None, D_FF_EXPERT, D_MODEL), wsel),
                  pl.BlockSpec(ln2g.shape, cst), pl.BlockSpec(ln2b.shape, cst)],
        out_specs=pl.BlockSpec((tm, D_MODEL), lambda i, e: (i, 0)),
        out_shape=jax.ShapeDtypeStruct((t, D_MODEL), F32),
        scratch_shapes=[pltpu.VMEM((tm, D_MODEL), BF16),
                        pltpu.VMEM((tm, N_EXPERTS), F32),
                        pltpu.VMEM((tm, D_MODEL), F32)],
        compiler_params=pltpu.CompilerParams(dimension_semantics=("arbitrary", "arbitrary"),
                                             vmem_limit_bytes=VMEM_LIMIT),
    )(x, wr, rb, w_gate, w_up, w_down, ln2g, ln2b)


def _block_diag(blocks):
    eye = jnp.eye(HALF_GROUPS, dtype=blocks.dtype)
    k, g, a, b = blocks.shape
    return jnp.einsum("kgab,gh->kgahb", blocks, eye).reshape(k, g * a, g * b)


def _prep_layer(l, w_in, w_out, ln_v_g, ln_v_b, w_s, b_s, a_re, a_im, log_dt, b_re, b_im, c_re, c_im,
                d_skip, w_glu, b_glu, ln1_g, ln1_b):
    a_re, a_im = a_re[l], a_im[l]
    dt = jnp.exp(log_dt[l])[:, None]
    decay = jnp.exp(a_re * dt)
    lb_re, lb_im = decay * jnp.cos(a_im * dt), decay * jnp.sin(a_im * dt)
    den = a_re * a_re + a_im * a_im
    nr, ni = lb_re - 1.0, lb_im
    zr = (nr * a_re + ni * a_im) / den
    zi = (ni * a_re - nr * a_im) / den
    bb_re = zr[..., None] * b_re[l] - zi[..., None] * b_im[l]
    bb_im = zr[..., None] * b_im[l] + zi[..., None] * b_re[l]

    def halves(a):
        return a.reshape((2, HALF_GROUPS) + a.shape[1:])

    bdb = jnp.concatenate([_block_diag(halves(bb_re.transpose(0, 2, 1))),
                           _block_diag(halves(bb_im.transpose(0, 2, 1)))], axis=2)
    bdc = jnp.concatenate([_block_diag(halves(c_re[l].transpose(0, 2, 1))),
                           _block_diag(halves(-c_im[l].transpose(0, 2, 1)))], axis=1)
    return dict(
        win=w_in[l].astype(BF16), wout=w_out[l].astype(BF16),
        lnvg=ln_v_g[l][None], lnvb=ln_v_b[l][None],
        wtril=jnp.tril(w_s[l]).astype(BF16),
        bsb=jnp.broadcast_to(b_s[l][:, :, None], (H_A, CHUNK, LANES)),
        ws0=jnp.repeat(w_s[l][:, 0, 0], P_A)[None], bs0=jnp.repeat(b_s[l][:, 0], P_A)[None],
        bdb=bdb.astype(BF16), bdc=bdc.astype(BF16),
        lr=lb_re.reshape(2, 1, HALF_STATE), li=lb_im.reshape(2, 1, HALF_STATE),
        dskip=d_skip[l].reshape(1, W_B), glu=_block_diag(halves(w_glu[l])).astype(BF16),
        bglu=b_glu[l].reshape(1, W_B), ln1g=ln1_g[l][None], ln1b=ln1_b[l][None])


def _state_to_cols(h_re, h_im):
    b = h_re.shape[0]
    re = h_re.reshape(b, 2, HALF_STATE)
    im = h_im.reshape(b, 2, HALF_STATE)
    return jnp.concatenate([re, im], axis=2).transpose(1, 0, 2)


def _cols_to_state(h):
    b = h.shape[1]
    re = h[:, :, :HALF_STATE].transpose(1, 0, 2).reshape(b, G_B, N_STATE)
    im = h[:, :, HALF_STATE:].transpose(1, 0, 2).reshape(b, G_B, N_STATE)
    return re, im


def kernel(x_prompt, x_sample, state_ssm_re, state_ssm_im, w_in, w_out, ln_v_g, ln_v_b, w_s, b_s, ssm_a_re, ssm_a_im, ssm_log_dt, ssm_b_re, ssm_b_im, ssm_c_re, ssm_c_im, ssm_d, w_glu, b_glu, ln1_g, ln1_b, ln2_g, ln2_b, w_router, router_bias, w_gate, w_up, w_down):
    depth = w_in.shape[0]
    alpha = float((2 * depth) ** 0.25)
    nb, seq, _ = x_prompt.shape
    ns = x_sample.shape[0]
    tokens = nb * seq
    nt = tokens // TILE_M + N_CLASSES
    rb = router_bias[None]
    wrt = w_router.T
    wrp = jnp.pad(w_router, ((0, 0), (0, LANES - N_EXPERTS)))
    rbcol = router_bias[:, None]
    tok = np.arange(nb * CHUNK)
    tri = jnp.asarray(tok[:, None] < tok[None, :], BF16)
    xp = x_prompt
    pos = None
    xs = x_sample.reshape(ns, D_MODEL)
    pr_re, pr_im, sm_re, sm_im, sm_v = [], [], [], [], []
    for l in range(depth):
        lw = _prep_layer(l, w_in, w_out, ln_v_g, ln_v_b, w_s, b_s, ssm_a_re, ssm_a_im, ssm_log_dt,
                         ssm_b_re, ssm_b_im, ssm_c_re, ssm_c_im, ssm_d, w_glu, b_glu, ln1_g, ln1_b)
        lw["wrt"], lw["rbcol"], lw["tri"] = wrt, rbcol, tri
        ln2g, ln2b = ln2_g[l][None], ln2_b[l][None]

        x1t, hfin, cls, rank, counts = _mixer_prompt(xp, lw, alpha, nb, seq, pos)
        pos, zstart, zlen, tail, tidx, tcls, ea, eb, nused = _plan(cls, rank, counts, nt)
        x_sorted = _dispatch(x1t, pos, zstart, zlen, tail, nb * CHUNK, nt * TILE_M)
        xp = _pair_experts(x_sorted, l, tidx, tcls, ea, eb, nused, wrp, w_gate, w_up, w_down,
                           ln2g, ln2b, alpha)
        hr, hi = _cols_to_state(hfin)
        pr_re.append(hr)
        pr_im.append(hi)

        x1s, hnew, v_new = _mixer_sample(xs, _state_to_cols(state_ssm_re[l], state_ssm_im[l]), lw, alpha)
        xs = _moe_dense(x1s, l, w_router, rb, w_gate, w_up, w_down, ln2g, ln2b, alpha, tm=ns)
        hr, hi = _cols_to_state(hnew)
        sm_re.append(hr)
        sm_im.append(hi)
        sm_v.append(v_new.reshape(ns, 1, W_A))
    y_prompt = _ungather(xp, pos, nb, seq)
    return (y_prompt, xs.reshape(ns, 1, D_MODEL), jnp.stack(pr_re), jnp.stack(pr_im),
            jnp.stack(sm_re), jnp.stack(sm_im), jnp.stack(sm_v))
```

```python
import functools

import jax
import jax.numpy as jnp
import numpy as np
from jax import lax
from jax.experimental import pallas as pl
from jax.experimental.pallas import tpu as pltpu

D_MODEL = 1024
W_A = 512
W_B = 512
CHUNK = 128
H_A = 4
P_A = W_A // H_A
GROUP_B = 16
G_B = W_B // GROUP_B
N_STATE = 64
N_EXPERTS = 16
N_EXPERT_GROUPS = 4
EXPERTS_PER_GROUP = N_EXPERTS // N_EXPERT_GROUPS
D_FF_EXPERT = D_MODEL // 4
LN_EPS = 1e-5

LANES = 128
SUBLANES = 8
HALF_GROUPS = 16
HALF_W = HALF_GROUPS * GROUP_B
HALF_STATE = HALF_GROUPS * N_STATE
VMEM_LIMIT = 56 * 1024 * 1024

PAIRS = ((0, 1), (0, 2), (0, 3), (1, 2), (1, 3), (2, 3))
N_CLASSES = N_EXPERT_GROUPS * len(PAIRS)
CLASS_ROWS = 32
TILE_M = 256
PAIR_SUBTILES = 1
TOK_ROWS = D_MODEL // LANES
ZERO_TOKENS = TILE_M // 2
DMA_UNROLL = 8
EA_TABLE = np.array([EXPERTS_PER_GROUP * g + a for g in range(N_EXPERT_GROUPS) for a, _ in PAIRS], np.int32)
EB_TABLE = np.array([EXPERTS_PER_GROUP * g + b for g in range(N_EXPERT_GROUPS) for _, b in PAIRS], np.int32)

F32 = jnp.float32
BF16 = jnp.bfloat16
I32 = jnp.int32
HIGHEST = lax.Precision.HIGHEST


def _layer_norm(x, g, b):
    mu = jnp.mean(x, axis=-1, keepdims=True)
    xc = x - mu
    var = jnp.mean(xc * xc, axis=-1, keepdims=True)
    return xc * lax.rsqrt(var + LN_EPS) * g + b


def _dot(a, b):
    return jnp.dot(a, b, preferred_element_type=F32)


def _route_classes(x1, wrt, rbcol):
    logits_t = lax.dot_general(wrt, x1, (((1,), (1,)), ((), ())),
                               preferred_element_type=F32, precision=HIGHEST)
    biased = jax.nn.sigmoid(logits_t) + rbcol
    rows = [biased[e:e + 1, :] for e in range(N_EXPERTS)]
    n = EXPERTS_PER_GROUP

    best = sel = None
    for g in range(N_EXPERT_GROUPS):
        v = rows[n * g:n * (g + 1)]
        gs = None
        for a, b in PAIRS:
            s = v[a] + v[b]
            gs = s if gs is None else jnp.maximum(gs, s)
        if g == 0:
            best, sel = gs, jnp.zeros(gs.shape, I32)
        else:
            upd = gs > best
            sel = jnp.where(upd, g, sel)
            best = jnp.where(upd, gs, best)

    cls = jnp.zeros(sel.shape, I32)
    for g in range(N_EXPERT_GROUPS):
        v = rows[n * g:n * (g + 1)]
        lo = jnp.full(sel.shape, n, I32)
        hi = jnp.full(sel.shape, -1, I32)
        for i in range(n):
            before = jnp.zeros(sel.shape, I32)
            for j in range(n):
                if j < i:
                    before = before + (v[j] >= v[i]).astype(I32)
                elif j > i:
                    before = before + (v[j] > v[i]).astype(I32)
            member = before < 2
            lo = jnp.where(member, jnp.minimum(lo, i), lo)
            hi = jnp.where(member, jnp.maximum(hi, i), hi)
        base = jnp.where(lo == 0, 0, jnp.where(lo == 1, 3, 5))
        cls = jnp.where(sel == g, g * len(PAIRS) + base + (hi - lo - 1), cls)
    return cls


def _to_token_tiles(ref, row0, x):
    n = x.shape[0]
    for c in range(TOK_ROWS):
        ref[pl.ds(row0 * TOK_ROWS + c, n, stride=TOK_ROWS), :] = x[:, c * LANES:(c + 1) * LANES]


def _from_token_tiles(ref, row0, n):
    return jnp.concatenate(
        [ref[pl.ds(row0 * TOK_ROWS + c, n, stride=TOK_ROWS), :] for c in range(TOK_ROWS)], axis=1)


def _gathered_tokens(step, nsteps, pos8_ref, src_hbm, bufs, sems, rows, consume):
    def start(s, base):
        def body(g, c):
            for i in range(DMA_UNROLL):
                r = g * DMA_UNROLL + i
                p8 = pl.multiple_of(pos8_ref[base + r], TOK_ROWS)
                pltpu.make_async_copy(src_hbm.at[pl.ds(p8, TOK_ROWS), :],
                                      bufs[s].at[pl.ds(pl.multiple_of(r * TOK_ROWS, TOK_ROWS), TOK_ROWS), :],
                                      sems.at[s]).start(priority=i % 2)
            return c
        lax.fori_loop(0, rows // DMA_UNROLL, body, 0)

    @pl.when(step == 0)
    def _():
        start(0, 0)

    for s in range(2):
        @pl.when(lax.rem(step, 2) == s)
        def _(s=s):
            pltpu.make_async_copy(src_hbm.at[pl.ds(0, rows * TOK_ROWS), :], bufs[s], sems.at[s]).wait()

            @pl.when(step + 1 < nsteps)
            def _():
                start(1 - s, (step + 1) * rows)
            consume(bufs[s])


def _mixer_kernel(alpha, nb, gather_in, *refs):
    if gather_in:
        pos8_ref, zs_hbm = refs[:2]
        refs = refs[2:]
    else:
        x_ref = refs[0]
        refs = refs[1:]
    (win_ref, wout_ref, lnvg_ref, lnvb_ref, wtril_ref, bsb_ref, bdb_ref, bdc_ref, lr_ref, li_ref,
     dskip_ref, glu_ref, bglu_ref, ln1g_ref, ln1b_ref, wrt_ref, rbcol_ref, tri_ref,
     x1t_ref, hfin_ref, cls_ref, rank_ref, cnt_ref,
     xb_ref, xs_slab, xs_tb, bu_ref, mix_ref, hstate_ref, carry_ref) = refs[:30]
    lt = CHUNK
    rows = nb * lt
    pitch = lt + SUBLANES
    step = pl.program_id(0)
    nsteps = pl.num_programs(0)

    @pl.when(step == 0)
    def _():
        hstate_ref[...] = jnp.zeros_like(hstate_ref)
        carry_ref[...] = jnp.zeros_like(carry_ref)

    if gather_in:
        xin_ref, xbuf0, xbuf1, gsem = refs[30:34]

        def consume(buf):
            xin_ref[...] = _from_token_tiles(buf, 0, rows)
        _gathered_tokens(step, nsteps, pos8_ref, zs_hbm, (xbuf0, xbuf1), gsem, rows, consume)

        def load_x():
            return xin_ref[...]
    else:
        def load_x():
            return x_ref[...].reshape(rows, D_MODEL)

    xb_ref[...] = load_x().astype(BF16)

    xs = _dot(xb_ref[...], win_ref[:, 2 * W_A:])
    for j in range(W_B // LANES):
        for b in range(nb):
            xs_slab[j, b * pitch:b * pitch + lt, :] = xs[b * lt:(b + 1) * lt, j * LANES:(j + 1) * LANES]

    def to_tb(t, c):
        r0 = pl.multiple_of(t * nb, SUBLANES)
        for j in range(W_B // LANES):
            xs_tb[pl.ds(r0, nb), j * LANES:(j + 1) * LANES] = xs_slab[j, pl.ds(t, nb, stride=pitch), :]
        return c
    lax.fori_loop(0, lt, to_tb, 0, unroll=4)

    nblk = HALF_STATE // LANES
    for k in range(2):
        bu_ref[...] = _dot(xs_tb[:, k * HALF_W:(k + 1) * HALF_W].astype(BF16), bdb_ref[k])
        lr = [jnp.broadcast_to(lr_ref[k, :, c * LANES:(c + 1) * LANES], (nb, LANES)) for c in range(nblk)]
        li = [jnp.broadcast_to(li_ref[k, :, c * LANES:(c + 1) * LANES], (nb, LANES)) for c in range(nblk)]
        h0 = tuple(hstate_ref[k, :, c * LANES:(c + 1) * LANES] for c in range(2 * nblk))

        def scan(t, h):
            r0 = pl.multiple_of(t * nb, SUBLANES)
            out = [None] * (2 * nblk)
            for c in range(nblk):
                re_sl = slice(c * LANES, (c + 1) * LANES)
                im_sl = slice(HALF_STATE + c * LANES, HALF_STATE + (c + 1) * LANES)
                hr, hi = h[c], h[nblk + c]
                nr = lr[c] * hr - li[c] * hi + bu_ref[pl.ds(r0, nb), re_sl]
                ni = lr[c] * hi + li[c] * hr + bu_ref[pl.ds(r0, nb), im_sl]
                bu_ref[pl.ds(r0, nb), re_sl] = nr
                bu_ref[pl.ds(r0, nb), im_sl] = ni
                out[c], out[nblk + c] = nr, ni
            return tuple(out)
        hf = lax.fori_loop(0, lt, scan, h0)
        for c in range(2 * nblk):
            hstate_ref[k, :, c * LANES:(c + 1) * LANES] = hf[c]
        mix_ref[:, k * HALF_W:(k + 1) * HALF_W] = _dot(bu_ref[...].astype(BF16), bdc_ref[k])
    hfin_ref[...] = hstate_ref[...]

    y = jax.nn.gelu(mix_ref[:, :W_B] + dskip_ref[...] * xs_tb[...])
    yb = y.astype(BF16)
    for k in range(2):
        sl = slice(k * HALF_W, (k + 1) * HALF_W)
        gl = _dot(yb[:, sl], glu_ref[k]) + bglu_ref[:, sl]
        z = y[:, sl] * jax.nn.sigmoid(gl)
        for j in range(HALF_W // LANES):
            xs_slab[k * (HALF_W // LANES) + j, 0:rows, :] = z[:, j * LANES:(j + 1) * LANES]

    def to_bt(t8, c):
        for b in range(nb):
            dst = pl.multiple_of(b * lt + t8 * SUBLANES, SUBLANES)
            for j in range(W_B // LANES):
                mix_ref[pl.ds(dst, SUBLANES), W_A + j * LANES:W_A + (j + 1) * LANES] = (
                    xs_slab[j, pl.ds(t8 * SUBLANES * nb + b, SUBLANES, stride=nb), :])
        return c
    lax.fori_loop(0, lt // SUBLANES, to_bt, 0)

    vg = jax.nn.gelu(_dot(xb_ref[...], win_ref[:, W_A:2 * W_A]))
    v = _layer_norm(vg, lnvg_ref[...], lnvb_ref[...]).astype(BF16)
    u = jax.nn.gelu(_dot(xb_ref[...], win_ref[:, :W_A]))
    for h in range(H_A):
        hs = slice(h * P_A, (h + 1) * P_A)
        vcat = jnp.concatenate([v[b * lt:(b + 1) * lt, hs] for b in range(nb)], axis=1)
        o = _dot(wtril_ref[h], vcat)
        for b in range(nb):
            rs = slice(b * lt, (b + 1) * lt)
            mix_ref[rs, hs] = u[rs, hs] * (o[:, b * LANES:(b + 1) * LANES] + bsb_ref[h])

    mix = _dot(mix_ref[...].astype(BF16), wout_ref[...])
    x1 = _layer_norm(alpha * load_x() + mix, ln1g_ref[...], ln1b_ref[...])
    _to_token_tiles(x1t_ref, 0, x1)

    cls = _route_classes(x1, wrt_ref[...], rbcol_ref[...])
    crow = lax.broadcasted_iota(I32, (CLASS_ROWS, rows), 0)
    onehot = jnp.where(crow == cls, 1.0, 0.0)
    prefix = _dot(onehot.astype(BF16), tri_ref[...])
    carry = carry_ref[:, 0:1]
    rank = jnp.sum(onehot * (prefix + carry), axis=0, keepdims=True)
    cls_ref[...] = cls.reshape(1, 1, rows)
    rank_ref[...] = rank.astype(I32).reshape(1, 1, rows)
    carry_ref[...] = carry_ref[...] + jnp.sum(onehot, axis=1, keepdims=True)
    cnt_ref[...] = carry_ref[...]


def _const_spec(shape, nprefetch=0):
    nd = len(shape)
    return pl.BlockSpec(shape, lambda *_: (0,) * nd, pipeline_mode=pl.Buffered(1))


def _mixer_prompt(x, lw, alpha, nb, seq, pos_prev=None):
    lt = CHUNK
    rows = nb * lt
    nsteps = seq // lt
    gather_in = pos_prev is not None
    weights = (lw["win"], lw["wout"], lw["lnvg"], lw["lnvb"], lw["wtril"], lw["bsb"], lw["bdb"],
               lw["bdc"], lw["lr"], lw["li"], lw["dskip"], lw["glu"], lw["bglu"], lw["ln1g"], lw["ln1b"],
               lw["wrt"], lw["rbcol"], lw["tri"])
    if gather_in:
        x_spec = pl.BlockSpec(memory_space=pl.ANY)
    else:
        x_spec = pl.BlockSpec((nb, lt, D_MODEL), lambda i, *_: (0, i, 0))
    scratch = [
        pltpu.VMEM((rows, D_MODEL), BF16),
        pltpu.VMEM((W_B // LANES, nb * (lt + SUBLANES), LANES), F32),
        pltpu.VMEM((rows, W_B), F32),
        pltpu.VMEM((rows, 2 * HALF_STATE), F32),
        pltpu.VMEM((rows, D_MODEL), F32),
        pltpu.VMEM((2, nb, 2 * HALF_STATE), F32),
        pltpu.VMEM((CLASS_ROWS, LANES), F32),
    ]
    if gather_in:
        scratch += [pltpu.VMEM((rows, D_MODEL), F32),
                    pltpu.VMEM((rows * TOK_ROWS, LANES), F32), pltpu.VMEM((rows * TOK_ROWS, LANES), F32),
                    pltpu.SemaphoreType.DMA((2,))]
    grid_spec = pltpu.PrefetchScalarGridSpec(
        num_scalar_prefetch=1 if gather_in else 0,
        grid=(nsteps,),
        in_specs=[x_spec] + [_const_spec(w.shape) for w in weights],
        out_specs=[pl.BlockSpec((rows * TOK_ROWS, LANES), lambda i, *_: (i, 0)),
                   pl.BlockSpec((2, nb, 2 * HALF_STATE), lambda i, *_: (0, 0, 0)),
                   pl.BlockSpec((1, 1, rows), lambda i, *_: (i, 0, 0)),
                   pl.BlockSpec((1, 1, rows), lambda i, *_: (i, 0, 0)),
                   pl.BlockSpec((CLASS_ROWS, LANES), lambda i, *_: (0, 0))],
        scratch_shapes=scratch)
    args = ((pos_prev, x) if gather_in else (x,)) + weights
    return pl.pallas_call(
        functools.partial(_mixer_kernel, alpha, nb, gather_in),
        grid_spec=grid_spec,
        out_shape=[jax.ShapeDtypeStruct((nb * seq * TOK_ROWS, LANES), F32),
                   jax.ShapeDtypeStruct((2, nb, 2 * HALF_STATE), F32),
                   jax.ShapeDtypeStruct((nsteps, 1, rows), I32),
                   jax.ShapeDtypeStruct((nsteps, 1, rows), I32),
                   jax.ShapeDtypeStruct((CLASS_ROWS, LANES), F32)],
        compiler_params=pltpu.CompilerParams(dimension_semantics=("arbitrary",),
                                             vmem_limit_bytes=VMEM_LIMIT),
    )(*args)


def _dispatch_kernel(rows, pos8_ref, zstart_ref, zlen_ref, tail_ref, x_ref, xs_hbm, zero_ref, sem, zsem):
    step = pl.program_id(0)
    ztok = ZERO_TOKENS

    @pl.when(step == 0)
    def _():
        zero_ref[...] = jnp.zeros_like(zero_ref)
        pieces = []
        for c in range(N_CLASSES):
            start = zstart_ref[c]
            zlen = zlen_ref[c]
            p = TILE_M // 2
            while p >= 1:
                hit = (zlen & p) != 0
                pieces.append((hit, pltpu.make_async_copy(
                    zero_ref.at[pl.ds(0, p * TOK_ROWS), :],
                    xs_hbm.at[pl.ds(pl.multiple_of(start * TOK_ROWS, TOK_ROWS), p * TOK_ROWS), :], zsem)))
                start = start + jnp.where(hit, p, 0)
                p //= 2
        for hit, cp in pieces:
            pl.when(hit)(cp.start)
        for hit, cp in pieces:
            pl.when(hit)(cp.wait)

        zrows = ztok * TOK_ROWS
        first = tail_ref[0] // ztok

        def tail_copy(q):
            return pltpu.make_async_copy(
                zero_ref, xs_hbm.at[pl.ds(pl.multiple_of(q * zrows, zrows), zrows), :], zsem)

        def tail_start(q, c):
            tail_copy(q).start()
            return c

        def tail_wait(q, c):
            tail_copy(q).wait()
            return c
        lax.fori_loop(first, xs_hbm.shape[0] // zrows, tail_start, 0)
        lax.fori_loop(first, xs_hbm.shape[0] // zrows, tail_wait, 0)

    base = step * rows

    def body(g, c):
        for i in range(DMA_UNROLL):
            r = g * DMA_UNROLL + i
            p8 = pl.multiple_of(pos8_ref[base + r], TOK_ROWS)
            pltpu.make_async_copy(x_ref.at[pl.ds(pl.multiple_of(r * TOK_ROWS, TOK_ROWS), TOK_ROWS), :],
                                  xs_hbm.at[pl.ds(p8, TOK_ROWS), :], sem).start(priority=i % 2)
        return c
    lax.fori_loop(0, rows // DMA_UNROLL, body, 0)
    pltpu.make_async_copy(x_ref, xs_hbm.at[pl.ds(0, rows * TOK_ROWS), :], sem).wait()


def _dispatch(x1t, pos8, zstart, zlen, tail, rows, ns_tokens):
    return pl.pallas_call(
        functools.partial(_dispatch_kernel, rows),
        grid_spec=pltpu.PrefetchScalarGridSpec(
            num_scalar_prefetch=4,
            grid=(x1t.shape[0] // (rows * TOK_ROWS),),
            in_specs=[pl.BlockSpec((rows * TOK_ROWS, LANES), lambda i, *_: (i, 0))],
            out_specs=pl.BlockSpec(memory_space=pl.ANY),
            scratch_shapes=[pltpu.VMEM((ZERO_TOKENS * TOK_ROWS, LANES), F32),
                            pltpu.SemaphoreType.DMA(()), pltpu.SemaphoreType.DMA(())]),
        out_shape=jax.ShapeDtypeStruct((ns_tokens * TOK_ROWS, LANES), F32),
        compiler_params=pltpu.CompilerParams(dimension_semantics=("arbitrary",),
                                             vmem_limit_bytes=VMEM_LIMIT),
    )(pos8, zstart, zlen, tail, x1t)


def _pair_kernel(alpha, tidx_ref, tcls_ref, ea_ref, eb_ref, nused_ref,
                 x_ref, wrp_ref, wga_ref, wua_ref, wda_ref, wgb_ref, wub_ref, wdb_ref,
                 ln2g_ref, ln2b_ref, z_ref, w1_ref, w2_ref):
    j = pl.program_id(0)
    f = D_FF_EXPERT
    m = TILE_M // PAIR_SUBTILES

    @pl.when(j >= nused_ref[0])
    def _():
        z_ref[...] = jnp.zeros_like(z_ref)

    @pl.when(j < nused_ref[0])
    def _():
        changed = jnp.logical_or(j == 0, tcls_ref[j] != tcls_ref[jnp.maximum(j - 1, 0)])

        @pl.when(changed)
        def _():
            w1_ref[:, 0 * f:1 * f] = wga_ref[...].astype(BF16)
            w1_ref[:, 1 * f:2 * f] = wua_ref[...].astype(BF16)
            w1_ref[:, 2 * f:3 * f] = wgb_ref[...].astype(BF16)
            w1_ref[:, 3 * f:4 * f] = wub_ref[...].astype(BF16)
            w1_ref[:, 4 * f:] = wrp_ref[...].astype(BF16)
            w2_ref[0:f, :] = wda_ref[...].astype(BF16)
            w2_ref[f:2 * f, :] = wdb_ref[...].astype(BF16)

        for s in range(PAIR_SUBTILES):
            x = _from_token_tiles(x_ref, s * m, m)
            gu = _dot(x.astype(BF16), w1_ref[...])
            scores = jax.nn.sigmoid(gu[:, 4 * f:])
            lane = lax.broadcasted_iota(I32, scores.shape, 1)
            sa = jnp.sum(jnp.where(lane == ea_ref[j], scores, 0.0), axis=-1, keepdims=True)
            sb = jnp.sum(jnp.where(lane == eb_ref[j], scores, 0.0), axis=-1, keepdims=True)
            tot = sa + sb
            ha = jax.nn.silu(gu[:, 0 * f:1 * f]) * gu[:, 1 * f:2 * f] * (sa / tot)
            hb = jax.nn.silu(gu[:, 2 * f:3 * f]) * gu[:, 3 * f:4 * f] * (sb / tot)
            moe = _dot(jnp.concatenate([ha, hb], axis=1).astype(BF16), w2_ref[...])
            _to_token_tiles(z_ref, s * m, _layer_norm(alpha * x + moe, ln2g_ref[...], ln2b_ref[...]))


def _pair_experts(xs, l, tidx, tcls, ea, eb, nused, wrp, w_gate, w_up, w_down, ln2g, ln2b, alpha):
    nt = tidx.shape[0]

    def wspec(shape, table):
        return pl.BlockSpec((None, None) + shape, lambda j, ti, tc, a, b, nu: (l, (a, b)[table][j], 0, 0))
    up = (D_MODEL, D_FF_EXPERT)
    dn = (D_FF_EXPERT, D_MODEL)
    cst = lambda j, *_: (0, 0)
    return pl.pallas_call(
        functools.partial(_pair_kernel, alpha),
        grid_spec=pltpu.PrefetchScalarGridSpec(
            num_scalar_prefetch=5,
            grid=(nt,),
            in_specs=[pl.BlockSpec((TILE_M * TOK_ROWS, LANES), lambda j, ti, *_: (ti[j], 0)),
                      pl.BlockSpec(wrp.shape, cst),
                      wspec(up, 0), wspec(up, 0), wspec(dn, 0), wspec(up, 1), wspec(up, 1), wspec(dn, 1),
                      pl.BlockSpec(ln2g.shape, cst), pl.BlockSpec(ln2b.shape, cst)],
            out_specs=pl.BlockSpec((TILE_M * TOK_ROWS, LANES), lambda j, *_: (j, 0)),
            scratch_shapes=[pltpu.VMEM((D_MODEL, 4 * D_FF_EXPERT + LANES), BF16),
                            pltpu.VMEM((2 * D_FF_EXPERT, D_MODEL), BF16)]),
        out_shape=jax.ShapeDtypeStruct(xs.shape, F32),
        compiler_params=pltpu.CompilerParams(dimension_semantics=("arbitrary",),
                                             vmem_limit_bytes=VMEM_LIMIT),
    )(tidx, tcls, ea, eb, nused, xs, wrp, w_gate, w_up, w_down, w_gate, w_up, w_down, ln2g, ln2b)


def _ungather_kernel(nb, pos8_ref, zs_hbm, out_ref, buf0, buf1, sems):
    rows = nb * CHUNK

    def consume(buf):
        out_ref[...] = _from_token_tiles(buf, 0, rows).reshape(nb, CHUNK, D_MODEL)
    _gathered_tokens(pl.program_id(0), pl.num_programs(0), pos8_ref, zs_hbm, (buf0, buf1), sems, rows, consume)


def _ungather(zs, pos8, nb, seq):
    rows = nb * CHUNK
    return pl.pallas_call(
        functools.partial(_ungather_kernel, nb),
        grid_spec=pltpu.PrefetchScalarGridSpec(
            num_scalar_prefetch=1,
            grid=(seq // CHUNK,),
            in_specs=[pl.BlockSpec(memory_space=pl.ANY)],
            out_specs=pl.BlockSpec((nb, CHUNK, D_MODEL), lambda i, *_: (0, i, 0)),
            scratch_shapes=[pltpu.VMEM((rows * TOK_ROWS, LANES), F32), pltpu.VMEM((rows * TOK_ROWS, LANES), F32),
                            pltpu.SemaphoreType.DMA((2,))]),
        out_shape=jax.ShapeDtypeStruct((nb, seq, D_MODEL), F32),
        compiler_params=pltpu.CompilerParams(dimension_semantics=("arbitrary",),
                                             vmem_limit_bytes=VMEM_LIMIT),
    )(pos8, zs)


def _plan(cls, rank, counts, nt):
    cnt = counts[:N_CLASSES, 0].astype(I32)
    ntile = (cnt + TILE_M - 1) // TILE_M
    padded = ntile * TILE_M
    off = jnp.cumsum(padded) - padded
    pos = (off[cls.reshape(-1)] + rank.reshape(-1)).astype(I32)
    tile_end = jnp.cumsum(ntile)
    nused = tile_end[-1:].astype(I32)
    tidx = jnp.minimum(jnp.arange(nt, dtype=I32), nused - 1)
    tcls = jnp.sum((tile_end[None, :] <= tidx[:, None]).astype(I32), axis=1)
    ea = jnp.asarray(EA_TABLE)[tcls]
    eb = jnp.asarray(EB_TABLE)[tcls]
    return pos * TOK_ROWS, off + cnt, padded - cnt, nused * TILE_M, tidx, tcls, ea, eb, nused


def _mixer_sample_kernel(alpha, x_ref, h0_ref, win_ref, wout_ref, lnvg_ref, lnvb_ref, ws0_ref, bs0_ref,
                         bdb_ref, bdc_ref, lr_ref, li_ref, dskip_ref, glu_ref, bglu_ref,
                         ln1g_ref, ln1b_ref,
                         x1_ref, hnew_ref, v_ref):
    x = x_ref[...]
    proj = _dot(x.astype(BF16), win_ref[...])
    u = jax.nn.gelu(proj[:, :W_A])
    v = _layer_norm(jax.nn.gelu(proj[:, W_A:2 * W_A]), lnvg_ref[...], lnvb_ref[...])
    v_ref[...] = v
    y_a = u * (ws0_ref[...] * v + bs0_ref[...])
    xs = proj[:, 2 * W_A:]
    zs = []
    for k in range(2):
        sl = slice(k * HALF_W, (k + 1) * HALF_W)
        bu = _dot(xs[:, sl].astype(BF16), bdb_ref[k])
        h0r = h0_ref[k, :, :HALF_STATE]
        h0i = h0_ref[k, :, HALF_STATE:]
        lr = lr_ref[k]
        li = li_ref[k]
        hr = lr * h0r - li * h0i + bu[:, :HALF_STATE]
        hi = lr * h0i + li * h0r + bu[:, HALF_STATE:]
        hnew_ref[k, :, :HALF_STATE] = hr
        hnew_ref[k, :, HALF_STATE:] = hi
        hcat = jnp.concatenate([hr, hi], axis=1).astype(BF16)
        y = jax.nn.gelu(_dot(hcat, bdc_ref[k]) + dskip_ref[:, sl] * xs[:, sl])
        gl = _dot(y.astype(BF16), glu_ref[k]) + bglu_ref[:, sl]
        zs.append(y * jax.nn.sigmoid(gl))
    cat = jnp.concatenate([y_a] + zs, axis=1).astype(BF16)
    mix = _dot(cat, wout_ref[...])
    x1_ref[...] = _layer_norm(alpha * x + mix, ln1g_ref[...], ln1b_ref[...])


def _mixer_sample(x, h0, lw, alpha):
    n = x.shape[0]
    return pl.pallas_call(
        functools.partial(_mixer_sample_kernel, alpha),
        out_shape=[jax.ShapeDtypeStruct((n, D_MODEL), F32),
                   jax.ShapeDtypeStruct((2, n, 2 * HALF_STATE), F32),
                   jax.ShapeDtypeStruct((n, W_A), F32)],
        compiler_params=pltpu.CompilerParams(vmem_limit_bytes=VMEM_LIMIT),
    )(x, h0, lw["win"], lw["wout"], lw["lnvg"], lw["lnvb"], lw["ws0"], lw["bs0"], lw["bdb"],
      lw["bdc"], lw["lr"], lw["li"], lw["dskip"], lw["glu"], lw["bglu"], lw["ln1g"], lw["ln1b"])


def _route(x, wr, rbias):
    logits = jnp.dot(x, wr, preferred_element_type=F32, precision=HIGHEST)
    scores = jax.nn.sigmoid(logits)
    biased = scores + rbias
    lane = lax.broadcasted_iota(I32, biased.shape, 1)
    grp = lane // EXPERTS_PER_GROUP
    neg = jnp.float32(-jnp.inf)

    def top2(vals):
        m1 = jnp.max(vals, axis=-1, keepdims=True)
        i1 = jnp.min(jnp.where(vals == m1, lane, N_EXPERTS), axis=-1, keepdims=True)
        rest = jnp.where(lane == i1, neg, vals)
        m2 = jnp.max(rest, axis=-1, keepdims=True)
        i2 = jnp.min(jnp.where(rest == m2, lane, N_EXPERTS), axis=-1, keepdims=True)
        return m1, i1, m2, i2

    best = sel = None
    for g in range(N_EXPERT_GROUPS):
        m1, _, m2, _ = top2(jnp.where(grp == g, biased, neg))
        gs = m1 + m2
        if g == 0:
            best, sel = gs, jnp.zeros(gs.shape, I32)
        else:
            upd = gs > best
            sel = jnp.where(upd, g, sel)
            best = jnp.where(upd, gs, best)
    _, i1, _, i2 = top2(jnp.where(grp == sel, biased, neg))
    s1 = jnp.sum(jnp.where(lane == i1, scores, 0.0), axis=-1, keepdims=True)
    s2 = jnp.sum(jnp.where(lane == i2, scores, 0.0), axis=-1, keepdims=True)
    tot = s1 + s2
    return jnp.where(lane == i1, s1 / tot, 0.0) + jnp.where(lane == i2, s2 / tot, 0.0)


def _moe_kernel(alpha, x_ref, wr_ref, rb_ref, wg_ref, wu_ref, wd_ref, ln2g_ref, ln2b_ref,
                out_ref, xb_ref, comb_ref, acc_ref):
    e = pl.program_id(1)

    @pl.when(e == 0)
    def _():
        x = x_ref[...]
        xb_ref[...] = x.astype(BF16)
        comb_ref[...] = _route(x, wr_ref[...], rb_ref[...])
        acc_ref[...] = jnp.zeros_like(acc_ref)

    xb = xb_ref[...]
    g = _dot(xb, wg_ref[...].astype(BF16))
    u = _dot(xb, wu_ref[...].astype(BF16))
    comb = comb_ref[...]
    lane = lax.broadcasted_iota(I32, comb.shape, 1)
    ce = jnp.sum(jnp.where(lane == e, comb, 0.0), axis=-1, keepdims=True)
    h = (jax.nn.silu(g) * u * ce).astype(BF16)
    acc_ref[...] += _dot(h, wd_ref[...].astype(BF16))

    @pl.when(e == N_EXPERTS - 1)
    def _():
        out_ref[...] = _layer_norm(alpha * x_ref[...] + acc_ref[...], ln2g_ref[...], ln2b_ref[...])


def _moe_dense(x, l, wr, rb, w_gate, w_up, w_down, ln2g, ln2b, alpha, tm):
    t = x.shape[0]
    cst = lambda i, e: (0, 0)
    wsel = lambda i, e: (l, e, 0, 0)
    return pl.pallas_call(
        functools.partial(_moe_kernel, alpha),
        grid=(t // tm, N_EXPERTS),
        in_specs=[pl.BlockSpec((tm, D_MODEL), lambda i, e: (i, 0)),
                  pl.BlockSpec(wr.shape, cst), pl.BlockSpec(rb.shape, cst),
                  pl.BlockSpec((None, None, D_MODEL, D_FF_EXPERT), wsel),
                  pl.BlockSpec((None, None, D_MODEL, D_FF_EXPERT), wsel),
                  pl.BlockSpec((None, None, D_FF_EXPERT, D_MODEL), wsel),
                  pl.BlockSpec(ln2g.shape, cst), pl.BlockSpec(ln2b.shape, cst)],
        out_specs=pl.BlockSpec((tm, D_MODEL), lambda i, e: (i, 0)),
        out_shape=jax.ShapeDtypeStruct((t, D_MODEL), F32),
        scratch_shapes=[pltpu.VMEM((tm, D_MODEL), BF16),
                        pltpu.VMEM((tm, N_EXPERTS), F32),
                        pltpu.VMEM((tm, D_MODEL), F32)],
        compiler_params=pltpu.CompilerParams(dimension_semantics=("arbitrary", "arbitrary"),
                                             vmem_limit_bytes=VMEM_LIMIT),
    )(x, wr, rb, w_gate, w_up, w_down, ln2g, ln2b)


def _block_diag(blocks):
    eye = jnp.eye(HALF_GROUPS, dtype=blocks.dtype)
    k, g, a, b = blocks.shape
    return jnp.einsum("kgab,gh->kgahb", blocks, eye).reshape(k, g * a, g * b)


def _prep_layer(l, w_in, w_out, ln_v_g, ln_v_b, w_s, b_s, a_re, a_im, log_dt, b_re, b_im, c_re, c_im,
                d_skip, w_glu, b_glu, ln1_g, ln1_b):
    a_re, a_im = a_re[l], a_im[l]
    dt = jnp.exp(log_dt[l])[:, None]
    decay = jnp.exp(a_re * dt)
    lb_re, lb_im = decay * jnp.cos(a_im * dt), decay * jnp.sin(a_im * dt)
    den = a_re * a_re + a_im * a_im
    nr, ni = lb_re - 1.0, lb_im
    zr = (nr * a_re + ni * a_im) / den
    zi = (ni * a_re - nr * a_im) / den
    bb_re = zr[..., None] * b_re[l] - zi[..., None] * b_im[l]
    bb_im = zr[..., None] * b_im[l] + zi[..., None] * b_re[l]

    def halves(a):
        return a.reshape((2, HALF_GROUPS) + a.shape[1:])

    bdb = jnp.concatenate([_block_diag(halves(bb_re.transpose(0, 2, 1))),
                           _block_diag(halves(bb_im.transpose(0, 2, 1)))], axis=2)
    bdc = jnp.concatenate([_block_diag(halves(c_re[l].transpose(0, 2, 1))),
                           _block_diag(halves(-c_im[l].transpose(0, 2, 1)))], axis=1)
    return dict(
        win=w_in[l].astype(BF16), wout=w_out[l].astype(BF16),
        lnvg=ln_v_g[l][None], lnvb=ln_v_b[l][None],
        wtril=jnp.tril(w_s[l]).astype(BF16),
        bsb=jnp.broadcast_to(b_s[l][:, :, None], (H_A, CHUNK, LANES)),
        ws0=jnp.repeat(w_s[l][:, 0, 0], P_A)[None], bs0=jnp.repeat(b_s[l][:, 0], P_A)[None],
        bdb=bdb.astype(BF16), bdc=bdc.astype(BF16),
        lr=lb_re.reshape(2, 1, HALF_STATE), li=lb_im.reshape(2, 1, HALF_STATE),
        dskip=d_skip[l].reshape(1, W_B), glu=_block_diag(halves(w_glu[l])).astype(BF16),
        bglu=b_glu[l].reshape(1, W_B), ln1g=ln1_g[l][None], ln1b=ln1_b[l][None])


def _state_to_cols(h_re, h_im):
    b = h_re.shape[0]
    re = h_re.reshape(b, 2, HALF_STATE)
    im = h_im.reshape(b, 2, HALF_STATE)
    return jnp.concatenate([re, im], axis=2).transpose(1, 0, 2)


def _cols_to_state(h):
    b = h.shape[1]
    re = h[:, :, :HALF_STATE].transpose(1, 0, 2).reshape(b, G_B, N_STATE)
    im = h[:, :, HALF_STATE:].transpose(1, 0, 2).reshape(b, G_B, N_STATE)
    return re, im


def kernel(x_prompt, x_sample, state_ssm_re, state_ssm_im, w_in, w_out, ln_v_g, ln_v_b, w_s, b_s, ssm_a_re, ssm_a_im, ssm_log_dt, ssm_b_re, ssm_b_im, ssm_c_re, ssm_c_im, ssm_d, w_glu, b_glu, ln1_g, ln1_b, ln2_g, ln2_b, w_router, router_bias, w_gate, w_up, w_down):
    depth = w_in.shape[0]
    alpha = float((2 * depth) ** 0.25)
    nb, seq, _ = x_prompt.shape
    ns = x_sample.shape[0]
    tokens = nb * seq
    nt = tokens // TILE_M + N_CLASSES
    rb = router_bias[None]
    wrt = w_router.T
    wrp = jnp.pad(w_router, ((0, 0), (0, LANES - N_EXPERTS)))
    rbcol = router_bias[:, None]
    tok = np.arange(nb * CHUNK)
    tri = jnp.asarray(tok[:, None] < tok[None, :], BF16)
    xp = x_prompt
    pos = None
    xs = x_sample.reshape(ns, D_MODEL)
    pr_re, pr_im, sm_re, sm_im, sm_v = [], [], [], [], []
    for l in range(depth):
        lw = _prep_layer(l, w_in, w_out, ln_v_g, ln_v_b, w_s, b_s, ssm_a_re, ssm_a_im, ssm_log_dt,
                         ssm_b_re, ssm_b_im, ssm_c_re, ssm_c_im, ssm_d, w_glu, b_glu, ln1_g, ln1_b)
        lw["wrt"], lw["rbcol"], lw["tri"] = wrt, rbcol, tri
        ln2g, ln2b = ln2_g[l][None], ln2_b[l][None]

        x1t, hfin, cls, rank, counts = _mixer_prompt(xp, lw, alpha, nb, seq, pos)
        pos, zstart, zlen, tail, tidx, tcls, ea, eb, nused = _plan(cls, rank, counts, nt)
        x_sorted = _dispatch(x1t, pos, zstart, zlen, tail, nb * CHUNK, nt * TILE_M)
        xp = _pair_experts(x_sorted, l, tidx, tcls, ea, eb, nused, wrp, w_gate, w_up, w_down,
                           ln2g, ln2b, alpha)
        hr, hi = _cols_to_state(hfin)
        pr_re.append(hr)
        pr_im.append(hi)

        x1s, hnew, v_new = _mixer_sample(xs, _state_to_cols(state_ssm_re[l], state_ssm_im[l]), lw, alpha)
        xs = _moe_dense(x1s, l, w_router, rb, w_gate, w_up, w_down, ln2g, ln2b, alpha, tm=ns)
        hr, hi = _cols_to_state(hnew)
        sm_re.append(hr)
        sm_im.append(hi)
        sm_v.append(v_new.reshape(ns, 1, W_A))
    y_prompt = _ungather(xp, pos, nb, seq)
    return (y_prompt, xs.reshape(ns, 1, D_MODEL), jnp.stack(pr_re), jnp.stack(pr_im),
            jnp.stack(sm_re), jnp.stack(sm_im), jnp.stack(sm_v))
```

```python
import functools

import jax
import jax.numpy as jnp
import numpy as np
from jax import lax
from jax.experimental import pallas as pl
from jax.experimental.pallas import tpu as pltpu

D_MODEL = 1024
W_A = 512
W_B = 512
CHUNK = 128
H_A = 4
P_A = W_A // H_A
GROUP_B = 16
G_B = W_B // GROUP_B
N_STATE = 64
N_EXPERTS = 16
N_EXPERT_GROUPS = 4
EXPERTS_PER_GROUP = N_EXPERTS // N_EXPERT_GROUPS
D_FF_EXPERT = D_MODEL // 4
LN_EPS = 1e-5

LANES = 128
SUBLANES = 8
HALF_GROUPS = 16
HALF_W = HALF_GROUPS * GROUP_B
HALF_STATE = HALF_GROUPS * N_STATE
VMEM_LIMIT = 56 * 1024 * 1024

PAIRS = ((0, 1), (0, 2), (0, 3), (1, 2), (1, 3), (2, 3))
N_CLASSES = N_EXPERT_GROUPS * len(PAIRS)
CLASS_ROWS = 32
TILE_M = 256
PAIR_SUBTILES = 1
TOK_ROWS = D_MODEL // LANES
ZERO_TOKENS = TILE_M // 2
DMA_UNROLL = 8
EA_TABLE = np.array([EXPERTS_PER_GROUP * g + a for g in range(N_EXPERT_GROUPS) for a, _ in PAIRS], np.int32)
EB_TABLE = np.array([EXPERTS_PER_GROUP * g + b for g in range(N_EXPERT_GROUPS) for _, b in PAIRS], np.int32)

F32 = jnp.float32
BF16 = jnp.bfloat16
I32 = jnp.int32
HIGHEST = lax.Precision.HIGHEST


def _layer_norm(x, g, b):
    mu = jnp.mean(x, axis=-1, keepdims=True)
    xc = x - mu
    var = jnp.mean(xc * xc, axis=-1, keepdims=True)
    return xc * lax.rsqrt(var + LN_EPS) * g + b


def _dot(a, b):
    return jnp.dot(a, b, preferred_element_type=F32)


def _route_classes(x1, wrt, rbcol):
    logits_t = lax.dot_general(wrt, x1, (((1,), (1,)), ((), ())),
                               preferred_element_type=F32, precision=HIGHEST)
    biased = jax.nn.sigmoid(logits_t) + rbcol
    rows = [biased[e:e + 1, :] for e in range(N_EXPERTS)]
    n = EXPERTS_PER_GROUP

    best = sel = None
    for g in range(N_EXPERT_GROUPS):
        v = rows[n * g:n * (g + 1)]
        gs = None
        for a, b in PAIRS:
            s = v[a] + v[b]
            gs = s if gs is None else jnp.maximum(gs, s)
        if g == 0:
            best, sel = gs, jnp.zeros(gs.shape, I32)
        else:
            upd = gs > best
            sel = jnp.where(upd, g, sel)
            best = jnp.where(upd, gs, best)

    cls = jnp.zeros(sel.shape, I32)
    for g in range(N_EXPERT_GROUPS):
        v = rows[n * g:n * (g + 1)]
        lo = jnp.full(sel.shape, n, I32)
        hi = jnp.full(sel.shape, -1, I32)
        for i in range(n):
            before = jnp.zeros(sel.shape, I32)
            for j in range(n):
                if j < i:
                    before = before + (v[j] >= v[i]).astype(I32)
                elif j > i:
                    before = before + (v[j] > v[i]).astype(I32)
            member = before < 2
            lo = jnp.where(member, jnp.minimum(lo, i), lo)
            hi = jnp.where(member, jnp.maximum(hi, i), hi)
        base = jnp.where(lo == 0, 0, jnp.where(lo == 1, 3, 5))
        cls = jnp.where(sel == g, g * len(PAIRS) + base + (hi - lo - 1), cls)
    return cls


def _to_token_tiles(ref, row0, x):
    n = x.shape[0]
    for c in range(TOK_ROWS):
        ref[pl.ds(row0 * TOK_ROWS + c, n, stride=TOK_ROWS), :] = x[:, c * LANES:(c + 1) * LANES]


def _from_token_tiles(ref, row0, n):
    return jnp.concatenate(
        [ref[pl.ds(row0 * TOK_ROWS + c, n, stride=TOK_ROWS), :] for c in range(TOK_ROWS)], axis=1)


def _gathered_tokens(step, nsteps, pos8_ref, src_hbm, bufs, sems, rows, consume):
    def start(s, base):
        def body(g, c):
            for i in range(DMA_UNROLL):
                r = g * DMA_UNROLL + i
                p8 = pl.multiple_of(pos8_ref[base + r], TOK_ROWS)
                pltpu.make_async_copy(src_hbm.at[pl.ds(p8, TOK_ROWS), :],
                                      bufs[s].at[pl.ds(pl.multiple_of(r * TOK_ROWS, TOK_ROWS), TOK_ROWS), :],
                                      sems.at[s]).start(priority=i % 2)
            return c
        lax.fori_loop(0, rows // DMA_UNROLL, body, 0)

    @pl.when(step == 0)
    def _():
        start(0, 0)

    for s in range(2):
        @pl.when(lax.rem(step, 2) == s)
        def _(s=s):
            pltpu.make_async_copy(src_hbm.at[pl.ds(0, rows * TOK_ROWS), :], bufs[s], sems.at[s]).wait()

            @pl.when(step + 1 < nsteps)
            def _():
                start(1 - s, (step + 1) * rows)
            consume(bufs[s])


def _mixer_kernel(alpha, nb, gather_in, *refs):
    if gather_in:
        pos8_ref, zs_hbm = refs[:2]
        refs = refs[2:]
    else:
        x_ref = refs[0]
        refs = refs[1:]
    (win_ref, wout_ref, lnvg_ref, lnvb_ref, wtril_ref, bsb_ref, bdb_ref, bdc_ref, lr_ref, li_ref,
     dskip_ref, glu_ref, bglu_ref, ln1g_ref, ln1b_ref, wrt_ref, rbcol_ref, tri_ref,
     x1t_ref, hfin_ref, cls_ref, rank_ref, cnt_ref,
     xb_ref, xs_slab, xs_tb, bu_ref, mix_ref, hstate_ref, carry_ref) = refs[:30]
    lt = CHUNK
    rows = nb * lt
    pitch = lt + SUBLANES
    step = pl.program_id(0)
    nsteps = pl.num_programs(0)

    @pl.when(step == 0)
    def _():
        hstate_ref[...] = jnp.zeros_like(hstate_ref)
        carry_ref[...] = jnp.zeros_like(carry_ref)

    if gather_in:
        xin_ref, xbuf0, xbuf1, gsem = refs[30:34]

        def consume(buf):
            xin_ref[...] = _from_token_tiles(buf, 0, rows)
        _gathered_tokens(step, nsteps, pos8_ref, zs_hbm, (xbuf0, xbuf1), gsem, rows, consume)

        def load_x():
            return xin_ref[...]
    else:
        def load_x():
            return x_ref[...].reshape(rows, D_MODEL)

    xb_ref[...] = load_x().astype(BF16)

    xs = _dot(xb_ref[...], win_ref[:, 2 * W_A:])
    for j in range(W_B // LANES):
        for b in range(nb):
            xs_slab[j, b * pitch:b * pitch + lt, :] = xs[b * lt:(b + 1) * lt, j * LANES:(j + 1) * LANES]

    def to_tb(t, c):
        r0 = pl.multiple_of(t * nb, SUBLANES)
        for j in range(W_B // LANES):
            xs_tb[pl.ds(r0, nb), j * LANES:(j + 1) * LANES] = xs_slab[j, pl.ds(t, nb, stride=pitch), :]
        return c
    lax.fori_loop(0, lt, to_tb, 0, unroll=4)

    nblk = HALF_STATE // LANES
    for k in range(2):
        bu_ref[...] = _dot(xs_tb[:, k * HALF_W:(k + 1) * HALF_W].astype(BF16), bdb_ref[k])
        lr = [jnp.broadcast_to(lr_ref[k, :, c * LANES:(c + 1) * LANES], (nb, LANES)) for c in range(nblk)]
        li = [jnp.broadcast_to(li_ref[k, :, c * LANES:(c + 1) * LANES], (nb, LANES)) for c in range(nblk)]
        h0 = tuple(hstate_ref[k, :, c * LANES:(c + 1) * LANES] for c in range(2 * nblk))

        def scan(t, h):
            r0 = pl.multiple_of(t * nb, SUBLANES)
            out = [None] * (2 * nblk)
            for c in range(nblk):
                re_sl = slice(c * LANES, (c + 1) * LANES)
                im_sl = slice(HALF_STATE + c * LANES, HALF_STATE + (c + 1) * LANES)
                hr, hi = h[c], h[nblk + c]
                nr = lr[c] * hr - li[c] * hi + bu_ref[pl.ds(r0, nb), re_sl]
                ni = lr[c] * hi + li[c] * hr + bu_ref[pl.ds(r0, nb), im_sl]
                bu_ref[pl.ds(r0, nb), re_sl] = nr
                bu_ref[pl.ds(r0, nb), im_sl] = ni
                out[c], out[nblk + c] = nr, ni
            return tuple(out)
        hf = lax.fori_loop(0, lt, scan, h0)
        for c in range(2 * nblk):
            hstate_ref[k, :, c * LANES:(c + 1) * LANES] = hf[c]
        mix_ref[:, k * HALF_W:(k + 1) * HALF_W] = _dot(bu_ref[...].astype(BF16), bdc_ref[k])
    hfin_ref[...] = hstate_ref[...]

    y = jax.nn.gelu(mix_ref[:, :W_B] + dskip_ref[...] * xs_tb[...])
    yb = y.astype(BF16)
    for k in range(2):
        sl = slice(k * HALF_W, (k + 1) * HALF_W)
        gl = _dot(yb[:, sl], glu_ref[k]) + bglu_ref[:, sl]
        z = y[:, sl] * jax.nn.sigmoid(gl)
        for j in range(HALF_W // LANES):
            xs_slab[k * (HALF_W // LANES) + j, 0:rows, :] = z[:, j * LANES:(j + 1) * LANES]

    def to_bt(t8, c):
        for b in range(nb):
            dst = pl.multiple_of(b * lt + t8 * SUBLANES, SUBLANES)
            for j in range(W_B // LANES):
                mix_ref[pl.ds(dst, SUBLANES), W_A + j * LANES:W_A + (j + 1) * LANES] = (
                    xs_slab[j, pl.ds(t8 * SUBLANES * nb + b, SUBLANES, stride=nb), :])
        return c
    lax.fori_loop(0, lt // SUBLANES, to_bt, 0)

    vg = jax.nn.gelu(_dot(xb_ref[...], win_ref[:, W_A:2 * W_A]))
    v = _layer_norm(vg, lnvg_ref[...], lnvb_ref[...]).astype(BF16)
    u = jax.nn.gelu(_dot(xb_ref[...], win_ref[:, :W_A]))
    for h in range(H_A):
        hs = slice(h * P_A, (h + 1) * P_A)
        vcat = jnp.concatenate([v[b * lt:(b + 1) * lt, hs] for b in range(nb)], axis=1)
        o = _dot(wtril_ref[h], vcat)
        for b in range(nb):
            rs = slice(b * lt, (b + 1) * lt)
            mix_ref[rs, hs] = u[rs, hs] * (o[:, b * LANES:(b + 1) * LANES] + bsb_ref[h])

    mix = _dot(mix_ref[...].astype(BF16), wout_ref[...])
    x1 = _layer_norm(alpha * load_x() + mix, ln1g_ref[...], ln1b_ref[...])
    _to_token_tiles(x1t_ref, 0, x1)

    cls = _route_classes(x1, wrt_ref[...], rbcol_ref[...])
    crow = lax.broadcasted_iota(I32, (CLASS_ROWS, rows), 0)
    onehot = jnp.where(crow == cls, 1.0, 0.0)
    prefix = _dot(onehot.astype(BF16), tri_ref[...])
    carry = carry_ref[:, 0:1]
    rank = jnp.sum(onehot * (prefix + carry), axis=0, keepdims=True)
    cls_ref[...] = cls.reshape(1, 1, rows)
    rank_ref[...] = rank.astype(I32).reshape(1, 1, rows)
    carry_ref[...] = carry_ref[...] + jnp.sum(onehot, axis=1, keepdims=True)
    cnt_ref[...] = carry_ref[...]


def _const_spec(shape):
    nd = len(shape)
    return pl.BlockSpec(shape, lambda *_: (0,) * nd, pipeline_mode=pl.Buffered(1))


def _layer_spec(shape, l):
    nd = len(shape)
    return pl.BlockSpec((None,) + tuple(shape[1:]), lambda *_: (l,) + (0,) * (nd - 1),
                        pipeline_mode=pl.Buffered(1))


MIXER_WEIGHTS = ("win", "wout", "lnvg", "lnvb", "wtril", "bsb", "bdb", "bdc", "lr", "li", "dskip", "glu",
                 "bglu", "ln1g", "ln1b")
SAMPLE_WEIGHTS = ("win", "wout", "lnvg", "lnvb", "ws0", "bs0", "bdb", "bdc", "lr", "li", "dskip", "glu",
                  "bglu", "ln1g", "ln1b")


def _mixer_prompt(x, lw, shared, l, alpha, nb, seq, pos_prev=None):
    lt = CHUNK
    rows = nb * lt
    nsteps = seq // lt
    gather_in = pos_prev is not None
    weights = tuple(lw[k] for k in MIXER_WEIGHTS) + tuple(shared)
    wspecs = [_layer_spec(lw[k].shape, l) for k in MIXER_WEIGHTS] + [_const_spec(w.shape) for w in shared]
    if gather_in:
        x_spec = pl.BlockSpec(memory_space=pl.ANY)
    else:
        x_spec = pl.BlockSpec((nb, lt, D_MODEL), lambda i, *_: (0, i, 0))
    scratch = [
        pltpu.VMEM((rows, D_MODEL), BF16),
        pltpu.VMEM((W_B // LANES, nb * (lt + SUBLANES), LANES), F32),
        pltpu.VMEM((rows, W_B), F32),
        pltpu.VMEM((rows, 2 * HALF_STATE), F32),
        pltpu.VMEM((rows, D_MODEL), F32),
        pltpu.VMEM((2, nb, 2 * HALF_STATE), F32),
        pltpu.VMEM((CLASS_ROWS, LANES), F32),
    ]
    if gather_in:
        scratch += [pltpu.VMEM((rows, D_MODEL), F32),
                    pltpu.VMEM((rows * TOK_ROWS, LANES), F32), pltpu.VMEM((rows * TOK_ROWS, LANES), F32),
                    pltpu.SemaphoreType.DMA((2,))]
    grid_spec = pltpu.PrefetchScalarGridSpec(
        num_scalar_prefetch=1 if gather_in else 0,
        grid=(nsteps,),
        in_specs=[x_spec] + wspecs,
        out_specs=[pl.BlockSpec((rows * TOK_ROWS, LANES), lambda i, *_: (i, 0)),
                   pl.BlockSpec((2, nb, 2 * HALF_STATE), lambda i, *_: (0, 0, 0)),
                   pl.BlockSpec((1, 1, rows), lambda i, *_: (i, 0, 0)),
                   pl.BlockSpec((1, 1, rows), lambda i, *_: (i, 0, 0)),
                   pl.BlockSpec((CLASS_ROWS, LANES), lambda i, *_: (0, 0))],
        scratch_shapes=scratch)
    args = ((pos_prev, x) if gather_in else (x,)) + weights
    return pl.pallas_call(
        functools.partial(_mixer_kernel, alpha, nb, gather_in),
        grid_spec=grid_spec,
        out_shape=[jax.ShapeDtypeStruct((nb * seq * TOK_ROWS, LANES), F32),
                   jax.ShapeDtypeStruct((2, nb, 2 * HALF_STATE), F32),
                   jax.ShapeDtypeStruct((nsteps, 1, rows), I32),
                   jax.ShapeDtypeStruct((nsteps, 1, rows), I32),
                   jax.ShapeDtypeStruct((CLASS_ROWS, LANES), F32)],
        compiler_params=pltpu.CompilerParams(dimension_semantics=("arbitrary",),
                                             vmem_limit_bytes=VMEM_LIMIT),
    )(*args)


def _dispatch_kernel(rows, pos8_ref, zstart_ref, zlen_ref, tail_ref, x_ref, xs_hbm, zero_ref, sem, zsem):
    step = pl.program_id(0)
    ztok = ZERO_TOKENS

    @pl.when(step == 0)
    def _():
        zero_ref[...] = jnp.zeros_like(zero_ref)
        pieces = []
        for c in range(N_CLASSES):
            start = zstart_ref[c]
            zlen = zlen_ref[c]
            p = TILE_M // 2
            while p >= 1:
                hit = (zlen & p) != 0
                pieces.append((hit, pltpu.make_async_copy(
                    zero_ref.at[pl.ds(0, p * TOK_ROWS), :],
                    xs_hbm.at[pl.ds(pl.multiple_of(start * TOK_ROWS, TOK_ROWS), p * TOK_ROWS), :], zsem)))
                start = start + jnp.where(hit, p, 0)
                p //= 2
        for hit, cp in pieces:
            pl.when(hit)(cp.start)
        for hit, cp in pieces:
            pl.when(hit)(cp.wait)

        zrows = ztok * TOK_ROWS
        first = tail_ref[0] // ztok

        def tail_copy(q):
            return pltpu.make_async_copy(
                zero_ref, xs_hbm.at[pl.ds(pl.multiple_of(q * zrows, zrows), zrows), :], zsem)

        def tail_start(q, c):
            tail_copy(q).start()
            return c

        def tail_wait(q, c):
            tail_copy(q).wait()
            return c
        lax.fori_loop(first, xs_hbm.shape[0] // zrows, tail_start, 0)
        lax.fori_loop(first, xs_hbm.shape[0] // zrows, tail_wait, 0)

    base = step * rows

    def body(g, c):
        for i in range(DMA_UNROLL):
            r = g * DMA_UNROLL + i
            p8 = pl.multiple_of(pos8_ref[base + r], TOK_ROWS)
            pltpu.make_async_copy(x_ref.at[pl.ds(pl.multiple_of(r * TOK_ROWS, TOK_ROWS), TOK_ROWS), :],
                                  xs_hbm.at[pl.ds(p8, TOK_ROWS), :], sem).start(priority=i % 2)
        return c
    lax.fori_loop(0, rows // DMA_UNROLL, body, 0)
    pltpu.make_async_copy(x_ref, xs_hbm.at[pl.ds(0, rows * TOK_ROWS), :], sem).wait()


def _dispatch(x1t, pos8, zstart, zlen, tail, rows, ns_tokens):
    return pl.pallas_call(
        functools.partial(_dispatch_kernel, rows),
        grid_spec=pltpu.PrefetchScalarGridSpec(
            num_scalar_prefetch=4,
            grid=(x1t.shape[0] // (rows * TOK_ROWS),),
            in_specs=[pl.BlockSpec((rows * TOK_ROWS, LANES), lambda i, *_: (i, 0))],
            out_specs=pl.BlockSpec(memory_space=pl.ANY),
            scratch_shapes=[pltpu.VMEM((ZERO_TOKENS * TOK_ROWS, LANES), F32),
                            pltpu.SemaphoreType.DMA(()), pltpu.SemaphoreType.DMA(())]),
        out_shape=jax.ShapeDtypeStruct((ns_tokens * TOK_ROWS, LANES), F32),
        compiler_params=pltpu.CompilerParams(dimension_semantics=("arbitrary",),
                                             vmem_limit_bytes=VMEM_LIMIT),
    )(pos8, zstart, zlen, tail, x1t)


def _pair_kernel(alpha, tidx_ref, tcls_ref, ea_ref, eb_ref, nused_ref,
                 x_ref, wrp_ref, wga_ref, wua_ref, wda_ref, wgb_ref, wub_ref, wdb_ref,
                 ln2g_ref, ln2b_ref, z_ref, w1_ref, w2_ref, xprev_ref, moe_ref):
    j = pl.program_id(0)
    f = D_FF_EXPERT
    nused = nused_ref[0]

    @pl.when(j == 0)
    def _():
        xprev_ref[...] = jnp.zeros_like(xprev_ref)
        moe_ref[...] = jnp.zeros_like(moe_ref)

    @pl.when(j > nused)
    def _():
        z_ref[...] = jnp.zeros_like(z_ref)

    @pl.when(j <= nused)
    def _():
        changed = jnp.logical_or(j == 0, tcls_ref[j] != tcls_ref[jnp.maximum(j - 1, 0)])

        @pl.when(changed)
        def _():
            w1_ref[:, 0 * f:1 * f] = wga_ref[...].astype(BF16)
            w1_ref[:, 1 * f:2 * f] = wua_ref[...].astype(BF16)
            w1_ref[:, 2 * f:3 * f] = wgb_ref[...].astype(BF16)
            w1_ref[:, 3 * f:4 * f] = wub_ref[...].astype(BF16)
            w1_ref[:, 4 * f:] = wrp_ref[...].astype(BF16)
            w2_ref[0:f, :] = wda_ref[...].astype(BF16)
            w2_ref[f:2 * f, :] = wdb_ref[...].astype(BF16)

        _to_token_tiles(z_ref, 0, _layer_norm(alpha * xprev_ref[...] + moe_ref[...], ln2g_ref[...], ln2b_ref[...]))

        x = _from_token_tiles(x_ref, 0, TILE_M)
        gu = _dot(x.astype(BF16), w1_ref[...])
        scores = jax.nn.sigmoid(gu[:, 4 * f:])
        lane = lax.broadcasted_iota(I32, scores.shape, 1)
        sa = jnp.sum(jnp.where(lane == ea_ref[j], scores, 0.0), axis=-1, keepdims=True)
        sb = jnp.sum(jnp.where(lane == eb_ref[j], scores, 0.0), axis=-1, keepdims=True)
        tot = sa + sb
        ha = jax.nn.silu(gu[:, 0 * f:1 * f]) * gu[:, 1 * f:2 * f] * (sa / tot)
        hb = jax.nn.silu(gu[:, 2 * f:3 * f]) * gu[:, 3 * f:4 * f] * (sb / tot)
        xprev_ref[...] = x
        moe_ref[...] = _dot(jnp.concatenate([ha, hb], axis=1).astype(BF16), w2_ref[...])


def _pair_experts(xs, l, tidx, tcls, ea, eb, nused, wrp, w_gate, w_up, w_down, ln2g, ln2b, alpha):
    nsteps = tidx.shape[0]

    def wspec(shape, table):
        return pl.BlockSpec((None, None) + shape, lambda j, ti, tc, a, b, nu: (l, (a, b)[table][j], 0, 0))
    up = (D_MODEL, D_FF_EXPERT)
    dn = (D_FF_EXPERT, D_MODEL)
    cst = lambda j, *_: (0, 0)
    lsel = lambda j, *_: (l, 0, 0)
    return pl.pallas_call(
        functools.partial(_pair_kernel, alpha),
        grid_spec=pltpu.PrefetchScalarGridSpec(
            num_scalar_prefetch=5,
            grid=(nsteps,),
            in_specs=[pl.BlockSpec((TILE_M * TOK_ROWS, LANES), lambda j, ti, *_: (ti[j], 0)),
                      pl.BlockSpec(wrp.shape, cst),
                      wspec(up, 0), wspec(up, 0), wspec(dn, 0), wspec(up, 1), wspec(up, 1), wspec(dn, 1),
                      pl.BlockSpec((None,) + ln2g.shape[1:], lsel), pl.BlockSpec((None,) + ln2b.shape[1:], lsel)],
            out_specs=pl.BlockSpec((TILE_M * TOK_ROWS, LANES), lambda j, *_: (jnp.maximum(j - 1, 0), 0)),
            scratch_shapes=[pltpu.VMEM((D_MODEL, 4 * D_FF_EXPERT + LANES), BF16),
                            pltpu.VMEM((2 * D_FF_EXPERT, D_MODEL), BF16),
                            pltpu.VMEM((TILE_M, D_MODEL), F32), pltpu.VMEM((TILE_M, D_MODEL), F32)]),
        out_shape=jax.ShapeDtypeStruct(xs.shape, F32),
        compiler_params=pltpu.CompilerParams(dimension_semantics=("arbitrary",),
                                             vmem_limit_bytes=VMEM_LIMIT),
    )(tidx, tcls, ea, eb, nused, xs, wrp, w_gate, w_up, w_down, w_gate, w_up, w_down, ln2g, ln2b)


def _ungather_kernel(nb, pos8_ref, zs_hbm, out_ref, buf0, buf1, sems):
    rows = nb * CHUNK

    def consume(buf):
        out_ref[...] = _from_token_tiles(buf, 0, rows).reshape(nb, CHUNK, D_MODEL)
    _gathered_tokens(pl.program_id(0), pl.num_programs(0), pos8_ref, zs_hbm, (buf0, buf1), sems, rows, consume)


def _ungather(zs, pos8, nb, seq):
    rows = nb * CHUNK
    return pl.pallas_call(
        functools.partial(_ungather_kernel, nb),
        grid_spec=pltpu.PrefetchScalarGridSpec(
            num_scalar_prefetch=1,
            grid=(seq // CHUNK,),
            in_specs=[pl.BlockSpec(memory_space=pl.ANY)],
            out_specs=pl.BlockSpec((nb, CHUNK, D_MODEL), lambda i, *_: (0, i, 0)),
            scratch_shapes=[pltpu.VMEM((rows * TOK_ROWS, LANES), F32), pltpu.VMEM((rows * TOK_ROWS, LANES), F32),
                            pltpu.SemaphoreType.DMA((2,))]),
        out_shape=jax.ShapeDtypeStruct((nb, seq, D_MODEL), F32),
        compiler_params=pltpu.CompilerParams(dimension_semantics=("arbitrary",),
                                             vmem_limit_bytes=VMEM_LIMIT),
    )(pos8, zs)


def _plan(cls, rank, counts, nt):
    cnt = counts[:N_CLASSES, 0].astype(I32)
    ntile = (cnt + TILE_M - 1) // TILE_M
    padded = ntile * TILE_M
    off = jnp.cumsum(padded) - padded
    classes = jnp.arange(N_CLASSES, dtype=I32)
    pos8 = (rank.reshape(-1) + jnp.sum(jnp.where(cls.reshape(-1, 1) == classes, off, 0), axis=1)) * TOK_ROWS
    tile_end = jnp.cumsum(ntile)
    nused = tile_end[-1:].astype(I32)
    tidx = jnp.minimum(jnp.arange(nt + 1, dtype=I32), nused - 1)
    tsel = tile_end[None, :] <= tidx[:, None]
    tcls = jnp.sum(tsel.astype(I32), axis=1)
    onehot = tcls[:, None] == classes
    ea = jnp.sum(jnp.where(onehot, jnp.asarray(EA_TABLE), 0), axis=1)
    eb = jnp.sum(jnp.where(onehot, jnp.asarray(EB_TABLE), 0), axis=1)
    return pos8.astype(I32), off + cnt, padded - cnt, nused * TILE_M, tidx, tcls, ea, eb, nused


def _mixer_sample_kernel(alpha, x_ref, h0_ref, win_ref, wout_ref, lnvg_ref, lnvb_ref, ws0_ref, bs0_ref,
                         bdb_ref, bdc_ref, lr_ref, li_ref, dskip_ref, glu_ref, bglu_ref,
                         ln1g_ref, ln1b_ref,
                         x1_ref, hnew_ref, v_ref):
    x = x_ref[...]
    proj = _dot(x.astype(BF16), win_ref[...])
    u = jax.nn.gelu(proj[:, :W_A])
    v = _layer_norm(jax.nn.gelu(proj[:, W_A:2 * W_A]), lnvg_ref[...], lnvb_ref[...])
    v_ref[...] = v
    y_a = u * (ws0_ref[...] * v + bs0_ref[...])
    xs = proj[:, 2 * W_A:]
    zs = []
    for k in range(2):
        sl = slice(k * HALF_W, (k + 1) * HALF_W)
        bu = _dot(xs[:, sl].astype(BF16), bdb_ref[k])
        h0r = h0_ref[k, :, :HALF_STATE]
        h0i = h0_ref[k, :, HALF_STATE:]
        lr = lr_ref[k]
        li = li_ref[k]
        hr = lr * h0r - li * h0i + bu[:, :HALF_STATE]
        hi = lr * h0i + li * h0r + bu[:, HALF_STATE:]
        hnew_ref[k, :, :HALF_STATE] = hr
        hnew_ref[k, :, HALF_STATE:] = hi
        hcat = jnp.concatenate([hr, hi], axis=1).astype(BF16)
        y = jax.nn.gelu(_dot(hcat, bdc_ref[k]) + dskip_ref[:, sl] * xs[:, sl])
        gl = _dot(y.astype(BF16), glu_ref[k]) + bglu_ref[:, sl]
        zs.append(y * jax.nn.sigmoid(gl))
    cat = jnp.concatenate([y_a] + zs, axis=1).astype(BF16)
    mix = _dot(cat, wout_ref[...])
    x1_ref[...] = _layer_norm(alpha * x + mix, ln1g_ref[...], ln1b_ref[...])


def _mixer_sample(x, h0, lw, l, alpha):
    n = x.shape[0]
    full = lambda shape: pl.BlockSpec(shape, lambda i: (0,) * len(shape))
    return pl.pallas_call(
        functools.partial(_mixer_sample_kernel, alpha),
        grid=(1,),
        in_specs=[full(x.shape), _layer_spec(h0.shape, l)] + [_layer_spec(lw[k].shape, l) for k in SAMPLE_WEIGHTS],
        out_specs=[full((n, D_MODEL)), full((2, n, 2 * HALF_STATE)), full((n, W_A))],
        out_shape=[jax.ShapeDtypeStruct((n, D_MODEL), F32),
                   jax.ShapeDtypeStruct((2, n, 2 * HALF_STATE), F32),
                   jax.ShapeDtypeStruct((n, W_A), F32)],
        compiler_params=pltpu.CompilerParams(dimension_semantics=("arbitrary",), vmem_limit_bytes=VMEM_LIMIT),
    )(x, h0, *[lw[k] for k in SAMPLE_WEIGHTS])


def _route(x, wr, rbias):
    logits = jnp.dot(x, wr, preferred_element_type=F32, precision=HIGHEST)
    scores = jax.nn.sigmoid(logits)
    biased = scores + rbias
    lane = lax.broadcasted_iota(I32, biased.shape, 1)
    grp = lane // EXPERTS_PER_GROUP
    neg = jnp.float32(-jnp.inf)

    def top2(vals):
        m1 = jnp.max(vals, axis=-1, keepdims=True)
        i1 = jnp.min(jnp.where(vals == m1, lane, N_EXPERTS), axis=-1, keepdims=True)
        rest = jnp.where(lane == i1, neg, vals)
        m2 = jnp.max(rest, axis=-1, keepdims=True)
        i2 = jnp.min(jnp.where(rest == m2, lane, N_EXPERTS), axis=-1, keepdims=True)
        return m1, i1, m2, i2

    best = sel = None
    for g in range(N_EXPERT_GROUPS):
        m1, _, m2, _ = top2(jnp.where(grp == g, biased, neg))
        gs = m1 + m2
        if g == 0:
            best, sel = gs, jnp.zeros(gs.shape, I32)
        else:
            upd = gs > best
            sel = jnp.where(upd, g, sel)
            best = jnp.where(upd, gs, best)
    _, i1, _, i2 = top2(jnp.where(grp == sel, biased, neg))
    s1 = jnp.sum(jnp.where(lane == i1, scores, 0.0), axis=-1, keepdims=True)
    s2 = jnp.sum(jnp.where(lane == i2, scores, 0.0), axis=-1, keepdims=True)
    tot = s1 + s2
    return jnp.where(lane == i1, s1 / tot, 0.0) + jnp.where(lane == i2, s2 / tot, 0.0)


def _moe_kernel(alpha, x_ref, wr_ref, rb_ref, wg_ref, wu_ref, wd_ref, ln2g_ref, ln2b_ref,
                out_ref, xb_ref, comb_ref, acc_ref):
    e = pl.program_id(1)

    @pl.when(e == 0)
    def _():
        x = x_ref[...]
        xb_ref[...] = x.astype(BF16)
        comb_ref[...] = _route(x, wr_ref[...], rb_ref[...])
        acc_ref[...] = jnp.zeros_like(acc_ref)

    xb = xb_ref[...]
    g = _dot(xb, wg_ref[...].astype(BF16))
    u = _dot(xb, wu_ref[...].astype(BF16))
    comb = comb_ref[...]
    lane = lax.broadcasted_iota(I32, comb.shape, 1)
    ce = jnp.sum(jnp.where(lane == e, comb, 0.0), axis=-1, keepdims=True)
    h = (jax.nn.silu(g) * u * ce).astype(BF16)
    acc_ref[...] += _dot(h, wd_ref[...].astype(BF16))

    @pl.when(e == N_EXPERTS - 1)
    def _():
        out_ref[...] = _layer_norm(alpha * x_ref[...] + acc_ref[...], ln2g_ref[...], ln2b_ref[...])


def _moe_dense(x, l, wr, rb, w_gate, w_up, w_down, ln2g, ln2b, alpha, tm):
    t = x.shape[0]
    cst = lambda i, e: (0, 0)
    wsel = lambda i, e: (l, e, 0, 0)
    return pl.pallas_call(
        functools.partial(_moe_kernel, alpha),
        grid=(t // tm, N_EXPERTS),
        in_specs=[pl.BlockSpec((tm, D_MODEL), lambda i, e: (i, 0)),
                  pl.BlockSpec(wr.shape, cst), pl.BlockSpec(rb.shape, cst),
                  pl.BlockSpec((None, None, D_MODEL, D_FF_EXPERT), wsel),
                  pl.BlockSpec((None, None, D_MODEL, D_FF_EXPERT), wsel),
                  pl.BlockSpec((None, None, D_FF_EXPERT, D_MODEL), wsel),
                  pl.BlockSpec((None,) + ln2g.shape[1:], lambda i, e: (l, 0, 0)),
                  pl.BlockSpec((None,) + ln2b.shape[1:], lambda i, e: (l, 0, 0))],
        out_specs=pl.BlockSpec((tm, D_MODEL), lambda i, e: (i, 0)),
        out_shape=jax.ShapeDtypeStruct((t, D_MODEL), F32),
        scratch_shapes=[pltpu.VMEM((tm, D_MODEL), BF16),
                        pltpu.VMEM((tm, N_EXPERTS), F32),
                        pltpu.VMEM((tm, D_MODEL), F32)],
        compiler_params=pltpu.CompilerParams(dimension_semantics=("arbitrary", "arbitrary"),
                                             vmem_limit_bytes=VMEM_LIMIT),
    )(x, wr, rb, w_gate, w_up, w_down, ln2g, ln2b)


def _block_diag(blocks):
    eye = jnp.eye(HALF_GROUPS, dtype=blocks.dtype)
    k, g, a, b = blocks.shape
    return jnp.einsum("kgab,gh->kgahb", blocks, eye).reshape(k, g * a, g * b)


def _prep_all(w_in, w_out, ln_v_g, ln_v_b, w_s, b_s, a_re, a_im, log_dt, b_re, b_im, c_re, c_im,
              d_skip, w_glu, b_glu, ln1_g, ln1_b):
    d = w_in.shape[0]
    dt = jnp.exp(log_dt)[..., None]
    decay = jnp.exp(a_re * dt)
    lb_re, lb_im = decay * jnp.cos(a_im * dt), decay * jnp.sin(a_im * dt)
    den = a_re * a_re + a_im * a_im
    nr, ni = lb_re - 1.0, lb_im
    zr = (nr * a_re + ni * a_im) / den
    zi = (ni * a_re - nr * a_im) / den
    bb_re = zr[..., None] * b_re - zi[..., None] * b_im
    bb_im = zr[..., None] * b_im + zi[..., None] * b_re

    def bd(a):
        blocks = _block_diag(a.reshape((d * 2, HALF_GROUPS) + a.shape[2:]))
        return blocks.reshape((d, 2) + blocks.shape[1:])

    bdb = jnp.concatenate([bd(bb_re.transpose(0, 1, 3, 2)), bd(bb_im.transpose(0, 1, 3, 2))], axis=3)
    bdc = jnp.concatenate([bd(c_re.transpose(0, 1, 3, 2)), bd(-c_im.transpose(0, 1, 3, 2))], axis=2)
    return dict(
        win=w_in.astype(BF16), wout=w_out.astype(BF16),
        lnvg=ln_v_g[:, None], lnvb=ln_v_b[:, None],
        wtril=jnp.tril(w_s).astype(BF16),
        bsb=jnp.broadcast_to(b_s[..., None], (d, H_A, CHUNK, LANES)),
        ws0=jnp.repeat(w_s[:, :, 0, 0], P_A, axis=1)[:, None], bs0=jnp.repeat(b_s[:, :, 0], P_A, axis=1)[:, None],
        bdb=bdb.astype(BF16), bdc=bdc.astype(BF16),
        lr=lb_re.reshape(d, 2, 1, HALF_STATE), li=lb_im.reshape(d, 2, 1, HALF_STATE),
        dskip=d_skip.reshape(d, 1, W_B), glu=bd(w_glu).astype(BF16),
        bglu=b_glu.reshape(d, 1, W_B), ln1g=ln1_g[:, None], ln1b=ln1_b[:, None])


def _state_to_cols(h_re, h_im):
    d, b = h_re.shape[:2]
    re = h_re.reshape(d, b, 2, HALF_STATE)
    im = h_im.reshape(d, b, 2, HALF_STATE)
    return jnp.concatenate([re, im], axis=3).transpose(0, 2, 1, 3)


def _cols_to_state(h):
    d, _, b, _ = h.shape
    re = h[..., :HALF_STATE].transpose(0, 2, 1, 3).reshape(d, b, G_B, N_STATE)
    im = h[..., HALF_STATE:].transpose(0, 2, 1, 3).reshape(d, b, G_B, N_STATE)
    return re, im


def kernel(x_prompt, x_sample, state_ssm_re, state_ssm_im, w_in, w_out, ln_v_g, ln_v_b, w_s, b_s, ssm_a_re, ssm_a_im, ssm_log_dt, ssm_b_re, ssm_b_im, ssm_c_re, ssm_c_im, ssm_d, w_glu, b_glu, ln1_g, ln1_b, ln2_g, ln2_b, w_router, router_bias, w_gate, w_up, w_down):
    depth = w_in.shape[0]
    alpha = float((2 * depth) ** 0.25)
    nb, seq, _ = x_prompt.shape
    ns = x_sample.shape[0]
    tokens = nb * seq
    nt = tokens // TILE_M + N_CLASSES
    rb = router_bias[None]
    wrt = w_router.T
    wrp = jnp.pad(w_router, ((0, 0), (0, LANES - N_EXPERTS)))
    rbcol = router_bias[:, None]
    tok = np.arange(nb * CHUNK)
    tri = jnp.asarray(tok[:, None] < tok[None, :], BF16)
    lw = _prep_all(w_in, w_out, ln_v_g, ln_v_b, w_s, b_s, ssm_a_re, ssm_a_im, ssm_log_dt,
                   ssm_b_re, ssm_b_im, ssm_c_re, ssm_c_im, ssm_d, w_glu, b_glu, ln1_g, ln1_b)
    shared = (wrt, rbcol, tri)
    ln2g, ln2b = ln2_g[:, None], ln2_b[:, None]
    h0s = _state_to_cols(state_ssm_re, state_ssm_im)
    xp = x_prompt
    pos = None
    xs = x_sample.reshape(ns, D_MODEL)
    pr_h, sm_h, sm_v = [], [], []
    for l in range(depth):
        x1t, hfin, cls, rank, counts = _mixer_prompt(xp, lw, shared, l, alpha, nb, seq, pos)
        pos, zstart, zlen, tail, tidx, tcls, ea, eb, nused = _plan(cls, rank, counts, nt)
        x_sorted = _dispatch(x1t, pos, zstart, zlen, tail, nb * CHUNK, nt * TILE_M)
        xp = _pair_experts(x_sorted, l, tidx, tcls, ea, eb, nused, wrp, w_gate, w_up, w_down,
                           ln2g, ln2b, alpha)
        pr_h.append(hfin)

        x1s, hnew, v_new = _mixer_sample(xs, h0s, lw, l, alpha)
        xs = _moe_dense(x1s, l, w_router, rb, w_gate, w_up, w_down, ln2g, ln2b, alpha, tm=ns)
        sm_h.append(hnew)
        sm_v.append(v_new.reshape(ns, 1, W_A))
    y_prompt = _ungather(xp, pos, nb, seq)
    pr_re, pr_im = _cols_to_state(jnp.stack(pr_h))
    sm_re, sm_im = _cols_to_state(jnp.stack(sm_h))
    return (y_prompt, xs.reshape(ns, 1, D_MODEL), pr_re, pr_im, sm_re, sm_im, jnp.stack(sm_v))
```

```python
import functools

import jax
import jax.numpy as jnp
import numpy as np
from jax import lax
from jax.experimental import pallas as pl
from jax.experimental.pallas import tpu as pltpu

D_MODEL = 1024
W_A = 512
W_B = 512
CHUNK = 128
H_A = 4
P_A = W_A // H_A
GROUP_B = 16
G_B = W_B // GROUP_B
N_STATE = 64
N_EXPERTS = 16
N_EXPERT_GROUPS = 4
EXPERTS_PER_GROUP = N_EXPERTS // N_EXPERT_GROUPS
D_FF_EXPERT = D_MODEL // 4
LN_EPS = 1e-5

LANES = 128
SUBLANES = 8
HALF_GROUPS = 16
HALF_W = HALF_GROUPS * GROUP_B
HALF_STATE = HALF_GROUPS * N_STATE
VMEM_LIMIT = 56 * 1024 * 1024

PAIRS = ((0, 1), (0, 2), (0, 3), (1, 2), (1, 3), (2, 3))
N_CLASSES = N_EXPERT_GROUPS * len(PAIRS)
CLASS_ROWS = 32
TILE_M = 256
PAIR_SUBTILES = 1
TOK_ROWS = D_MODEL // LANES
ZERO_TOKENS = TILE_M // 2
DMA_UNROLL = 8
S5_FOLD = 8
S5_PAIRS = G_B // 2
EA_TABLE = np.array([EXPERTS_PER_GROUP * g + a for g in range(N_EXPERT_GROUPS) for a, _ in PAIRS], np.int32)
EB_TABLE = np.array([EXPERTS_PER_GROUP * g + b for g in range(N_EXPERT_GROUPS) for _, b in PAIRS], np.int32)

F32 = jnp.float32
BF16 = jnp.bfloat16
I32 = jnp.int32
HIGHEST = lax.Precision.HIGHEST


def _layer_norm(x, g, b):
    mu = jnp.mean(x, axis=-1, keepdims=True)
    xc = x - mu
    var = jnp.mean(xc * xc, axis=-1, keepdims=True)
    return xc * lax.rsqrt(var + LN_EPS) * g + b


def _dot(a, b):
    return jnp.dot(a, b, preferred_element_type=F32)


def _route_classes(x1, wrt, rbcol):
    logits_t = lax.dot_general(wrt, x1, (((1,), (1,)), ((), ())),
                               preferred_element_type=F32, precision=HIGHEST)
    biased = jax.nn.sigmoid(logits_t) + rbcol
    rows = [biased[e:e + 1, :] for e in range(N_EXPERTS)]
    n = EXPERTS_PER_GROUP

    best = sel = None
    for g in range(N_EXPERT_GROUPS):
        v = rows[n * g:n * (g + 1)]
        gs = None
        for a, b in PAIRS:
            s = v[a] + v[b]
            gs = s if gs is None else jnp.maximum(gs, s)
        if g == 0:
            best, sel = gs, jnp.zeros(gs.shape, I32)
        else:
            upd = gs > best
            sel = jnp.where(upd, g, sel)
            best = jnp.where(upd, gs, best)

    cls = jnp.zeros(sel.shape, I32)
    for g in range(N_EXPERT_GROUPS):
        v = rows[n * g:n * (g + 1)]
        lo = jnp.full(sel.shape, n, I32)
        hi = jnp.full(sel.shape, -1, I32)
        for i in range(n):
            before = jnp.zeros(sel.shape, I32)
            for j in range(n):
                if j < i:
                    before = before + (v[j] >= v[i]).astype(I32)
                elif j > i:
                    before = before + (v[j] > v[i]).astype(I32)
            member = before < 2
            lo = jnp.where(member, jnp.minimum(lo, i), lo)
            hi = jnp.where(member, jnp.maximum(hi, i), hi)
        base = jnp.where(lo == 0, 0, jnp.where(lo == 1, 3, 5))
        cls = jnp.where(sel == g, g * len(PAIRS) + base + (hi - lo - 1), cls)
    return cls


def _to_token_tiles(ref, row0, x):
    n = x.shape[0]
    for c in range(TOK_ROWS):
        ref[pl.ds(row0 * TOK_ROWS + c, n, stride=TOK_ROWS), :] = x[:, c * LANES:(c + 1) * LANES]


def _from_token_tiles(ref, row0, n):
    return jnp.concatenate(
        [ref[pl.ds(row0 * TOK_ROWS + c, n, stride=TOK_ROWS), :] for c in range(TOK_ROWS)], axis=1)


def _gathered_tokens(step, nsteps, pos8_ref, src_hbm, bufs, sems, rows, consume):
    def start(s, base):
        def body(g, c):
            for i in range(DMA_UNROLL):
                r = g * DMA_UNROLL + i
                p8 = pl.multiple_of(pos8_ref[base + r], TOK_ROWS)
                pltpu.make_async_copy(src_hbm.at[pl.ds(p8, TOK_ROWS), :],
                                      bufs[s].at[pl.ds(pl.multiple_of(r * TOK_ROWS, TOK_ROWS), TOK_ROWS), :],
                                      sems.at[s]).start(priority=i % 2)
            return c
        lax.fori_loop(0, rows // DMA_UNROLL, body, 0)

    @pl.when(step == 0)
    def _():
        start(0, 0)

    for s in range(2):
        @pl.when(lax.rem(step, 2) == s)
        def _(s=s):
            pltpu.make_async_copy(src_hbm.at[pl.ds(0, rows * TOK_ROWS), :], bufs[s], sems.at[s]).wait()

            @pl.when(step + 1 < nsteps)
            def _():
                start(1 - s, (step + 1) * rows)
            consume(bufs[s])


def _mixer_kernel(alpha, nb, gather_in, *refs):
    if gather_in:
        pos8_ref, zs_hbm = refs[:2]
        refs = refs[2:]
    else:
        x_ref = refs[0]
        refs = refs[1:]
    (win_ref, wout_ref, lnvg_ref, lnvb_ref, wtril_ref, bsb_ref, m1_ref, m2_ref, lr8_ref, li8_ref,
     dskip_ref, glu_ref, bglu_ref, ln1g_ref, ln1b_ref, wrt_ref, rbcol_ref, tri_ref,
     x1t_ref, hfin_ref, cls_ref, rank_ref, cnt_ref,
     xb_ref, xs_slab, xs_scb, xst_ref, ht_ref, yt_ref, mix_ref, hstate_ref, carry_ref) = refs[:32]
    lt = CHUNK
    rows = nb * lt
    pitch = lt + SUBLANES
    step = pl.program_id(0)
    nsteps = pl.num_programs(0)

    @pl.when(step == 0)
    def _():
        hstate_ref[...] = jnp.zeros_like(hstate_ref)
        carry_ref[...] = jnp.zeros_like(carry_ref)

    if gather_in:
        xin_ref, xbuf0, xbuf1, gsem = refs[32:36]

        def consume(buf):
            xin_ref[...] = _from_token_tiles(buf, 0, rows)
        _gathered_tokens(step, nsteps, pos8_ref, zs_hbm, (xbuf0, xbuf1), gsem, rows, consume)

        def load_x():
            return xin_ref[...]
    else:
        def load_x():
            return x_ref[...].reshape(rows, D_MODEL)

    xb_ref[...] = load_x().astype(BF16)

    fold = S5_FOLD
    nchunk = lt // fold
    cb = nchunk * nb
    xs = _dot(xb_ref[...], win_ref[:, 2 * W_A:])
    for j in range(W_B // LANES):
        for b in range(nb):
            xs_slab[j, b * pitch:b * pitch + lt, :] = xs[b * lt:(b + 1) * lt, j * LANES:(j + 1) * LANES]

    def to_scb(t, c):
        s_, ch_ = lax.bitwise_and(t, fold - 1), lax.shift_right_logical(t, fold.bit_length() - 1)
        r0 = pl.multiple_of(s_ * cb + ch_ * nb, SUBLANES)
        for j in range(W_B // LANES):
            xs_scb[pl.ds(r0, nb), j * LANES:(j + 1) * LANES] = xs_slab[j, pl.ds(t, nb, stride=pitch), :]
        return c
    lax.fori_loop(0, lt, to_scb, 0, unroll=4)
    xst_ref[...] = xs_scb[...].T.astype(BF16)

    def chunk_inputs(pr):
        return jnp.concatenate(
            [xst_ref[(2 * pr + gi) * GROUP_B:(2 * pr + gi + 1) * GROUP_B, s * cb:(s + 1) * cb]
             for gi in range(2) for s in range(fold)], axis=0)

    pw = 2 * 2 * N_STATE
    for pr in range(S5_PAIRS):
        ht_ref[:, pr * pw:(pr + 1) * pw] = _dot(m1_ref[pr], chunk_inputs(pr)).T

    half_pairs = S5_PAIRS // 2
    for half in range(2):
        c0 = half * half_pairs * pw
        lr8 = [jnp.broadcast_to(lr8_ref[:, (half * half_pairs + p) * LANES:(half * half_pairs + p + 1) * LANES],
                                (nb, LANES)) for p in range(half_pairs)]
        li8 = [jnp.broadcast_to(li8_ref[:, (half * half_pairs + p) * LANES:(half * half_pairs + p + 1) * LANES],
                                (nb, LANES)) for p in range(half_pairs)]
        h = [hstate_ref[:, c0 + q * LANES:c0 + (q + 1) * LANES] for q in range(2 * half_pairs)]
        for c in range(nchunk):
            rs = slice(c * nb, (c + 1) * nb)
            for p in range(half_pairs):
                re_sl = slice(c0 + p * pw, c0 + p * pw + LANES)
                im_sl = slice(c0 + p * pw + LANES, c0 + (p + 1) * pw)
                hr, hi = h[2 * p], h[2 * p + 1]
                ur, ui = ht_ref[rs, re_sl], ht_ref[rs, im_sl]
                ht_ref[rs, re_sl] = hr
                ht_ref[rs, im_sl] = hi
                h[2 * p] = lr8[p] * hr - li8[p] * hi + ur
                h[2 * p + 1] = lr8[p] * hi + li8[p] * hr + ui
        for q in range(2 * half_pairs):
            hstate_ref[:, c0 + q * LANES:c0 + (q + 1) * LANES] = h[q]
    hfin_ref[...] = hstate_ref[...]

    for pr in range(S5_PAIRS):
        rhs = jnp.concatenate([ht_ref[:, pr * pw:(pr + 1) * pw].T.astype(BF16), chunk_inputs(pr)], axis=0)
        yt = _dot(m2_ref[pr], rhs)
        for gi in range(2):
            for s in range(fold):
                r0 = (gi * fold + s) * GROUP_B
                yt_ref[(2 * pr + gi) * GROUP_B:(2 * pr + gi + 1) * GROUP_B, s * cb:(s + 1) * cb] = (
                    yt[r0:r0 + GROUP_B, :])

    y = jax.nn.gelu(yt_ref[...].T + dskip_ref[...] * xs_scb[...])
    yb = y.astype(BF16)
    spitch = cb + SUBLANES
    for k in range(2):
        sl = slice(k * HALF_W, (k + 1) * HALF_W)
        gl = _dot(yb[:, sl], glu_ref[k]) + bglu_ref[:, sl]
        z = y[:, sl] * jax.nn.sigmoid(gl)
        for j in range(HALF_W // LANES):
            for s in range(fold):
                xs_slab[k * (HALF_W // LANES) + j, s * spitch:s * spitch + cb, :] = (
                    z[s * cb:(s + 1) * cb, j * LANES:(j + 1) * LANES])

    def to_bt(ch, c):
        for b in range(nb):
            dst = pl.multiple_of(b * lt + ch * fold, SUBLANES)
            for j in range(W_B // LANES):
                mix_ref[pl.ds(dst, fold), W_A + j * LANES:W_A + (j + 1) * LANES] = (
                    xs_slab[j, pl.ds(ch * nb + b, fold, stride=spitch), :])
        return c
    lax.fori_loop(0, nchunk, to_bt, 0)

    vg = jax.nn.gelu(_dot(xb_ref[...], win_ref[:, W_A:2 * W_A]))
    v = _layer_norm(vg, lnvg_ref[...], lnvb_ref[...]).astype(BF16)
    u = jax.nn.gelu(_dot(xb_ref[...], win_ref[:, :W_A]))
    for h in range(H_A):
        hs = slice(h * P_A, (h + 1) * P_A)
        vcat = jnp.concatenate([v[b * lt:(b + 1) * lt, hs] for b in range(nb)], axis=1)
        o = _dot(wtril_ref[h], vcat)
        for b in range(nb):
            rs = slice(b * lt, (b + 1) * lt)
            mix_ref[rs, hs] = u[rs, hs] * (o[:, b * LANES:(b + 1) * LANES] + bsb_ref[h])

    mix = _dot(mix_ref[...].astype(BF16), wout_ref[...])
    x1 = _layer_norm(alpha * load_x() + mix, ln1g_ref[...], ln1b_ref[...])
    _to_token_tiles(x1t_ref, 0, x1)

    cls = _route_classes(x1, wrt_ref[...], rbcol_ref[...])
    crow = lax.broadcasted_iota(I32, (CLASS_ROWS, rows), 0)
    onehot = jnp.where(crow == cls, 1.0, 0.0)
    prefix = _dot(onehot.astype(BF16), tri_ref[...])
    carry = carry_ref[:, 0:1]
    rank = jnp.sum(onehot * (prefix + carry), axis=0, keepdims=True)
    cls_ref[...] = cls.reshape(1, 1, rows)
    rank_ref[...] = rank.astype(I32).reshape(1, 1, rows)
    carry_ref[...] = carry_ref[...] + jnp.sum(onehot, axis=1, keepdims=True)
    cnt_ref[...] = carry_ref[...]


def _const_spec(shape):
    nd = len(shape)
    return pl.BlockSpec(shape, lambda *_: (0,) * nd, pipeline_mode=pl.Buffered(1))


def _layer_spec(shape, l):
    nd = len(shape)
    return pl.BlockSpec((None,) + tuple(shape[1:]), lambda *_: (l,) + (0,) * (nd - 1),
                        pipeline_mode=pl.Buffered(1))


MIXER_WEIGHTS = ("win", "wout", "lnvg", "lnvb", "wtril", "bsb", "m1", "m2", "lr8", "li8", "dskip", "glu",
                 "bglu", "ln1g", "ln1b")
SAMPLE_WEIGHTS = ("win", "wout", "lnvg", "lnvb", "ws0", "bs0", "bdb", "bdc", "lr", "li", "dskip", "glu",
                  "bglu", "ln1g", "ln1b")


def _mixer_prompt(x, lw, shared, l, alpha, nb, seq, pos_prev=None):
    lt = CHUNK
    rows = nb * lt
    nsteps = seq // lt
    gather_in = pos_prev is not None
    weights = tuple(lw[k] for k in MIXER_WEIGHTS) + tuple(shared)
    wspecs = [_layer_spec(lw[k].shape, l) for k in MIXER_WEIGHTS] + [_const_spec(w.shape) for w in shared]
    if gather_in:
        x_spec = pl.BlockSpec(memory_space=pl.ANY)
    else:
        x_spec = pl.BlockSpec((nb, lt, D_MODEL), lambda i, *_: (0, i, 0))
    scratch = [
        pltpu.VMEM((rows, D_MODEL), BF16),
        pltpu.VMEM((W_B // LANES, nb * (lt + SUBLANES), LANES), F32),
        pltpu.VMEM((rows, W_B), F32),
        pltpu.VMEM((W_B, rows), BF16),
        pltpu.VMEM((rows // S5_FOLD, 2 * G_B * N_STATE), F32),
        pltpu.VMEM((W_B, rows), F32),
        pltpu.VMEM((rows, D_MODEL), F32),
        pltpu.VMEM((nb, 2 * G_B * N_STATE), F32),
        pltpu.VMEM((CLASS_ROWS, LANES), F32),
    ]
    if gather_in:
        scratch += [pltpu.VMEM((rows, D_MODEL), F32),
                    pltpu.VMEM((rows * TOK_ROWS, LANES), F32), pltpu.VMEM((rows * TOK_ROWS, LANES), F32),
                    pltpu.SemaphoreType.DMA((2,))]
    grid_spec = pltpu.PrefetchScalarGridSpec(
        num_scalar_prefetch=1 if gather_in else 0,
        grid=(nsteps,),
        in_specs=[x_spec] + wspecs,
        out_specs=[pl.BlockSpec((rows * TOK_ROWS, LANES), lambda i, *_: (i, 0)),
                   pl.BlockSpec((nb, 2 * G_B * N_STATE), lambda i, *_: (0, 0)),
                   pl.BlockSpec((1, 1, rows), lambda i, *_: (i, 0, 0)),
                   pl.BlockSpec((1, 1, rows), lambda i, *_: (i, 0, 0)),
                   pl.BlockSpec((CLASS_ROWS, LANES), lambda i, *_: (0, 0))],
        scratch_shapes=scratch)
    args = ((pos_prev, x) if gather_in else (x,)) + weights
    return pl.pallas_call(
        functools.partial(_mixer_kernel, alpha, nb, gather_in),
        grid_spec=grid_spec,
        out_shape=[jax.ShapeDtypeStruct((nb * seq * TOK_ROWS, LANES), F32),
                   jax.ShapeDtypeStruct((nb, 2 * G_B * N_STATE), F32),
                   jax.ShapeDtypeStruct((nsteps, 1, rows), I32),
                   jax.ShapeDtypeStruct((nsteps, 1, rows), I32),
                   jax.ShapeDtypeStruct((CLASS_ROWS, LANES), F32)],
        compiler_params=pltpu.CompilerParams(dimension_semantics=("arbitrary",),
                                             vmem_limit_bytes=VMEM_LIMIT),
    )(*args)


def _dispatch_kernel(rows, pos8_ref, zstart_ref, zlen_ref, tail_ref, x_ref, xs_hbm, zero_ref, sem, zsem):
    step = pl.program_id(0)
    ztok = ZERO_TOKENS

    @pl.when(step == 0)
    def _():
        zero_ref[...] = jnp.zeros_like(zero_ref)
        pieces = []
        for c in range(N_CLASSES):
            start = zstart_ref[c]
            zlen = zlen_ref[c]
            p = TILE_M // 2
            while p >= 1:
                hit = (zlen & p) != 0
                pieces.append((hit, pltpu.make_async_copy(
                    zero_ref.at[pl.ds(0, p * TOK_ROWS), :],
                    xs_hbm.at[pl.ds(pl.multiple_of(start * TOK_ROWS, TOK_ROWS), p * TOK_ROWS), :], zsem)))
                start = start + jnp.where(hit, p, 0)
                p //= 2
        for hit, cp in pieces:
            pl.when(hit)(cp.start)
        for hit, cp in pieces:
            pl.when(hit)(cp.wait)

        zrows = ztok * TOK_ROWS
        first = tail_ref[0] // ztok

        def tail_copy(q):
            return pltpu.make_async_copy(
                zero_ref, xs_hbm.at[pl.ds(pl.multiple_of(q * zrows, zrows), zrows), :], zsem)

        def tail_start(q, c):
            tail_copy(q).start()
            return c

        def tail_wait(q, c):
            tail_copy(q).wait()
            return c
        lax.fori_loop(first, xs_hbm.shape[0] // zrows, tail_start, 0)
        lax.fori_loop(first, xs_hbm.shape[0] // zrows, tail_wait, 0)

    base = step * rows

    def body(g, c):
        for i in range(DMA_UNROLL):
            r = g * DMA_UNROLL + i
            p8 = pl.multiple_of(pos8_ref[base + r], TOK_ROWS)
            pltpu.make_async_copy(x_ref.at[pl.ds(pl.multiple_of(r * TOK_ROWS, TOK_ROWS), TOK_ROWS), :],
                                  xs_hbm.at[pl.ds(p8, TOK_ROWS), :], sem).start(priority=i % 2)
        return c
    lax.fori_loop(0, rows // DMA_UNROLL, body, 0)
    pltpu.make_async_copy(x_ref, xs_hbm.at[pl.ds(0, rows * TOK_ROWS), :], sem).wait()


def _dispatch(x1t, pos8, zstart, zlen, tail, rows, ns_tokens):
    return pl.pallas_call(
        functools.partial(_dispatch_kernel, rows),
        grid_spec=pltpu.PrefetchScalarGridSpec(
            num_scalar_prefetch=4,
            grid=(x1t.shape[0] // (rows * TOK_ROWS),),
            in_specs=[pl.BlockSpec((rows * TOK_ROWS, LANES), lambda i, *_: (i, 0))],
            out_specs=pl.BlockSpec(memory_space=pl.ANY),
            scratch_shapes=[pltpu.VMEM((ZERO_TOKENS * TOK_ROWS, LANES), F32),
                            pltpu.SemaphoreType.DMA(()), pltpu.SemaphoreType.DMA(())]),
        out_shape=jax.ShapeDtypeStruct((ns_tokens * TOK_ROWS, LANES), F32),
        compiler_params=pltpu.CompilerParams(dimension_semantics=("arbitrary",),
                                             vmem_limit_bytes=VMEM_LIMIT),
    )(pos8, zstart, zlen, tail, x1t)


def _pair_kernel(alpha, tidx_ref, tcls_ref, ea_ref, eb_ref, nused_ref,
                 x_ref, wrp_ref, wga_ref, wua_ref, wda_ref, wgb_ref, wub_ref, wdb_ref,
                 ln2g_ref, ln2b_ref, z_ref, w1_ref, w2_ref, xprev_ref, moe_ref):
    j = pl.program_id(0)
    f = D_FF_EXPERT
    nused = nused_ref[0]

    @pl.when(j == 0)
    def _():
        xprev_ref[...] = jnp.zeros_like(xprev_ref)
        moe_ref[...] = jnp.zeros_like(moe_ref)

    @pl.when(j > nused)
    def _():
        z_ref[...] = jnp.zeros_like(z_ref)

    @pl.when(j <= nused)
    def _():
        changed = jnp.logical_or(j == 0, tcls_ref[j] != tcls_ref[jnp.maximum(j - 1, 0)])

        @pl.when(changed)
        def _():
            w1_ref[:, 0 * f:1 * f] = wga_ref[...].astype(BF16)
            w1_ref[:, 1 * f:2 * f] = wua_ref[...].astype(BF16)
            w1_ref[:, 2 * f:3 * f] = wgb_ref[...].astype(BF16)
            w1_ref[:, 3 * f:4 * f] = wub_ref[...].astype(BF16)
            w1_ref[:, 4 * f:] = wrp_ref[...].astype(BF16)
            w2_ref[0:f, :] = wda_ref[...].astype(BF16)
            w2_ref[f:2 * f, :] = wdb_ref[...].astype(BF16)

        _to_token_tiles(z_ref, 0, _layer_norm(alpha * xprev_ref[...] + moe_ref[...], ln2g_ref[...], ln2b_ref[...]))

        x = _from_token_tiles(x_ref, 0, TILE_M)
        gu = _dot(x.astype(BF16), w1_ref[...])
        scores = jax.nn.sigmoid(gu[:, 4 * f:])
        lane = lax.broadcasted_iota(I32, scores.shape, 1)
        sa = jnp.sum(jnp.where(lane == ea_ref[j], scores, 0.0), axis=-1, keepdims=True)
        sb = jnp.sum(jnp.where(lane == eb_ref[j], scores, 0.0), axis=-1, keepdims=True)
        tot = sa + sb
        ha = jax.nn.silu(gu[:, 0 * f:1 * f]) * gu[:, 1 * f:2 * f] * (sa / tot)
        hb = jax.nn.silu(gu[:, 2 * f:3 * f]) * gu[:, 3 * f:4 * f] * (sb / tot)
        xprev_ref[...] = x
        moe_ref[...] = _dot(jnp.concatenate([ha, hb], axis=1).astype(BF16), w2_ref[...])


def _pair_experts(xs, l, tidx, tcls, ea, eb, nused, wrp, w_gate, w_up, w_down, ln2g, ln2b, alpha):
    nsteps = tidx.shape[0]

    def wspec(shape, table):
        return pl.BlockSpec((None, None) + shape, lambda j, ti, tc, a, b, nu: (l, (a, b)[table][j], 0, 0))
    up = (D_MODEL, D_FF_EXPERT)
    dn = (D_FF_EXPERT, D_MODEL)
    cst = lambda j, *_: (0, 0)
    lsel = lambda j, *_: (l, 0, 0)
    return pl.pallas_call(
        functools.partial(_pair_kernel, alpha),
        grid_spec=pltpu.PrefetchScalarGridSpec(
            num_scalar_prefetch=5,
            grid=(nsteps,),
            in_specs=[pl.BlockSpec((TILE_M * TOK_ROWS, LANES), lambda j, ti, *_: (ti[j], 0)),
                      pl.BlockSpec(wrp.shape, cst),
                      wspec(up, 0), wspec(up, 0), wspec(dn, 0), wspec(up, 1), wspec(up, 1), wspec(dn, 1),
                      pl.BlockSpec((None,) + ln2g.shape[1:], lsel), pl.BlockSpec((None,) + ln2b.shape[1:], lsel)],
            out_specs=pl.BlockSpec((TILE_M * TOK_ROWS, LANES), lambda j, *_: (jnp.maximum(j - 1, 0), 0)),
            scratch_shapes=[pltpu.VMEM((D_MODEL, 4 * D_FF_EXPERT + LANES), BF16),
                            pltpu.VMEM((2 * D_FF_EXPERT, D_MODEL), BF16),
                            pltpu.VMEM((TILE_M, D_MODEL), F32), pltpu.VMEM((TILE_M, D_MODEL), F32)]),
        out_shape=jax.ShapeDtypeStruct(xs.shape, F32),
        compiler_params=pltpu.CompilerParams(dimension_semantics=("arbitrary",),
                                             vmem_limit_bytes=VMEM_LIMIT),
    )(tidx, tcls, ea, eb, nused, xs, wrp, w_gate, w_up, w_down, w_gate, w_up, w_down, ln2g, ln2b)


def _ungather_kernel(nb, pos8_ref, zs_hbm, out_ref, buf0, buf1, sems):
    rows = nb * CHUNK

    def consume(buf):
        out_ref[...] = _from_token_tiles(buf, 0, rows).reshape(nb, CHUNK, D_MODEL)
    _gathered_tokens(pl.program_id(0), pl.num_programs(0), pos8_ref, zs_hbm, (buf0, buf1), sems, rows, consume)


def _ungather(zs, pos8, nb, seq):
    rows = nb * CHUNK
    return pl.pallas_call(
        functools.partial(_ungather_kernel, nb),
        grid_spec=pltpu.PrefetchScalarGridSpec(
            num_scalar_prefetch=1,
            grid=(seq // CHUNK,),
            in_specs=[pl.BlockSpec(memory_space=pl.ANY)],
            out_specs=pl.BlockSpec((nb, CHUNK, D_MODEL), lambda i, *_: (0, i, 0)),
            scratch_shapes=[pltpu.VMEM((rows * TOK_ROWS, LANES), F32), pltpu.VMEM((rows * TOK_ROWS, LANES), F32),
                            pltpu.SemaphoreType.DMA((2,))]),
        out_shape=jax.ShapeDtypeStruct((nb, seq, D_MODEL), F32),
        compiler_params=pltpu.CompilerParams(dimension_semantics=("arbitrary",),
                                             vmem_limit_bytes=VMEM_LIMIT),
    )(pos8, zs)


def _plan(cls, rank, counts, nt):
    cnt = counts[:N_CLASSES, 0].astype(I32)
    ntile = (cnt + TILE_M - 1) // TILE_M
    padded = ntile * TILE_M
    off = jnp.cumsum(padded) - padded
    classes = jnp.arange(N_CLASSES, dtype=I32)
    pos8 = (rank.reshape(-1) + jnp.sum(jnp.where(cls.reshape(-1, 1) == classes, off, 0), axis=1)) * TOK_ROWS
    tile_end = jnp.cumsum(ntile)
    nused = tile_end[-1:].astype(I32)
    tidx = jnp.minimum(jnp.arange(nt + 1, dtype=I32), nused - 1)
    tsel = tile_end[None, :] <= tidx[:, None]
    tcls = jnp.sum(tsel.astype(I32), axis=1)
    onehot = tcls[:, None] == classes
    ea = jnp.sum(jnp.where(onehot, jnp.asarray(EA_TABLE), 0), axis=1)
    eb = jnp.sum(jnp.where(onehot, jnp.asarray(EB_TABLE), 0), axis=1)
    return pos8.astype(I32), off + cnt, padded - cnt, nused * TILE_M, tidx, tcls, ea, eb, nused


def _mixer_sample_kernel(alpha, x_ref, h0_ref, win_ref, wout_ref, lnvg_ref, lnvb_ref, ws0_ref, bs0_ref,
                         bdb_ref, bdc_ref, lr_ref, li_ref, dskip_ref, glu_ref, bglu_ref,
                         ln1g_ref, ln1b_ref,
                         x1_ref, hnew_ref, v_ref):
    x = x_ref[...]
    proj = _dot(x.astype(BF16), win_ref[...])
    u = jax.nn.gelu(proj[:, :W_A])
    v = _layer_norm(jax.nn.gelu(proj[:, W_A:2 * W_A]), lnvg_ref[...], lnvb_ref[...])
    v_ref[...] = v
    y_a = u * (ws0_ref[...] * v + bs0_ref[...])
    xs = proj[:, 2 * W_A:]
    zs = []
    for k in range(2):
        sl = slice(k * HALF_W, (k + 1) * HALF_W)
        bu = _dot(xs[:, sl].astype(BF16), bdb_ref[k])
        h0r = h0_ref[k, :, :HALF_STATE]
        h0i = h0_ref[k, :, HALF_STATE:]
        lr = lr_ref[k]
        li = li_ref[k]
        hr = lr * h0r - li * h0i + bu[:, :HALF_STATE]
        hi = lr * h0i + li * h0r + bu[:, HALF_STATE:]
        hnew_ref[k, :, :HALF_STATE] = hr
        hnew_ref[k, :, HALF_STATE:] = hi
        hcat = jnp.concatenate([hr, hi], axis=1).astype(BF16)
        y = jax.nn.gelu(_dot(hcat, bdc_ref[k]) + dskip_ref[:, sl] * xs[:, sl])
        gl = _dot(y.astype(BF16), glu_ref[k]) + bglu_ref[:, sl]
        zs.append(y * jax.nn.sigmoid(gl))
    cat = jnp.concatenate([y_a] + zs, axis=1).astype(BF16)
    mix = _dot(cat, wout_ref[...])
    x1_ref[...] = _layer_norm(alpha * x + mix, ln1g_ref[...], ln1b_ref[...])


def _mixer_sample(x, h0, lw, l, alpha):
    n = x.shape[0]
    full = lambda shape: pl.BlockSpec(shape, lambda i: (0,) * len(shape))
    return pl.pallas_call(
        functools.partial(_mixer_sample_kernel, alpha),
        grid=(1,),
        in_specs=[full(x.shape), _layer_spec(h0.shape, l)] + [_layer_spec(lw[k].shape, l) for k in SAMPLE_WEIGHTS],
        out_specs=[full((n, D_MODEL)), full((2, n, 2 * HALF_STATE)), full((n, W_A))],
        out_shape=[jax.ShapeDtypeStruct((n, D_MODEL), F32),
                   jax.ShapeDtypeStruct((2, n, 2 * HALF_STATE), F32),
                   jax.ShapeDtypeStruct((n, W_A), F32)],
        compiler_params=pltpu.CompilerParams(dimension_semantics=("arbitrary",), vmem_limit_bytes=VMEM_LIMIT),
    )(x, h0, *[lw[k] for k in SAMPLE_WEIGHTS])


def _route(x, wr, rbias):
    logits = jnp.dot(x, wr, preferred_element_type=F32, precision=HIGHEST)
    scores = jax.nn.sigmoid(logits)
    biased = scores + rbias
    lane = lax.broadcasted_iota(I32, biased.shape, 1)
    grp = lane // EXPERTS_PER_GROUP
    neg = jnp.float32(-jnp.inf)

    def top2(vals):
        m1 = jnp.max(vals, axis=-1, keepdims=True)
        i1 = jnp.min(jnp.where(vals == m1, lane, N_EXPERTS), axis=-1, keepdims=True)
        rest = jnp.where(lane == i1, neg, vals)
        m2 = jnp.max(rest, axis=-1, keepdims=True)
        i2 = jnp.min(jnp.where(rest == m2, lane, N_EXPERTS), axis=-1, keepdims=True)
        return m1, i1, m2, i2

    best = sel = None
    for g in range(N_EXPERT_GROUPS):
        m1, _, m2, _ = top2(jnp.where(grp == g, biased, neg))
        gs = m1 + m2
        if g == 0:
            best, sel = gs, jnp.zeros(gs.shape, I32)
        else:
            upd = gs > best
            sel = jnp.where(upd, g, sel)
            best = jnp.where(upd, gs, best)
    _, i1, _, i2 = top2(jnp.where(grp == sel, biased, neg))
    s1 = jnp.sum(jnp.where(lane == i1, scores, 0.0), axis=-1, keepdims=True)
    s2 = jnp.sum(jnp.where(lane == i2, scores, 0.0), axis=-1, keepdims=True)
    tot = s1 + s2
    return jnp.where(lane == i1, s1 / tot, 0.0) + jnp.where(lane == i2, s2 / tot, 0.0)


def _moe_kernel(alpha, x_ref, wr_ref, rb_ref, wg_ref, wu_ref, wd_ref, ln2g_ref, ln2b_ref,
                out_ref, xb_ref, comb_ref, acc_ref):
    e = pl.program_id(1)

    @pl.when(e == 0)
    def _():
        x = x_ref[...]
        xb_ref[...] = x.astype(BF16)
        comb_ref[...] = _route(x, wr_ref[...], rb_ref[...])
        acc_ref[...] = jnp.zeros_like(acc_ref)

    xb = xb_ref[...]
    g = _dot(xb, wg_ref[...].astype(BF16))
    u = _dot(xb, wu_ref[...].astype(BF16))
    comb = comb_ref[...]
    lane = lax.broadcasted_iota(I32, comb.shape, 1)
    ce = jnp.sum(jnp.where(lane == e, comb, 0.0), axis=-1, keepdims=True)
    h = (jax.nn.silu(g) * u * ce).astype(BF16)
    acc_ref[...] += _dot(h, wd_ref[...].astype(BF16))

    @pl.when(e == N_EXPERTS - 1)
    def _():
        out_ref[...] = _layer_norm(alpha * x_ref[...] + acc_ref[...], ln2g_ref[...], ln2b_ref[...])


def _moe_dense(x, l, wr, rb, w_gate, w_up, w_down, ln2g, ln2b, alpha, tm):
    t = x.shape[0]
    cst = lambda i, e: (0, 0)
    wsel = lambda i, e: (l, e, 0, 0)
    return pl.pallas_call(
        functools.partial(_moe_kernel, alpha),
        grid=(t // tm, N_EXPERTS),
        in_specs=[pl.BlockSpec((tm, D_MODEL), lambda i, e: (i, 0)),
                  pl.BlockSpec(wr.shape, cst), pl.BlockSpec(rb.shape, cst),
                  pl.BlockSpec((None, None, D_MODEL, D_FF_EXPERT), wsel),
                  pl.BlockSpec((None, None, D_MODEL, D_FF_EXPERT), wsel),
                  pl.BlockSpec((None, None, D_FF_EXPERT, D_MODEL), wsel),
                  pl.BlockSpec((None,) + ln2g.shape[1:], lambda i, e: (l, 0, 0)),
                  pl.BlockSpec((None,) + ln2b.shape[1:], lambda i, e: (l, 0, 0))],
        out_specs=pl.BlockSpec((tm, D_MODEL), lambda i, e: (i, 0)),
        out_shape=jax.ShapeDtypeStruct((t, D_MODEL), F32),
        scratch_shapes=[pltpu.VMEM((tm, D_MODEL), BF16),
                        pltpu.VMEM((tm, N_EXPERTS), F32),
                        pltpu.VMEM((tm, D_MODEL), F32)],
        compiler_params=pltpu.CompilerParams(dimension_semantics=("arbitrary", "arbitrary"),
                                             vmem_limit_bytes=VMEM_LIMIT),
    )(x, wr, rb, w_gate, w_up, w_down, ln2g, ln2b)


def _block_diag(blocks):
    eye = jnp.eye(HALF_GROUPS, dtype=blocks.dtype)
    k, g, a, b = blocks.shape
    return jnp.einsum("kgab,gh->kgahb", blocks, eye).reshape(k, g * a, g * b)


def _prep_s5_folded(lb_re, lb_im, bb_re, bb_im, c_re, c_im):
    d, g, n = lb_re.shape
    s = S5_FOLD
    eye2 = jnp.eye(2, dtype=F32)
    pr, pi = [jnp.ones_like(lb_re)], [jnp.zeros_like(lb_re)]
    for _ in range(s):
        pr, pi = pr + [pr[-1] * lb_re - pi[-1] * lb_im], pi + [pr[-1] * lb_im + pi[-1] * lb_re]
    p_re, p_im = jnp.stack(pr), jnp.stack(pi)

    k_re, k_im = p_re[s - 1::-1][..., None], p_im[s - 1::-1][..., None]
    m1 = jnp.stack([k_re * bb_re - k_im * bb_im, k_re * bb_im + k_im * bb_re])
    m1 = m1.transpose(2, 3, 0, 4, 1, 5).reshape(d, g // 2, 2, 2, n, -1).transpose(0, 1, 3, 2, 4, 5)
    m1 = jnp.einsum("dkrgnc,gh->dkrgnhc", m1, eye2).reshape(d, g // 2, 4 * n, -1)

    j_re, j_im = p_re[1:, :, :, None, :], p_im[1:, :, :, None, :]
    hc = jnp.stack([c_re * j_re - c_im * j_im, -(c_re * j_im + c_im * j_re)])
    hc = hc.transpose(2, 3, 1, 4, 0, 5).reshape(d, g // 2, 2, -1, 2, n)
    hpart = jnp.einsum("dkgmrn,gh->dkgmrhn", hc, eye2).reshape(d, g // 2, hc.shape[2] * hc.shape[3], 4 * n)

    t_re, t_im = p_re[:s, :, :, None, :], p_im[:s, :, :, None, :]
    kern = (jnp.einsum("tdgqn,dgnp->tdgqp", c_re * t_re - c_im * t_im, bb_re, precision=HIGHEST)
            - jnp.einsum("tdgqn,dgnp->tdgqp", c_re * t_im + c_im * t_re, bb_im, precision=HIGHEST))
    jj, ss = np.arange(s)[:, None], np.arange(s)[None, :]
    causal = jnp.asarray(jj >= ss, F32)[:, :, None, None, None, None]
    kx = kern[np.clip(jj - ss, 0, s - 1)] * causal
    kx = kx.transpose(2, 3, 0, 4, 1, 5).reshape(d, g // 2, 2, hc.shape[3], -1)
    apart = jnp.einsum("dkgmc,gh->dkgmhc", kx, eye2).reshape(d, g // 2, 2 * kx.shape[3], -1)

    m2 = jnp.concatenate([hpart, apart], axis=-1)
    return (m1.astype(BF16), m2.astype(BF16), p_re[s].reshape(d, 1, g * n), p_im[s].reshape(d, 1, g * n))


def _prep_all(w_in, w_out, ln_v_g, ln_v_b, w_s, b_s, a_re, a_im, log_dt, b_re, b_im, c_re, c_im,
              d_skip, w_glu, b_glu, ln1_g, ln1_b):
    d = w_in.shape[0]
    dt = jnp.exp(log_dt)[..., None]
    decay = jnp.exp(a_re * dt)
    lb_re, lb_im = decay * jnp.cos(a_im * dt), decay * jnp.sin(a_im * dt)
    den = a_re * a_re + a_im * a_im
    nr, ni = lb_re - 1.0, lb_im
    zr = (nr * a_re + ni * a_im) / den
    zi = (ni * a_re - nr * a_im) / den
    bb_re = zr[..., None] * b_re - zi[..., None] * b_im
    bb_im = zr[..., None] * b_im + zi[..., None] * b_re

    def bd(a):
        blocks = _block_diag(a.reshape((d * 2, HALF_GROUPS) + a.shape[2:]))
        return blocks.reshape((d, 2) + blocks.shape[1:])

    bdb = jnp.concatenate([bd(bb_re.transpose(0, 1, 3, 2)), bd(bb_im.transpose(0, 1, 3, 2))], axis=3)
    bdc = jnp.concatenate([bd(c_re.transpose(0, 1, 3, 2)), bd(-c_im.transpose(0, 1, 3, 2))], axis=2)
    m1, m2, lr8, li8 = _prep_s5_folded(lb_re, lb_im, bb_re, bb_im, c_re, c_im)
    return dict(
        m1=m1, m2=m2, lr8=lr8, li8=li8,
        win=w_in.astype(BF16), wout=w_out.astype(BF16),
        lnvg=ln_v_g[:, None], lnvb=ln_v_b[:, None],
        wtril=jnp.tril(w_s).astype(BF16),
        bsb=jnp.broadcast_to(b_s[..., None], (d, H_A, CHUNK, LANES)),
        ws0=jnp.repeat(w_s[:, :, 0, 0], P_A, axis=1)[:, None], bs0=jnp.repeat(b_s[:, :, 0], P_A, axis=1)[:, None],
        bdb=bdb.astype(BF16), bdc=bdc.astype(BF16),
        lr=lb_re.reshape(d, 2, 1, HALF_STATE), li=lb_im.reshape(d, 2, 1, HALF_STATE),
        dskip=d_skip.reshape(d, 1, W_B), glu=bd(w_glu).astype(BF16),
        bglu=b_glu.reshape(d, 1, W_B), ln1g=ln1_g[:, None], ln1b=ln1_b[:, None])


def _state_to_cols(h_re, h_im):
    d, b = h_re.shape[:2]
    re = h_re.reshape(d, b, 2, HALF_STATE)
    im = h_im.reshape(d, b, 2, HALF_STATE)
    return jnp.concatenate([re, im], axis=3).transpose(0, 2, 1, 3)


def _pairs_to_state(h):
    d, b, _ = h.shape
    h = h.reshape(d, b, G_B // 2, 2, 2, N_STATE)
    return h[:, :, :, 0].reshape(d, b, G_B, N_STATE), h[:, :, :, 1].reshape(d, b, G_B, N_STATE)


def _cols_to_state(h):
    d, _, b, _ = h.shape
    re = h[..., :HALF_STATE].transpose(0, 2, 1, 3).reshape(d, b, G_B, N_STATE)
    im = h[..., HALF_STATE:].transpose(0, 2, 1, 3).reshape(d, b, G_B, N_STATE)
    return re, im


def kernel(x_prompt, x_sample, state_ssm_re, state_ssm_im, w_in, w_out, ln_v_g, ln_v_b, w_s, b_s, ssm_a_re, ssm_a_im, ssm_log_dt, ssm_b_re, ssm_b_im, ssm_c_re, ssm_c_im, ssm_d, w_glu, b_glu, ln1_g, ln1_b, ln2_g, ln2_b, w_router, router_bias, w_gate, w_up, w_down):
    depth = w_in.shape[0]
    alpha = float((2 * depth) ** 0.25)
    nb, seq, _ = x_prompt.shape
    ns = x_sample.shape[0]
    tokens = nb * seq
    nt = tokens // TILE_M + N_CLASSES
    rb = router_bias[None]
    wrt = w_router.T
    wrp = jnp.pad(w_router, ((0, 0), (0, LANES - N_EXPERTS)))
    rbcol = router_bias[:, None]
    tok = np.arange(nb * CHUNK)
    tri = jnp.asarray(tok[:, None] < tok[None, :], BF16)
    lw = _prep_all(w_in, w_out, ln_v_g, ln_v_b, w_s, b_s, ssm_a_re, ssm_a_im, ssm_log_dt,
                   ssm_b_re, ssm_b_im, ssm_c_re, ssm_c_im, ssm_d, w_glu, b_glu, ln1_g, ln1_b)
    shared = (wrt, rbcol, tri)
    ln2g, ln2b = ln2_g[:, None], ln2_b[:, None]
    h0s = _state_to_cols(state_ssm_re, state_ssm_im)
    xp = x_prompt
    pos = None
    xs = x_sample.reshape(ns, D_MODEL)
    pr_h, sm_h, sm_v = [], [], []
    for l in range(depth):
        x1t, hfin, cls, rank, counts = _mixer_prompt(xp, lw, shared, l, alpha, nb, seq, pos)
        pos, zstart, zlen, tail, tidx, tcls, ea, eb, nused = _plan(cls, rank, counts, nt)
        x_sorted = _dispatch(x1t, pos, zstart, zlen, tail, nb * CHUNK, nt * TILE_M)
        xp = _pair_experts(x_sorted, l, tidx, tcls, ea, eb, nused, wrp, w_gate, w_up, w_down,
                           ln2g, ln2b, alpha)
        pr_h.append(hfin)

        x1s, hnew, v_new = _mixer_sample(xs, h0s, lw, l, alpha)
        xs = _moe_dense(x1s, l, w_router, rb, w_gate, w_up, w_down, ln2g, ln2b, alpha, tm=ns)
        sm_h.append(hnew)
        sm_v.append(v_new.reshape(ns, 1, W_A))
    y_prompt = _ungather(xp, pos, nb, seq)
    pr_re, pr_im = _pairs_to_state(jnp.stack(pr_h))
    sm_re, sm_im = _cols_to_state(jnp.stack(sm_h))
    return (y_prompt, xs.reshape(ns, 1, D_MODEL), pr_re, pr_im, sm_re, sm_im, jnp.stack(sm_v))
```

```python
import functools

import jax
import jax.numpy as jnp
import numpy as np
from jax import lax
from jax.experimental import pallas as pl
from jax.experimental.pallas import tpu as pltpu

D_MODEL = 1024
W_A = 512
W_B = 512
CHUNK = 128
H_A = 4
P_A = W_A // H_A
GROUP_B = 16
G_B = W_B // GROUP_B
N_STATE = 64
N_EXPERTS = 16
N_EXPERT_GROUPS = 4
EXPERTS_PER_GROUP = N_EXPERTS // N_EXPERT_GROUPS
D_FF_EXPERT = D_MODEL // 4
LN_EPS = 1e-5

LANES = 128
SUBLANES = 8
HALF_GROUPS = 16
HALF_W = HALF_GROUPS * GROUP_B
HALF_STATE = HALF_GROUPS * N_STATE
VMEM_LIMIT = 56 * 1024 * 1024

PAIRS = ((0, 1), (0, 2), (0, 3), (1, 2), (1, 3), (2, 3))
N_CLASSES = N_EXPERT_GROUPS * len(PAIRS)
CLASS_ROWS = 32
TILE_M = 256
PAIR_SUBTILES = 1
TOK_ROWS = D_MODEL // LANES
ZERO_TOKENS = TILE_M // 2
DMA_UNROLL = 8
S5_FOLD = 8
S5_PAIRS = G_B // 2
EA_TABLE = np.array([EXPERTS_PER_GROUP * g + a for g in range(N_EXPERT_GROUPS) for a, _ in PAIRS], np.int32)
EB_TABLE = np.array([EXPERTS_PER_GROUP * g + b for g in range(N_EXPERT_GROUPS) for _, b in PAIRS], np.int32)

F32 = jnp.float32
BF16 = jnp.bfloat16
I32 = jnp.int32
HIGHEST = lax.Precision.HIGHEST


def _layer_norm(x, g, b):
    mu = jnp.mean(x, axis=-1, keepdims=True)
    xc = x - mu
    var = jnp.mean(xc * xc, axis=-1, keepdims=True)
    return xc * lax.rsqrt(var + LN_EPS) * g + b


def _dot(a, b):
    return jnp.dot(a, b, preferred_element_type=F32)


def _route_classes(x1, wrt, rbcol):
    logits_t = lax.dot_general(wrt, x1, (((1,), (1,)), ((), ())),
                               preferred_element_type=F32, precision=HIGHEST)
    biased = jax.nn.sigmoid(logits_t) + rbcol
    rows = [biased[e:e + 1, :] for e in range(N_EXPERTS)]
    n = EXPERTS_PER_GROUP

    best = sel = None
    for g in range(N_EXPERT_GROUPS):
        v = rows[n * g:n * (g + 1)]
        gs = None
        for a, b in PAIRS:
            s = v[a] + v[b]
            gs = s if gs is None else jnp.maximum(gs, s)
        if g == 0:
            best, sel = gs, jnp.zeros(gs.shape, I32)
        else:
            upd = gs > best
            sel = jnp.where(upd, g, sel)
            best = jnp.where(upd, gs, best)

    cls = jnp.zeros(sel.shape, I32)
    for g in range(N_EXPERT_GROUPS):
        v = rows[n * g:n * (g + 1)]
        lo = jnp.full(sel.shape, n, I32)
        hi = jnp.full(sel.shape, -1, I32)
        for i in range(n):
            before = jnp.zeros(sel.shape, I32)
            for j in range(n):
                if j < i:
                    before = before + (v[j] >= v[i]).astype(I32)
                elif j > i:
                    before = before + (v[j] > v[i]).astype(I32)
            member = before < 2
            lo = jnp.where(member, jnp.minimum(lo, i), lo)
            hi = jnp.where(member, jnp.maximum(hi, i), hi)
        base = jnp.where(lo == 0, 0, jnp.where(lo == 1, 3, 5))
        cls = jnp.where(sel == g, g * len(PAIRS) + base + (hi - lo - 1), cls)
    return cls


def _to_token_tiles(ref, row0, x):
    n = x.shape[0]
    for c in range(TOK_ROWS):
        ref[pl.ds(row0 * TOK_ROWS + c, n, stride=TOK_ROWS), :] = x[:, c * LANES:(c + 1) * LANES]


def _from_token_tiles(ref, row0, n):
    return jnp.concatenate(
        [ref[pl.ds(row0 * TOK_ROWS + c, n, stride=TOK_ROWS), :] for c in range(TOK_ROWS)], axis=1)


def _gathered_tokens(step, nsteps, pos8_ref, src_hbm, bufs, sems, rows, consume):
    def start(s, base):
        def body(g, c):
            for i in range(DMA_UNROLL):
                r = g * DMA_UNROLL + i
                p8 = pl.multiple_of(pos8_ref[base + r], TOK_ROWS)
                pltpu.make_async_copy(src_hbm.at[pl.ds(p8, TOK_ROWS), :],
                                      bufs[s].at[pl.ds(pl.multiple_of(r * TOK_ROWS, TOK_ROWS), TOK_ROWS), :],
                                      sems.at[s]).start(priority=i % 2)
            return c
        lax.fori_loop(0, rows // DMA_UNROLL, body, 0)

    @pl.when(step == 0)
    def _():
        start(0, 0)

    for s in range(2):
        @pl.when(lax.rem(step, 2) == s)
        def _(s=s):
            pltpu.make_async_copy(src_hbm.at[pl.ds(0, rows * TOK_ROWS), :], bufs[s], sems.at[s]).wait()

            @pl.when(step + 1 < nsteps)
            def _():
                start(1 - s, (step + 1) * rows)
            consume(bufs[s])


def _mixer_kernel(alpha, nb, gather_in, *refs):
    if gather_in:
        pos8_ref, zs_hbm = refs[:2]
        refs = refs[2:]
    else:
        x_ref = refs[0]
        refs = refs[1:]
    (win_ref, wout_ref, lnvg_ref, lnvb_ref, wtril_ref, bsb_ref, m1_ref, m2_ref, lr8_ref, li8_ref,
     dskip_ref, glu_ref, bglu_ref, ln1g_ref, ln1b_ref, wrt_ref, rbcol_ref, tri_ref,
     x1t_ref, hfin_ref, cls_ref, rank_ref, cnt_ref,
     xb_ref, xs_slab, xs_scb, xst_ref, ht_ref, yt_ref, mix_ref, hstate_ref, carry_ref) = refs[:32]
    lt = CHUNK
    rows = nb * lt
    pitch = lt + SUBLANES
    step = pl.program_id(0)
    nsteps = pl.num_programs(0)

    @pl.when(step == 0)
    def _():
        hstate_ref[...] = jnp.zeros_like(hstate_ref)
        carry_ref[...] = jnp.zeros_like(carry_ref)

    if gather_in:
        xin_ref, xbuf0, xbuf1, gsem = refs[32:36]

        def consume(buf):
            xin_ref[...] = _from_token_tiles(buf, 0, rows)
        _gathered_tokens(step, nsteps, pos8_ref, zs_hbm, (xbuf0, xbuf1), gsem, rows, consume)

        def load_x():
            return xin_ref[...]
    else:
        def load_x():
            return x_ref[...].reshape(rows, D_MODEL)

    xb_ref[...] = load_x().astype(BF16)

    fold = S5_FOLD
    nchunk = lt // fold
    cb = nchunk * nb
    xs = _dot(xb_ref[...], win_ref[:, 2 * W_A:])
    for j in range(W_B // LANES):
        for b in range(nb):
            xs_slab[j, b * pitch:b * pitch + lt, :] = xs[b * lt:(b + 1) * lt, j * LANES:(j + 1) * LANES]

    def to_scb(t, c):
        s_, ch_ = lax.bitwise_and(t, fold - 1), lax.shift_right_logical(t, fold.bit_length() - 1)
        r0 = pl.multiple_of(s_ * cb + ch_ * nb, SUBLANES)
        for j in range(W_B // LANES):
            xs_scb[pl.ds(r0, nb), j * LANES:(j + 1) * LANES] = xs_slab[j, pl.ds(t, nb, stride=pitch), :]
        return c
    lax.fori_loop(0, lt, to_scb, 0, unroll=4)
    xst_ref[...] = xs_scb[...].T.astype(BF16)

    def chunk_inputs(pr):
        return jnp.concatenate(
            [xst_ref[(2 * pr + gi) * GROUP_B:(2 * pr + gi + 1) * GROUP_B, s * cb:(s + 1) * cb]
             for gi in range(2) for s in range(fold)], axis=0)

    pw = 2 * 2 * N_STATE
    for pr in range(S5_PAIRS):
        ht_ref[:, pr * pw:(pr + 1) * pw] = _dot(m1_ref[pr], chunk_inputs(pr)).T

    half_pairs = S5_PAIRS // 2
    for half in range(2):
        c0 = half * half_pairs * pw
        lr8 = [jnp.broadcast_to(lr8_ref[:, (half * half_pairs + p) * LANES:(half * half_pairs + p + 1) * LANES],
                                (nb, LANES)) for p in range(half_pairs)]
        li8 = [jnp.broadcast_to(li8_ref[:, (half * half_pairs + p) * LANES:(half * half_pairs + p + 1) * LANES],
                                (nb, LANES)) for p in range(half_pairs)]
        h = [hstate_ref[:, c0 + q * LANES:c0 + (q + 1) * LANES] for q in range(2 * half_pairs)]
        for c in range(nchunk):
            rs = slice(c * nb, (c + 1) * nb)
            for p in range(half_pairs):
                re_sl = slice(c0 + p * pw, c0 + p * pw + LANES)
                im_sl = slice(c0 + p * pw + LANES, c0 + (p + 1) * pw)
                hr, hi = h[2 * p], h[2 * p + 1]
                ur, ui = ht_ref[rs, re_sl], ht_ref[rs, im_sl]
                ht_ref[rs, re_sl] = hr
                ht_ref[rs, im_sl] = hi
                h[2 * p] = lr8[p] * hr - li8[p] * hi + ur
                h[2 * p + 1] = lr8[p] * hi + li8[p] * hr + ui
        for q in range(2 * half_pairs):
            hstate_ref[:, c0 + q * LANES:c0 + (q + 1) * LANES] = h[q]
    hfin_ref[...] = hstate_ref[...]

    for pr in range(S5_PAIRS):
        rhs = jnp.concatenate([ht_ref[:, pr * pw:(pr + 1) * pw].T.astype(BF16), chunk_inputs(pr)], axis=0)
        yt = _dot(m2_ref[pr], rhs)
        for gi in range(2):
            for s in range(fold):
                r0 = (gi * fold + s) * GROUP_B
                yt_ref[(2 * pr + gi) * GROUP_B:(2 * pr + gi + 1) * GROUP_B, s * cb:(s + 1) * cb] = (
                    yt[r0:r0 + GROUP_B, :])

    y = jax.nn.gelu(yt_ref[...].T + dskip_ref[...] * xs_scb[...])
    yb = y.astype(BF16)
    spitch = cb + SUBLANES
    for k in range(2):
        sl = slice(k * HALF_W, (k + 1) * HALF_W)
        gl = _dot(yb[:, sl], glu_ref[k]) + bglu_ref[:, sl]
        z = y[:, sl] * jax.nn.sigmoid(gl)
        for j in range(HALF_W // LANES):
            for s in range(fold):
                xs_slab[k * (HALF_W // LANES) + j, s * spitch:s * spitch + cb, :] = (
                    z[s * cb:(s + 1) * cb, j * LANES:(j + 1) * LANES])

    def to_bt(ch, c):
        for b in range(nb):
            dst = pl.multiple_of(b * lt + ch * fold, SUBLANES)
            for j in range(W_B // LANES):
                mix_ref[pl.ds(dst, fold), W_A + j * LANES:W_A + (j + 1) * LANES] = (
                    xs_slab[j, pl.ds(ch * nb + b, fold, stride=spitch), :])
        return c
    lax.fori_loop(0, nchunk, to_bt, 0)

    vg = jax.nn.gelu(_dot(xb_ref[...], win_ref[:, W_A:2 * W_A]))
    v = _layer_norm(vg, lnvg_ref[...], lnvb_ref[...]).astype(BF16)
    u = jax.nn.gelu(_dot(xb_ref[...], win_ref[:, :W_A]))
    for h in range(H_A):
        hs = slice(h * P_A, (h + 1) * P_A)
        vcat = jnp.concatenate([v[b * lt:(b + 1) * lt, hs] for b in range(nb)], axis=1)
        o = _dot(wtril_ref[h], vcat)
        for b in range(nb):
            rs = slice(b * lt, (b + 1) * lt)
            mix_ref[rs, hs] = u[rs, hs] * (o[:, b * LANES:(b + 1) * LANES] + bsb_ref[h])

    mix = _dot(mix_ref[...].astype(BF16), wout_ref[...])
    x1 = _layer_norm(alpha * load_x() + mix, ln1g_ref[...], ln1b_ref[...])
    _to_token_tiles(x1t_ref, 0, x1)

    cls = _route_classes(x1, wrt_ref[...], rbcol_ref[...])
    crow = lax.broadcasted_iota(I32, (CLASS_ROWS, rows), 0)
    onehot = jnp.where(crow == cls, 1.0, 0.0)
    prefix = _dot(onehot.astype(BF16), tri_ref[...])
    carry = carry_ref[:, 0:1]
    rank = jnp.sum(onehot * (prefix + carry), axis=0, keepdims=True)
    cls_ref[...] = cls.reshape(1, 1, rows)
    rank_ref[...] = rank.astype(I32).reshape(1, 1, rows)
    carry_ref[...] = carry_ref[...] + jnp.sum(onehot, axis=1, keepdims=True)
    cnt_ref[...] = carry_ref[...]


def _const_spec(shape):
    nd = len(shape)
    return pl.BlockSpec(shape, lambda *_: (0,) * nd, pipeline_mode=pl.Buffered(1))


def _layer_spec(shape, l):
    nd = len(shape)
    return pl.BlockSpec((None,) + tuple(shape[1:]), lambda *_: (l,) + (0,) * (nd - 1),
                        pipeline_mode=pl.Buffered(1))


MIXER_WEIGHTS = ("win", "wout", "lnvg", "lnvb", "wtril", "bsb", "m1", "m2", "lr8", "li8", "dskip", "glu",
                 "bglu", "ln1g", "ln1b")
SAMPLE_WEIGHTS = ("win", "wout", "lnvg", "lnvb", "ws0", "bs0", "bdb", "bdc", "lr", "li", "dskip", "glu",
                  "bglu", "ln1g", "ln1b")


def _mixer_prompt(x, lw, shared, l, alpha, nb, seq, pos_prev=None):
    lt = CHUNK
    rows = nb * lt
    nsteps = seq // lt
    gather_in = pos_prev is not None
    weights = tuple(lw[k] for k in MIXER_WEIGHTS) + tuple(shared)
    wspecs = [_layer_spec(lw[k].shape, l) for k in MIXER_WEIGHTS] + [_const_spec(w.shape) for w in shared]
    if gather_in:
        x_spec = pl.BlockSpec(memory_space=pl.ANY)
    else:
        x_spec = pl.BlockSpec((nb, lt, D_MODEL), lambda i, *_: (0, i, 0))
    scratch = [
        pltpu.VMEM((rows, D_MODEL), BF16),
        pltpu.VMEM((W_B // LANES, nb * (lt + SUBLANES), LANES), F32),
        pltpu.VMEM((rows, W_B), F32),
        pltpu.VMEM((W_B, rows), BF16),
        pltpu.VMEM((rows // S5_FOLD, 2 * G_B * N_STATE), F32),
        pltpu.VMEM((W_B, rows), F32),
        pltpu.VMEM((rows, D_MODEL), F32),
        pltpu.VMEM((nb, 2 * G_B * N_STATE), F32),
        pltpu.VMEM((CLASS_ROWS, LANES), F32),
    ]
    if gather_in:
        scratch += [pltpu.VMEM((rows, D_MODEL), F32),
                    pltpu.VMEM((rows * TOK_ROWS, LANES), F32), pltpu.VMEM((rows * TOK_ROWS, LANES), F32),
                    pltpu.SemaphoreType.DMA((2,))]
    grid_spec = pltpu.PrefetchScalarGridSpec(
        num_scalar_prefetch=1 if gather_in else 0,
        grid=(nsteps,),
        in_specs=[x_spec] + wspecs,
        out_specs=[pl.BlockSpec((rows * TOK_ROWS, LANES), lambda i, *_: (i, 0)),
                   pl.BlockSpec((nb, 2 * G_B * N_STATE), lambda i, *_: (0, 0)),
                   pl.BlockSpec((1, 1, rows), lambda i, *_: (i, 0, 0)),
                   pl.BlockSpec((1, 1, rows), lambda i, *_: (i, 0, 0)),
                   pl.BlockSpec((CLASS_ROWS, LANES), lambda i, *_: (0, 0))],
        scratch_shapes=scratch)
    args = ((pos_prev, x) if gather_in else (x,)) + weights
    return pl.pallas_call(
        functools.partial(_mixer_kernel, alpha, nb, gather_in),
        grid_spec=grid_spec,
        out_shape=[jax.ShapeDtypeStruct((nb * seq * TOK_ROWS, LANES), F32),
                   jax.ShapeDtypeStruct((nb, 2 * G_B * N_STATE), F32),
                   jax.ShapeDtypeStruct((nsteps, 1, rows), I32),
                   jax.ShapeDtypeStruct((nsteps, 1, rows), I32),
                   jax.ShapeDtypeStruct((CLASS_ROWS, LANES), F32)],
        compiler_params=pltpu.CompilerParams(dimension_semantics=("arbitrary",),
                                             vmem_limit_bytes=VMEM_LIMIT),
    )(*args)


def _dispatch_kernel(rows, pos8_ref, zstart_ref, zlen_ref, tail_ref, x_ref, xs_hbm, zero_ref, sem, zsem):
    step = pl.program_id(0)
    ztok = ZERO_TOKENS

    @pl.when(step == 0)
    def _():
        zero_ref[...] = jnp.zeros_like(zero_ref)
        pieces = []
        for c in range(N_CLASSES):
            start = zstart_ref[c]
            zlen = zlen_ref[c]
            p = TILE_M // 2
            while p >= 1:
                hit = (zlen & p) != 0
                pieces.append((hit, pltpu.make_async_copy(
                    zero_ref.at[pl.ds(0, p * TOK_ROWS), :],
                    xs_hbm.at[pl.ds(pl.multiple_of(start * TOK_ROWS, TOK_ROWS), p * TOK_ROWS), :], zsem)))
                start = start + jnp.where(hit, p, 0)
                p //= 2
        for hit, cp in pieces:
            pl.when(hit)(cp.start)
        for hit, cp in pieces:
            pl.when(hit)(cp.wait)

        zrows = ztok * TOK_ROWS
        first = tail_ref[0] // ztok

        def tail_copy(q):
            return pltpu.make_async_copy(
                zero_ref, xs_hbm.at[pl.ds(pl.multiple_of(q * zrows, zrows), zrows), :], zsem)

        def tail_start(q, c):
            tail_copy(q).start()
            return c

        def tail_wait(q, c):
            tail_copy(q).wait()
            return c
        lax.fori_loop(first, xs_hbm.shape[0] // zrows, tail_start, 0)
        lax.fori_loop(first, xs_hbm.shape[0] // zrows, tail_wait, 0)

    base = step * rows

    def body(g, c):
        for i in range(DMA_UNROLL):
            r = g * DMA_UNROLL + i
            p8 = pl.multiple_of(pos8_ref[base + r], TOK_ROWS)
            pltpu.make_async_copy(x_ref.at[pl.ds(pl.multiple_of(r * TOK_ROWS, TOK_ROWS), TOK_ROWS), :],
                                  xs_hbm.at[pl.ds(p8, TOK_ROWS), :], sem).start(priority=i % 2)
        return c
    lax.fori_loop(0, rows // DMA_UNROLL, body, 0)
    pltpu.make_async_copy(x_ref, xs_hbm.at[pl.ds(0, rows * TOK_ROWS), :], sem).wait()


def _dispatch(x1t, pos8, zstart, zlen, tail, rows, ns_tokens):
    return pl.pallas_call(
        functools.partial(_dispatch_kernel, rows),
        grid_spec=pltpu.PrefetchScalarGridSpec(
            num_scalar_prefetch=4,
            grid=(x1t.shape[0] // (rows * TOK_ROWS),),
            in_specs=[pl.BlockSpec((rows * TOK_ROWS, LANES), lambda i, *_: (i, 0))],
            out_specs=pl.BlockSpec(memory_space=pl.ANY),
            scratch_shapes=[pltpu.VMEM((ZERO_TOKENS * TOK_ROWS, LANES), F32),
                            pltpu.SemaphoreType.DMA(()), pltpu.SemaphoreType.DMA(())]),
        out_shape=jax.ShapeDtypeStruct((ns_tokens * TOK_ROWS, LANES), F32),
        compiler_params=pltpu.CompilerParams(dimension_semantics=("arbitrary",),
                                             vmem_limit_bytes=VMEM_LIMIT),
    )(pos8, zstart, zlen, tail, x1t)


def _pair_kernel(alpha, tidx_ref, tcls_ref, ea_ref, eb_ref, nused_ref,
                 x_ref, wrp_ref, wga_ref, wua_ref, wda_ref, wgb_ref, wub_ref, wdb_ref,
                 ln2g_ref, ln2b_ref, z_ref, w1_ref, w2_ref, xprev_ref, moe_ref):
    j = pl.program_id(0)
    f = D_FF_EXPERT
    nused = nused_ref[0]

    @pl.when(j == 0)
    def _():
        xprev_ref[...] = jnp.zeros_like(xprev_ref)
        moe_ref[...] = jnp.zeros_like(moe_ref)

    @pl.when(j > nused)
    def _():
        z_ref[...] = jnp.zeros_like(z_ref)

    @pl.when(j <= nused)
    def _():
        changed = jnp.logical_or(j == 0, tcls_ref[j] != tcls_ref[jnp.maximum(j - 1, 0)])

        @pl.when(changed)
        def _():
            w1_ref[:, 0 * f:1 * f] = wga_ref[...].astype(BF16)
            w1_ref[:, 1 * f:2 * f] = wua_ref[...].astype(BF16)
            w1_ref[:, 2 * f:3 * f] = wgb_ref[...].astype(BF16)
            w1_ref[:, 3 * f:4 * f] = wub_ref[...].astype(BF16)
            w1_ref[:, 4 * f:] = wrp_ref[...].astype(BF16)
            w2_ref[0:f, :] = wda_ref[...].astype(BF16)
            w2_ref[f:2 * f, :] = wdb_ref[...].astype(BF16)

        _to_token_tiles(z_ref, 0, _layer_norm(alpha * xprev_ref[...] + moe_ref[...], ln2g_ref[...], ln2b_ref[...]))

        x = _from_token_tiles(x_ref, 0, TILE_M)
        gu = _dot(x.astype(BF16), w1_ref[...])
        scores = jax.nn.sigmoid(gu[:, 4 * f:])
        lane = lax.broadcasted_iota(I32, scores.shape, 1)
        sa = jnp.sum(jnp.where(lane == ea_ref[j], scores, 0.0), axis=-1, keepdims=True)
        sb = jnp.sum(jnp.where(lane == eb_ref[j], scores, 0.0), axis=-1, keepdims=True)
        tot = sa + sb
        ha = jax.nn.silu(gu[:, 0 * f:1 * f]) * gu[:, 1 * f:2 * f] * (sa / tot)
        hb = jax.nn.silu(gu[:, 2 * f:3 * f]) * gu[:, 3 * f:4 * f] * (sb / tot)
        xprev_ref[...] = x
        moe_ref[...] = _dot(jnp.concatenate([ha, hb], axis=1).astype(BF16), w2_ref[...])


def _pair_experts(xs, l, tidx, tcls, ea, eb, nused, wrp, w_gate, w_up, w_down, ln2g, ln2b, alpha):
    nsteps = tidx.shape[0]

    def wspec(shape, table):
        return pl.BlockSpec((None, None) + shape, lambda j, ti, tc, a, b, nu: (l, (a, b)[table][j], 0, 0))
    up = (D_MODEL, D_FF_EXPERT)
    dn = (D_FF_EXPERT, D_MODEL)
    cst = lambda j, *_: (0, 0)
    lsel = lambda j, *_: (l, 0, 0)
    return pl.pallas_call(
        functools.partial(_pair_kernel, alpha),
        grid_spec=pltpu.PrefetchScalarGridSpec(
            num_scalar_prefetch=5,
            grid=(nsteps,),
            in_specs=[pl.BlockSpec((TILE_M * TOK_ROWS, LANES), lambda j, ti, *_: (ti[j], 0)),
                      pl.BlockSpec(wrp.shape, cst),
                      wspec(up, 0), wspec(up, 0), wspec(dn, 0), wspec(up, 1), wspec(up, 1), wspec(dn, 1),
                      pl.BlockSpec((None,) + ln2g.shape[1:], lsel), pl.BlockSpec((None,) + ln2b.shape[1:], lsel)],
            out_specs=pl.BlockSpec((TILE_M * TOK_ROWS, LANES), lambda j, *_: (jnp.maximum(j - 1, 0), 0)),
            scratch_shapes=[pltpu.VMEM((D_MODEL, 4 * D_FF_EXPERT + LANES), BF16),
                            pltpu.VMEM((2 * D_FF_EXPERT, D_MODEL), BF16),
                            pltpu.VMEM((TILE_M, D_MODEL), F32), pltpu.VMEM((TILE_M, D_MODEL), F32)]),
        out_shape=jax.ShapeDtypeStruct(xs.shape, F32),
        compiler_params=pltpu.CompilerParams(dimension_semantics=("arbitrary",),
                                             vmem_limit_bytes=VMEM_LIMIT),
    )(tidx, tcls, ea, eb, nused, xs, wrp, w_gate, w_up, w_down, w_gate, w_up, w_down, ln2g, ln2b)


def _ungather_kernel(nb, pos8_ref, zs_hbm, out_ref, buf0, buf1, sems):
    rows = nb * CHUNK

    def consume(buf):
        out_ref[...] = _from_token_tiles(buf, 0, rows).reshape(nb, CHUNK, D_MODEL)
    _gathered_tokens(pl.program_id(0), pl.num_programs(0), pos8_ref, zs_hbm, (buf0, buf1), sems, rows, consume)


def _ungather(zs, pos8, nb, seq):
    rows = nb * CHUNK
    return pl.pallas_call(
        functools.partial(_ungather_kernel, nb),
        grid_spec=pltpu.PrefetchScalarGridSpec(
            num_scalar_prefetch=1,
            grid=(seq // CHUNK,),
            in_specs=[pl.BlockSpec(memory_space=pl.ANY)],
            out_specs=pl.BlockSpec((nb, CHUNK, D_MODEL), lambda i, *_: (0, i, 0)),
            scratch_shapes=[pltpu.VMEM((rows * TOK_ROWS, LANES), F32), pltpu.VMEM((rows * TOK_ROWS, LANES), F32),
                            pltpu.SemaphoreType.DMA((2,))]),
        out_shape=jax.ShapeDtypeStruct((nb, seq, D_MODEL), F32),
        compiler_params=pltpu.CompilerParams(dimension_semantics=("arbitrary",),
                                             vmem_limit_bytes=VMEM_LIMIT),
    )(pos8, zs)


def _plan(cls, rank, counts, nt):
    cnt = counts[:N_CLASSES, 0].astype(I32)
    ntile = (cnt + TILE_M - 1) // TILE_M
    padded = ntile * TILE_M
    off = jnp.cumsum(padded) - padded
    classes = jnp.arange(N_CLASSES, dtype=I32)
    pos8 = (rank.reshape(-1) + jnp.sum(jnp.where(cls.reshape(-1, 1) == classes, off, 0), axis=1)) * TOK_ROWS
    tile_end = jnp.cumsum(ntile)
    nused = tile_end[-1:].astype(I32)
    tidx = jnp.minimum(jnp.arange(nt + 1, dtype=I32), nused - 1)
    tsel = tile_end[None, :] <= tidx[:, None]
    tcls = jnp.sum(tsel.astype(I32), axis=1)
    onehot = tcls[:, None] == classes
    ea = jnp.sum(jnp.where(onehot, jnp.asarray(EA_TABLE), 0), axis=1)
    eb = jnp.sum(jnp.where(onehot, jnp.asarray(EB_TABLE), 0), axis=1)
    return pos8.astype(I32), off + cnt, padded - cnt, nused * TILE_M, tidx, tcls, ea, eb, nused


def _mixer_sample_kernel(alpha, x_ref, h0_ref, win_ref, wout_ref, lnvg_ref, lnvb_ref, ws0_ref, bs0_ref,
                         bdb_ref, bdc_ref, lr_ref, li_ref, dskip_ref, glu_ref, bglu_ref,
                         ln1g_ref, ln1b_ref,
                         x1_ref, hnew_ref, v_ref):
    x = x_ref[...]
    proj = _dot(x.astype(BF16), win_ref[...])
    u = jax.nn.gelu(proj[:, :W_A])
    v = _layer_norm(jax.nn.gelu(proj[:, W_A:2 * W_A]), lnvg_ref[...], lnvb_ref[...])
    v_ref[...] = v
    y_a = u * (ws0_ref[...] * v + bs0_ref[...])
    xs = proj[:, 2 * W_A:]
    zs = []
    for k in range(2):
        sl = slice(k * HALF_W, (k + 1) * HALF_W)
        bu = _dot(xs[:, sl].astype(BF16), bdb_ref[k])
        h0r = h0_ref[k, :, :HALF_STATE]
        h0i = h0_ref[k, :, HALF_STATE:]
        lr = lr_ref[k]
        li = li_ref[k]
        hr = lr * h0r - li * h0i + bu[:, :HALF_STATE]
        hi = lr * h0i + li * h0r + bu[:, HALF_STATE:]
        hnew_ref[k, :, :HALF_STATE] = hr
        hnew_ref[k, :, HALF_STATE:] = hi
        hcat = jnp.concatenate([hr, hi], axis=1).astype(BF16)
        y = jax.nn.gelu(_dot(hcat, bdc_ref[k]) + dskip_ref[:, sl] * xs[:, sl])
        gl = _dot(y.astype(BF16), glu_ref[k]) + bglu_ref[:, sl]
        zs.append(y * jax.nn.sigmoid(gl))
    cat = jnp.concatenate([y_a] + zs, axis=1).astype(BF16)
    mix = _dot(cat, wout_ref[...])
    x1_ref[...] = _layer_norm(alpha * x + mix, ln1g_ref[...], ln1b_ref[...])


def _mixer_sample(x, h0, lw, l, alpha):
    n = x.shape[0]
    full = lambda shape: pl.BlockSpec(shape, lambda i: (0,) * len(shape))
    return pl.pallas_call(
        functools.partial(_mixer_sample_kernel, alpha),
        grid=(1,),
        in_specs=[full(x.shape), _layer_spec(h0.shape, l)] + [_layer_spec(lw[k].shape, l) for k in SAMPLE_WEIGHTS],
        out_specs=[full((n, D_MODEL)), full((2, n, 2 * HALF_STATE)), full((n, W_A))],
        out_shape=[jax.ShapeDtypeStruct((n, D_MODEL), F32),
                   jax.ShapeDtypeStruct((2, n, 2 * HALF_STATE), F32),
                   jax.ShapeDtypeStruct((n, W_A), F32)],
        compiler_params=pltpu.CompilerParams(dimension_semantics=("arbitrary",), vmem_limit_bytes=VMEM_LIMIT),
    )(x, h0, *[lw[k] for k in SAMPLE_WEIGHTS])


def _route(x, wr, rbias):
    logits = jnp.dot(x, wr, preferred_element_type=F32, precision=HIGHEST)
    scores = jax.nn.sigmoid(logits)
    biased = scores + rbias
    lane = lax.broadcasted_iota(I32, biased.shape, 1)
    grp = lane // EXPERTS_PER_GROUP
    neg = jnp.float32(-jnp.inf)

    def top2(vals):
        m1 = jnp.max(vals, axis=-1, keepdims=True)
        i1 = jnp.min(jnp.where(vals == m1, lane, N_EXPERTS), axis=-1, keepdims=True)
        rest = jnp.where(lane == i1, neg, vals)
        m2 = jnp.max(rest, axis=-1, keepdims=True)
        i2 = jnp.min(jnp.where(rest == m2, lane, N_EXPERTS), axis=-1, keepdims=True)
        return m1, i1, m2, i2

    best = sel = None
    for g in range(N_EXPERT_GROUPS):
        m1, _, m2, _ = top2(jnp.where(grp == g, biased, neg))
        gs = m1 + m2
        if g == 0:
            best, sel = gs, jnp.zeros(gs.shape, I32)
        else:
            upd = gs > best
            sel = jnp.where(upd, g, sel)
            best = jnp.where(upd, gs, best)
    _, i1, _, i2 = top2(jnp.where(grp == sel, biased, neg))
    s1 = jnp.sum(jnp.where(lane == i1, scores, 0.0), axis=-1, keepdims=True)
    s2 = jnp.sum(jnp.where(lane == i2, scores, 0.0), axis=-1, keepdims=True)
    tot = s1 + s2
    return jnp.where(lane == i1, s1 / tot, 0.0) + jnp.where(lane == i2, s2 / tot, 0.0)


def _moe_kernel(alpha, x_ref, wr_ref, rb_ref, wg_ref, wu_ref, wd_ref, ln2g_ref, ln2b_ref,
                out_ref, xb_ref, comb_ref, acc_ref):
    e = pl.program_id(1)

    @pl.when(e == 0)
    def _():
        x = x_ref[...]
        xb_ref[...] = x.astype(BF16)
        comb_ref[...] = _route(x, wr_ref[...], rb_ref[...])
        acc_ref[...] = jnp.zeros_like(acc_ref)

    xb = xb_ref[...]
    g = _dot(xb, wg_ref[...].astype(BF16))
    u = _dot(xb, wu_ref[...].astype(BF16))
    comb = comb_ref[...]
    lane = lax.broadcasted_iota(I32, comb.shape, 1)
    ce = jnp.sum(jnp.where(lane == e, comb, 0.0), axis=-1, keepdims=True)
    h = (jax.nn.silu(g) * u * ce).astype(BF16)
    acc_ref[...] += _dot(h, wd_ref[...].astype(BF16))

    @pl.when(e == N_EXPERTS - 1)
    def _():
        out_ref[...] = _layer_norm(alpha * x_ref[...] + acc_ref[...], ln2g_ref[...], ln2b_ref[...])


def _moe_dense(x, l, wr, rb, w_gate, w_up, w_down, ln2g, ln2b, alpha, tm):
    t = x.shape[0]
    cst = lambda i, e: (0, 0)
    wsel = lambda i, e: (l, e, 0, 0)
    return pl.pallas_call(
        functools.partial(_moe_kernel, alpha),
        grid=(t // tm, N_EXPERTS),
        in_specs=[pl.BlockSpec((tm, D_MODEL), lambda i, e: (i, 0)),
                  pl.BlockSpec(wr.shape, cst), pl.BlockSpec(rb.shape, cst),
                  pl.BlockSpec((None, None, D_MODEL, D_FF_EXPERT), wsel),
                  pl.BlockSpec((None, None, D_MODEL, D_FF_EXPERT), wsel),
                  pl.BlockSpec((None, None, D_FF_EXPERT, D_MODEL), wsel),
                  pl.BlockSpec((None,) + ln2g.shape[1:], lambda i, e: (l, 0, 0)),
                  pl.BlockSpec((None,) + ln2b.shape[1:], lambda i, e: (l, 0, 0))],
        out_specs=pl.BlockSpec((tm, D_MODEL), lambda i, e: (i, 0)),
        out_shape=jax.ShapeDtypeStruct((t, D_MODEL), F32),
        scratch_shapes=[pltpu.VMEM((tm, D_MODEL), BF16),
                        pltpu.VMEM((tm, N_EXPERTS), F32),
                        pltpu.VMEM((tm, D_MODEL), F32)],
        compiler_params=pltpu.CompilerParams(dimension_semantics=("arbitrary", "arbitrary"),
                                             vmem_limit_bytes=VMEM_LIMIT),
    )(x, wr, rb, w_gate, w_up, w_down, ln2g, ln2b)


def _block_diag(blocks):
    eye = jnp.eye(HALF_GROUPS, dtype=blocks.dtype)
    k, g, a, b = blocks.shape
    return jnp.einsum("kgab,gh->kgahb", blocks, eye).reshape(k, g * a, g * b)


def _prep_s5_folded(lb_re, lb_im, bb_re, bb_im, c_re, c_im):
    d, g, n = lb_re.shape
    s = S5_FOLD
    pairs = g // 2
    pr, pi = [jnp.ones_like(lb_re)], [jnp.zeros_like(lb_re)]
    for _ in range(s):
        pr, pi = pr + [pr[-1] * lb_re - pi[-1] * lb_im], pi + [pr[-1] * lb_im + pi[-1] * lb_re]
    p_re, p_im = jnp.stack(pr, axis=2), jnp.stack(pi, axis=2)

    def pair_cols(re, im):
        z = jnp.zeros_like(re[:, :, 0])
        g0 = jnp.concatenate([re[:, :, 0], z, im[:, :, 0], z], axis=-1)
        g1 = jnp.concatenate([z, re[:, :, 1], z, im[:, :, 1]], axis=-1)
        return jnp.concatenate([g0, g1], axis=2)

    def pair_diag(a):
        z = jnp.zeros_like(a[:, :, 0])
        return jnp.concatenate([jnp.concatenate([a[:, :, 0], z], axis=-1),
                                jnp.concatenate([z, a[:, :, 1]], axis=-1)], axis=2)

    bt_re, bt_im = bb_re.transpose(0, 1, 3, 2)[:, :, None], bb_im.transpose(0, 1, 3, 2)[:, :, None]
    k_re, k_im = p_re[:, :, s - 1::-1, None, :], p_im[:, :, s - 1::-1, None, :]
    m1t = pair_cols((k_re * bt_re - k_im * bt_im).reshape(d, pairs, 2, -1, n),
                    (k_re * bt_im + k_im * bt_re).reshape(d, pairs, 2, -1, n))
    m1 = jnp.swapaxes(m1t, -1, -2)

    cq_re, cq_im = c_re[:, :, None], c_im[:, :, None]
    j_re, j_im = p_re[:, :, 1:, None, :], p_im[:, :, 1:, None, :]
    hpart = pair_cols((cq_re * j_re - cq_im * j_im).reshape(d, pairs, 2, -1, n),
                      (-(cq_re * j_im + cq_im * j_re)).reshape(d, pairs, 2, -1, n))

    t_re, t_im = p_re[:, :, :s, None, :], p_im[:, :, :s, None, :]
    kern = (jnp.einsum("dgtqn,dgnp->dgtqp", cq_re * t_re - cq_im * t_im, bb_re, precision=HIGHEST)
            - jnp.einsum("dgtqn,dgnp->dgtqp", cq_re * t_im + cq_im * t_re, bb_im, precision=HIGHEST))
    zero = jnp.zeros_like(kern[:, :, 0])
    kx = jnp.stack([jnp.concatenate([kern[:, :, j - q] if q <= j else zero for q in range(s)], axis=-1)
                    for j in range(s)], axis=2)
    apart = pair_diag(kx.reshape(d, pairs, 2, -1, kx.shape[-1]))

    m2 = jnp.concatenate([hpart, apart], axis=-1)
    return (m1.astype(BF16), m2.astype(BF16), p_re[:, :, s].reshape(d, 1, g * n), p_im[:, :, s].reshape(d, 1, g * n))


def _prep_all(w_in, w_out, ln_v_g, ln_v_b, w_s, b_s, a_re, a_im, log_dt, b_re, b_im, c_re, c_im,
              d_skip, w_glu, b_glu, ln1_g, ln1_b):
    d = w_in.shape[0]
    dt = jnp.exp(log_dt)[..., None]
    decay = jnp.exp(a_re * dt)
    lb_re, lb_im = decay * jnp.cos(a_im * dt), decay * jnp.sin(a_im * dt)
    den = a_re * a_re + a_im * a_im
    nr, ni = lb_re - 1.0, lb_im
    zr = (nr * a_re + ni * a_im) / den
    zi = (ni * a_re - nr * a_im) / den
    bb_re = zr[..., None] * b_re - zi[..., None] * b_im
    bb_im = zr[..., None] * b_im + zi[..., None] * b_re

    def bd(a):
        blocks = _block_diag(a.reshape((d * 2, HALF_GROUPS) + a.shape[2:]))
        return blocks.reshape((d, 2) + blocks.shape[1:])

    bdb = jnp.concatenate([bd(bb_re.transpose(0, 1, 3, 2)), bd(bb_im.transpose(0, 1, 3, 2))], axis=3)
    bdc = jnp.concatenate([bd(c_re.transpose(0, 1, 3, 2)), bd(-c_im.transpose(0, 1, 3, 2))], axis=2)
    m1, m2, lr8, li8 = _prep_s5_folded(lb_re, lb_im, bb_re, bb_im, c_re, c_im)
    return dict(
        m1=m1, m2=m2, lr8=lr8, li8=li8,
        win=w_in.astype(BF16), wout=w_out.astype(BF16),
        lnvg=ln_v_g[:, None], lnvb=ln_v_b[:, None],
        wtril=jnp.tril(w_s).astype(BF16),
        bsb=jnp.broadcast_to(b_s[..., None], (d, H_A, CHUNK, LANES)),
        ws0=jnp.repeat(w_s[:, :, 0, 0], P_A, axis=1)[:, None], bs0=jnp.repeat(b_s[:, :, 0], P_A, axis=1)[:, None],
        bdb=bdb.astype(BF16), bdc=bdc.astype(BF16),
        lr=lb_re.reshape(d, 2, 1, HALF_STATE), li=lb_im.reshape(d, 2, 1, HALF_STATE),
        dskip=d_skip.reshape(d, 1, W_B), glu=bd(w_glu).astype(BF16),
        bglu=b_glu.reshape(d, 1, W_B), ln1g=ln1_g[:, None], ln1b=ln1_b[:, None])


def _state_to_cols(h_re, h_im):
    d, b = h_re.shape[:2]
    re = h_re.reshape(d, b, 2, HALF_STATE)
    im = h_im.reshape(d, b, 2, HALF_STATE)
    return jnp.concatenate([re, im], axis=3).transpose(0, 2, 1, 3)


def _pairs_to_state(h):
    d, b, _ = h.shape
    h = h.reshape(d, b, G_B // 2, 2, 2, N_STATE)
    return h[:, :, :, 0].reshape(d, b, G_B, N_STATE), h[:, :, :, 1].reshape(d, b, G_B, N_STATE)


def _cols_to_state(h):
    d, _, b, _ = h.shape
    re = h[..., :HALF_STATE].transpose(0, 2, 1, 3).reshape(d, b, G_B, N_STATE)
    im = h[..., HALF_STATE:].transpose(0, 2, 1, 3).reshape(d, b, G_B, N_STATE)
    return re, im


def kernel(x_prompt, x_sample, state_ssm_re, state_ssm_im, w_in, w_out, ln_v_g, ln_v_b, w_s, b_s, ssm_a_re, ssm_a_im, ssm_log_dt, ssm_b_re, ssm_b_im, ssm_c_re, ssm_c_im, ssm_d, w_glu, b_glu, ln1_g, ln1_b, ln2_g, ln2_b, w_router, router_bias, w_gate, w_up, w_down):
    depth = w_in.shape[0]
    alpha = float((2 * depth) ** 0.25)
    nb, seq, _ = x_prompt.shape
    ns = x_sample.shape[0]
    tokens = nb * seq
    nt = tokens // TILE_M + N_CLASSES
    rb = router_bias[None]
    wrt = w_router.T
    wrp = jnp.pad(w_router, ((0, 0), (0, LANES - N_EXPERTS)))
    rbcol = router_bias[:, None]
    tok = np.arange(nb * CHUNK)
    tri = jnp.asarray(tok[:, None] < tok[None, :], BF16)
    lw = _prep_all(w_in, w_out, ln_v_g, ln_v_b, w_s, b_s, ssm_a_re, ssm_a_im, ssm_log_dt,
                   ssm_b_re, ssm_b_im, ssm_c_re, ssm_c_im, ssm_d, w_glu, b_glu, ln1_g, ln1_b)
    shared = (wrt, rbcol, tri)
    ln2g, ln2b = ln2_g[:, None], ln2_b[:, None]
    h0s = _state_to_cols(state_ssm_re, state_ssm_im)
    xp = x_prompt
    pos = None
    xs = x_sample.reshape(ns, D_MODEL)
    pr_h, sm_h, sm_v = [], [], []
    for l in range(depth):
        x1t, hfin, cls, rank, counts = _mixer_prompt(xp, lw, shared, l, alpha, nb, seq, pos)
        pos, zstart, zlen, tail, tidx, tcls, ea, eb, nused = _plan(cls, rank, counts, nt)
        x_sorted = _dispatch(x1t, pos, zstart, zlen, tail, nb * CHUNK, nt * TILE_M)
        xp = _pair_experts(x_sorted, l, tidx, tcls, ea, eb, nused, wrp, w_gate, w_up, w_down,
                           ln2g, ln2b, alpha)
        pr_h.append(hfin)

        x1s, hnew, v_new = _mixer_sample(xs, h0s, lw, l, alpha)
        xs = _moe_dense(x1s, l, w_router, rb, w_gate, w_up, w_down, ln2g, ln2b, alpha, tm=ns)
        sm_h.append(hnew)
        sm_v.append(v_new.reshape(ns, 1, W_A))
    y_prompt = _ungather(xp, pos, nb, seq)
    pr_re, pr_im = _pairs_to_state(jnp.stack(pr_h))
    sm_re, sm_im = _cols_to_state(jnp.stack(sm_h))
    return (y_prompt, xs.reshape(ns, 1, D_MODEL), pr_re, pr_im, sm_re, sm_im, jnp.stack(sm_v))
```

```python
import functools

import jax
import jax.numpy as jnp
import numpy as np
from jax import lax
from jax.experimental import pallas as pl
from jax.experimental.pallas import tpu as pltpu

D_MODEL = 1024
W_A = 512
W_B = 512
CHUNK = 128
H_A = 4
P_A = W_A // H_A
GROUP_B = 16
G_B = W_B // GROUP_B
N_STATE = 64
N_EXPERTS = 16
N_EXPERT_GROUPS = 4
EXPERTS_PER_GROUP = N_EXPERTS // N_EXPERT_GROUPS
D_FF_EXPERT = D_MODEL // 4
LN_EPS = 1e-5

LANES = 128
SUBLANES = 8
HALF_GROUPS = 16
HALF_W = HALF_GROUPS * GROUP_B
HALF_STATE = HALF_GROUPS * N_STATE
VMEM_LIMIT = 56 * 1024 * 1024

PAIRS = ((0, 1), (0, 2), (0, 3), (1, 2), (1, 3), (2, 3))
N_CLASSES = N_EXPERT_GROUPS * len(PAIRS)
CLASS_ROWS = 32
TILE_M = 256
PAIR_SUBTILES = 1
TOK_ROWS = D_MODEL // LANES
ZERO_TOKENS = TILE_M // 2
DMA_UNROLL = 8
DENSE_EXPERTS_PER_STEP = 4
S5_FOLD = 8
S5_PAIRS = G_B // 2
EA_TABLE = np.array([EXPERTS_PER_GROUP * g + a for g in range(N_EXPERT_GROUPS) for a, _ in PAIRS], np.int32)
EB_TABLE = np.array([EXPERTS_PER_GROUP * g + b for g in range(N_EXPERT_GROUPS) for _, b in PAIRS], np.int32)

F32 = jnp.float32
BF16 = jnp.bfloat16
I32 = jnp.int32
HIGHEST = lax.Precision.HIGHEST


def _layer_norm(x, g, b):
    mu = jnp.mean(x, axis=-1, keepdims=True)
    xc = x - mu
    var = jnp.mean(xc * xc, axis=-1, keepdims=True)
    return xc * lax.rsqrt(var + LN_EPS) * g + b


def _dot(a, b):
    return jnp.dot(a, b, preferred_element_type=F32)


def _route_classes(x1, wrt, rbcol):
    logits_t = lax.dot_general(wrt, x1, (((1,), (1,)), ((), ())),
                               preferred_element_type=F32, precision=HIGHEST)
    biased = jax.nn.sigmoid(logits_t) + rbcol
    rows = [biased[e:e + 1, :] for e in range(N_EXPERTS)]
    n = EXPERTS_PER_GROUP

    best = sel = None
    for g in range(N_EXPERT_GROUPS):
        v = rows[n * g:n * (g + 1)]
        gs = None
        for a, b in PAIRS:
            s = v[a] + v[b]
            gs = s if gs is None else jnp.maximum(gs, s)
        if g == 0:
            best, sel = gs, jnp.zeros(gs.shape, I32)
        else:
            upd = gs > best
            sel = jnp.where(upd, g, sel)
            best = jnp.where(upd, gs, best)

    cls = jnp.zeros(sel.shape, I32)
    for g in range(N_EXPERT_GROUPS):
        v = rows[n * g:n * (g + 1)]
        lo = jnp.full(sel.shape, n, I32)
        hi = jnp.full(sel.shape, -1, I32)
        for i in range(n):
            before = jnp.zeros(sel.shape, I32)
            for j in range(n):
                if j < i:
                    before = before + (v[j] >= v[i]).astype(I32)
                elif j > i:
                    before = before + (v[j] > v[i]).astype(I32)
            member = before < 2
            lo = jnp.where(member, jnp.minimum(lo, i), lo)
            hi = jnp.where(member, jnp.maximum(hi, i), hi)
        base = jnp.where(lo == 0, 0, jnp.where(lo == 1, 3, 5))
        cls = jnp.where(sel == g, g * len(PAIRS) + base + (hi - lo - 1), cls)
    return cls


def _to_token_tiles(ref, row0, x):
    n = x.shape[0]
    for c in range(TOK_ROWS):
        ref[pl.ds(row0 * TOK_ROWS + c, n, stride=TOK_ROWS), :] = x[:, c * LANES:(c + 1) * LANES]


def _from_token_tiles(ref, row0, n):
    return jnp.concatenate(
        [ref[pl.ds(row0 * TOK_ROWS + c, n, stride=TOK_ROWS), :] for c in range(TOK_ROWS)], axis=1)


def _gathered_tokens(step, nsteps, pos8_ref, src_hbm, bufs, sems, rows, consume):
    def start(s, base):
        def body(g, c):
            for i in range(DMA_UNROLL):
                r = g * DMA_UNROLL + i
                p8 = pl.multiple_of(pos8_ref[base + r], TOK_ROWS)
                pltpu.make_async_copy(src_hbm.at[pl.ds(p8, TOK_ROWS), :],
                                      bufs[s].at[pl.ds(pl.multiple_of(r * TOK_ROWS, TOK_ROWS), TOK_ROWS), :],
                                      sems.at[s]).start(priority=i % 2)
            return c
        lax.fori_loop(0, rows // DMA_UNROLL, body, 0)

    @pl.when(step == 0)
    def _():
        start(0, 0)

    for s in range(2):
        @pl.when(lax.rem(step, 2) == s)
        def _(s=s):
            pltpu.make_async_copy(src_hbm.at[pl.ds(0, rows * TOK_ROWS), :], bufs[s], sems.at[s]).wait()

            @pl.when(step + 1 < nsteps)
            def _():
                start(1 - s, (step + 1) * rows)
            consume(bufs[s])


def _mixer_kernel(alpha, nb, gather_in, *refs):
    if gather_in:
        pos8_ref, zs_hbm = refs[:2]
        refs = refs[2:]
    else:
        x_ref = refs[0]
        refs = refs[1:]
    (win_ref, wout_ref, lnvg_ref, lnvb_ref, wtril_ref, bsb_ref, m1_ref, m2_ref, lr8_ref, li8_ref,
     dskip_ref, glu_ref, bglu_ref, ln1g_ref, ln1b_ref, wrt_ref, rbcol_ref, tri_ref,
     x1t_ref, hfin_ref, cls_ref, rank_ref, cnt_ref,
     xb_ref, xs_slab, xs_scb, xst_ref, ht_ref, yt_ref, mix_ref, hstate_ref, carry_ref) = refs[:32]
    lt = CHUNK
    rows = nb * lt
    pitch = lt + SUBLANES
    step = pl.program_id(0)
    nsteps = pl.num_programs(0)

    @pl.when(step == 0)
    def _():
        hstate_ref[...] = jnp.zeros_like(hstate_ref)
        carry_ref[...] = jnp.zeros_like(carry_ref)

    if gather_in:
        xin_ref, xbuf0, xbuf1, gsem = refs[32:36]

        def consume(buf):
            xin_ref[...] = _from_token_tiles(buf, 0, rows)
        _gathered_tokens(step, nsteps, pos8_ref, zs_hbm, (xbuf0, xbuf1), gsem, rows, consume)

        def load_x():
            return xin_ref[...]
    else:
        def load_x():
            return x_ref[...].reshape(rows, D_MODEL)

    xb_ref[...] = load_x().astype(BF16)

    fold = S5_FOLD
    nchunk = lt // fold
    cb = nchunk * nb
    xs = _dot(xb_ref[...], win_ref[:, 2 * W_A:])
    for j in range(W_B // LANES):
        for b in range(nb):
            xs_slab[j, b * pitch:b * pitch + lt, :] = xs[b * lt:(b + 1) * lt, j * LANES:(j + 1) * LANES]

    def to_scb(t, c):
        s_, ch_ = lax.bitwise_and(t, fold - 1), lax.shift_right_logical(t, fold.bit_length() - 1)
        r0 = pl.multiple_of(s_ * cb + ch_ * nb, SUBLANES)
        for j in range(W_B // LANES):
            xs_scb[pl.ds(r0, nb), j * LANES:(j + 1) * LANES] = xs_slab[j, pl.ds(t, nb, stride=pitch), :]
        return c
    lax.fori_loop(0, lt, to_scb, 0, unroll=4)
    xst_ref[...] = xs_scb[...].T.astype(BF16)

    def chunk_inputs(pr):
        return jnp.concatenate(
            [xst_ref[(2 * pr + gi) * GROUP_B:(2 * pr + gi + 1) * GROUP_B, s * cb:(s + 1) * cb]
             for gi in range(2) for s in range(fold)], axis=0)

    pw = 2 * 2 * N_STATE
    for pr in range(S5_PAIRS):
        ht_ref[:, pr * pw:(pr + 1) * pw] = _dot(m1_ref[pr], chunk_inputs(pr)).T

    half_pairs = S5_PAIRS // 2
    for half in range(2):
        c0 = half * half_pairs * pw
        lr8 = [jnp.broadcast_to(lr8_ref[:, (half * half_pairs + p) * LANES:(half * half_pairs + p + 1) * LANES],
                                (nb, LANES)) for p in range(half_pairs)]
        li8 = [jnp.broadcast_to(li8_ref[:, (half * half_pairs + p) * LANES:(half * half_pairs + p + 1) * LANES],
                                (nb, LANES)) for p in range(half_pairs)]
        h = [hstate_ref[:, c0 + q * LANES:c0 + (q + 1) * LANES] for q in range(2 * half_pairs)]
        for c in range(nchunk):
            rs = slice(c * nb, (c + 1) * nb)
            for p in range(half_pairs):
                re_sl = slice(c0 + p * pw, c0 + p * pw + LANES)
                im_sl = slice(c0 + p * pw + LANES, c0 + (p + 1) * pw)
                hr, hi = h[2 * p], h[2 * p + 1]
                ur, ui = ht_ref[rs, re_sl], ht_ref[rs, im_sl]
                ht_ref[rs, re_sl] = hr
                ht_ref[rs, im_sl] = hi
                h[2 * p] = lr8[p] * hr - li8[p] * hi + ur
                h[2 * p + 1] = lr8[p] * hi + li8[p] * hr + ui
        for q in range(2 * half_pairs):
            hstate_ref[:, c0 + q * LANES:c0 + (q + 1) * LANES] = h[q]
    hfin_ref[...] = hstate_ref[...]

    for pr in range(S5_PAIRS):
        rhs = jnp.concatenate([ht_ref[:, pr * pw:(pr + 1) * pw].T.astype(BF16), chunk_inputs(pr)], axis=0)
        yt = _dot(m2_ref[pr], rhs)
        for gi in range(2):
            for s in range(fold):
                r0 = (gi * fold + s) * GROUP_B
                yt_ref[(2 * pr + gi) * GROUP_B:(2 * pr + gi + 1) * GROUP_B, s * cb:(s + 1) * cb] = (
                    yt[r0:r0 + GROUP_B, :])

    y = jax.nn.gelu(yt_ref[...].T + dskip_ref[...] * xs_scb[...])
    yb = y.astype(BF16)
    spitch = cb + SUBLANES
    for k in range(2):
        sl = slice(k * HALF_W, (k + 1) * HALF_W)
        gl = _dot(yb[:, sl], glu_ref[k]) + bglu_ref[:, sl]
        z = y[:, sl] * jax.nn.sigmoid(gl)
        for j in range(HALF_W // LANES):
            for s in range(fold):
                xs_slab[k * (HALF_W // LANES) + j, s * spitch:s * spitch + cb, :] = (
                    z[s * cb:(s + 1) * cb, j * LANES:(j + 1) * LANES])

    def to_bt(ch, c):
        for b in range(nb):
            dst = pl.multiple_of(b * lt + ch * fold, SUBLANES)
            for j in range(W_B // LANES):
                mix_ref[pl.ds(dst, fold), W_A + j * LANES:W_A + (j + 1) * LANES] = (
                    xs_slab[j, pl.ds(ch * nb + b, fold, stride=spitch), :])
        return c
    lax.fori_loop(0, nchunk, to_bt, 0)

    vg = jax.nn.gelu(_dot(xb_ref[...], win_ref[:, W_A:2 * W_A]))
    v = _layer_norm(vg, lnvg_ref[...], lnvb_ref[...]).astype(BF16)
    u = jax.nn.gelu(_dot(xb_ref[...], win_ref[:, :W_A]))
    for h in range(H_A):
        hs = slice(h * P_A, (h + 1) * P_A)
        vcat = jnp.concatenate([v[b * lt:(b + 1) * lt, hs] for b in range(nb)], axis=1)
        o = _dot(wtril_ref[h], vcat)
        for b in range(nb):
            rs = slice(b * lt, (b + 1) * lt)
            mix_ref[rs, hs] = u[rs, hs] * (o[:, b * LANES:(b + 1) * LANES] + bsb_ref[h])

    mix = _dot(mix_ref[...].astype(BF16), wout_ref[...])
    x1 = _layer_norm(alpha * load_x() + mix, ln1g_ref[...], ln1b_ref[...])
    _to_token_tiles(x1t_ref, 0, x1)

    cls = _route_classes(x1, wrt_ref[...], rbcol_ref[...])
    crow = lax.broadcasted_iota(I32, (CLASS_ROWS, rows), 0)
    onehot = jnp.where(crow == cls, 1.0, 0.0)
    prefix = _dot(onehot.astype(BF16), tri_ref[...])
    carry = carry_ref[:, 0:1]
    rank = jnp.sum(onehot * (prefix + carry), axis=0, keepdims=True)
    cls_ref[...] = cls.reshape(1, 1, rows)
    rank_ref[...] = rank.astype(I32).reshape(1, 1, rows)
    carry_ref[...] = carry_ref[...] + jnp.sum(onehot, axis=1, keepdims=True)
    cnt_ref[...] = carry_ref[...]


def _const_spec(shape):
    nd = len(shape)
    return pl.BlockSpec(shape, lambda *_: (0,) * nd, pipeline_mode=pl.Buffered(1))


def _layer_spec(shape, l):
    nd = len(shape)
    return pl.BlockSpec((None,) + tuple(shape[1:]), lambda *_: (l,) + (0,) * (nd - 1),
                        pipeline_mode=pl.Buffered(1))


MIXER_WEIGHTS = ("win", "wout", "lnvg", "lnvb", "wtril", "bsb", "m1", "m2", "lr8", "li8", "dskip", "glu",
                 "bglu", "ln1g", "ln1b")
SAMPLE_WEIGHTS = ("win", "wout", "lnvg", "lnvb", "ws0", "bs0", "bs", "cs", "lown", "lswap", "dskip", "glu",
                  "bglu", "ln1g", "ln1b")


def _mixer_prompt(x, lw, shared, l, alpha, nb, seq, pos_prev=None):
    lt = CHUNK
    rows = nb * lt
    nsteps = seq // lt
    gather_in = pos_prev is not None
    weights = tuple(lw[k] for k in MIXER_WEIGHTS) + tuple(shared)
    wspecs = [_layer_spec(lw[k].shape, l) for k in MIXER_WEIGHTS] + [_const_spec(w.shape) for w in shared]
    if gather_in:
        x_spec = pl.BlockSpec(memory_space=pl.ANY)
    else:
        x_spec = pl.BlockSpec((nb, lt, D_MODEL), lambda i, *_: (0, i, 0))
    scratch = [
        pltpu.VMEM((rows, D_MODEL), BF16),
        pltpu.VMEM((W_B // LANES, nb * (lt + SUBLANES), LANES), F32),
        pltpu.VMEM((rows, W_B), F32),
        pltpu.VMEM((W_B, rows), BF16),
        pltpu.VMEM((rows // S5_FOLD, 2 * G_B * N_STATE), F32),
        pltpu.VMEM((W_B, rows), F32),
        pltpu.VMEM((rows, D_MODEL), F32),
        pltpu.VMEM((nb, 2 * G_B * N_STATE), F32),
        pltpu.VMEM((CLASS_ROWS, LANES), F32),
    ]
    if gather_in:
        scratch += [pltpu.VMEM((rows, D_MODEL), F32),
                    pltpu.VMEM((rows * TOK_ROWS, LANES), F32), pltpu.VMEM((rows * TOK_ROWS, LANES), F32),
                    pltpu.SemaphoreType.DMA((2,))]
    grid_spec = pltpu.PrefetchScalarGridSpec(
        num_scalar_prefetch=1 if gather_in else 0,
        grid=(nsteps,),
        in_specs=[x_spec] + wspecs,
        out_specs=[pl.BlockSpec((rows * TOK_ROWS, LANES), lambda i, *_: (i, 0)),
                   pl.BlockSpec((nb, 2 * G_B * N_STATE), lambda i, *_: (0, 0)),
                   pl.BlockSpec((1, 1, rows), lambda i, *_: (i, 0, 0)),
                   pl.BlockSpec((1, 1, rows), lambda i, *_: (i, 0, 0)),
                   pl.BlockSpec((CLASS_ROWS, LANES), lambda i, *_: (0, 0))],
        scratch_shapes=scratch)
    args = ((pos_prev, x) if gather_in else (x,)) + weights
    return pl.pallas_call(
        functools.partial(_mixer_kernel, alpha, nb, gather_in),
        grid_spec=grid_spec,
        out_shape=[jax.ShapeDtypeStruct((nb * seq * TOK_ROWS, LANES), F32),
                   jax.ShapeDtypeStruct((nb, 2 * G_B * N_STATE), F32),
                   jax.ShapeDtypeStruct((nsteps, 1, rows), I32),
                   jax.ShapeDtypeStruct((nsteps, 1, rows), I32),
                   jax.ShapeDtypeStruct((CLASS_ROWS, LANES), F32)],
        compiler_params=pltpu.CompilerParams(dimension_semantics=("arbitrary",),
                                             vmem_limit_bytes=VMEM_LIMIT),
    )(*args)


def _dispatch_kernel(rows, pos8_ref, zstart_ref, zlen_ref, tail_ref, x_ref, xs_hbm, zero_ref, sem, zsem):
    step = pl.program_id(0)
    ztok = ZERO_TOKENS

    @pl.when(step == 0)
    def _():
        zero_ref[...] = jnp.zeros_like(zero_ref)
        pieces = []
        for c in range(N_CLASSES):
            start = zstart_ref[c]
            zlen = zlen_ref[c]
            p = TILE_M // 2
            while p >= 1:
                hit = (zlen & p) != 0
                pieces.append((hit, pltpu.make_async_copy(
                    zero_ref.at[pl.ds(0, p * TOK_ROWS), :],
                    xs_hbm.at[pl.ds(pl.multiple_of(start * TOK_ROWS, TOK_ROWS), p * TOK_ROWS), :], zsem)))
                start = start + jnp.where(hit, p, 0)
                p //= 2
        for hit, cp in pieces:
            pl.when(hit)(cp.start)
        for hit, cp in pieces:
            pl.when(hit)(cp.wait)

        zrows = ztok * TOK_ROWS
        first = tail_ref[0] // ztok

        def tail_copy(q):
            return pltpu.make_async_copy(
                zero_ref, xs_hbm.at[pl.ds(pl.multiple_of(q * zrows, zrows), zrows), :], zsem)

        def tail_start(q, c):
            tail_copy(q).start()
            return c

        def tail_wait(q, c):
            tail_copy(q).wait()
            return c
        lax.fori_loop(first, xs_hbm.shape[0] // zrows, tail_start, 0)
        lax.fori_loop(first, xs_hbm.shape[0] // zrows, tail_wait, 0)

    base = step * rows

    def body(g, c):
        for i in range(DMA_UNROLL):
            r = g * DMA_UNROLL + i
            p8 = pl.multiple_of(pos8_ref[base + r], TOK_ROWS)
            pltpu.make_async_copy(x_ref.at[pl.ds(pl.multiple_of(r * TOK_ROWS, TOK_ROWS), TOK_ROWS), :],
                                  xs_hbm.at[pl.ds(p8, TOK_ROWS), :], sem).start(priority=i % 2)
        return c
    lax.fori_loop(0, rows // DMA_UNROLL, body, 0)
    pltpu.make_async_copy(x_ref, xs_hbm.at[pl.ds(0, rows * TOK_ROWS), :], sem).wait()


def _dispatch(x1t, pos8, zstart, zlen, tail, rows, ns_tokens):
    return pl.pallas_call(
        functools.partial(_dispatch_kernel, rows),
        grid_spec=pltpu.PrefetchScalarGridSpec(
            num_scalar_prefetch=4,
            grid=(x1t.shape[0] // (rows * TOK_ROWS),),
            in_specs=[pl.BlockSpec((rows * TOK_ROWS, LANES), lambda i, *_: (i, 0))],
            out_specs=pl.BlockSpec(memory_space=pl.ANY),
            scratch_shapes=[pltpu.VMEM((ZERO_TOKENS * TOK_ROWS, LANES), F32),
                            pltpu.SemaphoreType.DMA(()), pltpu.SemaphoreType.DMA(())]),
        out_shape=jax.ShapeDtypeStruct((ns_tokens * TOK_ROWS, LANES), F32),
        compiler_params=pltpu.CompilerParams(dimension_semantics=("arbitrary",),
                                             vmem_limit_bytes=VMEM_LIMIT),
    )(pos8, zstart, zlen, tail, x1t)


def _pair_kernel(alpha, tidx_ref, tcls_ref, ea_ref, eb_ref, nused_ref,
                 x_ref, wrp_ref, wga_ref, wua_ref, wda_ref, wgb_ref, wub_ref, wdb_ref,
                 ln2g_ref, ln2b_ref, z_ref, w1_ref, w2_ref, xprev_ref, moe_ref):
    j = pl.program_id(0)
    f = D_FF_EXPERT
    nused = nused_ref[0]

    @pl.when(j == 0)
    def _():
        xprev_ref[...] = jnp.zeros_like(xprev_ref)
        moe_ref[...] = jnp.zeros_like(moe_ref)

    @pl.when(j > nused)
    def _():
        z_ref[...] = jnp.zeros_like(z_ref)

    @pl.when(j <= nused)
    def _():
        changed = jnp.logical_or(j == 0, tcls_ref[j] != tcls_ref[jnp.maximum(j - 1, 0)])

        @pl.when(changed)
        def _():
            w1_ref[:, 0 * f:1 * f] = wga_ref[...].astype(BF16)
            w1_ref[:, 1 * f:2 * f] = wua_ref[...].astype(BF16)
            w1_ref[:, 2 * f:3 * f] = wgb_ref[...].astype(BF16)
            w1_ref[:, 3 * f:4 * f] = wub_ref[...].astype(BF16)
            w1_ref[:, 4 * f:] = wrp_ref[...].astype(BF16)
            w2_ref[0:f, :] = wda_ref[...].astype(BF16)
            w2_ref[f:2 * f, :] = wdb_ref[...].astype(BF16)

        _to_token_tiles(z_ref, 0, _layer_norm(alpha * xprev_ref[...] + moe_ref[...], ln2g_ref[...], ln2b_ref[...]))

        x = _from_token_tiles(x_ref, 0, TILE_M)
        gu = _dot(x.astype(BF16), w1_ref[...])
        scores = jax.nn.sigmoid(gu[:, 4 * f:])
        lane = lax.broadcasted_iota(I32, scores.shape, 1)
        sa = jnp.sum(jnp.where(lane == ea_ref[j], scores, 0.0), axis=-1, keepdims=True)
        sb = jnp.sum(jnp.where(lane == eb_ref[j], scores, 0.0), axis=-1, keepdims=True)
        tot = sa + sb
        ha = jax.nn.silu(gu[:, 0 * f:1 * f]) * gu[:, 1 * f:2 * f] * (sa / tot)
        hb = jax.nn.silu(gu[:, 2 * f:3 * f]) * gu[:, 3 * f:4 * f] * (sb / tot)
        xprev_ref[...] = x
        moe_ref[...] = _dot(jnp.concatenate([ha, hb], axis=1).astype(BF16), w2_ref[...])


def _pair_experts(xs, l, tidx, tcls, ea, eb, nused, wrp, w_gate, w_up, w_down, ln2g, ln2b, alpha):
    nsteps = tidx.shape[0]

    def wspec(shape, table):
        return pl.BlockSpec((None, None) + shape, lambda j, ti, tc, a, b, nu: (l, (a, b)[table][j], 0, 0))
    up = (D_MODEL, D_FF_EXPERT)
    dn = (D_FF_EXPERT, D_MODEL)
    cst = lambda j, *_: (0, 0)
    lsel = lambda j, *_: (l, 0, 0)
    return pl.pallas_call(
        functools.partial(_pair_kernel, alpha),
        grid_spec=pltpu.PrefetchScalarGridSpec(
            num_scalar_prefetch=5,
            grid=(nsteps,),
            in_specs=[pl.BlockSpec((TILE_M * TOK_ROWS, LANES), lambda j, ti, *_: (ti[j], 0)),
                      pl.BlockSpec(wrp.shape, cst),
                      wspec(up, 0), wspec(up, 0), wspec(dn, 0), wspec(up, 1), wspec(up, 1), wspec(dn, 1),
                      pl.BlockSpec((None,) + ln2g.shape[1:], lsel), pl.BlockSpec((None,) + ln2b.shape[1:], lsel)],
            out_specs=pl.BlockSpec((TILE_M * TOK_ROWS, LANES), lambda j, *_: (jnp.maximum(j - 1, 0), 0)),
            scratch_shapes=[pltpu.VMEM((D_MODEL, 4 * D_FF_EXPERT + LANES), BF16),
                            pltpu.VMEM((2 * D_FF_EXPERT, D_MODEL), BF16),
                            pltpu.VMEM((TILE_M, D_MODEL), F32), pltpu.VMEM((TILE_M, D_MODEL), F32)]),
        out_shape=jax.ShapeDtypeStruct(xs.shape, F32),
        compiler_params=pltpu.CompilerParams(dimension_semantics=("arbitrary",),
                                             vmem_limit_bytes=VMEM_LIMIT),
    )(tidx, tcls, ea, eb, nused, xs, wrp, w_gate, w_up, w_down, w_gate, w_up, w_down, ln2g, ln2b)


def _ungather_kernel(nb, pos8_ref, zs_hbm, out_ref, buf0, buf1, sems):
    rows = nb * CHUNK

    def consume(buf):
        out_ref[...] = _from_token_tiles(buf, 0, rows).reshape(nb, CHUNK, D_MODEL)
    _gathered_tokens(pl.program_id(0), pl.num_programs(0), pos8_ref, zs_hbm, (buf0, buf1), sems, rows, consume)


def _ungather(zs, pos8, nb, seq):
    rows = nb * CHUNK
    return pl.pallas_call(
        functools.partial(_ungather_kernel, nb),
        grid_spec=pltpu.PrefetchScalarGridSpec(
            num_scalar_prefetch=1,
            grid=(seq // CHUNK,),
            in_specs=[pl.BlockSpec(memory_space=pl.ANY)],
            out_specs=pl.BlockSpec((nb, CHUNK, D_MODEL), lambda i, *_: (0, i, 0)),
            scratch_shapes=[pltpu.VMEM((rows * TOK_ROWS, LANES), F32), pltpu.VMEM((rows * TOK_ROWS, LANES), F32),
                            pltpu.SemaphoreType.DMA((2,))]),
        out_shape=jax.ShapeDtypeStruct((nb, seq, D_MODEL), F32),
        compiler_params=pltpu.CompilerParams(dimension_semantics=("arbitrary",),
                                             vmem_limit_bytes=VMEM_LIMIT),
    )(pos8, zs)


def _plan(cls, rank, counts, nt):
    cnt = counts[:N_CLASSES, 0].astype(I32)
    ntile = (cnt + TILE_M - 1) // TILE_M
    padded = ntile * TILE_M
    off = jnp.cumsum(padded) - padded
    classes = jnp.arange(N_CLASSES, dtype=I32)
    pos8 = (rank.reshape(-1) + jnp.sum(jnp.where(cls.reshape(-1, 1) == classes, off, 0), axis=1)) * TOK_ROWS
    tile_end = jnp.cumsum(ntile)
    nused = tile_end[-1:].astype(I32)
    tidx = jnp.minimum(jnp.arange(nt + 1, dtype=I32), nused - 1)
    tsel = tile_end[None, :] <= tidx[:, None]
    tcls = jnp.sum(tsel.astype(I32), axis=1)
    onehot = tcls[:, None] == classes
    ea = jnp.sum(jnp.where(onehot, jnp.asarray(EA_TABLE), 0), axis=1)
    eb = jnp.sum(jnp.where(onehot, jnp.asarray(EB_TABLE), 0), axis=1)
    return pos8.astype(I32), off + cnt, padded - cnt, nused * TILE_M, tidx, tcls, ea, eb, nused


def _mixer_sample_kernel(alpha, x_ref, h0_ref, win_ref, wout_ref, lnvg_ref, lnvb_ref, ws0_ref, bs0_ref,
                         bs_ref, cs_ref, lown_ref, lswap_ref, dskip_ref, glu_ref, bglu_ref,
                         ln1g_ref, ln1b_ref,
                         x1_ref, hnew_ref, v_ref):
    x = x_ref[...]
    proj = _dot(x.astype(BF16), win_ref[...])
    u = jax.nn.gelu(proj[:, :W_A])
    v = _layer_norm(jax.nn.gelu(proj[:, W_A:2 * W_A]), lnvg_ref[...], lnvb_ref[...])
    v_ref[...] = v
    y_a = u * (ws0_ref[...] * v + bs0_ref[...])
    xs = proj[:, 2 * W_A:]
    h0 = h0_ref[...]
    nblocks = h0.shape[1] // LANES
    partner = jnp.concatenate([h0[:, (i ^ 1) * LANES:((i ^ 1) + 1) * LANES] for i in range(nblocks)], axis=1)
    hn = lown_ref[...] * h0 + lswap_ref[...] * partner + _dot(xs.astype(BF16), bs_ref[...])
    hnew_ref[...] = hn
    y = jax.nn.gelu(_dot(hn.astype(BF16), cs_ref[...]) + dskip_ref[...] * xs)
    yb = y.astype(BF16)
    zs = []
    for k in range(2):
        sl = slice(k * HALF_W, (k + 1) * HALF_W)
        gl = _dot(yb[:, sl], glu_ref[k]) + bglu_ref[:, sl]
        zs.append(y[:, sl] * jax.nn.sigmoid(gl))
    cat = jnp.concatenate([y_a] + zs, axis=1).astype(BF16)
    mix = _dot(cat, wout_ref[...])
    x1_ref[...] = _layer_norm(alpha * x + mix, ln1g_ref[...], ln1b_ref[...])


def _mixer_sample(x, h0, lw, l, alpha):
    n = x.shape[0]
    full = lambda shape: pl.BlockSpec(shape, lambda i: (0,) * len(shape))
    return pl.pallas_call(
        functools.partial(_mixer_sample_kernel, alpha),
        grid=(1,),
        in_specs=[full(x.shape), _layer_spec(h0.shape, l)] + [_layer_spec(lw[k].shape, l) for k in SAMPLE_WEIGHTS],
        out_specs=[full((n, D_MODEL)), full((n, 2 * G_B * N_STATE)), full((n, W_A))],
        out_shape=[jax.ShapeDtypeStruct((n, D_MODEL), F32),
                   jax.ShapeDtypeStruct((n, 2 * G_B * N_STATE), F32),
                   jax.ShapeDtypeStruct((n, W_A), F32)],
        compiler_params=pltpu.CompilerParams(dimension_semantics=("arbitrary",), vmem_limit_bytes=VMEM_LIMIT),
    )(x, h0, *[lw[k] for k in SAMPLE_WEIGHTS])


def _route(x, wr, rbias):
    logits = jnp.dot(x, wr, preferred_element_type=F32, precision=HIGHEST)
    scores = jax.nn.sigmoid(logits)
    biased = scores + rbias
    lane = lax.broadcasted_iota(I32, biased.shape, 1)
    grp = lane // EXPERTS_PER_GROUP
    neg = jnp.float32(-jnp.inf)

    def top2(vals):
        m1 = jnp.max(vals, axis=-1, keepdims=True)
        i1 = jnp.min(jnp.where(vals == m1, lane, N_EXPERTS), axis=-1, keepdims=True)
        rest = jnp.where(lane == i1, neg, vals)
        m2 = jnp.max(rest, axis=-1, keepdims=True)
        i2 = jnp.min(jnp.where(rest == m2, lane, N_EXPERTS), axis=-1, keepdims=True)
        return m1, i1, m2, i2

    best = sel = None
    for g in range(N_EXPERT_GROUPS):
        m1, _, m2, _ = top2(jnp.where(grp == g, biased, neg))
        gs = m1 + m2
        if g == 0:
            best, sel = gs, jnp.zeros(gs.shape, I32)
        else:
            upd = gs > best
            sel = jnp.where(upd, g, sel)
            best = jnp.where(upd, gs, best)
    _, i1, _, i2 = top2(jnp.where(grp == sel, biased, neg))
    s1 = jnp.sum(jnp.where(lane == i1, scores, 0.0), axis=-1, keepdims=True)
    s2 = jnp.sum(jnp.where(lane == i2, scores, 0.0), axis=-1, keepdims=True)
    tot = s1 + s2
    return jnp.where(lane == i1, s1 / tot, 0.0) + jnp.where(lane == i2, s2 / tot, 0.0)


def _moe_kernel(alpha, x_ref, wr_ref, rb_ref, wg_ref, wu_ref, wd_ref, ln2g_ref, ln2b_ref,
                out_ref, xb_ref, comb_ref, acc_ref):
    step = pl.program_id(1)

    @pl.when(step == 0)
    def _():
        x = x_ref[...]
        xb_ref[...] = x.astype(BF16)
        comb_ref[...] = _route(x, wr_ref[...], rb_ref[...])
        acc_ref[...] = jnp.zeros_like(acc_ref)

    xb = xb_ref[...]
    comb = comb_ref[...]
    lane = lax.broadcasted_iota(I32, comb.shape, 1)
    acc = acc_ref[...]
    for k in range(DENSE_EXPERTS_PER_STEP):
        g = _dot(xb, wg_ref[k].astype(BF16))
        u = _dot(xb, wu_ref[k].astype(BF16))
        ce = jnp.sum(jnp.where(lane == step * DENSE_EXPERTS_PER_STEP + k, comb, 0.0), axis=-1, keepdims=True)
        h = (jax.nn.silu(g) * u * ce).astype(BF16)
        acc = acc + _dot(h, wd_ref[k].astype(BF16))
    acc_ref[...] = acc

    @pl.when(step == pl.num_programs(1) - 1)
    def _():
        out_ref[...] = _layer_norm(alpha * x_ref[...] + acc_ref[...], ln2g_ref[...], ln2b_ref[...])


def _moe_dense(x, l, wr, rb, w_gate, w_up, w_down, ln2g, ln2b, alpha, tm):
    t = x.shape[0]
    cst = lambda i, e: (0, 0)
    wsel = lambda i, e: (l, e, 0, 0)
    return pl.pallas_call(
        functools.partial(_moe_kernel, alpha),
        grid=(t // tm, N_EXPERTS // DENSE_EXPERTS_PER_STEP),
        in_specs=[pl.BlockSpec((tm, D_MODEL), lambda i, e: (i, 0)),
                  pl.BlockSpec(wr.shape, cst), pl.BlockSpec(rb.shape, cst),
                  pl.BlockSpec((None, DENSE_EXPERTS_PER_STEP, D_MODEL, D_FF_EXPERT), wsel),
                  pl.BlockSpec((None, DENSE_EXPERTS_PER_STEP, D_MODEL, D_FF_EXPERT), wsel),
                  pl.BlockSpec((None, DENSE_EXPERTS_PER_STEP, D_FF_EXPERT, D_MODEL), wsel),
                  pl.BlockSpec((None,) + ln2g.shape[1:], lambda i, e: (l, 0, 0)),
                  pl.BlockSpec((None,) + ln2b.shape[1:], lambda i, e: (l, 0, 0))],
        out_specs=pl.BlockSpec((tm, D_MODEL), lambda i, e: (i, 0)),
        out_shape=jax.ShapeDtypeStruct((t, D_MODEL), F32),
        scratch_shapes=[pltpu.VMEM((tm, D_MODEL), BF16),
                        pltpu.VMEM((tm, N_EXPERTS), F32),
                        pltpu.VMEM((tm, D_MODEL), F32)],
        compiler_params=pltpu.CompilerParams(dimension_semantics=("arbitrary", "arbitrary"),
                                             vmem_limit_bytes=VMEM_LIMIT),
    )(x, wr, rb, w_gate, w_up, w_down, ln2g, ln2b)


def _block_diag(blocks):
    eye = jnp.eye(HALF_GROUPS, dtype=blocks.dtype)
    k, g, a, b = blocks.shape
    return jnp.einsum("kgab,gh->kgahb", blocks, eye).reshape(k, g * a, g * b)


def _prep_s5_folded(lb_re, lb_im, bb_re, bb_im, c_re, c_im):
    d, g, n = lb_re.shape
    s = S5_FOLD
    pairs = g // 2
    pr, pi = [jnp.ones_like(lb_re)], [jnp.zeros_like(lb_re)]
    for _ in range(s):
        pr, pi = pr + [pr[-1] * lb_re - pi[-1] * lb_im], pi + [pr[-1] * lb_im + pi[-1] * lb_re]
    p_re, p_im = jnp.stack(pr, axis=2), jnp.stack(pi, axis=2)

    def pair_cols(re, im):
        z = jnp.zeros_like(re[:, :, 0])
        g0 = jnp.concatenate([re[:, :, 0], z, im[:, :, 0], z], axis=-1)
        g1 = jnp.concatenate([z, re[:, :, 1], z, im[:, :, 1]], axis=-1)
        return jnp.concatenate([g0, g1], axis=2)

    def pair_diag(a):
        z = jnp.zeros_like(a[:, :, 0])
        return jnp.concatenate([jnp.concatenate([a[:, :, 0], z], axis=-1),
                                jnp.concatenate([z, a[:, :, 1]], axis=-1)], axis=2)

    bt_re, bt_im = bb_re.transpose(0, 1, 3, 2)[:, :, None], bb_im.transpose(0, 1, 3, 2)[:, :, None]
    k_re, k_im = p_re[:, :, s - 1::-1, None, :], p_im[:, :, s - 1::-1, None, :]
    m1t = pair_cols((k_re * bt_re - k_im * bt_im).reshape(d, pairs, 2, -1, n),
                    (k_re * bt_im + k_im * bt_re).reshape(d, pairs, 2, -1, n))
    m1 = jnp.swapaxes(m1t, -1, -2)

    cq_re, cq_im = c_re[:, :, None], c_im[:, :, None]
    j_re, j_im = p_re[:, :, 1:, None, :], p_im[:, :, 1:, None, :]
    hpart = pair_cols((cq_re * j_re - cq_im * j_im).reshape(d, pairs, 2, -1, n),
                      (-(cq_re * j_im + cq_im * j_re)).reshape(d, pairs, 2, -1, n))

    t_re, t_im = p_re[:, :, :s, None, :], p_im[:, :, :s, None, :]
    kern = (jnp.einsum("dgtqn,dgnp->dgtqp", cq_re * t_re - cq_im * t_im, bb_re, precision=HIGHEST)
            - jnp.einsum("dgtqn,dgnp->dgtqp", cq_re * t_im + cq_im * t_re, bb_im, precision=HIGHEST))
    zero = jnp.zeros_like(kern[:, :, 0])
    kx = jnp.stack([jnp.concatenate([kern[:, :, j - q] if q <= j else zero for q in range(s)], axis=-1)
                    for j in range(s)], axis=2)
    apart = pair_diag(kx.reshape(d, pairs, 2, -1, kx.shape[-1]))

    m2 = jnp.concatenate([hpart, apart], axis=-1)
    return (m1.astype(BF16), m2.astype(BF16), p_re[:, :, s].reshape(d, 1, g * n), p_im[:, :, s].reshape(d, 1, g * n))


def _prep_all(w_in, w_out, ln_v_g, ln_v_b, w_s, b_s, a_re, a_im, log_dt, b_re, b_im, c_re, c_im,
              d_skip, w_glu, b_glu, ln1_g, ln1_b):
    d = w_in.shape[0]
    dt = jnp.exp(log_dt)[..., None]
    decay = jnp.exp(a_re * dt)
    lb_re, lb_im = decay * jnp.cos(a_im * dt), decay * jnp.sin(a_im * dt)
    den = a_re * a_re + a_im * a_im
    nr, ni = lb_re - 1.0, lb_im
    zr = (nr * a_re + ni * a_im) / den
    zi = (ni * a_re - nr * a_im) / den
    bb_re = zr[..., None] * b_re - zi[..., None] * b_im
    bb_im = zr[..., None] * b_im + zi[..., None] * b_re

    def bd(a):
        blocks = _block_diag(a.reshape((d * 2, HALF_GROUPS) + a.shape[2:]))
        return blocks.reshape((d, 2) + blocks.shape[1:])

    g, n = lb_re.shape[1:]
    gid = np.arange(g)
    same = jnp.asarray(gid[:, None] == gid[None, :])[None, :, None, :, None]

    def by_group(re, im):
        parts = [jnp.where(same, a[:, :, :, None, :], 0.0).reshape(d, g * a.shape[2], g // 2, 2 * n)
                 for a in (re, im)]
        return jnp.stack(parts, axis=3).reshape(d, g * re.shape[2], 2 * g * n)

    bs = by_group(bb_re.transpose(0, 1, 3, 2), bb_im.transpose(0, 1, 3, 2))
    cs = jnp.swapaxes(by_group(c_re, -c_im), -1, -2)
    pair_blocks = lambda a, b: jnp.stack([a.reshape(d, g // 2, 2 * n), b.reshape(d, g // 2, 2 * n)],
                                         axis=2).reshape(d, 1, 2 * g * n)
    m1, m2, lr8, li8 = _prep_s5_folded(lb_re, lb_im, bb_re, bb_im, c_re, c_im)
    return dict(
        m1=m1, m2=m2, lr8=lr8, li8=li8,
        win=w_in.astype(BF16), wout=w_out.astype(BF16),
        lnvg=ln_v_g[:, None], lnvb=ln_v_b[:, None],
        wtril=jnp.tril(w_s).astype(BF16),
        bsb=jnp.broadcast_to(b_s[..., None], (d, H_A, CHUNK, LANES)),
        ws0=jnp.repeat(w_s[:, :, 0, 0], P_A, axis=1)[:, None], bs0=jnp.repeat(b_s[:, :, 0], P_A, axis=1)[:, None],
        bs=bs.astype(BF16), cs=cs.astype(BF16),
        lown=pair_blocks(lb_re, lb_re), lswap=pair_blocks(-lb_im, lb_im),
        dskip=d_skip.reshape(d, 1, W_B), glu=bd(w_glu).astype(BF16),
        bglu=b_glu.reshape(d, 1, W_B), ln1g=ln1_g[:, None], ln1b=ln1_b[:, None])


def _state_to_pairs(h_re, h_im):
    d, b = h_re.shape[:2]
    shape = (d, b, G_B // 2, 2 * N_STATE)
    return jnp.stack([h_re.reshape(shape), h_im.reshape(shape)], axis=3).reshape(d, b, 2 * G_B * N_STATE)


def _pairs_to_state(h):
    d, b, _ = h.shape
    h = h.reshape(d, b, G_B // 2, 2, 2, N_STATE)
    return h[:, :, :, 0].reshape(d, b, G_B, N_STATE), h[:, :, :, 1].reshape(d, b, G_B, N_STATE)


def kernel(x_prompt, x_sample, state_ssm_re, state_ssm_im, w_in, w_out, ln_v_g, ln_v_b, w_s, b_s, ssm_a_re, ssm_a_im, ssm_log_dt, ssm_b_re, ssm_b_im, ssm_c_re, ssm_c_im, ssm_d, w_glu, b_glu, ln1_g, ln1_b, ln2_g, ln2_b, w_router, router_bias, w_gate, w_up, w_down):
    depth = w_in.shape[0]
    alpha = float((2 * depth) ** 0.25)
    nb, seq, _ = x_prompt.shape
    ns = x_sample.shape[0]
    tokens = nb * seq
    nt = tokens // TILE_M + N_CLASSES
    rb = router_bias[None]
    wrt = w_router.T
    wrp = jnp.pad(w_router, ((0, 0), (0, LANES - N_EXPERTS)))
    rbcol = router_bias[:, None]
    tok = np.arange(nb * CHUNK)
    tri = jnp.asarray(tok[:, None] < tok[None, :], BF16)
    lw = _prep_all(w_in, w_out, ln_v_g, ln_v_b, w_s, b_s, ssm_a_re, ssm_a_im, ssm_log_dt,
                   ssm_b_re, ssm_b_im, ssm_c_re, ssm_c_im, ssm_d, w_glu, b_glu, ln1_g, ln1_b)
    shared = (wrt, rbcol, tri)
    ln2g, ln2b = ln2_g[:, None], ln2_b[:, None]
    h0s = _state_to_pairs(state_ssm_re, state_ssm_im)
    xp = x_prompt
    pos = None
    xs = x_sample.reshape(ns, D_MODEL)
    pr_h, sm_h, sm_v = [], [], []
    for l in range(depth):
        x1t, hfin, cls, rank, counts = _mixer_prompt(xp, lw, shared, l, alpha, nb, seq, pos)
        pos, zstart, zlen, tail, tidx, tcls, ea, eb, nused = _plan(cls, rank, counts, nt)
        x_sorted = _dispatch(x1t, pos, zstart, zlen, tail, nb * CHUNK, nt * TILE_M)
        xp = _pair_experts(x_sorted, l, tidx, tcls, ea, eb, nused, wrp, w_gate, w_up, w_down,
                           ln2g, ln2b, alpha)
        pr_h.append(hfin)

        x1s, hnew, v_new = _mixer_sample(xs, h0s, lw, l, alpha)
        xs = _moe_dense(x1s, l, w_router, rb, w_gate, w_up, w_down, ln2g, ln2b, alpha, tm=ns)
        sm_h.append(hnew)
        sm_v.append(v_new.reshape(ns, 1, W_A))
    y_prompt = _ungather(xp, pos, nb, seq)
    pr_re, pr_im = _pairs_to_state(jnp.stack(pr_h))
    sm_re, sm_im = _pairs_to_state(jnp.stack(sm_h))
    return (y_prompt, xs.reshape(ns, 1, D_MODEL), pr_re, pr_im, sm_re, sm_im, jnp.stack(sm_v))
```

```python
import functools

import jax
import jax.numpy as jnp
import numpy as np
from jax import lax
from jax.experimental import pallas as pl
from jax.experimental.pallas import tpu as pltpu

D_MODEL = 1024
W_A = 512
W_B = 512
CHUNK = 128
H_A = 4
P_A = W_A // H_A
GROUP_B = 16
G_B = W_B // GROUP_B
N_STATE = 64
N_EXPERTS = 16
N_EXPERT_GROUPS = 4
EXPERTS_PER_GROUP = N_EXPERTS // N_EXPERT_GROUPS
D_FF_EXPERT = D_MODEL // 4
LN_EPS = 1e-5

LANES = 128
SUBLANES = 8
HALF_GROUPS = 16
HALF_W = HALF_GROUPS * GROUP_B
HALF_STATE = HALF_GROUPS * N_STATE
VMEM_LIMIT = 56 * 1024 * 1024

PAIRS = ((0, 1), (0, 2), (0, 3), (1, 2), (1, 3), (2, 3))
N_CLASSES = N_EXPERT_GROUPS * len(PAIRS)
CLASS_ROWS = 32
TILE_M = 256
PAIR_SUBTILES = 1
TOK_ROWS = D_MODEL // LANES
ZERO_TOKENS = TILE_M // 2
DMA_UNROLL = 8
DENSE_EXPERTS_PER_STEP = 4
S5_FOLD = 8
S5_PAIRS = G_B // 2
EA_TABLE = np.array([EXPERTS_PER_GROUP * g + a for g in range(N_EXPERT_GROUPS) for a, _ in PAIRS], np.int32)
EB_TABLE = np.array([EXPERTS_PER_GROUP * g + b for g in range(N_EXPERT_GROUPS) for _, b in PAIRS], np.int32)

F32 = jnp.float32
BF16 = jnp.bfloat16
I32 = jnp.int32
HIGHEST = lax.Precision.HIGHEST


def _layer_norm(x, g, b):
    mu = jnp.mean(x, axis=-1, keepdims=True)
    xc = x - mu
    var = jnp.mean(xc * xc, axis=-1, keepdims=True)
    return xc * lax.rsqrt(var + LN_EPS) * g + b


def _dot(a, b):
    return jnp.dot(a, b, preferred_element_type=F32)


def _route_classes(x1, wrt, rbcol):
    def split(a):
        hi = a.astype(BF16)
        return hi, (a - hi.astype(F32)).astype(BF16)

    def dot_t(a, b):
        return lax.dot_general(a, b, (((1,), (1,)), ((), ())), preferred_element_type=F32)
    w_hi, w_lo = split(wrt)
    x_hi, x_lo = split(x1)
    logits_t = dot_t(w_hi, x_hi) + (dot_t(w_hi, x_lo) + dot_t(w_lo, x_hi))
    biased = jax.nn.sigmoid(logits_t) + rbcol
    rows = [biased[e:e + 1, :] for e in range(N_EXPERTS)]
    n = EXPERTS_PER_GROUP

    best = sel = None
    for g in range(N_EXPERT_GROUPS):
        v = rows[n * g:n * (g + 1)]
        gs = None
        for a, b in PAIRS:
            s = v[a] + v[b]
            gs = s if gs is None else jnp.maximum(gs, s)
        if g == 0:
            best, sel = gs, jnp.zeros(gs.shape, I32)
        else:
            upd = gs > best
            sel = jnp.where(upd, g, sel)
            best = jnp.where(upd, gs, best)

    cls = jnp.zeros(sel.shape, I32)
    for g in range(N_EXPERT_GROUPS):
        v = rows[n * g:n * (g + 1)]
        lo = jnp.full(sel.shape, n, I32)
        hi = jnp.full(sel.shape, -1, I32)
        for i in range(n):
            before = jnp.zeros(sel.shape, I32)
            for j in range(n):
                if j < i:
                    before = before + (v[j] >= v[i]).astype(I32)
                elif j > i:
                    before = before + (v[j] > v[i]).astype(I32)
            member = before < 2
            lo = jnp.where(member, jnp.minimum(lo, i), lo)
            hi = jnp.where(member, jnp.maximum(hi, i), hi)
        base = jnp.where(lo == 0, 0, jnp.where(lo == 1, 3, 5))
        cls = jnp.where(sel == g, g * len(PAIRS) + base + (hi - lo - 1), cls)
    return cls


def _to_token_tiles(ref, row0, x):
    n = x.shape[0]
    for c in range(TOK_ROWS):
        ref[pl.ds(row0 * TOK_ROWS + c, n, stride=TOK_ROWS), :] = x[:, c * LANES:(c + 1) * LANES]


def _from_token_tiles(ref, row0, n):
    return jnp.concatenate(
        [ref[pl.ds(row0 * TOK_ROWS + c, n, stride=TOK_ROWS), :] for c in range(TOK_ROWS)], axis=1)


def _gathered_tokens(step, nsteps, pos8_ref, src_hbm, bufs, sems, rows, consume):
    def start(s, base):
        def body(g, c):
            for i in range(DMA_UNROLL):
                r = g * DMA_UNROLL + i
                p8 = pl.multiple_of(pos8_ref[base + r], TOK_ROWS)
                pltpu.make_async_copy(src_hbm.at[pl.ds(p8, TOK_ROWS), :],
                                      bufs[s].at[pl.ds(pl.multiple_of(r * TOK_ROWS, TOK_ROWS), TOK_ROWS), :],
                                      sems.at[s]).start(priority=i % 2)
            return c
        lax.fori_loop(0, rows // DMA_UNROLL, body, 0)

    @pl.when(step == 0)
    def _():
        start(0, 0)

    for s in range(2):
        @pl.when(lax.rem(step, 2) == s)
        def _(s=s):
            pltpu.make_async_copy(src_hbm.at[pl.ds(0, rows * TOK_ROWS), :], bufs[s], sems.at[s]).wait()

            @pl.when(step + 1 < nsteps)
            def _():
                start(1 - s, (step + 1) * rows)
            consume(bufs[s])


def _mixer_kernel(alpha, nb, gather_in, *refs):
    if gather_in:
        pos8_ref, zs_hbm = refs[:2]
        refs = refs[2:]
    else:
        x_ref = refs[0]
        refs = refs[1:]
    (win_ref, wout_ref, lnvg_ref, lnvb_ref, wtril_ref, bsb_ref, m1_ref, m2_ref, lr8_ref, li8_ref,
     dskip_ref, glu_ref, bglu_ref, ln1g_ref, ln1b_ref, wrt_ref, rbcol_ref, tri_ref,
     x1t_ref, hfin_ref, cls_ref, rank_ref, cnt_ref,
     xb_ref, xs_slab, xs_scb, xst_ref, ht_ref, yt_ref, mix_ref, hstate_ref, carry_ref) = refs[:32]
    lt = CHUNK
    rows = nb * lt
    pitch = lt + SUBLANES
    step = pl.program_id(0)
    nsteps = pl.num_programs(0)

    @pl.when(step == 0)
    def _():
        hstate_ref[...] = jnp.zeros_like(hstate_ref)
        carry_ref[...] = jnp.zeros_like(carry_ref)

    if gather_in:
        xin_ref, xbuf0, xbuf1, gsem = refs[32:36]

        def consume(buf):
            xin_ref[...] = _from_token_tiles(buf, 0, rows)
        _gathered_tokens(step, nsteps, pos8_ref, zs_hbm, (xbuf0, xbuf1), gsem, rows, consume)

        def load_x():
            return xin_ref[...]
    else:
        def load_x():
            return x_ref[...].reshape(rows, D_MODEL)

    xb_ref[...] = load_x().astype(BF16)

    fold = S5_FOLD
    nchunk = lt // fold
    cb = nchunk * nb
    xs = _dot(xb_ref[...], win_ref[:, 2 * W_A:])
    for j in range(W_B // LANES):
        for b in range(nb):
            xs_slab[j, b * pitch:b * pitch + lt, :] = xs[b * lt:(b + 1) * lt, j * LANES:(j + 1) * LANES]

    def to_scb(t, c):
        s_, ch_ = lax.bitwise_and(t, fold - 1), lax.shift_right_logical(t, fold.bit_length() - 1)
        r0 = pl.multiple_of(s_ * cb + ch_ * nb, SUBLANES)
        for j in range(W_B // LANES):
            xs_scb[pl.ds(r0, nb), j * LANES:(j + 1) * LANES] = xs_slab[j, pl.ds(t, nb, stride=pitch), :]
        return c
    lax.fori_loop(0, lt, to_scb, 0, unroll=4)
    xst_ref[...] = xs_scb[...].T.astype(BF16)

    def chunk_inputs(pr):
        return jnp.concatenate(
            [xst_ref[(2 * pr + gi) * GROUP_B:(2 * pr + gi + 1) * GROUP_B, s * cb:(s + 1) * cb]
             for gi in range(2) for s in range(fold)], axis=0)

    pw = 2 * 2 * N_STATE
    for pr in range(S5_PAIRS):
        ht_ref[:, pr * pw:(pr + 1) * pw] = _dot(m1_ref[pr], chunk_inputs(pr)).T

    half_pairs = S5_PAIRS // 2
    for half in range(2):
        c0 = half * half_pairs * pw
        lr8 = [jnp.broadcast_to(lr8_ref[:, (half * half_pairs + p) * LANES:(half * half_pairs + p + 1) * LANES],
                                (nb, LANES)) for p in range(half_pairs)]
        li8 = [jnp.broadcast_to(li8_ref[:, (half * half_pairs + p) * LANES:(half * half_pairs + p + 1) * LANES],
                                (nb, LANES)) for p in range(half_pairs)]
        h = [hstate_ref[:, c0 + q * LANES:c0 + (q + 1) * LANES] for q in range(2 * half_pairs)]
        for c in range(nchunk):
            rs = slice(c * nb, (c + 1) * nb)
            for p in range(half_pairs):
                re_sl = slice(c0 + p * pw, c0 + p * pw + LANES)
                im_sl = slice(c0 + p * pw + LANES, c0 + (p + 1) * pw)
                hr, hi = h[2 * p], h[2 * p + 1]
                ur, ui = ht_ref[rs, re_sl], ht_ref[rs, im_sl]
                ht_ref[rs, re_sl] = hr
                ht_ref[rs, im_sl] = hi
                h[2 * p] = lr8[p] * hr - li8[p] * hi + ur
                h[2 * p + 1] = lr8[p] * hi + li8[p] * hr + ui
        for q in range(2 * half_pairs):
            hstate_ref[:, c0 + q * LANES:c0 + (q + 1) * LANES] = h[q]
    hfin_ref[...] = hstate_ref[...]

    for pr in range(S5_PAIRS):
        rhs = jnp.concatenate([ht_ref[:, pr * pw:(pr + 1) * pw].T.astype(BF16), chunk_inputs(pr)], axis=0)
        yt = _dot(m2_ref[pr], rhs)
        for gi in range(2):
            for s in range(fold):
                r0 = (gi * fold + s) * GROUP_B
                yt_ref[(2 * pr + gi) * GROUP_B:(2 * pr + gi + 1) * GROUP_B, s * cb:(s + 1) * cb] = (
                    yt[r0:r0 + GROUP_B, :])

    y = jax.nn.gelu(yt_ref[...].T + dskip_ref[...] * xs_scb[...])
    yb = y.astype(BF16)
    spitch = cb + SUBLANES
    for k in range(2):
        sl = slice(k * HALF_W, (k + 1) * HALF_W)
        gl = _dot(yb[:, sl], glu_ref[k]) + bglu_ref[:, sl]
        z = y[:, sl] * jax.nn.sigmoid(gl)
        for j in range(HALF_W // LANES):
            for s in range(fold):
                xs_slab[k * (HALF_W // LANES) + j, s * spitch:s * spitch + cb, :] = (
                    z[s * cb:(s + 1) * cb, j * LANES:(j + 1) * LANES])

    def to_bt(ch, c):
        for b in range(nb):
            dst = pl.multiple_of(b * lt + ch * fold, SUBLANES)
            for j in range(W_B // LANES):
                mix_ref[pl.ds(dst, fold), W_A + j * LANES:W_A + (j + 1) * LANES] = (
                    xs_slab[j, pl.ds(ch * nb + b, fold, stride=spitch), :])
        return c
    lax.fori_loop(0, nchunk, to_bt, 0)

    vg = jax.nn.gelu(_dot(xb_ref[...], win_ref[:, W_A:2 * W_A]))
    v = _layer_norm(vg, lnvg_ref[...], lnvb_ref[...]).astype(BF16)
    u = jax.nn.gelu(_dot(xb_ref[...], win_ref[:, :W_A]))
    for h in range(H_A):
        hs = slice(h * P_A, (h + 1) * P_A)
        vcat = jnp.concatenate([v[b * lt:(b + 1) * lt, hs] for b in range(nb)], axis=1)
        o = _dot(wtril_ref[h], vcat)
        for b in range(nb):
            rs = slice(b * lt, (b + 1) * lt)
            mix_ref[rs, hs] = u[rs, hs] * (o[:, b * LANES:(b + 1) * LANES] + bsb_ref[h])

    mix = _dot(mix_ref[...].astype(BF16), wout_ref[...])
    x1 = _layer_norm(alpha * load_x() + mix, ln1g_ref[...], ln1b_ref[...])
    _to_token_tiles(x1t_ref, 0, x1)

    cls = _route_classes(x1, wrt_ref[...], rbcol_ref[...])
    crow = lax.broadcasted_iota(I32, (CLASS_ROWS, rows), 0)
    onehot = jnp.where(crow == cls, 1.0, 0.0)
    prefix = _dot(onehot.astype(BF16), tri_ref[...])
    carry = carry_ref[:, 0:1]
    rank = jnp.sum(onehot * (prefix + carry), axis=0, keepdims=True)
    cls_ref[...] = cls.reshape(1, 1, rows)
    rank_ref[...] = rank.astype(I32).reshape(1, 1, rows)
    carry_ref[...] = carry_ref[...] + jnp.sum(onehot, axis=1, keepdims=True)
    cnt_ref[...] = carry_ref[...]


def _const_spec(shape):
    nd = len(shape)
    return pl.BlockSpec(shape, lambda *_: (0,) * nd, pipeline_mode=pl.Buffered(1))


def _layer_spec(shape, l):
    nd = len(shape)
    return pl.BlockSpec((None,) + tuple(shape[1:]), lambda *_: (l,) + (0,) * (nd - 1),
                        pipeline_mode=pl.Buffered(1))


MIXER_WEIGHTS = ("win", "wout", "lnvg", "lnvb", "wtril", "bsb", "m1", "m2", "lr8", "li8", "dskip", "glu",
                 "bglu", "ln1g", "ln1b")
SAMPLE_WEIGHTS = ("win", "wout", "lnvg", "lnvb", "ws0", "bs0", "bdb", "bdc", "lr", "li", "dskip", "glu",
                  "bglu", "ln1g", "ln1b")


def _mixer_prompt(x, lw, shared, l, alpha, nb, seq, pos_prev=None):
    lt = CHUNK
    rows = nb * lt
    nsteps = seq // lt
    gather_in = pos_prev is not None
    weights = tuple(lw[k] for k in MIXER_WEIGHTS) + tuple(shared)
    wspecs = [_layer_spec(lw[k].shape, l) for k in MIXER_WEIGHTS] + [_const_spec(w.shape) for w in shared]
    if gather_in:
        x_spec = pl.BlockSpec(memory_space=pl.ANY)
    else:
        x_spec = pl.BlockSpec((nb, lt, D_MODEL), lambda i, *_: (0, i, 0))
    scratch = [
        pltpu.VMEM((rows, D_MODEL), BF16),
        pltpu.VMEM((W_B // LANES, nb * (lt + SUBLANES), LANES), F32),
        pltpu.VMEM((rows, W_B), F32),
        pltpu.VMEM((W_B, rows), BF16),
        pltpu.VMEM((rows // S5_FOLD, 2 * G_B * N_STATE), F32),
        pltpu.VMEM((W_B, rows), F32),
        pltpu.VMEM((rows, D_MODEL), F32),
        pltpu.VMEM((nb, 2 * G_B * N_STATE), F32),
        pltpu.VMEM((CLASS_ROWS, LANES), F32),
    ]
    if gather_in:
        scratch += [pltpu.VMEM((rows, D_MODEL), F32),
                    pltpu.VMEM((rows * TOK_ROWS, LANES), F32), pltpu.VMEM((rows * TOK_ROWS, LANES), F32),
                    pltpu.SemaphoreType.DMA((2,))]
    grid_spec = pltpu.PrefetchScalarGridSpec(
        num_scalar_prefetch=1 if gather_in else 0,
        grid=(nsteps,),
        in_specs=[x_spec] + wspecs,
        out_specs=[pl.BlockSpec((rows * TOK_ROWS, LANES), lambda i, *_: (i, 0)),
                   pl.BlockSpec((nb, 2 * G_B * N_STATE), lambda i, *_: (0, 0)),
                   pl.BlockSpec((1, 1, rows), lambda i, *_: (i, 0, 0)),
                   pl.BlockSpec((1, 1, rows), lambda i, *_: (i, 0, 0)),
                   pl.BlockSpec((CLASS_ROWS, LANES), lambda i, *_: (0, 0))],
        scratch_shapes=scratch)
    args = ((pos_prev, x) if gather_in else (x,)) + weights
    return pl.pallas_call(
        functools.partial(_mixer_kernel, alpha, nb, gather_in),
        grid_spec=grid_spec,
        out_shape=[jax.ShapeDtypeStruct((nb * seq * TOK_ROWS, LANES), F32),
                   jax.ShapeDtypeStruct((nb, 2 * G_B * N_STATE), F32),
                   jax.ShapeDtypeStruct((nsteps, 1, rows), I32),
                   jax.ShapeDtypeStruct((nsteps, 1, rows), I32),
                   jax.ShapeDtypeStruct((CLASS_ROWS, LANES), F32)],
        compiler_params=pltpu.CompilerParams(dimension_semantics=("arbitrary",),
                                             vmem_limit_bytes=VMEM_LIMIT),
    )(*args)


def _dispatch_kernel(rows, pos8_ref, zstart_ref, zlen_ref, tail_ref, x_ref, xs_hbm, zero_ref, sem, zsem):
    step = pl.program_id(0)
    ztok = ZERO_TOKENS

    @pl.when(step == 0)
    def _():
        zero_ref[...] = jnp.zeros_like(zero_ref)
        pieces = []
        for c in range(N_CLASSES):
            start = zstart_ref[c]
            zlen = zlen_ref[c]
            p = TILE_M // 2
            while p >= 1:
                hit = (zlen & p) != 0
                pieces.append((hit, pltpu.make_async_copy(
                    zero_ref.at[pl.ds(0, p * TOK_ROWS), :],
                    xs_hbm.at[pl.ds(pl.multiple_of(start * TOK_ROWS, TOK_ROWS), p * TOK_ROWS), :], zsem)))
                start = start + jnp.where(hit, p, 0)
                p //= 2
        for hit, cp in pieces:
            pl.when(hit)(cp.start)
        for hit, cp in pieces:
            pl.when(hit)(cp.wait)

        zrows = ztok * TOK_ROWS
        first = tail_ref[0] // ztok

        def tail_copy(q):
            return pltpu.make_async_copy(
                zero_ref, xs_hbm.at[pl.ds(pl.multiple_of(q * zrows, zrows), zrows), :], zsem)

        def tail_start(q, c):
            tail_copy(q).start()
            return c

        def tail_wait(q, c):
            tail_copy(q).wait()
            return c
        lax.fori_loop(first, xs_hbm.shape[0] // zrows, tail_start, 0)
        lax.fori_loop(first, xs_hbm.shape[0] // zrows, tail_wait, 0)

    base = step * rows

    def body(g, c):
        for i in range(DMA_UNROLL):
            r = g * DMA_UNROLL + i
            p8 = pl.multiple_of(pos8_ref[base + r], TOK_ROWS)
            pltpu.make_async_copy(x_ref.at[pl.ds(pl.multiple_of(r * TOK_ROWS, TOK_ROWS), TOK_ROWS), :],
                                  xs_hbm.at[pl.ds(p8, TOK_ROWS), :], sem).start(priority=i % 2)
        return c
    lax.fori_loop(0, rows // DMA_UNROLL, body, 0)
    pltpu.make_async_copy(x_ref, xs_hbm.at[pl.ds(0, rows * TOK_ROWS), :], sem).wait()


def _dispatch(x1t, pos8, zstart, zlen, tail, rows, ns_tokens):
    return pl.pallas_call(
        functools.partial(_dispatch_kernel, rows),
        grid_spec=pltpu.PrefetchScalarGridSpec(
            num_scalar_prefetch=4,
            grid=(x1t.shape[0] // (rows * TOK_ROWS),),
            in_specs=[pl.BlockSpec((rows * TOK_ROWS, LANES), lambda i, *_: (i, 0))],
            out_specs=pl.BlockSpec(memory_space=pl.ANY),
            scratch_shapes=[pltpu.VMEM((ZERO_TOKENS * TOK_ROWS, LANES), F32),
                            pltpu.SemaphoreType.DMA(()), pltpu.SemaphoreType.DMA(())]),
        out_shape=jax.ShapeDtypeStruct((ns_tokens * TOK_ROWS, LANES), F32),
        compiler_params=pltpu.CompilerParams(dimension_semantics=("arbitrary",),
                                             vmem_limit_bytes=VMEM_LIMIT),
    )(pos8, zstart, zlen, tail, x1t)


def _pair_kernel(alpha, tidx_ref, tcls_ref, ea_ref, eb_ref, nused_ref,
                 x_ref, wrp_ref, wga_ref, wua_ref, wda_ref, wgb_ref, wub_ref, wdb_ref,
                 ln2g_ref, ln2b_ref, z_ref, w1_ref, w2_ref, xprev_ref, moe_ref):
    j = pl.program_id(0)
    f = D_FF_EXPERT
    nused = nused_ref[0]

    @pl.when(j == 0)
    def _():
        xprev_ref[...] = jnp.zeros_like(xprev_ref)
        moe_ref[...] = jnp.zeros_like(moe_ref)

    @pl.when(j > nused)
    def _():
        z_ref[...] = jnp.zeros_like(z_ref)

    @pl.when(j <= nused)
    def _():
        changed = jnp.logical_or(j == 0, tcls_ref[j] != tcls_ref[jnp.maximum(j - 1, 0)])

        @pl.when(changed)
        def _():
            w1_ref[:, 0 * f:1 * f] = wga_ref[...].astype(BF16)
            w1_ref[:, 1 * f:2 * f] = wua_ref[...].astype(BF16)
            w1_ref[:, 2 * f:3 * f] = wgb_ref[...].astype(BF16)
            w1_ref[:, 3 * f:4 * f] = wub_ref[...].astype(BF16)
            w1_ref[:, 4 * f:] = wrp_ref[...].astype(BF16)
            w2_ref[0:f, :] = wda_ref[...].astype(BF16)
            w2_ref[f:2 * f, :] = wdb_ref[...].astype(BF16)

        _to_token_tiles(z_ref, 0, _layer_norm(alpha * xprev_ref[...] + moe_ref[...], ln2g_ref[...], ln2b_ref[...]))

        x = _from_token_tiles(x_ref, 0, TILE_M)
        gu = _dot(x.astype(BF16), w1_ref[...])
        scores = jax.nn.sigmoid(gu[:, 4 * f:])
        lane = lax.broadcasted_iota(I32, scores.shape, 1)
        sa = jnp.sum(jnp.where(lane == ea_ref[j], scores, 0.0), axis=-1, keepdims=True)
        sb = jnp.sum(jnp.where(lane == eb_ref[j], scores, 0.0), axis=-1, keepdims=True)
        tot = sa + sb
        ha = jax.nn.silu(gu[:, 0 * f:1 * f]) * gu[:, 1 * f:2 * f] * (sa / tot)
        hb = jax.nn.silu(gu[:, 2 * f:3 * f]) * gu[:, 3 * f:4 * f] * (sb / tot)
        xprev_ref[...] = x
        moe_ref[...] = _dot(jnp.concatenate([ha, hb], axis=1).astype(BF16), w2_ref[...])


def _pair_experts(xs, l, tidx, tcls, ea, eb, nused, wrp, w_gate, w_up, w_down, ln2g, ln2b, alpha):
    nsteps = tidx.shape[0]

    def wspec(shape, table):
        return pl.BlockSpec((None, None) + shape, lambda j, ti, tc, a, b, nu: (l, (a, b)[table][j], 0, 0))
    up = (D_MODEL, D_FF_EXPERT)
    dn = (D_FF_EXPERT, D_MODEL)
    cst = lambda j, *_: (0, 0)
    lsel = lambda j, *_: (l, 0, 0)
    return pl.pallas_call(
        functools.partial(_pair_kernel, alpha),
        grid_spec=pltpu.PrefetchScalarGridSpec(
            num_scalar_prefetch=5,
            grid=(nsteps,),
            in_specs=[pl.BlockSpec((TILE_M * TOK_ROWS, LANES), lambda j, ti, *_: (ti[j], 0)),
                      pl.BlockSpec(wrp.shape, cst),
                      wspec(up, 0), wspec(up, 0), wspec(dn, 0), wspec(up, 1), wspec(up, 1), wspec(dn, 1),
                      pl.BlockSpec((None,) + ln2g.shape[1:], lsel), pl.BlockSpec((None,) + ln2b.shape[1:], lsel)],
            out_specs=pl.BlockSpec((TILE_M * TOK_ROWS, LANES), lambda j, *_: (jnp.maximum(j - 1, 0), 0)),
            scratch_shapes=[pltpu.VMEM((D_MODEL, 4 * D_FF_EXPERT + LANES), BF16),
                            pltpu.VMEM((2 * D_FF_EXPERT, D_MODEL), BF16),
                            pltpu.VMEM((TILE_M, D_MODEL), F32), pltpu.VMEM((TILE_M, D_MODEL), F32)]),
        out_shape=jax.ShapeDtypeStruct(xs.shape, F32),
        compiler_params=pltpu.CompilerParams(dimension_semantics=("arbitrary",),
                                             vmem_limit_bytes=VMEM_LIMIT),
    )(tidx, tcls, ea, eb, nused, xs, wrp, w_gate, w_up, w_down, w_gate, w_up, w_down, ln2g, ln2b)


def _ungather_kernel(nb, pos8_ref, zs_hbm, out_ref, buf0, buf1, sems):
    rows = nb * CHUNK

    def consume(buf):
        out_ref[...] = _from_token_tiles(buf, 0, rows).reshape(nb, CHUNK, D_MODEL)
    _gathered_tokens(pl.program_id(0), pl.num_programs(0), pos8_ref, zs_hbm, (buf0, buf1), sems, rows, consume)


def _ungather(zs, pos8, nb, seq):
    rows = nb * CHUNK
    return pl.pallas_call(
        functools.partial(_ungather_kernel, nb),
        grid_spec=pltpu.PrefetchScalarGridSpec(
            num_scalar_prefetch=1,
            grid=(seq // CHUNK,),
            in_specs=[pl.BlockSpec(memory_space=pl.ANY)],
            out_specs=pl.BlockSpec((nb, CHUNK, D_MODEL), lambda i, *_: (0, i, 0)),
            scratch_shapes=[pltpu.VMEM((rows * TOK_ROWS, LANES), F32), pltpu.VMEM((rows * TOK_ROWS, LANES), F32),
                            pltpu.SemaphoreType.DMA((2,))]),
        out_shape=jax.ShapeDtypeStruct((nb, seq, D_MODEL), F32),
        compiler_params=pltpu.CompilerParams(dimension_semantics=("arbitrary",),
                                             vmem_limit_bytes=VMEM_LIMIT),
    )(pos8, zs)


def _plan(cls, rank, counts, nt):
    cnt = counts[:N_CLASSES, 0].astype(I32)
    ntile = (cnt + TILE_M - 1) // TILE_M
    padded = ntile * TILE_M
    off = jnp.cumsum(padded) - padded
    classes = jnp.arange(N_CLASSES, dtype=I32)
    pos8 = (rank.reshape(-1) + jnp.sum(jnp.where(cls.reshape(-1, 1) == classes, off, 0), axis=1)) * TOK_ROWS
    tile_end = jnp.cumsum(ntile)
    nused = tile_end[-1:].astype(I32)
    tidx = jnp.minimum(jnp.arange(nt + 1, dtype=I32), nused - 1)
    tsel = tile_end[None, :] <= tidx[:, None]
    tcls = jnp.sum(tsel.astype(I32), axis=1)
    onehot = tcls[:, None] == classes
    ea = jnp.sum(jnp.where(onehot, jnp.asarray(EA_TABLE), 0), axis=1)
    eb = jnp.sum(jnp.where(onehot, jnp.asarray(EB_TABLE), 0), axis=1)
    return pos8.astype(I32), off + cnt, padded - cnt, nused * TILE_M, tidx, tcls, ea, eb, nused


def _mixer_sample_kernel(alpha, x_ref, h0_ref, win_ref, wout_ref, lnvg_ref, lnvb_ref, ws0_ref, bs0_ref,
                         bdb_ref, bdc_ref, lr_ref, li_ref, dskip_ref, glu_ref, bglu_ref,
                         ln1g_ref, ln1b_ref,
                         x1_ref, hnew_ref, v_ref):
    x = x_ref[...]
    proj = _dot(x.astype(BF16), win_ref[...])
    u = jax.nn.gelu(proj[:, :W_A])
    v = _layer_norm(jax.nn.gelu(proj[:, W_A:2 * W_A]), lnvg_ref[...], lnvb_ref[...])
    v_ref[...] = v
    y_a = u * (ws0_ref[...] * v + bs0_ref[...])
    xs = proj[:, 2 * W_A:]
    zs = []
    for k in range(2):
        sl = slice(k * HALF_W, (k + 1) * HALF_W)
        bu = _dot(xs[:, sl].astype(BF16), bdb_ref[k])
        h0r = h0_ref[k, :, :HALF_STATE]
        h0i = h0_ref[k, :, HALF_STATE:]
        lr = lr_ref[k]
        li = li_ref[k]
        hr = lr * h0r - li * h0i + bu[:, :HALF_STATE]
        hi = lr * h0i + li * h0r + bu[:, HALF_STATE:]
        hnew_ref[k, :, :HALF_STATE] = hr
        hnew_ref[k, :, HALF_STATE:] = hi
        hcat = jnp.concatenate([hr, hi], axis=1).astype(BF16)
        y = jax.nn.gelu(_dot(hcat, bdc_ref[k]) + dskip_ref[:, sl] * xs[:, sl])
        gl = _dot(y.astype(BF16), glu_ref[k]) + bglu_ref[:, sl]
        zs.append(y * jax.nn.sigmoid(gl))
    cat = jnp.concatenate([y_a] + zs, axis=1).astype(BF16)
    mix = _dot(cat, wout_ref[...])
    x1_ref[...] = _layer_norm(alpha * x + mix, ln1g_ref[...], ln1b_ref[...])


def _mixer_sample(x, h0, lw, l, alpha):
    n = x.shape[0]
    full = lambda shape: pl.BlockSpec(shape, lambda i: (0,) * len(shape))
    return pl.pallas_call(
        functools.partial(_mixer_sample_kernel, alpha),
        grid=(1,),
        in_specs=[full(x.shape), _layer_spec(h0.shape, l)] + [_layer_spec(lw[k].shape, l) for k in SAMPLE_WEIGHTS],
        out_specs=[full((n, D_MODEL)), full((2, n, 2 * HALF_STATE)), full((n, W_A))],
        out_shape=[jax.ShapeDtypeStruct((n, D_MODEL), F32),
                   jax.ShapeDtypeStruct((2, n, 2 * HALF_STATE), F32),
                   jax.ShapeDtypeStruct((n, W_A), F32)],
        compiler_params=pltpu.CompilerParams(dimension_semantics=("arbitrary",), vmem_limit_bytes=VMEM_LIMIT),
    )(x, h0, *[lw[k] for k in SAMPLE_WEIGHTS])


def _route(x, wr, rbias):
    logits = jnp.dot(x, wr, preferred_element_type=F32, precision=HIGHEST)
    scores = jax.nn.sigmoid(logits)
    biased = scores + rbias
    lane = lax.broadcasted_iota(I32, biased.shape, 1)
    grp = lane // EXPERTS_PER_GROUP
    neg = jnp.float32(-jnp.inf)

    def top2(vals):
        m1 = jnp.max(vals, axis=-1, keepdims=True)
        i1 = jnp.min(jnp.where(vals == m1, lane, N_EXPERTS), axis=-1, keepdims=True)
        rest = jnp.where(lane == i1, neg, vals)
        m2 = jnp.max(rest, axis=-1, keepdims=True)
        i2 = jnp.min(jnp.where(rest == m2, lane, N_EXPERTS), axis=-1, keepdims=True)
        return m1, i1, m2, i2

    best = sel = None
    for g in range(N_EXPERT_GROUPS):
        m1, _, m2, _ = top2(jnp.where(grp == g, biased, neg))
        gs = m1 + m2
        if g == 0:
            best, sel = gs, jnp.zeros(gs.shape, I32)
        else:
            upd = gs > best
            sel = jnp.where(upd, g, sel)
            best = jnp.where(upd, gs, best)
    _, i1, _, i2 = top2(jnp.where(grp == sel, biased, neg))
    s1 = jnp.sum(jnp.where(lane == i1, scores, 0.0), axis=-1, keepdims=True)
    s2 = jnp.sum(jnp.where(lane == i2, scores, 0.0), axis=-1, keepdims=True)
    tot = s1 + s2
    return jnp.where(lane == i1, s1 / tot, 0.0) + jnp.where(lane == i2, s2 / tot, 0.0)


def _moe_kernel(alpha, x_ref, wr_ref, rb_ref, wg_ref, wu_ref, wd_ref, ln2g_ref, ln2b_ref,
                out_ref, xb_ref, comb_ref, acc_ref):
    step = pl.program_id(1)

    @pl.when(step == 0)
    def _():
        x = x_ref[...]
        xb_ref[...] = x.astype(BF16)
        comb_ref[...] = _route(x, wr_ref[...], rb_ref[...])
        acc_ref[...] = jnp.zeros_like(acc_ref)

    xb = xb_ref[...]
    comb = comb_ref[...]
    lane = lax.broadcasted_iota(I32, comb.shape, 1)
    acc = acc_ref[...]
    for k in range(DENSE_EXPERTS_PER_STEP):
        g = _dot(xb, wg_ref[k].astype(BF16))
        u = _dot(xb, wu_ref[k].astype(BF16))
        ce = jnp.sum(jnp.where(lane == step * DENSE_EXPERTS_PER_STEP + k, comb, 0.0), axis=-1, keepdims=True)
        h = (jax.nn.silu(g) * u * ce).astype(BF16)
        acc = acc + _dot(h, wd_ref[k].astype(BF16))
    acc_ref[...] = acc

    @pl.when(step == pl.num_programs(1) - 1)
    def _():
        out_ref[...] = _layer_norm(alpha * x_ref[...] + acc_ref[...], ln2g_ref[...], ln2b_ref[...])


def _moe_dense(x, l, wr, rb, w_gate, w_up, w_down, ln2g, ln2b, alpha, tm):
    t = x.shape[0]
    cst = lambda i, e: (0, 0)
    wsel = lambda i, e: (l, e, 0, 0)
    return pl.pallas_call(
        functools.partial(_moe_kernel, alpha),
        grid=(t // tm, N_EXPERTS // DENSE_EXPERTS_PER_STEP),
        in_specs=[pl.BlockSpec((tm, D_MODEL), lambda i, e: (i, 0)),
                  pl.BlockSpec(wr.shape, cst), pl.BlockSpec(rb.shape, cst),
                  pl.BlockSpec((None, DENSE_EXPERTS_PER_STEP, D_MODEL, D_FF_EXPERT), wsel),
                  pl.BlockSpec((None, DENSE_EXPERTS_PER_STEP, D_MODEL, D_FF_EXPERT), wsel),
                  pl.BlockSpec((None, DENSE_EXPERTS_PER_STEP, D_FF_EXPERT, D_MODEL), wsel),
                  pl.BlockSpec((None,) + ln2g.shape[1:], lambda i, e: (l, 0, 0)),
                  pl.BlockSpec((None,) + ln2b.shape[1:], lambda i, e: (l, 0, 0))],
        out_specs=pl.BlockSpec((tm, D_MODEL), lambda i, e: (i, 0)),
        out_shape=jax.ShapeDtypeStruct((t, D_MODEL), F32),
        scratch_shapes=[pltpu.VMEM((tm, D_MODEL), BF16),
                        pltpu.VMEM((tm, N_EXPERTS), F32),
                        pltpu.VMEM((tm, D_MODEL), F32)],
        compiler_params=pltpu.CompilerParams(dimension_semantics=("arbitrary", "arbitrary"),
                                             vmem_limit_bytes=VMEM_LIMIT),
    )(x, wr, rb, w_gate, w_up, w_down, ln2g, ln2b)


def _block_diag(blocks):
    eye = jnp.eye(HALF_GROUPS, dtype=blocks.dtype)
    k, g, a, b = blocks.shape
    return jnp.einsum("kgab,gh->kgahb", blocks, eye).reshape(k, g * a, g * b)


def _prep_s5_folded(lb_re, lb_im, bb_re, bb_im, c_re, c_im):
    d, g, n = lb_re.shape
    s = S5_FOLD
    pairs = g // 2
    pr, pi = [jnp.ones_like(lb_re)], [jnp.zeros_like(lb_re)]
    for _ in range(s):
        pr, pi = pr + [pr[-1] * lb_re - pi[-1] * lb_im], pi + [pr[-1] * lb_im + pi[-1] * lb_re]
    p_re, p_im = jnp.stack(pr, axis=2), jnp.stack(pi, axis=2)

    def pair_cols(re, im):
        z = jnp.zeros_like(re[:, :, 0])
        g0 = jnp.concatenate([re[:, :, 0], z, im[:, :, 0], z], axis=-1)
        g1 = jnp.concatenate([z, re[:, :, 1], z, im[:, :, 1]], axis=-1)
        return jnp.concatenate([g0, g1], axis=2)

    def pair_diag(a):
        z = jnp.zeros_like(a[:, :, 0])
        return jnp.concatenate([jnp.concatenate([a[:, :, 0], z], axis=-1),
                                jnp.concatenate([z, a[:, :, 1]], axis=-1)], axis=2)

    bt_re, bt_im = bb_re.transpose(0, 1, 3, 2)[:, :, None], bb_im.transpose(0, 1, 3, 2)[:, :, None]
    k_re, k_im = p_re[:, :, s - 1::-1, None, :], p_im[:, :, s - 1::-1, None, :]
    m1t = pair_cols((k_re * bt_re - k_im * bt_im).reshape(d, pairs, 2, -1, n),
                    (k_re * bt_im + k_im * bt_re).reshape(d, pairs, 2, -1, n))
    m1 = jnp.swapaxes(m1t, -1, -2)

    cq_re, cq_im = c_re[:, :, None], c_im[:, :, None]
    j_re, j_im = p_re[:, :, 1:, None, :], p_im[:, :, 1:, None, :]
    hpart = pair_cols((cq_re * j_re - cq_im * j_im).reshape(d, pairs, 2, -1, n),
                      (-(cq_re * j_im + cq_im * j_re)).reshape(d, pairs, 2, -1, n))

    t_re, t_im = p_re[:, :, :s, None, :], p_im[:, :, :s, None, :]
    kern = (jnp.einsum("dgtqn,dgnp->dgtqp", cq_re * t_re - cq_im * t_im, bb_re, precision=HIGHEST)
            - jnp.einsum("dgtqn,dgnp->dgtqp", cq_re * t_im + cq_im * t_re, bb_im, precision=HIGHEST))
    zero = jnp.zeros_like(kern[:, :, 0])
    kx = jnp.stack([jnp.concatenate([kern[:, :, j - q] if q <= j else zero for q in range(s)], axis=-1)
                    for j in range(s)], axis=2)
    apart = pair_diag(kx.reshape(d, pairs, 2, -1, kx.shape[-1]))

    m2 = jnp.concatenate([hpart, apart], axis=-1)
    return (m1.astype(BF16), m2.astype(BF16), p_re[:, :, s].reshape(d, 1, g * n), p_im[:, :, s].reshape(d, 1, g * n))


def _prep_all(w_in, w_out, ln_v_g, ln_v_b, w_s, b_s, a_re, a_im, log_dt, b_re, b_im, c_re, c_im,
              d_skip, w_glu, b_glu, ln1_g, ln1_b):
    d = w_in.shape[0]
    dt = jnp.exp(log_dt)[..., None]
    decay = jnp.exp(a_re * dt)
    lb_re, lb_im = decay * jnp.cos(a_im * dt), decay * jnp.sin(a_im * dt)
    den = a_re * a_re + a_im * a_im
    nr, ni = lb_re - 1.0, lb_im
    zr = (nr * a_re + ni * a_im) / den
    zi = (ni * a_re - nr * a_im) / den
    bb_re = zr[..., None] * b_re - zi[..., None] * b_im
    bb_im = zr[..., None] * b_im + zi[..., None] * b_re

    def bd(a):
        blocks = _block_diag(a.reshape((d * 2, HALF_GROUPS) + a.shape[2:]))
        return blocks.reshape((d, 2) + blocks.shape[1:])

    bdb = jnp.concatenate([bd(bb_re.transpose(0, 1, 3, 2)), bd(bb_im.transpose(0, 1, 3, 2))], axis=3)
    bdc = jnp.concatenate([bd(c_re.transpose(0, 1, 3, 2)), bd(-c_im.transpose(0, 1, 3, 2))], axis=2)
    m1, m2, lr8, li8 = _prep_s5_folded(lb_re, lb_im, bb_re, bb_im, c_re, c_im)
    return dict(
        m1=m1, m2=m2, lr8=lr8, li8=li8,
        win=w_in.astype(BF16), wout=w_out.astype(BF16),
        lnvg=ln_v_g[:, None], lnvb=ln_v_b[:, None],
        wtril=jnp.tril(w_s).astype(BF16),
        bsb=jnp.broadcast_to(b_s[..., None], (d, H_A, CHUNK, LANES)),
        ws0=jnp.repeat(w_s[:, :, 0, 0], P_A, axis=1)[:, None], bs0=jnp.repeat(b_s[:, :, 0], P_A, axis=1)[:, None],
        bdb=bdb.astype(BF16), bdc=bdc.astype(BF16),
        lr=lb_re.reshape(d, 2, 1, HALF_STATE), li=lb_im.reshape(d, 2, 1, HALF_STATE),
        dskip=d_skip.reshape(d, 1, W_B), glu=bd(w_glu).astype(BF16),
        bglu=b_glu.reshape(d, 1, W_B), ln1g=ln1_g[:, None], ln1b=ln1_b[:, None])


def _state_to_cols(h_re, h_im):
    d, b = h_re.shape[:2]
    re = h_re.reshape(d, b, 2, HALF_STATE)
    im = h_im.reshape(d, b, 2, HALF_STATE)
    return jnp.concatenate([re, im], axis=3).transpose(0, 2, 1, 3)


def _cols_to_state(h):
    d, _, b, _ = h.shape
    re = h[..., :HALF_STATE].transpose(0, 2, 1, 3).reshape(d, b, G_B, N_STATE)
    im = h[..., HALF_STATE:].transpose(0, 2, 1, 3).reshape(d, b, G_B, N_STATE)
    return re, im


def _pairs_to_state(h):
    d, b, _ = h.shape
    h = h.reshape(d, b, G_B // 2, 2, 2, N_STATE)
    return h[:, :, :, 0].reshape(d, b, G_B, N_STATE), h[:, :, :, 1].reshape(d, b, G_B, N_STATE)


def kernel(x_prompt, x_sample, state_ssm_re, state_ssm_im, w_in, w_out, ln_v_g, ln_v_b, w_s, b_s, ssm_a_re, ssm_a_im, ssm_log_dt, ssm_b_re, ssm_b_im, ssm_c_re, ssm_c_im, ssm_d, w_glu, b_glu, ln1_g, ln1_b, ln2_g, ln2_b, w_router, router_bias, w_gate, w_up, w_down):
    depth = w_in.shape[0]
    alpha = float((2 * depth) ** 0.25)
    nb, seq, _ = x_prompt.shape
    ns = x_sample.shape[0]
    tokens = nb * seq
    nt = tokens // TILE_M + N_CLASSES
    rb = router_bias[None]
    wrt = w_router.T
    wrp = jnp.pad(w_router, ((0, 0), (0, LANES - N_EXPERTS)))
    rbcol = router_bias[:, None]
    tok = np.arange(nb * CHUNK)
    tri = jnp.asarray(tok[:, None] < tok[None, :], BF16)
    lw = _prep_all(w_in, w_out, ln_v_g, ln_v_b, w_s, b_s, ssm_a_re, ssm_a_im, ssm_log_dt,
                   ssm_b_re, ssm_b_im, ssm_c_re, ssm_c_im, ssm_d, w_glu, b_glu, ln1_g, ln1_b)
    shared = (wrt, rbcol, tri)
    ln2g, ln2b = ln2_g[:, None], ln2_b[:, None]
    h0s = _state_to_cols(state_ssm_re, state_ssm_im)
    xp = x_prompt
    pos = None
    xs = x_sample.reshape(ns, D_MODEL)
    pr_h, sm_h, sm_v = [], [], []
    for l in range(depth):
        x1t, hfin, cls, rank, counts = _mixer_prompt(xp, lw, shared, l, alpha, nb, seq, pos)
        pos, zstart, zlen, tail, tidx, tcls, ea, eb, nused = _plan(cls, rank, counts, nt)
        x_sorted = _dispatch(x1t, pos, zstart, zlen, tail, nb * CHUNK, nt * TILE_M)
        xp = _pair_experts(x_sorted, l, tidx, tcls, ea, eb, nused, wrp, w_gate, w_up, w_down,
                           ln2g, ln2b, alpha)
        pr_h.append(hfin)

        x1s, hnew, v_new = _mixer_sample(xs, h0s, lw, l, alpha)
        xs = _moe_dense(x1s, l, w_router, rb, w_gate, w_up, w_down, ln2g, ln2b, alpha, tm=ns)
        sm_h.append(hnew)
        sm_v.append(v_new.reshape(ns, 1, W_A))
    y_prompt = _ungather(xp, pos, nb, seq)
    pr_re, pr_im = _pairs_to_state(jnp.stack(pr_h))
    sm_re, sm_im = _cols_to_state(jnp.stack(sm_h))
    return (y_prompt, xs.reshape(ns, 1, D_MODEL), pr_re, pr_im, sm_re, sm_im, jnp.stack(sm_v))
```

```python
import functools

import jax
import jax.numpy as jnp
import numpy as np
from jax import lax
from jax.experimental import pallas as pl
from jax.experimental.pallas import tpu as pltpu

D_MODEL = 1024
W_A = 512
W_B = 512
CHUNK = 128
H_A = 4
P_A = W_A // H_A
GROUP_B = 16
G_B = W_B // GROUP_B
N_STATE = 64
N_EXPERTS = 16
N_EXPERT_GROUPS = 4
EXPERTS_PER_GROUP = N_EXPERTS // N_EXPERT_GROUPS
D_FF_EXPERT = D_MODEL // 4
LN_EPS = 1e-5

LANES = 128
SUBLANES = 8
HALF_GROUPS = 16
HALF_W = HALF_GROUPS * GROUP_B
HALF_STATE = HALF_GROUPS * N_STATE
VMEM_LIMIT = 56 * 1024 * 1024

PAIRS = ((0, 1), (0, 2), (0, 3), (1, 2), (1, 3), (2, 3))
N_CLASSES = N_EXPERT_GROUPS * len(PAIRS)
CLASS_ROWS = 32
TILE_M = 256
PAIR_SUBTILES = 1
TOK_ROWS = D_MODEL // LANES
ZERO_TOKENS = TILE_M // 2
DMA_UNROLL = 8
DENSE_EXPERTS_PER_STEP = 4
S5_FOLD = 8
S5_PAIRS = G_B // 2
EA_TABLE = np.array([EXPERTS_PER_GROUP * g + a for g in range(N_EXPERT_GROUPS) for a, _ in PAIRS], np.int32)
EB_TABLE = np.array([EXPERTS_PER_GROUP * g + b for g in range(N_EXPERT_GROUPS) for _, b in PAIRS], np.int32)

F32 = jnp.float32
BF16 = jnp.bfloat16
I32 = jnp.int32
HIGHEST = lax.Precision.HIGHEST


def _layer_norm(x, g, b):
    mu = jnp.mean(x, axis=-1, keepdims=True)
    xc = x - mu
    var = jnp.mean(xc * xc, axis=-1, keepdims=True)
    return xc * lax.rsqrt(var + LN_EPS) * g + b


def _dot(a, b):
    return jnp.dot(a, b, preferred_element_type=F32)


def _route_classes(x1, wrt, rbcol):
    def split(a):
        hi = a.astype(BF16)
        return hi, (a - hi.astype(F32)).astype(BF16)

    def dot_t(a, b):
        return lax.dot_general(a, b, (((1,), (1,)), ((), ())), preferred_element_type=F32)
    w_hi, w_lo = split(wrt)
    x_hi, x_lo = split(x1)
    logits_t = dot_t(w_hi, x_hi) + (dot_t(w_hi, x_lo) + dot_t(w_lo, x_hi))
    biased = jax.nn.sigmoid(logits_t) + rbcol
    rows = [biased[e:e + 1, :] for e in range(N_EXPERTS)]
    n = EXPERTS_PER_GROUP

    best = sel = None
    for g in range(N_EXPERT_GROUPS):
        v = rows[n * g:n * (g + 1)]
        gs = None
        for a, b in PAIRS:
            s = v[a] + v[b]
            gs = s if gs is None else jnp.maximum(gs, s)
        if g == 0:
            best, sel = gs, jnp.zeros(gs.shape, I32)
        else:
            upd = gs > best
            sel = jnp.where(upd, g, sel)
            best = jnp.where(upd, gs, best)

    cls = jnp.zeros(sel.shape, I32)
    for g in range(N_EXPERT_GROUPS):
        v = rows[n * g:n * (g + 1)]
        lo = jnp.full(sel.shape, n, I32)
        hi = jnp.full(sel.shape, -1, I32)
        for i in range(n):
            before = jnp.zeros(sel.shape, I32)
            for j in range(n):
                if j < i:
                    before = before + (v[j] >= v[i]).astype(I32)
                elif j > i:
                    before = before + (v[j] > v[i]).astype(I32)
            member = before < 2
            lo = jnp.where(member, jnp.minimum(lo, i), lo)
            hi = jnp.where(member, jnp.maximum(hi, i), hi)
        base = jnp.where(lo == 0, 0, jnp.where(lo == 1, 3, 5))
        cls = jnp.where(sel == g, g * len(PAIRS) + base + (hi - lo - 1), cls)
    return cls


def _to_token_tiles(ref, row0, x):
    n = x.shape[0]
    for c in range(TOK_ROWS):
        ref[pl.ds(row0 * TOK_ROWS + c, n, stride=TOK_ROWS), :] = x[:, c * LANES:(c + 1) * LANES]


def _from_token_tiles(ref, row0, n):
    return jnp.concatenate(
        [ref[pl.ds(row0 * TOK_ROWS + c, n, stride=TOK_ROWS), :] for c in range(TOK_ROWS)], axis=1)


def _gathered_tokens(step, nsteps, pos8_ref, src_hbm, bufs, sems, rows, consume):
    def start(s, base):
        def body(g, c):
            for i in range(DMA_UNROLL):
                r = g * DMA_UNROLL + i
                p8 = pl.multiple_of(pos8_ref[base + r], TOK_ROWS)
                pltpu.make_async_copy(src_hbm.at[pl.ds(p8, TOK_ROWS), :],
                                      bufs[s].at[pl.ds(pl.multiple_of(r * TOK_ROWS, TOK_ROWS), TOK_ROWS), :],
                                      sems.at[s]).start(priority=i % 2)
            return c
        lax.fori_loop(0, rows // DMA_UNROLL, body, 0)

    @pl.when(step == 0)
    def _():
        start(0, 0)

    for s in range(2):
        @pl.when(lax.rem(step, 2) == s)
        def _(s=s):
            pltpu.make_async_copy(src_hbm.at[pl.ds(0, rows * TOK_ROWS), :], bufs[s], sems.at[s]).wait()

            @pl.when(step + 1 < nsteps)
            def _():
                start(1 - s, (step + 1) * rows)
            consume(bufs[s])


def _mixer_kernel(alpha, nb, gather_in, *refs):
    if gather_in:
        pos8_ref, zs_hbm = refs[:2]
        refs = refs[2:]
    else:
        x_ref = refs[0]
        refs = refs[1:]
    (win_ref, wout_ref, lnvg_ref, lnvb_ref, wtril_ref, bsb_ref, m1_ref, m2_ref, lr8_ref, li8_ref,
     dskip_ref, glu_ref, bglu_ref, ln1g_ref, ln1b_ref, wrt_ref, rbcol_ref, tri_ref,
     x1t_ref, hfin_ref, cls_ref, rank_ref, cnt_ref,
     xb_ref, xs_slab, xs_scb, xst_ref, ht_ref, yt_ref, mix_ref, hstate_ref, carry_ref) = refs[:32]
    lt = CHUNK
    rows = nb * lt
    pitch = lt + SUBLANES
    step = pl.program_id(0)
    nsteps = pl.num_programs(0)

    @pl.when(step == 0)
    def _():
        hstate_ref[...] = jnp.zeros_like(hstate_ref)
        carry_ref[...] = jnp.zeros_like(carry_ref)

    if gather_in:
        xin_ref, xbuf0, xbuf1, gsem = refs[32:36]

        def consume(buf):
            xin_ref[...] = _from_token_tiles(buf, 0, rows)
        _gathered_tokens(step, nsteps, pos8_ref, zs_hbm, (xbuf0, xbuf1), gsem, rows, consume)

        def load_x():
            return xin_ref[...]
    else:
        def load_x():
            return x_ref[...].reshape(rows, D_MODEL)

    xb_ref[...] = load_x().astype(BF16)

    fold = S5_FOLD
    nchunk = lt // fold
    cb = nchunk * nb
    xs = _dot(xb_ref[...], win_ref[:, 2 * W_A:])
    for j in range(W_B // LANES):
        for b in range(nb):
            xs_slab[j, b * pitch:b * pitch + lt, :] = xs[b * lt:(b + 1) * lt, j * LANES:(j + 1) * LANES]

    def to_scb(t, c):
        s_, ch_ = lax.bitwise_and(t, fold - 1), lax.shift_right_logical(t, fold.bit_length() - 1)
        r0 = pl.multiple_of(s_ * cb + ch_ * nb, SUBLANES)
        for j in range(W_B // LANES):
            xs_scb[pl.ds(r0, nb), j * LANES:(j + 1) * LANES] = xs_slab[j, pl.ds(t, nb, stride=pitch), :]
        return c
    lax.fori_loop(0, lt, to_scb, 0, unroll=4)
    xst_ref[...] = xs_scb[...].T.astype(BF16)

    def chunk_inputs(pr):
        return jnp.concatenate(
            [xst_ref[(2 * pr + gi) * GROUP_B:(2 * pr + gi + 1) * GROUP_B, s * cb:(s + 1) * cb]
             for gi in range(2) for s in range(fold)], axis=0)

    pw = 2 * 2 * N_STATE
    for pr in range(S5_PAIRS):
        ht_ref[:, pr * pw:(pr + 1) * pw] = _dot(m1_ref[pr], chunk_inputs(pr)).T

    half_pairs = S5_PAIRS // 2
    for half in range(2):
        c0 = half * half_pairs * pw
        lr8 = [jnp.broadcast_to(lr8_ref[:, (half * half_pairs + p) * LANES:(half * half_pairs + p + 1) * LANES],
                                (nb, LANES)) for p in range(half_pairs)]
        li8 = [jnp.broadcast_to(li8_ref[:, (half * half_pairs + p) * LANES:(half * half_pairs + p + 1) * LANES],
                                (nb, LANES)) for p in range(half_pairs)]
        h = [hstate_ref[:, c0 + q * LANES:c0 + (q + 1) * LANES] for q in range(2 * half_pairs)]
        for c in range(nchunk):
            rs = slice(c * nb, (c + 1) * nb)
            for p in range(half_pairs):
                re_sl = slice(c0 + p * pw, c0 + p * pw + LANES)
                im_sl = slice(c0 + p * pw + LANES, c0 + (p + 1) * pw)
                hr, hi = h[2 * p], h[2 * p + 1]
                ur, ui = ht_ref[rs, re_sl], ht_ref[rs, im_sl]
                ht_ref[rs, re_sl] = hr
                ht_ref[rs, im_sl] = hi
                h[2 * p] = lr8[p] * hr - li8[p] * hi + ur
                h[2 * p + 1] = lr8[p] * hi + li8[p] * hr + ui
        for q in range(2 * half_pairs):
            hstate_ref[:, c0 + q * LANES:c0 + (q + 1) * LANES] = h[q]
    hfin_ref[...] = hstate_ref[...]

    for pr in range(S5_PAIRS):
        rhs = jnp.concatenate([ht_ref[:, pr * pw:(pr + 1) * pw].T.astype(BF16), chunk_inputs(pr)], axis=0)
        yt = _dot(m2_ref[pr], rhs)
        for gi in range(2):
            for s in range(fold):
                r0 = (gi * fold + s) * GROUP_B
                yt_ref[(2 * pr + gi) * GROUP_B:(2 * pr + gi + 1) * GROUP_B, s * cb:(s + 1) * cb] = (
                    yt[r0:r0 + GROUP_B, :])

    y = jax.nn.gelu(yt_ref[...].T + dskip_ref[...] * xs_scb[...])
    yb = y.astype(BF16)
    spitch = cb + SUBLANES
    for k in range(2):
        sl = slice(k * HALF_W, (k + 1) * HALF_W)
        gl = _dot(yb[:, sl], glu_ref[k]) + bglu_ref[:, sl]
        z = y[:, sl] * jax.nn.sigmoid(gl)
        for j in range(HALF_W // LANES):
            for s in range(fold):
                xs_slab[k * (HALF_W // LANES) + j, s * spitch:s * spitch + cb, :] = (
                    z[s * cb:(s + 1) * cb, j * LANES:(j + 1) * LANES])

    def to_bt(ch, c):
        for b in range(nb):
            dst = pl.multiple_of(b * lt + ch * fold, SUBLANES)
            for j in range(W_B // LANES):
                mix_ref[pl.ds(dst, fold), W_A + j * LANES:W_A + (j + 1) * LANES] = (
                    xs_slab[j, pl.ds(ch * nb + b, fold, stride=spitch), :])
        return c
    lax.fori_loop(0, nchunk, to_bt, 0)

    vg = jax.nn.gelu(_dot(xb_ref[...], win_ref[:, W_A:2 * W_A]))
    v = _layer_norm(vg, lnvg_ref[...], lnvb_ref[...]).astype(BF16)
    u = jax.nn.gelu(_dot(xb_ref[...], win_ref[:, :W_A]))
    for h in range(H_A):
        hs = slice(h * P_A, (h + 1) * P_A)
        vcat = jnp.concatenate([v[b * lt:(b + 1) * lt, hs] for b in range(nb)], axis=1)
        o = _dot(wtril_ref[h], vcat)
        for b in range(nb):
            rs = slice(b * lt, (b + 1) * lt)
            mix_ref[rs, hs] = u[rs, hs] * (o[:, b * LANES:(b + 1) * LANES] + bsb_ref[h])

    mix = _dot(mix_ref[...].astype(BF16), wout_ref[...])
    x1 = _layer_norm(alpha * load_x() + mix, ln1g_ref[...], ln1b_ref[...])
    _to_token_tiles(x1t_ref, 0, x1)

    cls = _route_classes(x1, wrt_ref[...], rbcol_ref[...])
    crow = lax.broadcasted_iota(I32, (CLASS_ROWS, rows), 0)
    onehot = jnp.where(crow == cls, 1.0, 0.0)
    prefix = _dot(onehot.astype(BF16), tri_ref[...])
    carry = carry_ref[:, 0:1]
    rank = jnp.sum(onehot * (prefix + carry), axis=0, keepdims=True)
    cls_ref[...] = cls.reshape(1, 1, rows)
    rank_ref[...] = rank.astype(I32).reshape(1, 1, rows)
    carry_ref[...] = carry_ref[...] + jnp.sum(onehot, axis=1, keepdims=True)
    cnt_ref[...] = carry_ref[...]


def _const_spec(shape):
    nd = len(shape)
    return pl.BlockSpec(shape, lambda *_: (0,) * nd, pipeline_mode=pl.Buffered(1))


def _layer_spec(shape, l):
    nd = len(shape)
    return pl.BlockSpec((None,) + tuple(shape[1:]), lambda *_: (l,) + (0,) * (nd - 1),
                        pipeline_mode=pl.Buffered(1))


MIXER_WEIGHTS = ("win", "wout", "lnvg", "lnvb", "wtril", "bsb", "m1", "m2", "lr8", "li8", "dskip", "glu",
                 "bglu", "ln1g", "ln1b")
SAMPLE_WEIGHTS = ("win", "wout", "lnvg", "lnvb", "ws0", "bs0", "bdb", "bdc", "lr", "li", "dskip", "glu",
                  "bglu", "ln1g", "ln1b")


def _mixer_prompt(x, lw, shared, l, alpha, nb, seq, pos_prev=None):
    lt = CHUNK
    rows = nb * lt
    nsteps = seq // lt
    gather_in = pos_prev is not None
    weights = tuple(lw[k] for k in MIXER_WEIGHTS) + tuple(shared)
    wspecs = [_layer_spec(lw[k].shape, l) for k in MIXER_WEIGHTS] + [_const_spec(w.shape) for w in shared]
    if gather_in:
        x_spec = pl.BlockSpec(memory_space=pl.ANY)
    else:
        x_spec = pl.BlockSpec((nb, lt, D_MODEL), lambda i, *_: (0, i, 0))
    scratch = [
        pltpu.VMEM((rows, D_MODEL), BF16),
        pltpu.VMEM((W_B // LANES, nb * (lt + SUBLANES), LANES), F32),
        pltpu.VMEM((rows, W_B), F32),
        pltpu.VMEM((W_B, rows), BF16),
        pltpu.VMEM((rows // S5_FOLD, 2 * G_B * N_STATE), F32),
        pltpu.VMEM((W_B, rows), F32),
        pltpu.VMEM((rows, D_MODEL), F32),
        pltpu.VMEM((nb, 2 * G_B * N_STATE), F32),
        pltpu.VMEM((CLASS_ROWS, LANES), F32),
    ]
    if gather_in:
        scratch += [pltpu.VMEM((rows, D_MODEL), F32),
                    pltpu.VMEM((rows * TOK_ROWS, LANES), F32), pltpu.VMEM((rows * TOK_ROWS, LANES), F32),
                    pltpu.SemaphoreType.DMA((2,))]
    grid_spec = pltpu.PrefetchScalarGridSpec(
        num_scalar_prefetch=1 if gather_in else 0,
        grid=(nsteps,),
        in_specs=[x_spec] + wspecs,
        out_specs=[pl.BlockSpec((rows * TOK_ROWS, LANES), lambda i, *_: (i, 0)),
                   pl.BlockSpec((nb, 2 * G_B * N_STATE), lambda i, *_: (0, 0)),
                   pl.BlockSpec((1, 1, rows), lambda i, *_: (i, 0, 0)),
                   pl.BlockSpec((1, 1, rows), lambda i, *_: (i, 0, 0)),
                   pl.BlockSpec((CLASS_ROWS, LANES), lambda i, *_: (0, 0))],
        scratch_shapes=scratch)
    args = ((pos_prev, x) if gather_in else (x,)) + weights
    return pl.pallas_call(
        functools.partial(_mixer_kernel, alpha, nb, gather_in),
        grid_spec=grid_spec,
        out_shape=[jax.ShapeDtypeStruct((nb * seq * TOK_ROWS, LANES), F32),
                   jax.ShapeDtypeStruct((nb, 2 * G_B * N_STATE), F32),
                   jax.ShapeDtypeStruct((nsteps, 1, rows), I32),
                   jax.ShapeDtypeStruct((nsteps, 1, rows), I32),
                   jax.ShapeDtypeStruct((CLASS_ROWS, LANES), F32)],
        compiler_params=pltpu.CompilerParams(dimension_semantics=("arbitrary",),
                                             vmem_limit_bytes=VMEM_LIMIT),
    )(*args)


def _dispatch_kernel(rows, pos8_ref, zstart_ref, zlen_ref, tail_ref, x_ref, xs_hbm, zero_ref, sem, zsem):
    step = pl.program_id(0)
    ztok = ZERO_TOKENS

    @pl.when(step == 0)
    def _():
        zero_ref[...] = jnp.zeros_like(zero_ref)
        pieces = []
        for c in range(N_CLASSES):
            start = zstart_ref[c]
            zlen = zlen_ref[c]
            p = TILE_M // 2
            while p >= 1:
                hit = (zlen & p) != 0
                pieces.append((hit, pltpu.make_async_copy(
                    zero_ref.at[pl.ds(0, p * TOK_ROWS), :],
                    xs_hbm.at[pl.ds(pl.multiple_of(start * TOK_ROWS, TOK_ROWS), p * TOK_ROWS), :], zsem)))
                start = start + jnp.where(hit, p, 0)
                p //= 2
        for hit, cp in pieces:
            pl.when(hit)(cp.start)
        for hit, cp in pieces:
            pl.when(hit)(cp.wait)

        zrows = ztok * TOK_ROWS
        first = tail_ref[0] // ztok

        def tail_copy(q):
            return pltpu.make_async_copy(
                zero_ref, xs_hbm.at[pl.ds(pl.multiple_of(q * zrows, zrows), zrows), :], zsem)

        def tail_start(q, c):
            tail_copy(q).start()
            return c

        def tail_wait(q, c):
            tail_copy(q).wait()
            return c
        lax.fori_loop(first, xs_hbm.shape[0] // zrows, tail_start, 0)
        lax.fori_loop(first, xs_hbm.shape[0] // zrows, tail_wait, 0)

    base = step * rows

    def body(g, c):
        for i in range(DMA_UNROLL):
            r = g * DMA_UNROLL + i
            p8 = pl.multiple_of(pos8_ref[base + r], TOK_ROWS)
            pltpu.make_async_copy(x_ref.at[pl.ds(pl.multiple_of(r * TOK_ROWS, TOK_ROWS), TOK_ROWS), :],
                                  xs_hbm.at[pl.ds(p8, TOK_ROWS), :], sem).start(priority=i % 2)
        return c
    lax.fori_loop(0, rows // DMA_UNROLL, body, 0)
    pltpu.make_async_copy(x_ref, xs_hbm.at[pl.ds(0, rows * TOK_ROWS), :], sem).wait()


def _dispatch(x1t, pos8, zstart, zlen, tail, rows, ns_tokens):
    return pl.pallas_call(
        functools.partial(_dispatch_kernel, rows),
        grid_spec=pltpu.PrefetchScalarGridSpec(
            num_scalar_prefetch=4,
            grid=(x1t.shape[0] // (rows * TOK_ROWS),),
            in_specs=[pl.BlockSpec((rows * TOK_ROWS, LANES), lambda i, *_: (i, 0))],
            out_specs=pl.BlockSpec(memory_space=pl.ANY),
            scratch_shapes=[pltpu.VMEM((ZERO_TOKENS * TOK_ROWS, LANES), F32),
                            pltpu.SemaphoreType.DMA(()), pltpu.SemaphoreType.DMA(())]),
        out_shape=jax.ShapeDtypeStruct((ns_tokens * TOK_ROWS, LANES), F32),
        compiler_params=pltpu.CompilerParams(dimension_semantics=("arbitrary",),
                                             vmem_limit_bytes=VMEM_LIMIT),
    )(pos8, zstart, zlen, tail, x1t)


def _pair_kernel(alpha, l, tidx_ref, tcls_ref, ea_ref, eb_ref, nused_ref, slot_ref, ean_ref, ebn_ref, hasn_ref,
                 x_ref, wrp_ref, wg_hbm, wu_hbm, wd_hbm,
                 ln2g_ref, ln2b_ref, z_ref, w1_ref, w2_ref, xprev_ref, moe_ref, sg_ref, su_ref, sd_ref, wsem):
    j = pl.program_id(0)
    f = D_FF_EXPERT
    nused = nused_ref[0]

    def weight_copies(e_a, e_b, slot):
        cps = []
        for k, e in enumerate((e_a, e_b)):
            cps += [pltpu.make_async_copy(wg_hbm.at[l, e], sg_ref.at[slot, k], wsem.at[slot]),
                    pltpu.make_async_copy(wu_hbm.at[l, e], su_ref.at[slot, k], wsem.at[slot]),
                    pltpu.make_async_copy(wd_hbm.at[l, e], sd_ref.at[slot, k], wsem.at[slot])]
        return cps

    @pl.when(j == 0)
    def _():
        xprev_ref[...] = jnp.zeros_like(xprev_ref)
        moe_ref[...] = jnp.zeros_like(moe_ref)
        for cp in weight_copies(ea_ref[0], eb_ref[0], slot_ref[0]):
            cp.start()

    @pl.when(j > nused)
    def _():
        z_ref[...] = jnp.zeros_like(z_ref)

    @pl.when(j <= nused)
    def _():
        changed = jnp.logical_or(j == 0, tcls_ref[j] != tcls_ref[jnp.maximum(j - 1, 0)])

        @pl.when(changed)
        def _():
            slot = slot_ref[j]
            for cp in weight_copies(ea_ref[j], eb_ref[j], slot):
                cp.wait()

            @pl.when(hasn_ref[j] != 0)
            def _():
                for cp in weight_copies(ean_ref[j], ebn_ref[j], 1 - slot):
                    cp.start()
            w1_ref[:, 0 * f:1 * f] = sg_ref[slot, 0].astype(BF16)
            w1_ref[:, 1 * f:2 * f] = su_ref[slot, 0].astype(BF16)
            w1_ref[:, 2 * f:3 * f] = sg_ref[slot, 1].astype(BF16)
            w1_ref[:, 3 * f:4 * f] = su_ref[slot, 1].astype(BF16)
            w1_ref[:, 4 * f:] = wrp_ref[...].astype(BF16)
            w2_ref[0:f, :] = sd_ref[slot, 0].astype(BF16)
            w2_ref[f:2 * f, :] = sd_ref[slot, 1].astype(BF16)

        _to_token_tiles(z_ref, 0, _layer_norm(alpha * xprev_ref[...] + moe_ref[...], ln2g_ref[...], ln2b_ref[...]))

        x = _from_token_tiles(x_ref, 0, TILE_M)
        gu = _dot(x.astype(BF16), w1_ref[...])
        scores = jax.nn.sigmoid(gu[:, 4 * f:])
        lane = lax.broadcasted_iota(I32, scores.shape, 1)
        sa = jnp.sum(jnp.where(lane == ea_ref[j], scores, 0.0), axis=-1, keepdims=True)
        sb = jnp.sum(jnp.where(lane == eb_ref[j], scores, 0.0), axis=-1, keepdims=True)
        tot = sa + sb
        ha = jax.nn.silu(gu[:, 0 * f:1 * f]) * gu[:, 1 * f:2 * f] * (sa / tot)
        hb = jax.nn.silu(gu[:, 2 * f:3 * f]) * gu[:, 3 * f:4 * f] * (sb / tot)
        xprev_ref[...] = x
        moe_ref[...] = _dot(jnp.concatenate([ha, hb], axis=1).astype(BF16), w2_ref[...])


def _pair_experts(xs, l, tables, wrp, w_gate, w_up, w_down, ln2g, ln2b, alpha):
    nsteps = tables[0].shape[0]
    cst = lambda j, *_: (0, 0)
    lsel = lambda j, *_: (l, 0, 0)
    hbm = pl.BlockSpec(memory_space=pl.ANY)
    return pl.pallas_call(
        functools.partial(_pair_kernel, alpha, l),
        grid_spec=pltpu.PrefetchScalarGridSpec(
            num_scalar_prefetch=len(tables),
            grid=(nsteps,),
            in_specs=[pl.BlockSpec((TILE_M * TOK_ROWS, LANES), lambda j, ti, *_: (ti[j], 0)),
                      pl.BlockSpec(wrp.shape, cst), hbm, hbm, hbm,
                      pl.BlockSpec((None,) + ln2g.shape[1:], lsel), pl.BlockSpec((None,) + ln2b.shape[1:], lsel)],
            out_specs=pl.BlockSpec((TILE_M * TOK_ROWS, LANES), lambda j, *_: (jnp.maximum(j - 1, 0), 0)),
            scratch_shapes=[pltpu.VMEM((D_MODEL, 4 * D_FF_EXPERT + LANES), BF16),
                            pltpu.VMEM((2 * D_FF_EXPERT, D_MODEL), BF16),
                            pltpu.VMEM((TILE_M, D_MODEL), F32), pltpu.VMEM((TILE_M, D_MODEL), F32),
                            pltpu.VMEM((2, 2, D_MODEL, D_FF_EXPERT), F32),
                            pltpu.VMEM((2, 2, D_MODEL, D_FF_EXPERT), F32),
                            pltpu.VMEM((2, 2, D_FF_EXPERT, D_MODEL), F32),
                            pltpu.SemaphoreType.DMA((2,))]),
        out_shape=jax.ShapeDtypeStruct(xs.shape, F32),
        compiler_params=pltpu.CompilerParams(dimension_semantics=("arbitrary",),
                                             vmem_limit_bytes=VMEM_LIMIT),
    )(*tables, xs, wrp, w_gate, w_up, w_down, ln2g, ln2b)


def _ungather_kernel(nb, pos8_ref, zs_hbm, out_ref, buf0, buf1, sems):
    rows = nb * CHUNK

    def consume(buf):
        out_ref[...] = _from_token_tiles(buf, 0, rows).reshape(nb, CHUNK, D_MODEL)
    _gathered_tokens(pl.program_id(0), pl.num_programs(0), pos8_ref, zs_hbm, (buf0, buf1), sems, rows, consume)


def _ungather(zs, pos8, nb, seq):
    rows = nb * CHUNK
    return pl.pallas_call(
        functools.partial(_ungather_kernel, nb),
        grid_spec=pltpu.PrefetchScalarGridSpec(
            num_scalar_prefetch=1,
            grid=(seq // CHUNK,),
            in_specs=[pl.BlockSpec(memory_space=pl.ANY)],
            out_specs=pl.BlockSpec((nb, CHUNK, D_MODEL), lambda i, *_: (0, i, 0)),
            scratch_shapes=[pltpu.VMEM((rows * TOK_ROWS, LANES), F32), pltpu.VMEM((rows * TOK_ROWS, LANES), F32),
                            pltpu.SemaphoreType.DMA((2,))]),
        out_shape=jax.ShapeDtypeStruct((nb, seq, D_MODEL), F32),
        compiler_params=pltpu.CompilerParams(dimension_semantics=("arbitrary",),
                                             vmem_limit_bytes=VMEM_LIMIT),
    )(pos8, zs)


def _plan(cls, rank, counts, nt):
    cnt = counts[:N_CLASSES, 0].astype(I32)
    ntile = (cnt + TILE_M - 1) // TILE_M
    padded = ntile * TILE_M
    off = jnp.cumsum(padded) - padded
    classes = jnp.arange(N_CLASSES, dtype=I32)
    pos8 = (rank.reshape(-1) + jnp.sum(jnp.where(cls.reshape(-1, 1) == classes, off, 0), axis=1)) * TOK_ROWS
    tile_end = jnp.cumsum(ntile)
    nused = tile_end[-1:].astype(I32)
    tidx = jnp.minimum(jnp.arange(nt + 1, dtype=I32), nused - 1)
    tsel = tile_end[None, :] <= tidx[:, None]
    tcls = jnp.sum(tsel.astype(I32), axis=1)
    used = ntile > 0
    later = jnp.where(used[None, :] & (classes[None, :] > classes[:, None]), classes[None, :], N_CLASSES)
    nxt = jnp.min(later, axis=1)
    has_next = (nxt < N_CLASSES).astype(I32)
    nxt = jnp.minimum(nxt, N_CLASSES - 1)
    ea_c, eb_c = jnp.asarray(EA_TABLE), jnp.asarray(EB_TABLE)
    per_class = jnp.stack([ea_c, eb_c, (jnp.cumsum(used.astype(I32)) - 1) & 1, ea_c[nxt], eb_c[nxt], has_next])
    onehot = tcls[:, None] == classes
    ea, eb, slot, ean, ebn, hasn = jnp.sum(jnp.where(onehot[None], per_class[:, None, :], 0), axis=2)
    tables = (tidx, tcls, ea, eb, nused, slot, ean, ebn, hasn)
    return pos8.astype(I32), off + cnt, padded - cnt, nused * TILE_M, tables


def _mixer_sample_kernel(alpha, x_ref, h0_ref, win_ref, wout_ref, lnvg_ref, lnvb_ref, ws0_ref, bs0_ref,
                         bdb_ref, bdc_ref, lr_ref, li_ref, dskip_ref, glu_ref, bglu_ref,
                         ln1g_ref, ln1b_ref,
                         x1_ref, hnew_ref, v_ref):
    x = x_ref[...]
    proj = _dot(x.astype(BF16), win_ref[...])
    u = jax.nn.gelu(proj[:, :W_A])
    v = _layer_norm(jax.nn.gelu(proj[:, W_A:2 * W_A]), lnvg_ref[...], lnvb_ref[...])
    v_ref[...] = v
    y_a = u * (ws0_ref[...] * v + bs0_ref[...])
    xs = proj[:, 2 * W_A:]
    zs = []
    for k in range(2):
        sl = slice(k * HALF_W, (k + 1) * HALF_W)
        bu = _dot(xs[:, sl].astype(BF16), bdb_ref[k])
        h0r = h0_ref[k, :, :HALF_STATE]
        h0i = h0_ref[k, :, HALF_STATE:]
        lr = lr_ref[k]
        li = li_ref[k]
        hr = lr * h0r - li * h0i + bu[:, :HALF_STATE]
        hi = lr * h0i + li * h0r + bu[:, HALF_STATE:]
        hnew_ref[k, :, :HALF_STATE] = hr
        hnew_ref[k, :, HALF_STATE:] = hi
        hcat = jnp.concatenate([hr, hi], axis=1).astype(BF16)
        y = jax.nn.gelu(_dot(hcat, bdc_ref[k]) + dskip_ref[:, sl] * xs[:, sl])
        gl = _dot(y.astype(BF16), glu_ref[k]) + bglu_ref[:, sl]
        zs.append(y * jax.nn.sigmoid(gl))
    cat = jnp.concatenate([y_a] + zs, axis=1).astype(BF16)
    mix = _dot(cat, wout_ref[...])
    x1_ref[...] = _layer_norm(alpha * x + mix, ln1g_ref[...], ln1b_ref[...])


def _mixer_sample(x, h0, lw, l, alpha):
    n = x.shape[0]
    full = lambda shape: pl.BlockSpec(shape, lambda i: (0,) * len(shape))
    return pl.pallas_call(
        functools.partial(_mixer_sample_kernel, alpha),
        grid=(1,),
        in_specs=[full(x.shape), _layer_spec(h0.shape, l)] + [_layer_spec(lw[k].shape, l) for k in SAMPLE_WEIGHTS],
        out_specs=[full((n, D_MODEL)), full((2, n, 2 * HALF_STATE)), full((n, W_A))],
        out_shape=[jax.ShapeDtypeStruct((n, D_MODEL), F32),
                   jax.ShapeDtypeStruct((2, n, 2 * HALF_STATE), F32),
                   jax.ShapeDtypeStruct((n, W_A), F32)],
        compiler_params=pltpu.CompilerParams(dimension_semantics=("arbitrary",), vmem_limit_bytes=VMEM_LIMIT),
    )(x, h0, *[lw[k] for k in SAMPLE_WEIGHTS])


def _route(x, wr, rbias):
    logits = jnp.dot(x, wr, preferred_element_type=F32, precision=HIGHEST)
    scores = jax.nn.sigmoid(logits)
    biased = scores + rbias
    lane = lax.broadcasted_iota(I32, biased.shape, 1)
    grp = lane // EXPERTS_PER_GROUP
    neg = jnp.float32(-jnp.inf)

    def top2(vals):
        m1 = jnp.max(vals, axis=-1, keepdims=True)
        i1 = jnp.min(jnp.where(vals == m1, lane, N_EXPERTS), axis=-1, keepdims=True)
        rest = jnp.where(lane == i1, neg, vals)
        m2 = jnp.max(rest, axis=-1, keepdims=True)
        i2 = jnp.min(jnp.where(rest == m2, lane, N_EXPERTS), axis=-1, keepdims=True)
        return m1, i1, m2, i2

    best = sel = None
    for g in range(N_EXPERT_GROUPS):
        m1, _, m2, _ = top2(jnp.where(grp == g, biased, neg))
        gs = m1 + m2
        if g == 0:
            best, sel = gs, jnp.zeros(gs.shape, I32)
        else:
            upd = gs > best
            sel = jnp.where(upd, g, sel)
            best = jnp.where(upd, gs, best)
    _, i1, _, i2 = top2(jnp.where(grp == sel, biased, neg))
    s1 = jnp.sum(jnp.where(lane == i1, scores, 0.0), axis=-1, keepdims=True)
    s2 = jnp.sum(jnp.where(lane == i2, scores, 0.0), axis=-1, keepdims=True)
    tot = s1 + s2
    return jnp.where(lane == i1, s1 / tot, 0.0) + jnp.where(lane == i2, s2 / tot, 0.0)


def _moe_kernel(alpha, x_ref, wr_ref, rb_ref, wg_ref, wu_ref, wd_ref, ln2g_ref, ln2b_ref,
                out_ref, xb_ref, comb_ref, acc_ref):
    step = pl.program_id(1)

    @pl.when(step == 0)
    def _():
        x = x_ref[...]
        xb_ref[...] = x.astype(BF16)
        comb_ref[...] = _route(x, wr_ref[...], rb_ref[...])
        acc_ref[...] = jnp.zeros_like(acc_ref)

    xb = xb_ref[...]
    comb = comb_ref[...]
    lane = lax.broadcasted_iota(I32, comb.shape, 1)
    acc = acc_ref[...]
    for k in range(DENSE_EXPERTS_PER_STEP):
        g = _dot(xb, wg_ref[k].astype(BF16))
        u = _dot(xb, wu_ref[k].astype(BF16))
        ce = jnp.sum(jnp.where(lane == step * DENSE_EXPERTS_PER_STEP + k, comb, 0.0), axis=-1, keepdims=True)
        h = (jax.nn.silu(g) * u * ce).astype(BF16)
        acc = acc + _dot(h, wd_ref[k].astype(BF16))
    acc_ref[...] = acc

    @pl.when(step == pl.num_programs(1) - 1)
    def _():
        out_ref[...] = _layer_norm(alpha * x_ref[...] + acc_ref[...], ln2g_ref[...], ln2b_ref[...])


def _moe_dense(x, l, wr, rb, w_gate, w_up, w_down, ln2g, ln2b, alpha, tm):
    t = x.shape[0]
    cst = lambda i, e: (0, 0)
    wsel = lambda i, e: (l, e, 0, 0)
    return pl.pallas_call(
        functools.partial(_moe_kernel, alpha),
        grid=(t // tm, N_EXPERTS // DENSE_EXPERTS_PER_STEP),
        in_specs=[pl.BlockSpec((tm, D_MODEL), lambda i, e: (i, 0)),
                  pl.BlockSpec(wr.shape, cst), pl.BlockSpec(rb.shape, cst),
                  pl.BlockSpec((None, DENSE_EXPERTS_PER_STEP, D_MODEL, D_FF_EXPERT), wsel),
                  pl.BlockSpec((None, DENSE_EXPERTS_PER_STEP, D_MODEL, D_FF_EXPERT), wsel),
                  pl.BlockSpec((None, DENSE_EXPERTS_PER_STEP, D_FF_EXPERT, D_MODEL), wsel),
                  pl.BlockSpec((None,) + ln2g.shape[1:], lambda i, e: (l, 0, 0)),
                  pl.BlockSpec((None,) + ln2b.shape[1:], lambda i, e: (l, 0, 0))],
        out_specs=pl.BlockSpec((tm, D_MODEL), lambda i, e: (i, 0)),
        out_shape=jax.ShapeDtypeStruct((t, D_MODEL), F32),
        scratch_shapes=[pltpu.VMEM((tm, D_MODEL), BF16),
                        pltpu.VMEM((tm, N_EXPERTS), F32),
                        pltpu.VMEM((tm, D_MODEL), F32)],
        compiler_params=pltpu.CompilerParams(dimension_semantics=("arbitrary", "arbitrary"),
                                             vmem_limit_bytes=VMEM_LIMIT),
    )(x, wr, rb, w_gate, w_up, w_down, ln2g, ln2b)


def _block_diag(blocks):
    eye = jnp.eye(HALF_GROUPS, dtype=blocks.dtype)
    k, g, a, b = blocks.shape
    return jnp.einsum("kgab,gh->kgahb", blocks, eye).reshape(k, g * a, g * b)


def _prep_s5_folded(lb_re, lb_im, bb_re, bb_im, c_re, c_im):
    d, g, n = lb_re.shape
    s = S5_FOLD
    pairs = g // 2
    pr, pi = [jnp.ones_like(lb_re)], [jnp.zeros_like(lb_re)]
    for _ in range(s):
        pr, pi = pr + [pr[-1] * lb_re - pi[-1] * lb_im], pi + [pr[-1] * lb_im + pi[-1] * lb_re]
    p_re, p_im = jnp.stack(pr, axis=2), jnp.stack(pi, axis=2)

    def pair_cols(re, im):
        z = jnp.zeros_like(re[:, :, 0])
        g0 = jnp.concatenate([re[:, :, 0], z, im[:, :, 0], z], axis=-1)
        g1 = jnp.concatenate([z, re[:, :, 1], z, im[:, :, 1]], axis=-1)
        return jnp.concatenate([g0, g1], axis=2)

    def pair_diag(a):
        z = jnp.zeros_like(a[:, :, 0])
        return jnp.concatenate([jnp.concatenate([a[:, :, 0], z], axis=-1),
                                jnp.concatenate([z, a[:, :, 1]], axis=-1)], axis=2)

    bt_re, bt_im = bb_re.transpose(0, 1, 3, 2)[:, :, None], bb_im.transpose(0, 1, 3, 2)[:, :, None]
    k_re, k_im = p_re[:, :, s - 1::-1, None, :], p_im[:, :, s - 1::-1, None, :]
    m1t = pair_cols((k_re * bt_re - k_im * bt_im).reshape(d, pairs, 2, -1, n),
                    (k_re * bt_im + k_im * bt_re).reshape(d, pairs, 2, -1, n))
    m1 = jnp.swapaxes(m1t, -1, -2)

    cq_re, cq_im = c_re[:, :, None], c_im[:, :, None]
    j_re, j_im = p_re[:, :, 1:, None, :], p_im[:, :, 1:, None, :]
    hpart = pair_cols((cq_re * j_re - cq_im * j_im).reshape(d, pairs, 2, -1, n),
                      (-(cq_re * j_im + cq_im * j_re)).reshape(d, pairs, 2, -1, n))

    t_re, t_im = p_re[:, :, :s, None, :], p_im[:, :, :s, None, :]
    kern = (jnp.einsum("dgtqn,dgnp->dgtqp", cq_re * t_re - cq_im * t_im, bb_re, precision=HIGHEST)
            - jnp.einsum("dgtqn,dgnp->dgtqp", cq_re * t_im + cq_im * t_re, bb_im, precision=HIGHEST))
    zero = jnp.zeros_like(kern[:, :, 0])
    kx = jnp.stack([jnp.concatenate([kern[:, :, j - q] if q <= j else zero for q in range(s)], axis=-1)
                    for j in range(s)], axis=2)
    apart = pair_diag(kx.reshape(d, pairs, 2, -1, kx.shape[-1]))

    m2 = jnp.concatenate([hpart, apart], axis=-1)
    return (m1.astype(BF16), m2.astype(BF16), p_re[:, :, s].reshape(d, 1, g * n), p_im[:, :, s].reshape(d, 1, g * n))


def _prep_all(w_in, w_out, ln_v_g, ln_v_b, w_s, b_s, a_re, a_im, log_dt, b_re, b_im, c_re, c_im,
              d_skip, w_glu, b_glu, ln1_g, ln1_b):
    d = w_in.shape[0]
    dt = jnp.exp(log_dt)[..., None]
    decay = jnp.exp(a_re * dt)
    lb_re, lb_im = decay * jnp.cos(a_im * dt), decay * jnp.sin(a_im * dt)
    den = a_re * a_re + a_im * a_im
    nr, ni = lb_re - 1.0, lb_im
    zr = (nr * a_re + ni * a_im) / den
    zi = (ni * a_re - nr * a_im) / den
    bb_re = zr[..., None] * b_re - zi[..., None] * b_im
    bb_im = zr[..., None] * b_im + zi[..., None] * b_re

    def bd(a):
        blocks = _block_diag(a.reshape((d * 2, HALF_GROUPS) + a.shape[2:]))
        return blocks.reshape((d, 2) + blocks.shape[1:])

    bdb = jnp.concatenate([bd(bb_re.transpose(0, 1, 3, 2)), bd(bb_im.transpose(0, 1, 3, 2))], axis=3)
    bdc = jnp.concatenate([bd(c_re.transpose(0, 1, 3, 2)), bd(-c_im.transpose(0, 1, 3, 2))], axis=2)
    m1, m2, lr8, li8 = _prep_s5_folded(lb_re, lb_im, bb_re, bb_im, c_re, c_im)
    return dict(
        m1=m1, m2=m2, lr8=lr8, li8=li8,
        win=w_in.astype(BF16), wout=w_out.astype(BF16),
        lnvg=ln_v_g[:, None], lnvb=ln_v_b[:, None],
        wtril=jnp.tril(w_s).astype(BF16),
        bsb=jnp.broadcast_to(b_s[..., None], (d, H_A, CHUNK, LANES)),
        ws0=jnp.repeat(w_s[:, :, 0, 0], P_A, axis=1)[:, None], bs0=jnp.repeat(b_s[:, :, 0], P_A, axis=1)[:, None],
        bdb=bdb.astype(BF16), bdc=bdc.astype(BF16),
        lr=lb_re.reshape(d, 2, 1, HALF_STATE), li=lb_im.reshape(d, 2, 1, HALF_STATE),
        dskip=d_skip.reshape(d, 1, W_B), glu=bd(w_glu).astype(BF16),
        bglu=b_glu.reshape(d, 1, W_B), ln1g=ln1_g[:, None], ln1b=ln1_b[:, None])


def _state_to_cols(h_re, h_im):
    d, b = h_re.shape[:2]
    re = h_re.reshape(d, b, 2, HALF_STATE)
    im = h_im.reshape(d, b, 2, HALF_STATE)
    return jnp.concatenate([re, im], axis=3).transpose(0, 2, 1, 3)


def _cols_to_state(h):
    d, _, b, _ = h.shape
    re = h[..., :HALF_STATE].transpose(0, 2, 1, 3).reshape(d, b, G_B, N_STATE)
    im = h[..., HALF_STATE:].transpose(0, 2, 1, 3).reshape(d, b, G_B, N_STATE)
    return re, im


def _pairs_to_state(h):
    d, b, _ = h.shape
    h = h.reshape(d, b, G_B // 2, 2, 2, N_STATE)
    return h[:, :, :, 0].reshape(d, b, G_B, N_STATE), h[:, :, :, 1].reshape(d, b, G_B, N_STATE)


def kernel(x_prompt, x_sample, state_ssm_re, state_ssm_im, w_in, w_out, ln_v_g, ln_v_b, w_s, b_s, ssm_a_re, ssm_a_im, ssm_log_dt, ssm_b_re, ssm_b_im, ssm_c_re, ssm_c_im, ssm_d, w_glu, b_glu, ln1_g, ln1_b, ln2_g, ln2_b, w_router, router_bias, w_gate, w_up, w_down):
    depth = w_in.shape[0]
    alpha = float((2 * depth) ** 0.25)
    nb, seq, _ = x_prompt.shape
    ns = x_sample.shape[0]
    tokens = nb * seq
    nt = tokens // TILE_M + N_CLASSES
    rb = router_bias[None]
    wrt = w_router.T
    wrp = jnp.pad(w_router, ((0, 0), (0, LANES - N_EXPERTS)))
    rbcol = router_bias[:, None]
    tok = np.arange(nb * CHUNK)
    tri = jnp.asarray(tok[:, None] < tok[None, :], BF16)
    lw = _prep_all(w_in, w_out, ln_v_g, ln_v_b, w_s, b_s, ssm_a_re, ssm_a_im, ssm_log_dt,
                   ssm_b_re, ssm_b_im, ssm_c_re, ssm_c_im, ssm_d, w_glu, b_glu, ln1_g, ln1_b)
    shared = (wrt, rbcol, tri)
    ln2g, ln2b = ln2_g[:, None], ln2_b[:, None]
    h0s = _state_to_cols(state_ssm_re, state_ssm_im)
    xp = x_prompt
    pos = None
    xs = x_sample.reshape(ns, D_MODEL)
    pr_h, sm_h, sm_v = [], [], []
    for l in range(depth):
        x1t, hfin, cls, rank, counts = _mixer_prompt(xp, lw, shared, l, alpha, nb, seq, pos)
        pos, zstart, zlen, tail, tables = _plan(cls, rank, counts, nt)
        x_sorted = _dispatch(x1t, pos, zstart, zlen, tail, nb * CHUNK, nt * TILE_M)
        xp = _pair_experts(x_sorted, l, tables, wrp, w_gate, w_up, w_down, ln2g, ln2b, alpha)
        pr_h.append(hfin)

        x1s, hnew, v_new = _mixer_sample(xs, h0s, lw, l, alpha)
        xs = _moe_dense(x1s, l, w_router, rb, w_gate, w_up, w_down, ln2g, ln2b, alpha, tm=ns)
        sm_h.append(hnew)
        sm_v.append(v_new.reshape(ns, 1, W_A))
    y_prompt = _ungather(xp, pos, nb, seq)
    pr_re, pr_im = _pairs_to_state(jnp.stack(pr_h))
    sm_re, sm_im = _cols_to_state(jnp.stack(sm_h))
    return (y_prompt, xs.reshape(ns, 1, D_MODEL), pr_re, pr_im, sm_re, sm_im, jnp.stack(sm_v))
```

```python
import functools

import jax
import jax.numpy as jnp
import numpy as np
from jax import lax
from jax.experimental import pallas as pl
from jax.experimental.pallas import tpu as pltpu

D_MODEL = 1024
W_A = 512
W_B = 512
CHUNK = 128
H_A = 4
P_A = W_A // H_A
GROUP_B = 16
G_B = W_B // GROUP_B
N_STATE = 64
N_EXPERTS = 16
N_EXPERT_GROUPS = 4
EXPERTS_PER_GROUP = N_EXPERTS // N_EXPERT_GROUPS
D_FF_EXPERT = D_MODEL // 4
LN_EPS = 1e-5

LANES = 128
SUBLANES = 8
HALF_GROUPS = 16
HALF_W = HALF_GROUPS * GROUP_B
HALF_STATE = HALF_GROUPS * N_STATE
VMEM_LIMIT = 56 * 1024 * 1024

PAIRS = ((0, 1), (0, 2), (0, 3), (1, 3), (1, 2), (3, 2))
N_CLASSES = N_EXPERT_GROUPS * len(PAIRS)
CLASS_ROWS = 32
TILE_M = 256
PAIR_SUBTILES = 1
TOK_ROWS = D_MODEL // LANES
ZERO_TOKENS = TILE_M // 2
DMA_UNROLL = 8
DENSE_EXPERTS_PER_STEP = 4
S5_FOLD = 8
S5_PAIRS = G_B // 2
EA_TABLE = np.array([EXPERTS_PER_GROUP * g + a for g in range(N_EXPERT_GROUPS) for a, _ in PAIRS], np.int32)
EB_TABLE = np.array([EXPERTS_PER_GROUP * g + b for g in range(N_EXPERT_GROUPS) for _, b in PAIRS], np.int32)

F32 = jnp.float32
BF16 = jnp.bfloat16
I32 = jnp.int32
HIGHEST = lax.Precision.HIGHEST


def _layer_norm(x, g, b):
    mu = jnp.mean(x, axis=-1, keepdims=True)
    xc = x - mu
    var = jnp.mean(xc * xc, axis=-1, keepdims=True)
    return xc * lax.rsqrt(var + LN_EPS) * g + b


def _dot(a, b):
    return jnp.dot(a, b, preferred_element_type=F32)


def _route_classes(x1, wrt, rbcol):
    def split(a):
        hi = a.astype(BF16)
        return hi, (a - hi.astype(F32)).astype(BF16)

    def dot_t(a, b):
        return lax.dot_general(a, b, (((1,), (1,)), ((), ())), preferred_element_type=F32)
    w_hi, w_lo = split(wrt)
    x_hi, x_lo = split(x1)
    logits_t = dot_t(w_hi, x_hi) + (dot_t(w_hi, x_lo) + dot_t(w_lo, x_hi))
    biased = jax.nn.sigmoid(logits_t) + rbcol
    rows = [biased[e:e + 1, :] for e in range(N_EXPERTS)]
    n = EXPERTS_PER_GROUP

    best = sel = None
    for g in range(N_EXPERT_GROUPS):
        v = rows[n * g:n * (g + 1)]
        gs = None
        for a, b in PAIRS:
            s = v[a] + v[b]
            gs = s if gs is None else jnp.maximum(gs, s)
        if g == 0:
            best, sel = gs, jnp.zeros(gs.shape, I32)
        else:
            upd = gs > best
            sel = jnp.where(upd, g, sel)
            best = jnp.where(upd, gs, best)

    cls = jnp.zeros(sel.shape, I32)
    for g in range(N_EXPERT_GROUPS):
        v = rows[n * g:n * (g + 1)]
        lo = jnp.full(sel.shape, n, I32)
        hi = jnp.full(sel.shape, -1, I32)
        for i in range(n):
            before = jnp.zeros(sel.shape, I32)
            for j in range(n):
                if j < i:
                    before = before + (v[j] >= v[i]).astype(I32)
                elif j > i:
                    before = before + (v[j] > v[i]).astype(I32)
            member = before < 2
            lo = jnp.where(member, jnp.minimum(lo, i), lo)
            hi = jnp.where(member, jnp.maximum(hi, i), hi)
        pidx = jnp.zeros(sel.shape, I32)
        for k, (a, b) in enumerate(PAIRS):
            pidx = jnp.where((lo == min(a, b)) & (hi == max(a, b)), k, pidx)
        cls = jnp.where(sel == g, g * len(PAIRS) + pidx, cls)
    return cls


def _to_token_tiles(ref, row0, x):
    n = x.shape[0]
    for c in range(TOK_ROWS):
        ref[pl.ds(row0 * TOK_ROWS + c, n, stride=TOK_ROWS), :] = x[:, c * LANES:(c + 1) * LANES]


def _from_token_tiles(ref, row0, n):
    return jnp.concatenate(
        [ref[pl.ds(row0 * TOK_ROWS + c, n, stride=TOK_ROWS), :] for c in range(TOK_ROWS)], axis=1)


def _gathered_tokens(step, nsteps, pos8_ref, src_hbm, bufs, sems, rows, consume):
    def start(s, base):
        def body(g, c):
            for i in range(DMA_UNROLL):
                r = g * DMA_UNROLL + i
                p8 = pl.multiple_of(pos8_ref[base + r], TOK_ROWS)
                pltpu.make_async_copy(src_hbm.at[pl.ds(p8, TOK_ROWS), :],
                                      bufs[s].at[pl.ds(pl.multiple_of(r * TOK_ROWS, TOK_ROWS), TOK_ROWS), :],
                                      sems.at[s]).start(priority=i % 2)
            return c
        lax.fori_loop(0, rows // DMA_UNROLL, body, 0)

    @pl.when(step == 0)
    def _():
        start(0, 0)

    for s in range(2):
        @pl.when(lax.rem(step, 2) == s)
        def _(s=s):
            pltpu.make_async_copy(src_hbm.at[pl.ds(0, rows * TOK_ROWS), :], bufs[s], sems.at[s]).wait()

            @pl.when(step + 1 < nsteps)
            def _():
                start(1 - s, (step + 1) * rows)
            consume(bufs[s])


def _mixer_kernel(alpha, nb, gather_in, *refs):
    if gather_in:
        pos8_ref, zs_hbm = refs[:2]
        refs = refs[2:]
    else:
        x_ref = refs[0]
        refs = refs[1:]
    (win_ref, wout_ref, lnvg_ref, lnvb_ref, wtril_ref, bsb_ref, m1_ref, m2_ref, lr8_ref, li8_ref,
     dskip_ref, glu_ref, bglu_ref, ln1g_ref, ln1b_ref, wrt_ref, rbcol_ref, tri_ref,
     x1t_ref, hfin_ref, cls_ref, rank_ref, cnt_ref,
     xb_ref, xs_slab, xs_scb, xst_ref, ht_ref, yt_ref, mix_ref, hstate_ref, carry_ref) = refs[:32]
    lt = CHUNK
    rows = nb * lt
    pitch = lt + SUBLANES
    step = pl.program_id(0)
    nsteps = pl.num_programs(0)

    @pl.when(step == 0)
    def _():
        hstate_ref[...] = jnp.zeros_like(hstate_ref)
        carry_ref[...] = jnp.zeros_like(carry_ref)

    if gather_in:
        xin_ref, xbuf0, xbuf1, gsem = refs[32:36]

        def consume(buf):
            xin_ref[...] = _from_token_tiles(buf, 0, rows)
        _gathered_tokens(step, nsteps, pos8_ref, zs_hbm, (xbuf0, xbuf1), gsem, rows, consume)

        def load_x():
            return xin_ref[...]
    else:
        def load_x():
            return x_ref[...].reshape(rows, D_MODEL)

    xb_ref[...] = load_x().astype(BF16)

    fold = S5_FOLD
    nchunk = lt // fold
    cb = nchunk * nb
    xs = _dot(xb_ref[...], win_ref[:, 2 * W_A:])
    for j in range(W_B // LANES):
        for b in range(nb):
            xs_slab[j, b * pitch:b * pitch + lt, :] = xs[b * lt:(b + 1) * lt, j * LANES:(j + 1) * LANES]

    def to_scb(t, c):
        s_, ch_ = lax.bitwise_and(t, fold - 1), lax.shift_right_logical(t, fold.bit_length() - 1)
        r0 = pl.multiple_of(s_ * cb + ch_ * nb, SUBLANES)
        for j in range(W_B // LANES):
            xs_scb[pl.ds(r0, nb), j * LANES:(j + 1) * LANES] = xs_slab[j, pl.ds(t, nb, stride=pitch), :]
        return c
    lax.fori_loop(0, lt, to_scb, 0, unroll=4)
    xst_ref[...] = xs_scb[...].T.astype(BF16)

    def chunk_inputs(pr):
        return jnp.concatenate(
            [xst_ref[(2 * pr + gi) * GROUP_B:(2 * pr + gi + 1) * GROUP_B, s * cb:(s + 1) * cb]
             for gi in range(2) for s in range(fold)], axis=0)

    pw = 2 * 2 * N_STATE
    for pr in range(S5_PAIRS):
        ht_ref[:, pr * pw:(pr + 1) * pw] = _dot(m1_ref[pr], chunk_inputs(pr)).T

    half_pairs = S5_PAIRS // 2
    for half in range(2):
        c0 = half * half_pairs * pw
        lr8 = [jnp.broadcast_to(lr8_ref[:, (half * half_pairs + p) * LANES:(half * half_pairs + p + 1) * LANES],
                                (nb, LANES)) for p in range(half_pairs)]
        li8 = [jnp.broadcast_to(li8_ref[:, (half * half_pairs + p) * LANES:(half * half_pairs + p + 1) * LANES],
                                (nb, LANES)) for p in range(half_pairs)]
        h = [hstate_ref[:, c0 + q * LANES:c0 + (q + 1) * LANES] for q in range(2 * half_pairs)]
        for c in range(nchunk):
            rs = slice(c * nb, (c + 1) * nb)
            for p in range(half_pairs):
                re_sl = slice(c0 + p * pw, c0 + p * pw + LANES)
                im_sl = slice(c0 + p * pw + LANES, c0 + (p + 1) * pw)
                hr, hi = h[2 * p], h[2 * p + 1]
                ur, ui = ht_ref[rs, re_sl], ht_ref[rs, im_sl]
                ht_ref[rs, re_sl] = hr
                ht_ref[rs, im_sl] = hi
                h[2 * p] = lr8[p] * hr - li8[p] * hi + ur
                h[2 * p + 1] = lr8[p] * hi + li8[p] * hr + ui
        for q in range(2 * half_pairs):
            hstate_ref[:, c0 + q * LANES:c0 + (q + 1) * LANES] = h[q]
    hfin_ref[...] = hstate_ref[...]

    for pr in range(S5_PAIRS):
        rhs = jnp.concatenate([ht_ref[:, pr * pw:(pr + 1) * pw].T.astype(BF16), chunk_inputs(pr)], axis=0)
        yt = _dot(m2_ref[pr], rhs)
        for gi in range(2):
            for s in range(fold):
                r0 = (gi * fold + s) * GROUP_B
                yt_ref[(2 * pr + gi) * GROUP_B:(2 * pr + gi + 1) * GROUP_B, s * cb:(s + 1) * cb] = (
                    yt[r0:r0 + GROUP_B, :])

    y = jax.nn.gelu(yt_ref[...].T + dskip_ref[...] * xs_scb[...])
    yb = y.astype(BF16)
    spitch = cb + SUBLANES
    for k in range(2):
        sl = slice(k * HALF_W, (k + 1) * HALF_W)
        gl = _dot(yb[:, sl], glu_ref[k]) + bglu_ref[:, sl]
        z = y[:, sl] * jax.nn.sigmoid(gl)
        for j in range(HALF_W // LANES):
            for s in range(fold):
                xs_slab[k * (HALF_W // LANES) + j, s * spitch:s * spitch + cb, :] = (
                    z[s * cb:(s + 1) * cb, j * LANES:(j + 1) * LANES])

    def to_bt(ch, c):
        for b in range(nb):
            dst = pl.multiple_of(b * lt + ch * fold, SUBLANES)
            for j in range(W_B // LANES):
                mix_ref[pl.ds(dst, fold), W_A + j * LANES:W_A + (j + 1) * LANES] = (
                    xs_slab[j, pl.ds(ch * nb + b, fold, stride=spitch), :])
        return c
    lax.fori_loop(0, nchunk, to_bt, 0)

    vg = jax.nn.gelu(_dot(xb_ref[...], win_ref[:, W_A:2 * W_A]))
    v = _layer_norm(vg, lnvg_ref[...], lnvb_ref[...]).astype(BF16)
    u = jax.nn.gelu(_dot(xb_ref[...], win_ref[:, :W_A]))
    for h in range(H_A):
        hs = slice(h * P_A, (h + 1) * P_A)
        vcat = jnp.concatenate([v[b * lt:(b + 1) * lt, hs] for b in range(nb)], axis=1)
        o = _dot(wtril_ref[h], vcat)
        for b in range(nb):
            rs = slice(b * lt, (b + 1) * lt)
            mix_ref[rs, hs] = u[rs, hs] * (o[:, b * LANES:(b + 1) * LANES] + bsb_ref[h])

    mix = _dot(mix_ref[...].astype(BF16), wout_ref[...])
    x1 = _layer_norm(alpha * load_x() + mix, ln1g_ref[...], ln1b_ref[...])
    _to_token_tiles(x1t_ref, 0, x1)

    cls = _route_classes(x1, wrt_ref[...], rbcol_ref[...])
    crow = lax.broadcasted_iota(I32, (CLASS_ROWS, rows), 0)
    onehot = jnp.where(crow == cls, 1.0, 0.0)
    prefix = _dot(onehot.astype(BF16), tri_ref[...])
    carry = carry_ref[:, 0:1]
    rank = jnp.sum(onehot * (prefix + carry), axis=0, keepdims=True)
    cls_ref[...] = cls.reshape(1, 1, rows)
    rank_ref[...] = rank.astype(I32).reshape(1, 1, rows)
    carry_ref[...] = carry_ref[...] + jnp.sum(onehot, axis=1, keepdims=True)
    cnt_ref[...] = carry_ref[...]


def _const_spec(shape):
    nd = len(shape)
    return pl.BlockSpec(shape, lambda *_: (0,) * nd, pipeline_mode=pl.Buffered(1))


def _layer_spec(shape, l):
    nd = len(shape)
    return pl.BlockSpec((None,) + tuple(shape[1:]), lambda *_: (l,) + (0,) * (nd - 1),
                        pipeline_mode=pl.Buffered(1))


MIXER_WEIGHTS = ("win", "wout", "lnvg", "lnvb", "wtril", "bsb", "m1", "m2", "lr8", "li8", "dskip", "glu",
                 "bglu", "ln1g", "ln1b")
SAMPLE_WEIGHTS = ("win", "wout", "lnvg", "lnvb", "ws0", "bs0", "bdb", "bdc", "lr", "li", "dskip", "glu",
                  "bglu", "ln1g", "ln1b")


def _mixer_prompt(x, lw, shared, l, alpha, nb, seq, pos_prev=None):
    lt = CHUNK
    rows = nb * lt
    nsteps = seq // lt
    gather_in = pos_prev is not None
    weights = tuple(lw[k] for k in MIXER_WEIGHTS) + tuple(shared)
    wspecs = [_layer_spec(lw[k].shape, l) for k in MIXER_WEIGHTS] + [_const_spec(w.shape) for w in shared]
    if gather_in:
        x_spec = pl.BlockSpec(memory_space=pl.ANY)
    else:
        x_spec = pl.BlockSpec((nb, lt, D_MODEL), lambda i, *_: (0, i, 0))
    scratch = [
        pltpu.VMEM((rows, D_MODEL), BF16),
        pltpu.VMEM((W_B // LANES, nb * (lt + SUBLANES), LANES), F32),
        pltpu.VMEM((rows, W_B), F32),
        pltpu.VMEM((W_B, rows), BF16),
        pltpu.VMEM((rows // S5_FOLD, 2 * G_B * N_STATE), F32),
        pltpu.VMEM((W_B, rows), F32),
        pltpu.VMEM((rows, D_MODEL), F32),
        pltpu.VMEM((nb, 2 * G_B * N_STATE), F32),
        pltpu.VMEM((CLASS_ROWS, LANES), F32),
    ]
    if gather_in:
        scratch += [pltpu.VMEM((rows, D_MODEL), F32),
                    pltpu.VMEM((rows * TOK_ROWS, LANES), F32), pltpu.VMEM((rows * TOK_ROWS, LANES), F32),
                    pltpu.SemaphoreType.DMA((2,))]
    grid_spec = pltpu.PrefetchScalarGridSpec(
        num_scalar_prefetch=1 if gather_in else 0,
        grid=(nsteps,),
        in_specs=[x_spec] + wspecs,
        out_specs=[pl.BlockSpec((rows * TOK_ROWS, LANES), lambda i, *_: (i, 0)),
                   pl.BlockSpec((nb, 2 * G_B * N_STATE), lambda i, *_: (0, 0)),
                   pl.BlockSpec((1, 1, rows), lambda i, *_: (i, 0, 0)),
                   pl.BlockSpec((1, 1, rows), lambda i, *_: (i, 0, 0)),
                   pl.BlockSpec((CLASS_ROWS, LANES), lambda i, *_: (0, 0))],
        scratch_shapes=scratch)
    args = ((pos_prev, x) if gather_in else (x,)) + weights
    return pl.pallas_call(
        functools.partial(_mixer_kernel, alpha, nb, gather_in),
        grid_spec=grid_spec,
        out_shape=[jax.ShapeDtypeStruct((nb * seq * TOK_ROWS, LANES), F32),
                   jax.ShapeDtypeStruct((nb, 2 * G_B * N_STATE), F32),
                   jax.ShapeDtypeStruct((nsteps, 1, rows), I32),
                   jax.ShapeDtypeStruct((nsteps, 1, rows), I32),
                   jax.ShapeDtypeStruct((CLASS_ROWS, LANES), F32)],
        compiler_params=pltpu.CompilerParams(dimension_semantics=("arbitrary",),
                                             vmem_limit_bytes=VMEM_LIMIT),
    )(*args)


def _dispatch_kernel(rows, pos8_ref, zstart_ref, zlen_ref, tail_ref, x_ref, xs_hbm, zero_ref, sem, zsem):
    step = pl.program_id(0)
    ztok = ZERO_TOKENS

    @pl.when(step == 0)
    def _():
        zero_ref[...] = jnp.zeros_like(zero_ref)
        pieces = []
        for c in range(N_CLASSES):
            start = zstart_ref[c]
            zlen = zlen_ref[c]
            p = TILE_M // 2
            while p >= 1:
                hit = (zlen & p) != 0
                pieces.append((hit, pltpu.make_async_copy(
                    zero_ref.at[pl.ds(0, p * TOK_ROWS), :],
                    xs_hbm.at[pl.ds(pl.multiple_of(start * TOK_ROWS, TOK_ROWS), p * TOK_ROWS), :], zsem)))
                start = start + jnp.where(hit, p, 0)
                p //= 2
        for hit, cp in pieces:
            pl.when(hit)(cp.start)
        for hit, cp in pieces:
            pl.when(hit)(cp.wait)

        zrows = ztok * TOK_ROWS
        first = tail_ref[0] // ztok

        def tail_copy(q):
            return pltpu.make_async_copy(
                zero_ref, xs_hbm.at[pl.ds(pl.multiple_of(q * zrows, zrows), zrows), :], zsem)

        def tail_start(q, c):
            tail_copy(q).start()
            return c

        def tail_wait(q, c):
            tail_copy(q).wait()
            return c
        lax.fori_loop(first, xs_hbm.shape[0] // zrows, tail_start, 0)
        lax.fori_loop(first, xs_hbm.shape[0] // zrows, tail_wait, 0)

    base = step * rows

    def body(g, c):
        for i in range(DMA_UNROLL):
            r = g * DMA_UNROLL + i
            p8 = pl.multiple_of(pos8_ref[base + r], TOK_ROWS)
            pltpu.make_async_copy(x_ref.at[pl.ds(pl.multiple_of(r * TOK_ROWS, TOK_ROWS), TOK_ROWS), :],
                                  xs_hbm.at[pl.ds(p8, TOK_ROWS), :], sem).start(priority=i % 2)
        return c
    lax.fori_loop(0, rows // DMA_UNROLL, body, 0)
    pltpu.make_async_copy(x_ref, xs_hbm.at[pl.ds(0, rows * TOK_ROWS), :], sem).wait()


def _dispatch(x1t, pos8, zstart, zlen, tail, rows, ns_tokens):
    return pl.pallas_call(
        functools.partial(_dispatch_kernel, rows),
        grid_spec=pltpu.PrefetchScalarGridSpec(
            num_scalar_prefetch=4,
            grid=(x1t.shape[0] // (rows * TOK_ROWS),),
            in_specs=[pl.BlockSpec((rows * TOK_ROWS, LANES), lambda i, *_: (i, 0))],
            out_specs=pl.BlockSpec(memory_space=pl.ANY),
            scratch_shapes=[pltpu.VMEM((ZERO_TOKENS * TOK_ROWS, LANES), F32),
                            pltpu.SemaphoreType.DMA(()), pltpu.SemaphoreType.DMA(())]),
        out_shape=jax.ShapeDtypeStruct((ns_tokens * TOK_ROWS, LANES), F32),
        compiler_params=pltpu.CompilerParams(dimension_semantics=("arbitrary",),
                                             vmem_limit_bytes=VMEM_LIMIT),
    )(pos8, zstart, zlen, tail, x1t)


def _pair_kernel(alpha, l, tidx_ref, tcls_ref, nused_ref,
                 ea_ref, cha_ref, bufa_ref, nxa_ref, hna_ref, eb_ref, chb_ref, bufb_ref, nxb_ref, hnb_ref,
                 x_ref, wrp_ref, wg_hbm, wu_hbm, wd_hbm,
                 ln2g_ref, ln2b_ref, z_ref, w1_ref, w2_ref, xprev_ref, moe_ref, sg_ref, su_ref, sd_ref, wsem):
    j = pl.program_id(0)
    f = D_FF_EXPERT
    nused = nused_ref[0]
    slots = ((ea_ref, cha_ref, bufa_ref, nxa_ref, hna_ref), (eb_ref, chb_ref, bufb_ref, nxb_ref, hnb_ref))

    def weight_copies(k, e, buf):
        return [pltpu.make_async_copy(wg_hbm.at[l, e], sg_ref.at[k, buf], wsem.at[k, buf]),
                pltpu.make_async_copy(wu_hbm.at[l, e], su_ref.at[k, buf], wsem.at[k, buf]),
                pltpu.make_async_copy(wd_hbm.at[l, e], sd_ref.at[k, buf], wsem.at[k, buf])]

    @pl.when(j == 0)
    def _():
        xprev_ref[...] = jnp.zeros_like(xprev_ref)
        moe_ref[...] = jnp.zeros_like(moe_ref)
        w1_ref[:, 4 * f:] = wrp_ref[...].astype(BF16)
        for k, (e_ref, _, buf_ref, _, _) in enumerate(slots):
            for cp in weight_copies(k, e_ref[0], buf_ref[0]):
                cp.start()

    @pl.when(j > nused)
    def _():
        z_ref[...] = jnp.zeros_like(z_ref)

    @pl.when(j <= nused)
    def _():
        class_start = jnp.logical_or(j == 0, tcls_ref[j] != tcls_ref[jnp.maximum(j - 1, 0)])

        for k, (e_ref, ch_ref, buf_ref, nx_ref, hn_ref) in enumerate(slots):
            @pl.when(jnp.logical_and(class_start, ch_ref[j] != 0))
            def _(k=k, e_ref=e_ref, buf_ref=buf_ref, nx_ref=nx_ref, hn_ref=hn_ref):
                buf = buf_ref[j]
                for cp in weight_copies(k, e_ref[j], buf):
                    cp.wait()

                @pl.when(hn_ref[j] != 0)
                def _():
                    for cp in weight_copies(k, nx_ref[j], 1 - buf):
                        cp.start()
                w1_ref[:, 2 * k * f:(2 * k + 1) * f] = sg_ref[k, buf].astype(BF16)
                w1_ref[:, (2 * k + 1) * f:(2 * k + 2) * f] = su_ref[k, buf].astype(BF16)
                w2_ref[k * f:(k + 1) * f, :] = sd_ref[k, buf].astype(BF16)

        _to_token_tiles(z_ref, 0, _layer_norm(alpha * xprev_ref[...] + moe_ref[...], ln2g_ref[...], ln2b_ref[...]))

        x = _from_token_tiles(x_ref, 0, TILE_M)
        gu = _dot(x.astype(BF16), w1_ref[...])
        scores = jax.nn.sigmoid(gu[:, 4 * f:])
        lane = lax.broadcasted_iota(I32, scores.shape, 1)
        sa = jnp.sum(jnp.where(lane == ea_ref[j], scores, 0.0), axis=-1, keepdims=True)
        sb = jnp.sum(jnp.where(lane == eb_ref[j], scores, 0.0), axis=-1, keepdims=True)
        tot = sa + sb
        ha = jax.nn.silu(gu[:, 0 * f:1 * f]) * gu[:, 1 * f:2 * f] * (sa / tot)
        hb = jax.nn.silu(gu[:, 2 * f:3 * f]) * gu[:, 3 * f:4 * f] * (sb / tot)
        xprev_ref[...] = x
        moe_ref[...] = _dot(jnp.concatenate([ha, hb], axis=1).astype(BF16), w2_ref[...])


def _pair_experts(xs, l, tables, wrp, w_gate, w_up, w_down, ln2g, ln2b, alpha):
    nsteps = tables[0].shape[0]
    cst = lambda j, *_: (0, 0)
    lsel = lambda j, *_: (l, 0, 0)
    hbm = pl.BlockSpec(memory_space=pl.ANY)
    return pl.pallas_call(
        functools.partial(_pair_kernel, alpha, l),
        grid_spec=pltpu.PrefetchScalarGridSpec(
            num_scalar_prefetch=len(tables),
            grid=(nsteps,),
            in_specs=[pl.BlockSpec((TILE_M * TOK_ROWS, LANES), lambda j, ti, *_: (ti[j], 0)),
                      pl.BlockSpec(wrp.shape, cst), hbm, hbm, hbm,
                      pl.BlockSpec((None,) + ln2g.shape[1:], lsel), pl.BlockSpec((None,) + ln2b.shape[1:], lsel)],
            out_specs=pl.BlockSpec((TILE_M * TOK_ROWS, LANES), lambda j, *_: (jnp.maximum(j - 1, 0), 0)),
            scratch_shapes=[pltpu.VMEM((D_MODEL, 4 * D_FF_EXPERT + LANES), BF16),
                            pltpu.VMEM((2 * D_FF_EXPERT, D_MODEL), BF16),
                            pltpu.VMEM((TILE_M, D_MODEL), F32), pltpu.VMEM((TILE_M, D_MODEL), F32),
                            pltpu.VMEM((2, 2, D_MODEL, D_FF_EXPERT), F32),
                            pltpu.VMEM((2, 2, D_MODEL, D_FF_EXPERT), F32),
                            pltpu.VMEM((2, 2, D_FF_EXPERT, D_MODEL), F32),
                            pltpu.SemaphoreType.DMA((2, 2))]),
        out_shape=jax.ShapeDtypeStruct(xs.shape, F32),
        compiler_params=pltpu.CompilerParams(dimension_semantics=("arbitrary",),
                                             vmem_limit_bytes=VMEM_LIMIT),
    )(*tables, xs, wrp, w_gate, w_up, w_down, ln2g, ln2b)


def _ungather_kernel(nb, pos8_ref, zs_hbm, out_ref, buf0, buf1, sems):
    rows = nb * CHUNK

    def consume(buf):
        out_ref[...] = _from_token_tiles(buf, 0, rows).reshape(nb, CHUNK, D_MODEL)
    _gathered_tokens(pl.program_id(0), pl.num_programs(0), pos8_ref, zs_hbm, (buf0, buf1), sems, rows, consume)


def _ungather(zs, pos8, nb, seq):
    rows = nb * CHUNK
    return pl.pallas_call(
        functools.partial(_ungather_kernel, nb),
        grid_spec=pltpu.PrefetchScalarGridSpec(
            num_scalar_prefetch=1,
            grid=(seq // CHUNK,),
            in_specs=[pl.BlockSpec(memory_space=pl.ANY)],
            out_specs=pl.BlockSpec((nb, CHUNK, D_MODEL), lambda i, *_: (0, i, 0)),
            scratch_shapes=[pltpu.VMEM((rows * TOK_ROWS, LANES), F32), pltpu.VMEM((rows * TOK_ROWS, LANES), F32),
                            pltpu.SemaphoreType.DMA((2,))]),
        out_shape=jax.ShapeDtypeStruct((nb, seq, D_MODEL), F32),
        compiler_params=pltpu.CompilerParams(dimension_semantics=("arbitrary",),
                                             vmem_limit_bytes=VMEM_LIMIT),
    )(pos8, zs)


def _plan(cls, rank, counts, nt):
    cnt = counts[:N_CLASSES, 0].astype(I32)
    ntile = (cnt + TILE_M - 1) // TILE_M
    padded = ntile * TILE_M
    off = jnp.cumsum(padded) - padded
    classes = jnp.arange(N_CLASSES, dtype=I32)
    pos8 = (rank.reshape(-1) + jnp.sum(jnp.where(cls.reshape(-1, 1) == classes, off, 0), axis=1)) * TOK_ROWS
    tile_end = jnp.cumsum(ntile)
    nused = tile_end[-1:].astype(I32)
    tidx = jnp.minimum(jnp.arange(nt + 1, dtype=I32), nused - 1)
    tsel = tile_end[None, :] <= tidx[:, None]
    tcls = jnp.sum(tsel.astype(I32), axis=1)
    used = ntile > 0
    earlier = used[None, :] & (classes[None, :] < classes[:, None])
    later = classes[None, :] > classes[:, None]
    prev_used = jnp.max(jnp.where(earlier, classes[None, :], -1), axis=1)
    per_class = []
    for table in (EA_TABLE, EB_TABLE):
        e_c = jnp.asarray(table)
        change = used & ((prev_used < 0) | (e_c[jnp.maximum(prev_used, 0)] != e_c))
        nxt = jnp.min(jnp.where(change[None, :] & later, classes[None, :], N_CLASSES), axis=1)
        per_class += [e_c, change.astype(I32), (jnp.cumsum(change.astype(I32)) - 1) & 1,
                      e_c[jnp.minimum(nxt, N_CLASSES - 1)], (nxt < N_CLASSES).astype(I32)]
    onehot = tcls[:, None] == classes
    per_tile = jnp.sum(jnp.where(onehot[None], jnp.stack(per_class)[:, None, :], 0), axis=2)
    tables = (tidx, tcls, nused) + tuple(per_tile)
    return pos8.astype(I32), off + cnt, padded - cnt, nused * TILE_M, tables


def _mixer_sample_kernel(alpha, x_ref, h0_ref, win_ref, wout_ref, lnvg_ref, lnvb_ref, ws0_ref, bs0_ref,
                         bdb_ref, bdc_ref, lr_ref, li_ref, dskip_ref, glu_ref, bglu_ref,
                         ln1g_ref, ln1b_ref,
                         x1_ref, hnew_ref, v_ref):
    x = x_ref[...]
    proj = _dot(x.astype(BF16), win_ref[...])
    u = jax.nn.gelu(proj[:, :W_A])
    v = _layer_norm(jax.nn.gelu(proj[:, W_A:2 * W_A]), lnvg_ref[...], lnvb_ref[...])
    v_ref[...] = v
    y_a = u * (ws0_ref[...] * v + bs0_ref[...])
    xs = proj[:, 2 * W_A:]
    zs = []
    for k in range(2):
        sl = slice(k * HALF_W, (k + 1) * HALF_W)
        bu = _dot(xs[:, sl].astype(BF16), bdb_ref[k])
        h0r = h0_ref[k, :, :HALF_STATE]
        h0i = h0_ref[k, :, HALF_STATE:]
        lr = lr_ref[k]
        li = li_ref[k]
        hr = lr * h0r - li * h0i + bu[:, :HALF_STATE]
        hi = lr * h0i + li * h0r + bu[:, HALF_STATE:]
        hnew_ref[k, :, :HALF_STATE] = hr
        hnew_ref[k, :, HALF_STATE:] = hi
        hcat = jnp.concatenate([hr, hi], axis=1).astype(BF16)
        y = jax.nn.gelu(_dot(hcat, bdc_ref[k]) + dskip_ref[:, sl] * xs[:, sl])
        gl = _dot(y.astype(BF16), glu_ref[k]) + bglu_ref[:, sl]
        zs.append(y * jax.nn.sigmoid(gl))
    cat = jnp.concatenate([y_a] + zs, axis=1).astype(BF16)
    mix = _dot(cat, wout_ref[...])
    x1_ref[...] = _layer_norm(alpha * x + mix, ln1g_ref[...], ln1b_ref[...])


def _mixer_sample(x, h0, lw, l, alpha):
    n = x.shape[0]
    full = lambda shape: pl.BlockSpec(shape, lambda i: (0,) * len(shape))
    return pl.pallas_call(
        functools.partial(_mixer_sample_kernel, alpha),
        grid=(1,),
        in_specs=[full(x.shape), _layer_spec(h0.shape, l)] + [_layer_spec(lw[k].shape, l) for k in SAMPLE_WEIGHTS],
        out_specs=[full((n, D_MODEL)), full((2, n, 2 * HALF_STATE)), full((n, W_A))],
        out_shape=[jax.ShapeDtypeStruct((n, D_MODEL), F32),
                   jax.ShapeDtypeStruct((2, n, 2 * HALF_STATE), F32),
                   jax.ShapeDtypeStruct((n, W_A), F32)],
        compiler_params=pltpu.CompilerParams(dimension_semantics=("arbitrary",), vmem_limit_bytes=VMEM_LIMIT),
    )(x, h0, *[lw[k] for k in SAMPLE_WEIGHTS])


def _route(x, wr, rbias):
    logits = jnp.dot(x, wr, preferred_element_type=F32, precision=HIGHEST)
    scores = jax.nn.sigmoid(logits)
    biased = scores + rbias
    lane = lax.broadcasted_iota(I32, biased.shape, 1)
    grp = lane // EXPERTS_PER_GROUP
    neg = jnp.float32(-jnp.inf)

    def top2(vals):
        m1 = jnp.max(vals, axis=-1, keepdims=True)
        i1 = jnp.min(jnp.where(vals == m1, lane, N_EXPERTS), axis=-1, keepdims=True)
        rest = jnp.where(lane == i1, neg, vals)
        m2 = jnp.max(rest, axis=-1, keepdims=True)
        i2 = jnp.min(jnp.where(rest == m2, lane, N_EXPERTS), axis=-1, keepdims=True)
        return m1, i1, m2, i2

    best = sel = None
    for g in range(N_EXPERT_GROUPS):
        m1, _, m2, _ = top2(jnp.where(grp == g, biased, neg))
        gs = m1 + m2
        if g == 0:
            best, sel = gs, jnp.zeros(gs.shape, I32)
        else:
            upd = gs > best
            sel = jnp.where(upd, g, sel)
            best = jnp.where(upd, gs, best)
    _, i1, _, i2 = top2(jnp.where(grp == sel, biased, neg))
    s1 = jnp.sum(jnp.where(lane == i1, scores, 0.0), axis=-1, keepdims=True)
    s2 = jnp.sum(jnp.where(lane == i2, scores, 0.0), axis=-1, keepdims=True)
    tot = s1 + s2
    return jnp.where(lane == i1, s1 / tot, 0.0) + jnp.where(lane == i2, s2 / tot, 0.0)


def _moe_kernel(alpha, x_ref, wr_ref, rb_ref, wg_ref, wu_ref, wd_ref, ln2g_ref, ln2b_ref,
                out_ref, xb_ref, comb_ref, acc_ref):
    step = pl.program_id(1)

    @pl.when(step == 0)
    def _():
        x = x_ref[...]
        xb_ref[...] = x.astype(BF16)
        comb_ref[...] = _route(x, wr_ref[...], rb_ref[...])
        acc_ref[...] = jnp.zeros_like(acc_ref)

    xb = xb_ref[...]
    comb = comb_ref[...]
    lane = lax.broadcasted_iota(I32, comb.shape, 1)
    acc = acc_ref[...]
    for k in range(DENSE_EXPERTS_PER_STEP):
        g = _dot(xb, wg_ref[k].astype(BF16))
        u = _dot(xb, wu_ref[k].astype(BF16))
        ce = jnp.sum(jnp.where(lane == step * DENSE_EXPERTS_PER_STEP + k, comb, 0.0), axis=-1, keepdims=True)
        h = (jax.nn.silu(g) * u * ce).astype(BF16)
        acc = acc + _dot(h, wd_ref[k].astype(BF16))
    acc_ref[...] = acc

    @pl.when(step == pl.num_programs(1) - 1)
    def _():
        out_ref[...] = _layer_norm(alpha * x_ref[...] + acc_ref[...], ln2g_ref[...], ln2b_ref[...])


def _moe_dense(x, l, wr, rb, w_gate, w_up, w_down, ln2g, ln2b, alpha, tm):
    t = x.shape[0]
    cst = lambda i, e: (0, 0)
    wsel = lambda i, e: (l, e, 0, 0)
    return pl.pallas_call(
        functools.partial(_moe_kernel, alpha),
        grid=(t // tm, N_EXPERTS // DENSE_EXPERTS_PER_STEP),
        in_specs=[pl.BlockSpec((tm, D_MODEL), lambda i, e: (i, 0)),
                  pl.BlockSpec(wr.shape, cst), pl.BlockSpec(rb.shape, cst),
                  pl.BlockSpec((None, DENSE_EXPERTS_PER_STEP, D_MODEL, D_FF_EXPERT), wsel),
                  pl.BlockSpec((None, DENSE_EXPERTS_PER_STEP, D_MODEL, D_FF_EXPERT), wsel),
                  pl.BlockSpec((None, DENSE_EXPERTS_PER_STEP, D_FF_EXPERT, D_MODEL), wsel),
                  pl.BlockSpec((None,) + ln2g.shape[1:], lambda i, e: (l, 0, 0)),
                  pl.BlockSpec((None,) + ln2b.shape[1:], lambda i, e: (l, 0, 0))],
        out_specs=pl.BlockSpec((tm, D_MODEL), lambda i, e: (i, 0)),
        out_shape=jax.ShapeDtypeStruct((t, D_MODEL), F32),
        scratch_shapes=[pltpu.VMEM((tm, D_MODEL), BF16),
                        pltpu.VMEM((tm, N_EXPERTS), F32),
                        pltpu.VMEM((tm, D_MODEL), F32)],
        compiler_params=pltpu.CompilerParams(dimension_semantics=("arbitrary", "arbitrary"),
                                             vmem_limit_bytes=VMEM_LIMIT),
    )(x, wr, rb, w_gate, w_up, w_down, ln2g, ln2b)


def _block_diag(blocks):
    eye = jnp.eye(HALF_GROUPS, dtype=blocks.dtype)
    k, g, a, b = blocks.shape
    return jnp.einsum("kgab,gh->kgahb", blocks, eye).reshape(k, g * a, g * b)


def _prep_s5_folded(lb_re, lb_im, bb_re, bb_im, c_re, c_im):
    d, g, n = lb_re.shape
    s = S5_FOLD
    pairs = g // 2
    pr, pi = [jnp.ones_like(lb_re)], [jnp.zeros_like(lb_re)]
    for _ in range(s):
        pr, pi = pr + [pr[-1] * lb_re - pi[-1] * lb_im], pi + [pr[-1] * lb_im + pi[-1] * lb_re]
    p_re, p_im = jnp.stack(pr, axis=2), jnp.stack(pi, axis=2)

    def pair_cols(re, im):
        z = jnp.zeros_like(re[:, :, 0])
        g0 = jnp.concatenate([re[:, :, 0], z, im[:, :, 0], z], axis=-1)
        g1 = jnp.concatenate([z, re[:, :, 1], z, im[:, :, 1]], axis=-1)
        return jnp.concatenate([g0, g1], axis=2)

    def pair_diag(a):
        z = jnp.zeros_like(a[:, :, 0])
        return jnp.concatenate([jnp.concatenate([a[:, :, 0], z], axis=-1),
                                jnp.concatenate([z, a[:, :, 1]], axis=-1)], axis=2)

    bt_re, bt_im = bb_re.transpose(0, 1, 3, 2)[:, :, None], bb_im.transpose(0, 1, 3, 2)[:, :, None]
    k_re, k_im = p_re[:, :, s - 1::-1, None, :], p_im[:, :, s - 1::-1, None, :]
    m1t = pair_cols((k_re * bt_re - k_im * bt_im).reshape(d, pairs, 2, -1, n),
                    (k_re * bt_im + k_im * bt_re).reshape(d, pairs, 2, -1, n))
    m1 = jnp.swapaxes(m1t, -1, -2)

    cq_re, cq_im = c_re[:, :, None], c_im[:, :, None]
    j_re, j_im = p_re[:, :, 1:, None, :], p_im[:, :, 1:, None, :]
    hpart = pair_cols((cq_re * j_re - cq_im * j_im).reshape(d, pairs, 2, -1, n),
                      (-(cq_re * j_im + cq_im * j_re)).reshape(d, pairs, 2, -1, n))

    t_re, t_im = p_re[:, :, :s, None, :], p_im[:, :, :s, None, :]
    kern = (jnp.einsum("dgtqn,dgnp->dgtqp", cq_re * t_re - cq_im * t_im, bb_re, precision=HIGHEST)
            - jnp.einsum("dgtqn,dgnp->dgtqp", cq_re * t_im + cq_im * t_re, bb_im, precision=HIGHEST))
    zero = jnp.zeros_like(kern[:, :, 0])
    kx = jnp.stack([jnp.concatenate([kern[:, :, j - q] if q <= j else zero for q in range(s)], axis=-1)
                    for j in range(s)], axis=2)
    apart = pair_diag(kx.reshape(d, pairs, 2, -1, kx.shape[-1]))

    m2 = jnp.concatenate([hpart, apart], axis=-1)
    return (m1.astype(BF16), m2.astype(BF16), p_re[:, :, s].reshape(d, 1, g * n), p_im[:, :, s].reshape(d, 1, g * n))


def _prep_all(w_in, w_out, ln_v_g, ln_v_b, w_s, b_s, a_re, a_im, log_dt, b_re, b_im, c_re, c_im,
              d_skip, w_glu, b_glu, ln1_g, ln1_b):
    d = w_in.shape[0]
    dt = jnp.exp(log_dt)[..., None]
    decay = jnp.exp(a_re * dt)
    lb_re, lb_im = decay * jnp.cos(a_im * dt), decay * jnp.sin(a_im * dt)
    den = a_re * a_re + a_im * a_im
    nr, ni = lb_re - 1.0, lb_im
    zr = (nr * a_re + ni * a_im) / den
    zi = (ni * a_re - nr * a_im) / den
    bb_re = zr[..., None] * b_re - zi[..., None] * b_im
    bb_im = zr[..., None] * b_im + zi[..., None] * b_re

    def bd(a):
        blocks = _block_diag(a.reshape((d * 2, HALF_GROUPS) + a.shape[2:]))
        return blocks.reshape((d, 2) + blocks.shape[1:])

    bdb = jnp.concatenate([bd(bb_re.transpose(0, 1, 3, 2)), bd(bb_im.transpose(0, 1, 3, 2))], axis=3)
    bdc = jnp.concatenate([bd(c_re.transpose(0, 1, 3, 2)), bd(-c_im.transpose(0, 1, 3, 2))], axis=2)
    m1, m2, lr8, li8 = _prep_s5_folded(lb_re, lb_im, bb_re, bb_im, c_re, c_im)
    return dict(
        m1=m1, m2=m2, lr8=lr8, li8=li8,
        win=w_in.astype(BF16), wout=w_out.astype(BF16),
        lnvg=ln_v_g[:, None], lnvb=ln_v_b[:, None],
        wtril=jnp.tril(w_s).astype(BF16),
        bsb=jnp.broadcast_to(b_s[..., None], (d, H_A, CHUNK, LANES)),
        ws0=jnp.repeat(w_s[:, :, 0, 0], P_A, axis=1)[:, None], bs0=jnp.repeat(b_s[:, :, 0], P_A, axis=1)[:, None],
        bdb=bdb.astype(BF16), bdc=bdc.astype(BF16),
        lr=lb_re.reshape(d, 2, 1, HALF_STATE), li=lb_im.reshape(d, 2, 1, HALF_STATE),
        dskip=d_skip.reshape(d, 1, W_B), glu=bd(w_glu).astype(BF16),
        bglu=b_glu.reshape(d, 1, W_B), ln1g=ln1_g[:, None], ln1b=ln1_b[:, None])


def _state_to_cols(h_re, h_im):
    d, b = h_re.shape[:2]
    re = h_re.reshape(d, b, 2, HALF_STATE)
    im = h_im.reshape(d, b, 2, HALF_STATE)
    return jnp.concatenate([re, im], axis=3).transpose(0, 2, 1, 3)


def _cols_to_state(h):
    d, _, b, _ = h.shape
    re = h[..., :HALF_STATE].transpose(0, 2, 1, 3).reshape(d, b, G_B, N_STATE)
    im = h[..., HALF_STATE:].transpose(0, 2, 1, 3).reshape(d, b, G_B, N_STATE)
    return re, im


def _pairs_to_state(h):
    d, b, _ = h.shape
    h = h.reshape(d, b, G_B // 2, 2, 2, N_STATE)
    return h[:, :, :, 0].reshape(d, b, G_B, N_STATE), h[:, :, :, 1].reshape(d, b, G_B, N_STATE)


def kernel(x_prompt, x_sample, state_ssm_re, state_ssm_im, w_in, w_out, ln_v_g, ln_v_b, w_s, b_s, ssm_a_re, ssm_a_im, ssm_log_dt, ssm_b_re, ssm_b_im, ssm_c_re, ssm_c_im, ssm_d, w_glu, b_glu, ln1_g, ln1_b, ln2_g, ln2_b, w_router, router_bias, w_gate, w_up, w_down):
    depth = w_in.shape[0]
    alpha = float((2 * depth) ** 0.25)
    nb, seq, _ = x_prompt.shape
    ns = x_sample.shape[0]
    tokens = nb * seq
    nt = tokens // TILE_M + N_CLASSES
    rb = router_bias[None]
    wrt = w_router.T
    wrp = jnp.pad(w_router, ((0, 0), (0, LANES - N_EXPERTS)))
    rbcol = router_bias[:, None]
    tok = np.arange(nb * CHUNK)
    tri = jnp.asarray(tok[:, None] < tok[None, :], BF16)
    lw = _prep_all(w_in, w_out, ln_v_g, ln_v_b, w_s, b_s, ssm_a_re, ssm_a_im, ssm_log_dt,
                   ssm_b_re, ssm_b_im, ssm_c_re, ssm_c_im, ssm_d, w_glu, b_glu, ln1_g, ln1_b)
    shared = (wrt, rbcol, tri)
    ln2g, ln2b = ln2_g[:, None], ln2_b[:, None]
    h0s = _state_to_cols(state_ssm_re, state_ssm_im)
    xp = x_prompt
    pos = None
    xs = x_sample.reshape(ns, D_MODEL)
    pr_h, sm_h, sm_v = [], [], []
    for l in range(depth):
        x1t, hfin, cls, rank, counts = _mixer_prompt(xp, lw, shared, l, alpha, nb, seq, pos)
        pos, zstart, zlen, tail, tables = _plan(cls, rank, counts, nt)
        x_sorted = _dispatch(x1t, pos, zstart, zlen, tail, nb * CHUNK, nt * TILE_M)
        xp = _pair_experts(x_sorted, l, tables, wrp, w_gate, w_up, w_down, ln2g, ln2b, alpha)
        pr_h.append(hfin)

        x1s, hnew, v_new = _mixer_sample(xs, h0s, lw, l, alpha)
        xs = _moe_dense(x1s, l, w_router, rb, w_gate, w_up, w_down, ln2g, ln2b, alpha, tm=ns)
        sm_h.append(hnew)
        sm_v.append(v_new.reshape(ns, 1, W_A))
    y_prompt = _ungather(xp, pos, nb, seq)
    pr_re, pr_im = _pairs_to_state(jnp.stack(pr_h))
    sm_re, sm_im = _cols_to_state(jnp.stack(sm_h))
    return (y_prompt, xs.reshape(ns, 1, D_MODEL), pr_re, pr_im, sm_re, sm_im, jnp.stack(sm_v))
```

```python
import functools

import jax
import jax.numpy as jnp
import numpy as np
from jax import lax
from jax.experimental import pallas as pl
from jax.experimental.pallas import tpu as pltpu

D_MODEL = 1024
W_A = 512
W_B = 512
CHUNK = 128
H_A = 4
P_A = W_A // H_A
GROUP_B = 16
G_B = W_B // GROUP_B
N_STATE = 64
N_EXPERTS = 16
N_EXPERT_GROUPS = 4
EXPERTS_PER_GROUP = N_EXPERTS // N_EXPERT_GROUPS
D_FF_EXPERT = D_MODEL // 4
LN_EPS = 1e-5

LANES = 128
SUBLANES = 8
HALF_GROUPS = 16
HALF_W = HALF_GROUPS * GROUP_B
HALF_STATE = HALF_GROUPS * N_STATE
VMEM_LIMIT = 56 * 1024 * 1024

PAIRS = ((0, 1), (0, 2), (0, 3), (1, 3), (1, 2), (3, 2))
N_CLASSES = N_EXPERT_GROUPS * len(PAIRS)
CLASS_ROWS = 32
TILE_M = 256
TOK_ROWS = D_MODEL // LANES
ZERO_TOKENS = TILE_M // 2
DMA_UNROLL = 16
DENSE_EXPERTS_PER_STEP = 4
S5_FOLD = 8
S5_PAIRS = G_B // 2
EA_TABLE = np.array([EXPERTS_PER_GROUP * g + a for g in range(N_EXPERT_GROUPS) for a, _ in PAIRS], np.int32)
EB_TABLE = np.array([EXPERTS_PER_GROUP * g + b for g in range(N_EXPERT_GROUPS) for _, b in PAIRS], np.int32)

F32 = jnp.float32
BF16 = jnp.bfloat16
I32 = jnp.int32
HIGHEST = lax.Precision.HIGHEST


def _layer_norm(x, g, b):
    mu = jnp.mean(x, axis=-1, keepdims=True)
    xc = x - mu
    var = jnp.mean(xc * xc, axis=-1, keepdims=True)
    return xc * lax.rsqrt(var + LN_EPS) * g + b


def _dot(a, b):
    return jnp.dot(a, b, preferred_element_type=F32)


def _route_classes(x1, wrt, rbcol):
    def split(a):
        hi = a.astype(BF16)
        return hi, (a - hi.astype(F32)).astype(BF16)

    def dot_t(a, b):
        return lax.dot_general(a, b, (((1,), (1,)), ((), ())), preferred_element_type=F32)
    w_hi, w_lo = split(wrt)
    x_hi, x_lo = split(x1)
    logits_t = dot_t(w_hi, x_hi) + (dot_t(w_hi, x_lo) + dot_t(w_lo, x_hi))
    biased = jax.nn.sigmoid(logits_t) + rbcol
    rows = [biased[e:e + 1, :] for e in range(N_EXPERTS)]
    n = EXPERTS_PER_GROUP

    best = sel = None
    for g in range(N_EXPERT_GROUPS):
        v = rows[n * g:n * (g + 1)]
        gs = None
        for a, b in PAIRS:
            s = v[a] + v[b]
            gs = s if gs is None else jnp.maximum(gs, s)
        if g == 0:
            best, sel = gs, jnp.zeros(gs.shape, I32)
        else:
            upd = gs > best
            sel = jnp.where(upd, g, sel)
            best = jnp.where(upd, gs, best)

    cls = jnp.zeros(sel.shape, I32)
    for g in range(N_EXPERT_GROUPS):
        v = rows[n * g:n * (g + 1)]
        lo = jnp.full(sel.shape, n, I32)
        hi = jnp.full(sel.shape, -1, I32)
        for i in range(n):
            before = jnp.zeros(sel.shape, I32)
            for j in range(n):
                if j < i:
                    before = before + (v[j] >= v[i]).astype(I32)
                elif j > i:
                    before = before + (v[j] > v[i]).astype(I32)
            member = before < 2
            lo = jnp.where(member, jnp.minimum(lo, i), lo)
            hi = jnp.where(member, jnp.maximum(hi, i), hi)
        pidx = jnp.zeros(sel.shape, I32)
        for k, (a, b) in enumerate(PAIRS):
            pidx = jnp.where((lo == min(a, b)) & (hi == max(a, b)), k, pidx)
        cls = jnp.where(sel == g, g * len(PAIRS) + pidx, cls)
    return cls


def _to_token_tiles(ref, row0, x):
    n = x.shape[0]
    for c in range(TOK_ROWS):
        ref[pl.ds(row0 * TOK_ROWS + c, n, stride=TOK_ROWS), :] = x[:, c * LANES:(c + 1) * LANES]


def _from_token_tiles(ref, row0, n):
    return jnp.concatenate(
        [ref[pl.ds(row0 * TOK_ROWS + c, n, stride=TOK_ROWS), :] for c in range(TOK_ROWS)], axis=1)


def _gathered_tokens(step, nsteps, pos8_ref, src_hbm, bufs, sems, rows, consume):
    def start(s, base):
        def body(g, c):
            idx0 = base + g * DMA_UNROLL
            row0 = pl.multiple_of(g * (DMA_UNROLL * TOK_ROWS), DMA_UNROLL * TOK_ROWS)
            for i in range(DMA_UNROLL):
                p8 = pl.multiple_of(pos8_ref[idx0 + i], TOK_ROWS)
                pltpu.make_async_copy(src_hbm.at[pl.ds(p8, TOK_ROWS), :],
                                      bufs[s].at[pl.ds(row0 + i * TOK_ROWS, TOK_ROWS), :],
                                      sems.at[s]).start(priority=i % 2)
            return c
        lax.fori_loop(0, rows // DMA_UNROLL, body, 0)

    @pl.when(step == 0)
    def _():
        start(0, 0)

    for s in range(2):
        @pl.when(lax.rem(step, 2) == s)
        def _(s=s):
            pltpu.make_async_copy(src_hbm.at[pl.ds(0, rows * TOK_ROWS), :], bufs[s], sems.at[s]).wait()

            @pl.when(step + 1 < nsteps)
            def _():
                start(1 - s, (step + 1) * rows)
            consume(bufs[s])


def _mixer_kernel(alpha, nb, gather_in, *refs):
    if gather_in:
        pos8_ref, zs_hbm = refs[:2]
        refs = refs[2:]
    else:
        x_ref = refs[0]
        refs = refs[1:]
    (win_ref, wout_ref, lnvg_ref, lnvb_ref, wtril_ref, bsb_ref, m1_ref, m2_ref, lr8_ref, li8_ref,
     dskip_ref, glu_ref, bglu_ref, ln1g_ref, ln1b_ref, wrt_ref, rbcol_ref, tri_ref,
     x1t_ref, hfin_ref, cls_ref, rank_ref, cnt_ref,
     xb_ref, xs_slab, xs_scb, xst_ref, ht_ref, yt_ref, mix_ref, hstate_ref, carry_ref) = refs[:32]
    lt = CHUNK
    rows = nb * lt
    pitch = lt + SUBLANES
    step = pl.program_id(0)
    nsteps = pl.num_programs(0)

    @pl.when(step == 0)
    def _():
        hstate_ref[...] = jnp.zeros_like(hstate_ref)
        carry_ref[...] = jnp.zeros_like(carry_ref)

    if gather_in:
        xin_ref, xbuf0, xbuf1, gsem = refs[32:36]

        def consume(buf):
            xin_ref[...] = _from_token_tiles(buf, 0, rows)
        _gathered_tokens(step, nsteps, pos8_ref, zs_hbm, (xbuf0, xbuf1), gsem, rows, consume)

        def load_x():
            return xin_ref[...]
    else:
        def load_x():
            return x_ref[...].reshape(rows, D_MODEL)

    xb_ref[...] = load_x().astype(BF16)

    fold = S5_FOLD
    nchunk = lt // fold
    cb = nchunk * nb
    xs = _dot(xb_ref[...], win_ref[:, 2 * W_A:])
    for j in range(W_B // LANES):
        for b in range(nb):
            xs_slab[j, b * pitch:b * pitch + lt, :] = xs[b * lt:(b + 1) * lt, j * LANES:(j + 1) * LANES]

    def to_scb(t, c):
        s_, ch_ = lax.bitwise_and(t, fold - 1), lax.shift_right_logical(t, fold.bit_length() - 1)
        r0 = pl.multiple_of(s_ * cb + ch_ * nb, SUBLANES)
        for j in range(W_B // LANES):
            xs_scb[pl.ds(r0, nb), j * LANES:(j + 1) * LANES] = xs_slab[j, pl.ds(t, nb, stride=pitch), :]
        return c
    lax.fori_loop(0, lt, to_scb, 0, unroll=4)
    xst_ref[...] = xs_scb[...].T.astype(BF16)

    vg = jax.nn.gelu(_dot(xb_ref[...], win_ref[:, W_A:2 * W_A]))
    v = _layer_norm(vg, lnvg_ref[...], lnvb_ref[...]).astype(BF16)
    u = jax.nn.gelu(_dot(xb_ref[...], win_ref[:, :W_A]))
    for h in range(H_A):
        hs = slice(h * P_A, (h + 1) * P_A)
        vcat = jnp.concatenate([v[b * lt:(b + 1) * lt, hs] for b in range(nb)], axis=1)
        o = _dot(wtril_ref[h], vcat)
        for b in range(nb):
            rs = slice(b * lt, (b + 1) * lt)
            mix_ref[rs, hs] = u[rs, hs] * (o[:, b * LANES:(b + 1) * LANES] + bsb_ref[h])

    def chunk_inputs(pr):
        return jnp.concatenate(
            [xst_ref[(2 * pr + gi) * GROUP_B:(2 * pr + gi + 1) * GROUP_B, s * cb:(s + 1) * cb]
             for gi in range(2) for s in range(fold)], axis=0)

    pw = 2 * 2 * N_STATE
    for pr in range(S5_PAIRS):
        ht_ref[:, pr * pw:(pr + 1) * pw] = _dot(m1_ref[pr], chunk_inputs(pr)).T

    half_pairs = S5_PAIRS // 2
    for half in range(2):
        c0 = half * half_pairs * pw
        lr8 = [jnp.broadcast_to(lr8_ref[:, (half * half_pairs + p) * LANES:(half * half_pairs + p + 1) * LANES],
                                (nb, LANES)) for p in range(half_pairs)]
        li8 = [jnp.broadcast_to(li8_ref[:, (half * half_pairs + p) * LANES:(half * half_pairs + p + 1) * LANES],
                                (nb, LANES)) for p in range(half_pairs)]
        h = [hstate_ref[:, c0 + q * LANES:c0 + (q + 1) * LANES] for q in range(2 * half_pairs)]
        for c in range(nchunk):
            rs = slice(c * nb, (c + 1) * nb)
            for p in range(half_pairs):
                re_sl = slice(c0 + p * pw, c0 + p * pw + LANES)
                im_sl = slice(c0 + p * pw + LANES, c0 + (p + 1) * pw)
                hr, hi = h[2 * p], h[2 * p + 1]
                ur, ui = ht_ref[rs, re_sl], ht_ref[rs, im_sl]
                ht_ref[rs, re_sl] = hr
                ht_ref[rs, im_sl] = hi
                h[2 * p] = lr8[p] * hr - li8[p] * hi + ur
                h[2 * p + 1] = lr8[p] * hi + li8[p] * hr + ui
        for q in range(2 * half_pairs):
            hstate_ref[:, c0 + q * LANES:c0 + (q + 1) * LANES] = h[q]
    hfin_ref[...] = hstate_ref[...]

    for pr in range(S5_PAIRS):
        rhs = jnp.concatenate([ht_ref[:, pr * pw:(pr + 1) * pw].T.astype(BF16), chunk_inputs(pr)], axis=0)
        yt = _dot(m2_ref[pr], rhs)
        for gi in range(2):
            for s in range(fold):
                r0 = (gi * fold + s) * GROUP_B
                yt_ref[(2 * pr + gi) * GROUP_B:(2 * pr + gi + 1) * GROUP_B, s * cb:(s + 1) * cb] = (
                    yt[r0:r0 + GROUP_B, :])

    y = jax.nn.gelu(yt_ref[...].T + dskip_ref[...] * xs_scb[...])
    yb = y.astype(BF16)
    spitch = cb + SUBLANES
    for k in range(2):
        sl = slice(k * HALF_W, (k + 1) * HALF_W)
        gl = _dot(yb[:, sl], glu_ref[k]) + bglu_ref[:, sl]
        z = y[:, sl] * jax.nn.sigmoid(gl)
        for j in range(HALF_W // LANES):
            for s in range(fold):
                xs_slab[k * (HALF_W // LANES) + j, s * spitch:s * spitch + cb, :] = (
                    z[s * cb:(s + 1) * cb, j * LANES:(j + 1) * LANES])

    def to_bt(ch, c):
        for b in range(nb):
            dst = pl.multiple_of(b * lt + ch * fold, SUBLANES)
            for j in range(W_B // LANES):
                mix_ref[pl.ds(dst, fold), W_A + j * LANES:W_A + (j + 1) * LANES] = (
                    xs_slab[j, pl.ds(ch * nb + b, fold, stride=spitch), :])
        return c
    lax.fori_loop(0, nchunk, to_bt, 0)

    mix = _dot(mix_ref[...].astype(BF16), wout_ref[...])
    x1 = _layer_norm(alpha * load_x() + mix, ln1g_ref[...], ln1b_ref[...])
    _to_token_tiles(x1t_ref, 0, x1)

    cls = _route_classes(x1, wrt_ref[...], rbcol_ref[...])
    crow = lax.broadcasted_iota(I32, (CLASS_ROWS, rows), 0)
    onehot = jnp.where(crow == cls, 1.0, 0.0)
    prefix = _dot(onehot.astype(BF16), tri_ref[...])
    carry = carry_ref[:, 0:1]
    rank = jnp.sum(onehot * (prefix + carry), axis=0, keepdims=True)
    cls_ref[...] = cls.reshape(1, 1, rows)
    rank_ref[...] = rank.astype(I32).reshape(1, 1, rows)
    carry_ref[...] = carry_ref[...] + jnp.sum(onehot, axis=1, keepdims=True)
    cnt_ref[...] = carry_ref[...]


def _const_spec(shape):
    nd = len(shape)
    return pl.BlockSpec(shape, lambda *_: (0,) * nd, pipeline_mode=pl.Buffered(1))


def _layer_spec(shape, l):
    nd = len(shape)
    return pl.BlockSpec((None,) + tuple(shape[1:]), lambda *_: (l,) + (0,) * (nd - 1),
                        pipeline_mode=pl.Buffered(1))


MIXER_WEIGHTS = ("win", "wout", "lnvg", "lnvb", "wtril", "bsb", "m1", "m2", "lr8", "li8", "dskip", "glu",
                 "bglu", "ln1g", "ln1b")
SAMPLE_WEIGHTS = ("win", "wout", "lnvg", "lnvb", "ws0", "bs0", "bdb", "bdc", "lr", "li", "dskip", "glu",
                  "bglu", "ln1g", "ln1b")


def _mixer_prompt(x, lw, shared, l, alpha, nb, seq, pos_prev=None):
    lt = CHUNK
    rows = nb * lt
    nsteps = seq // lt
    gather_in = pos_prev is not None
    weights = tuple(lw[k] for k in MIXER_WEIGHTS) + tuple(shared)
    wspecs = [_layer_spec(lw[k].shape, l) for k in MIXER_WEIGHTS] + [_const_spec(w.shape) for w in shared]
    if gather_in:
        x_spec = pl.BlockSpec(memory_space=pl.ANY)
    else:
        x_spec = pl.BlockSpec((nb, lt, D_MODEL), lambda i, *_: (0, i, 0))
    scratch = [
        pltpu.VMEM((rows, D_MODEL), BF16),
        pltpu.VMEM((W_B // LANES, nb * (lt + SUBLANES), LANES), F32),
        pltpu.VMEM((rows, W_B), F32),
        pltpu.VMEM((W_B, rows), BF16),
        pltpu.VMEM((rows // S5_FOLD, 2 * G_B * N_STATE), F32),
        pltpu.VMEM((W_B, rows), F32),
        pltpu.VMEM((rows, D_MODEL), F32),
        pltpu.VMEM((nb, 2 * G_B * N_STATE), F32),
        pltpu.VMEM((CLASS_ROWS, LANES), F32),
    ]
    if gather_in:
        scratch += [pltpu.VMEM((rows, D_MODEL), F32),
                    pltpu.VMEM((rows * TOK_ROWS, LANES), F32), pltpu.VMEM((rows * TOK_ROWS, LANES), F32),
                    pltpu.SemaphoreType.DMA((2,))]
    grid_spec = pltpu.PrefetchScalarGridSpec(
        num_scalar_prefetch=1 if gather_in else 0,
        grid=(nsteps,),
        in_specs=[x_spec] + wspecs,
        out_specs=[pl.BlockSpec((rows * TOK_ROWS, LANES), lambda i, *_: (i, 0)),
                   pl.BlockSpec((nb, 2 * G_B * N_STATE), lambda i, *_: (0, 0)),
                   pl.BlockSpec((1, 1, rows), lambda i, *_: (i, 0, 0)),
                   pl.BlockSpec((1, 1, rows), lambda i, *_: (i, 0, 0)),
                   pl.BlockSpec((CLASS_ROWS, LANES), lambda i, *_: (0, 0))],
        scratch_shapes=scratch)
    args = ((pos_prev, x) if gather_in else (x,)) + weights
    return pl.pallas_call(
        functools.partial(_mixer_kernel, alpha, nb, gather_in),
        grid_spec=grid_spec,
        out_shape=[jax.ShapeDtypeStruct((nb * seq * TOK_ROWS, LANES), F32),
                   jax.ShapeDtypeStruct((nb, 2 * G_B * N_STATE), F32),
                   jax.ShapeDtypeStruct((nsteps, 1, rows), I32),
                   jax.ShapeDtypeStruct((nsteps, 1, rows), I32),
                   jax.ShapeDtypeStruct((CLASS_ROWS, LANES), F32)],
        compiler_params=pltpu.CompilerParams(dimension_semantics=("arbitrary",),
                                             vmem_limit_bytes=VMEM_LIMIT),
    )(*args)


def _dispatch_kernel(rows, pos8_ref, zstart_ref, zlen_ref, tail_ref, x_ref, xs_hbm, zero_ref, sem, zsem):
    step = pl.program_id(0)
    ztok = ZERO_TOKENS

    @pl.when(step == 0)
    def _():
        zero_ref[...] = jnp.zeros_like(zero_ref)
        pieces = []
        for c in range(N_CLASSES):
            start = zstart_ref[c]
            zlen = zlen_ref[c]
            p = TILE_M // 2
            while p >= 1:
                hit = (zlen & p) != 0
                pieces.append((hit, pltpu.make_async_copy(
                    zero_ref.at[pl.ds(0, p * TOK_ROWS), :],
                    xs_hbm.at[pl.ds(pl.multiple_of(start * TOK_ROWS, TOK_ROWS), p * TOK_ROWS), :], zsem)))
                start = start + jnp.where(hit, p, 0)
                p //= 2
        for hit, cp in pieces:
            pl.when(hit)(cp.start)
        for hit, cp in pieces:
            pl.when(hit)(cp.wait)

        zrows = ztok * TOK_ROWS
        first = tail_ref[0] // ztok

        def tail_copy(q):
            return pltpu.make_async_copy(
                zero_ref, xs_hbm.at[pl.ds(pl.multiple_of(q * zrows, zrows), zrows), :], zsem)

        def tail_start(q, c):
            tail_copy(q).start()
            return c

        def tail_wait(q, c):
            tail_copy(q).wait()
            return c
        lax.fori_loop(first, xs_hbm.shape[0] // zrows, tail_start, 0)
        lax.fori_loop(first, xs_hbm.shape[0] // zrows, tail_wait, 0)

    base = step * rows

    def body(g, c):
        idx0 = base + g * DMA_UNROLL
        row0 = pl.multiple_of(g * (DMA_UNROLL * TOK_ROWS), DMA_UNROLL * TOK_ROWS)
        for i in range(DMA_UNROLL):
            p8 = pl.multiple_of(pos8_ref[idx0 + i], TOK_ROWS)
            pltpu.make_async_copy(x_ref.at[pl.ds(row0 + i * TOK_ROWS, TOK_ROWS), :],
                                  xs_hbm.at[pl.ds(p8, TOK_ROWS), :], sem).start(priority=i % 2)
        return c
    lax.fori_loop(0, rows // DMA_UNROLL, body, 0)
    pltpu.make_async_copy(x_ref, xs_hbm.at[pl.ds(0, rows * TOK_ROWS), :], sem).wait()


def _dispatch(x1t, pos8, zstart, zlen, tail, rows, ns_tokens):
    return pl.pallas_call(
        functools.partial(_dispatch_kernel, rows),
        grid_spec=pltpu.PrefetchScalarGridSpec(
            num_scalar_prefetch=4,
            grid=(x1t.shape[0] // (rows * TOK_ROWS),),
            in_specs=[pl.BlockSpec((rows * TOK_ROWS, LANES), lambda i, *_: (i, 0))],
            out_specs=pl.BlockSpec(memory_space=pl.ANY),
            scratch_shapes=[pltpu.VMEM((ZERO_TOKENS * TOK_ROWS, LANES), F32),
                            pltpu.SemaphoreType.DMA(()), pltpu.SemaphoreType.DMA(())]),
        out_shape=jax.ShapeDtypeStruct((ns_tokens * TOK_ROWS, LANES), F32),
        compiler_params=pltpu.CompilerParams(dimension_semantics=("arbitrary",),
                                             vmem_limit_bytes=VMEM_LIMIT),
    )(pos8, zstart, zlen, tail, x1t)


def _pair_kernel(alpha, l, tidx_ref, tcls_ref, nused_ref,
                 ea_ref, cha_ref, bufa_ref, nxa_ref, hna_ref, eb_ref, chb_ref, bufb_ref, nxb_ref, hnb_ref,
                 x_ref, wrp_ref, wg_hbm, wu_hbm, wd_hbm,
                 ln2g_ref, ln2b_ref, z_ref, w1_ref, w2_ref, xprev_ref, moe_ref, sg_ref, su_ref, sd_ref, wsem):
    j = pl.program_id(0)
    f = D_FF_EXPERT
    nused = nused_ref[0]
    slots = ((ea_ref, cha_ref, bufa_ref, nxa_ref, hna_ref), (eb_ref, chb_ref, bufb_ref, nxb_ref, hnb_ref))

    def weight_copies(k, e, buf):
        return [pltpu.make_async_copy(wg_hbm.at[l, e], sg_ref.at[k, buf], wsem.at[k, buf]),
                pltpu.make_async_copy(wu_hbm.at[l, e], su_ref.at[k, buf], wsem.at[k, buf]),
                pltpu.make_async_copy(wd_hbm.at[l, e], sd_ref.at[k, buf], wsem.at[k, buf])]

    @pl.when(j == 0)
    def _():
        xprev_ref[...] = jnp.zeros_like(xprev_ref)
        moe_ref[...] = jnp.zeros_like(moe_ref)
        w1_ref[:, 4 * f:] = wrp_ref[...].astype(BF16)
        for k, (e_ref, _, buf_ref, _, _) in enumerate(slots):
            for cp in weight_copies(k, e_ref[0], buf_ref[0]):
                cp.start()

    @pl.when(j > nused)
    def _():
        z_ref[...] = jnp.zeros_like(z_ref)

    @pl.when(j <= nused)
    def _():
        class_start = jnp.logical_or(j == 0, tcls_ref[j] != tcls_ref[jnp.maximum(j - 1, 0)])

        for k, (e_ref, ch_ref, buf_ref, nx_ref, hn_ref) in enumerate(slots):
            @pl.when(jnp.logical_and(class_start, ch_ref[j] != 0))
            def _(k=k, e_ref=e_ref, buf_ref=buf_ref, nx_ref=nx_ref, hn_ref=hn_ref):
                buf = buf_ref[j]
                for cp in weight_copies(k, e_ref[j], buf):
                    cp.wait()

                @pl.when(hn_ref[j] != 0)
                def _():
                    for cp in weight_copies(k, nx_ref[j], 1 - buf):
                        cp.start()
                w1_ref[:, 2 * k * f:(2 * k + 1) * f] = sg_ref[k, buf].astype(BF16)
                w1_ref[:, (2 * k + 1) * f:(2 * k + 2) * f] = su_ref[k, buf].astype(BF16)
                w2_ref[k * f:(k + 1) * f, :] = sd_ref[k, buf].astype(BF16)

        _to_token_tiles(z_ref, 0, _layer_norm(alpha * xprev_ref[...] + moe_ref[...], ln2g_ref[...], ln2b_ref[...]))

        x = _from_token_tiles(x_ref, 0, TILE_M)
        gu = _dot(x.astype(BF16), w1_ref[...])
        scores = jax.nn.sigmoid(gu[:, 4 * f:])
        lane = lax.broadcasted_iota(I32, scores.shape, 1)
        sa = jnp.sum(jnp.where(lane == ea_ref[j], scores, 0.0), axis=-1, keepdims=True)
        sb = jnp.sum(jnp.where(lane == eb_ref[j], scores, 0.0), axis=-1, keepdims=True)
        tot = sa + sb
        ha = jax.nn.silu(gu[:, 0 * f:1 * f]) * gu[:, 1 * f:2 * f] * (sa / tot)
        hb = jax.nn.silu(gu[:, 2 * f:3 * f]) * gu[:, 3 * f:4 * f] * (sb / tot)
        xprev_ref[...] = x
        moe_ref[...] = _dot(jnp.concatenate([ha, hb], axis=1).astype(BF16), w2_ref[...])


def _pair_experts(xs, l, tables, wrp, w_gate, w_up, w_down, ln2g, ln2b, alpha):
    nsteps = tables[0].shape[0]
    cst = lambda j, *_: (0, 0)
    lsel = lambda j, *_: (l, 0, 0)
    hbm = pl.BlockSpec(memory_space=pl.ANY)
    return pl.pallas_call(
        functools.partial(_pair_kernel, alpha, l),
        grid_spec=pltpu.PrefetchScalarGridSpec(
            num_scalar_prefetch=len(tables),
            grid=(nsteps,),
            in_specs=[pl.BlockSpec((TILE_M * TOK_ROWS, LANES), lambda j, ti, *_: (ti[j], 0)),
                      pl.BlockSpec(wrp.shape, cst), hbm, hbm, hbm,
                      pl.BlockSpec((None,) + ln2g.shape[1:], lsel), pl.BlockSpec((None,) + ln2b.shape[1:], lsel)],
            out_specs=pl.BlockSpec((TILE_M * TOK_ROWS, LANES), lambda j, *_: (jnp.maximum(j - 1, 0), 0)),
            scratch_shapes=[pltpu.VMEM((D_MODEL, 4 * D_FF_EXPERT + LANES), BF16),
                            pltpu.VMEM((2 * D_FF_EXPERT, D_MODEL), BF16),
                            pltpu.VMEM((TILE_M, D_MODEL), F32), pltpu.VMEM((TILE_M, D_MODEL), F32),
                            pltpu.VMEM((2, 2, D_MODEL, D_FF_EXPERT), F32),
                            pltpu.VMEM((2, 2, D_MODEL, D_FF_EXPERT), F32),
                            pltpu.VMEM((2, 2, D_FF_EXPERT, D_MODEL), F32),
                            pltpu.SemaphoreType.DMA((2, 2))]),
        out_shape=jax.ShapeDtypeStruct(xs.shape, F32),
        compiler_params=pltpu.CompilerParams(dimension_semantics=("arbitrary",),
                                             vmem_limit_bytes=VMEM_LIMIT),
    )(*tables, xs, wrp, w_gate, w_up, w_down, ln2g, ln2b)


def _ungather_kernel(nb, pos8_ref, zs_hbm, out_ref, buf0, buf1, sems):
    rows = nb * CHUNK

    def consume(buf):
        out_ref[...] = _from_token_tiles(buf, 0, rows).reshape(nb, CHUNK, D_MODEL)
    _gathered_tokens(pl.program_id(0), pl.num_programs(0), pos8_ref, zs_hbm, (buf0, buf1), sems, rows, consume)


def _ungather(zs, pos8, nb, seq):
    rows = nb * CHUNK
    return pl.pallas_call(
        functools.partial(_ungather_kernel, nb),
        grid_spec=pltpu.PrefetchScalarGridSpec(
            num_scalar_prefetch=1,
            grid=(seq // CHUNK,),
            in_specs=[pl.BlockSpec(memory_space=pl.ANY)],
            out_specs=pl.BlockSpec((nb, CHUNK, D_MODEL), lambda i, *_: (0, i, 0)),
            scratch_shapes=[pltpu.VMEM((rows * TOK_ROWS, LANES), F32), pltpu.VMEM((rows * TOK_ROWS, LANES), F32),
                            pltpu.SemaphoreType.DMA((2,))]),
        out_shape=jax.ShapeDtypeStruct((nb, seq, D_MODEL), F32),
        compiler_params=pltpu.CompilerParams(dimension_semantics=("arbitrary",),
                                             vmem_limit_bytes=VMEM_LIMIT),
    )(pos8, zs)


def _plan(cls, rank, counts, nt):
    cnt = counts[:N_CLASSES, 0].astype(I32)
    ntile = (cnt + TILE_M - 1) // TILE_M
    padded = ntile * TILE_M
    off = jnp.cumsum(padded) - padded
    classes = jnp.arange(N_CLASSES, dtype=I32)
    pos8 = (rank.reshape(-1) + jnp.sum(jnp.where(cls.reshape(-1, 1) == classes, off, 0), axis=1)) * TOK_ROWS
    tile_end = jnp.cumsum(ntile)
    nused = tile_end[-1:].astype(I32)
    tidx = jnp.minimum(jnp.arange(nt + 1, dtype=I32), nused - 1)
    tsel = tile_end[None, :] <= tidx[:, None]
    tcls = jnp.sum(tsel.astype(I32), axis=1)
    used = ntile > 0
    earlier = used[None, :] & (classes[None, :] < classes[:, None])
    later = classes[None, :] > classes[:, None]
    prev_used = jnp.max(jnp.where(earlier, classes[None, :], -1), axis=1)
    per_class = []
    for table in (EA_TABLE, EB_TABLE):
        e_c = jnp.asarray(table)
        change = used & ((prev_used < 0) | (e_c[jnp.maximum(prev_used, 0)] != e_c))
        nxt = jnp.min(jnp.where(change[None, :] & later, classes[None, :], N_CLASSES), axis=1)
        per_class += [e_c, change.astype(I32), (jnp.cumsum(change.astype(I32)) - 1) & 1,
                      e_c[jnp.minimum(nxt, N_CLASSES - 1)], (nxt < N_CLASSES).astype(I32)]
    onehot = tcls[:, None] == classes
    per_tile = jnp.sum(jnp.where(onehot[None], jnp.stack(per_class)[:, None, :], 0), axis=2)
    tables = (tidx, tcls, nused) + tuple(per_tile)
    return pos8.astype(I32), off + cnt, padded - cnt, nused * TILE_M, tables


def _mixer_sample_kernel(alpha, x_ref, h0_ref, win_ref, wout_ref, lnvg_ref, lnvb_ref, ws0_ref, bs0_ref,
                         bdb_ref, bdc_ref, lr_ref, li_ref, dskip_ref, glu_ref, bglu_ref,
                         ln1g_ref, ln1b_ref,
                         x1_ref, hnew_ref, v_ref):
    x = x_ref[...]
    proj = _dot(x.astype(BF16), win_ref[...])
    u = jax.nn.gelu(proj[:, :W_A])
    v = _layer_norm(jax.nn.gelu(proj[:, W_A:2 * W_A]), lnvg_ref[...], lnvb_ref[...])
    v_ref[...] = v
    y_a = u * (ws0_ref[...] * v + bs0_ref[...])
    xs = proj[:, 2 * W_A:]
    zs = []
    for k in range(2):
        sl = slice(k * HALF_W, (k + 1) * HALF_W)
        bu = _dot(xs[:, sl].astype(BF16), bdb_ref[k])
        h0r = h0_ref[k, :, :HALF_STATE]
        h0i = h0_ref[k, :, HALF_STATE:]
        lr = lr_ref[k]
        li = li_ref[k]
        hr = lr * h0r - li * h0i + bu[:, :HALF_STATE]
        hi = lr * h0i + li * h0r + bu[:, HALF_STATE:]
        hnew_ref[k, :, :HALF_STATE] = hr
        hnew_ref[k, :, HALF_STATE:] = hi
        hcat = jnp.concatenate([hr, hi], axis=1).astype(BF16)
        y = jax.nn.gelu(_dot(hcat, bdc_ref[k]) + dskip_ref[:, sl] * xs[:, sl])
        gl = _dot(y.astype(BF16), glu_ref[k]) + bglu_ref[:, sl]
        zs.append(y * jax.nn.sigmoid(gl))
    cat = jnp.concatenate([y_a] + zs, axis=1).astype(BF16)
    mix = _dot(cat, wout_ref[...])
    x1_ref[...] = _layer_norm(alpha * x + mix, ln1g_ref[...], ln1b_ref[...])


def _mixer_sample(x, h0, lw, l, alpha):
    n = x.shape[0]
    full = lambda shape: pl.BlockSpec(shape, lambda i: (0,) * len(shape))
    return pl.pallas_call(
        functools.partial(_mixer_sample_kernel, alpha),
        grid=(1,),
        in_specs=[full(x.shape), _layer_spec(h0.shape, l)] + [_layer_spec(lw[k].shape, l) for k in SAMPLE_WEIGHTS],
        out_specs=[full((n, D_MODEL)), full((2, n, 2 * HALF_STATE)), full((n, W_A))],
        out_shape=[jax.ShapeDtypeStruct((n, D_MODEL), F32),
                   jax.ShapeDtypeStruct((2, n, 2 * HALF_STATE), F32),
                   jax.ShapeDtypeStruct((n, W_A), F32)],
        compiler_params=pltpu.CompilerParams(dimension_semantics=("arbitrary",), vmem_limit_bytes=VMEM_LIMIT),
    )(x, h0, *[lw[k] for k in SAMPLE_WEIGHTS])


def _route(x, wr, rbias):
    logits = jnp.dot(x, wr, preferred_element_type=F32, precision=HIGHEST)
    scores = jax.nn.sigmoid(logits)
    biased = scores + rbias
    lane = lax.broadcasted_iota(I32, biased.shape, 1)
    grp = lane // EXPERTS_PER_GROUP
    neg = jnp.float32(-jnp.inf)

    def top2(vals):
        m1 = jnp.max(vals, axis=-1, keepdims=True)
        i1 = jnp.min(jnp.where(vals == m1, lane, N_EXPERTS), axis=-1, keepdims=True)
        rest = jnp.where(lane == i1, neg, vals)
        m2 = jnp.max(rest, axis=-1, keepdims=True)
        i2 = jnp.min(jnp.where(rest == m2, lane, N_EXPERTS), axis=-1, keepdims=True)
        return m1, i1, m2, i2

    best = sel = None
    for g in range(N_EXPERT_GROUPS):
        m1, _, m2, _ = top2(jnp.where(grp == g, biased, neg))
        gs = m1 + m2
        if g == 0:
            best, sel = gs, jnp.zeros(gs.shape, I32)
        else:
            upd = gs > best
            sel = jnp.where(upd, g, sel)
            best = jnp.where(upd, gs, best)
    _, i1, _, i2 = top2(jnp.where(grp == sel, biased, neg))
    s1 = jnp.sum(jnp.where(lane == i1, scores, 0.0), axis=-1, keepdims=True)
    s2 = jnp.sum(jnp.where(lane == i2, scores, 0.0), axis=-1, keepdims=True)
    tot = s1 + s2
    return jnp.where(lane == i1, s1 / tot, 0.0) + jnp.where(lane == i2, s2 / tot, 0.0)


def _moe_kernel(alpha, x_ref, wr_ref, rb_ref, wg_ref, wu_ref, wd_ref, ln2g_ref, ln2b_ref,
                out_ref, xb_ref, comb_ref, acc_ref):
    step = pl.program_id(1)

    @pl.when(step == 0)
    def _():
        x = x_ref[...]
        xb_ref[...] = x.astype(BF16)
        comb_ref[...] = _route(x, wr_ref[...], rb_ref[...])
        acc_ref[...] = jnp.zeros_like(acc_ref)

    xb = xb_ref[...]
    comb = comb_ref[...]
    lane = lax.broadcasted_iota(I32, comb.shape, 1)
    acc = acc_ref[...]
    for k in range(DENSE_EXPERTS_PER_STEP):
        g = _dot(xb, wg_ref[k].astype(BF16))
        u = _dot(xb, wu_ref[k].astype(BF16))
        ce = jnp.sum(jnp.where(lane == step * DENSE_EXPERTS_PER_STEP + k, comb, 0.0), axis=-1, keepdims=True)
        h = (jax.nn.silu(g) * u * ce).astype(BF16)
        acc = acc + _dot(h, wd_ref[k].astype(BF16))
    acc_ref[...] = acc

    @pl.when(step == pl.num_programs(1) - 1)
    def _():
        out_ref[...] = _layer_norm(alpha * x_ref[...] + acc_ref[...], ln2g_ref[...], ln2b_ref[...])


def _moe_dense(x, l, wr, rb, w_gate, w_up, w_down, ln2g, ln2b, alpha, tm):
    t = x.shape[0]
    cst = lambda i, e: (0, 0)
    wsel = lambda i, e: (l, e, 0, 0)
    return pl.pallas_call(
        functools.partial(_moe_kernel, alpha),
        grid=(t // tm, N_EXPERTS // DENSE_EXPERTS_PER_STEP),
        in_specs=[pl.BlockSpec((tm, D_MODEL), lambda i, e: (i, 0)),
                  pl.BlockSpec(wr.shape, cst), pl.BlockSpec(rb.shape, cst),
                  pl.BlockSpec((None, DENSE_EXPERTS_PER_STEP, D_MODEL, D_FF_EXPERT), wsel),
                  pl.BlockSpec((None, DENSE_EXPERTS_PER_STEP, D_MODEL, D_FF_EXPERT), wsel),
                  pl.BlockSpec((None, DENSE_EXPERTS_PER_STEP, D_FF_EXPERT, D_MODEL), wsel),
                  pl.BlockSpec((None,) + ln2g.shape[1:], lambda i, e: (l, 0, 0)),
                  pl.BlockSpec((None,) + ln2b.shape[1:], lambda i, e: (l, 0, 0))],
        out_specs=pl.BlockSpec((tm, D_MODEL), lambda i, e: (i, 0)),
        out_shape=jax.ShapeDtypeStruct((t, D_MODEL), F32),
        scratch_shapes=[pltpu.VMEM((tm, D_MODEL), BF16),
                        pltpu.VMEM((tm, N_EXPERTS), F32),
                        pltpu.VMEM((tm, D_MODEL), F32)],
        compiler_params=pltpu.CompilerParams(dimension_semantics=("arbitrary", "arbitrary"),
                                             vmem_limit_bytes=VMEM_LIMIT),
    )(x, wr, rb, w_gate, w_up, w_down, ln2g, ln2b)


def _block_diag(blocks):
    eye = jnp.eye(HALF_GROUPS, dtype=blocks.dtype)
    k, g, a, b = blocks.shape
    return jnp.einsum("kgab,gh->kgahb", blocks, eye).reshape(k, g * a, g * b)


def _prep_s5_folded(lb_re, lb_im, bb_re, bb_im, c_re, c_im):
    d, g, n = lb_re.shape
    s = S5_FOLD
    pairs = g // 2
    pr, pi = [jnp.ones_like(lb_re)], [jnp.zeros_like(lb_re)]
    for _ in range(s):
        pr, pi = pr + [pr[-1] * lb_re - pi[-1] * lb_im], pi + [pr[-1] * lb_im + pi[-1] * lb_re]
    p_re, p_im = jnp.stack(pr, axis=2), jnp.stack(pi, axis=2)

    def pair_cols(re, im):
        z = jnp.zeros_like(re[:, :, 0])
        g0 = jnp.concatenate([re[:, :, 0], z, im[:, :, 0], z], axis=-1)
        g1 = jnp.concatenate([z, re[:, :, 1], z, im[:, :, 1]], axis=-1)
        return jnp.concatenate([g0, g1], axis=2)

    def pair_diag(a):
        z = jnp.zeros_like(a[:, :, 0])
        return jnp.concatenate([jnp.concatenate([a[:, :, 0], z], axis=-1),
                                jnp.concatenate([z, a[:, :, 1]], axis=-1)], axis=2)

    bt_re, bt_im = bb_re.transpose(0, 1, 3, 2)[:, :, None], bb_im.transpose(0, 1, 3, 2)[:, :, None]
    k_re, k_im = p_re[:, :, s - 1::-1, None, :], p_im[:, :, s - 1::-1, None, :]
    m1t = pair_cols((k_re * bt_re - k_im * bt_im).reshape(d, pairs, 2, -1, n),
                    (k_re * bt_im + k_im * bt_re).reshape(d, pairs, 2, -1, n))
    m1 = jnp.swapaxes(m1t, -1, -2)

    cq_re, cq_im = c_re[:, :, None], c_im[:, :, None]
    j_re, j_im = p_re[:, :, 1:, None, :], p_im[:, :, 1:, None, :]
    hpart = pair_cols((cq_re * j_re - cq_im * j_im).reshape(d, pairs, 2, -1, n),
                      (-(cq_re * j_im + cq_im * j_re)).reshape(d, pairs, 2, -1, n))

    t_re, t_im = p_re[:, :, :s, None, :], p_im[:, :, :s, None, :]
    kern = (jnp.einsum("dgtqn,dgnp->dgtqp", cq_re * t_re - cq_im * t_im, bb_re, precision=HIGHEST)
            - jnp.einsum("dgtqn,dgnp->dgtqp", cq_re * t_im + cq_im * t_re, bb_im, precision=HIGHEST))
    zero = jnp.zeros_like(kern[:, :, 0])
    kx = jnp.stack([jnp.concatenate([kern[:, :, j - q] if q <= j else zero for q in range(s)], axis=-1)
                    for j in range(s)], axis=2)
    apart = pair_diag(kx.reshape(d, pairs, 2, -1, kx.shape[-1]))

    m2 = jnp.concatenate([hpart, apart], axis=-1)
    return (m1.astype(BF16), m2.astype(BF16), p_re[:, :, s].reshape(d, 1, g * n), p_im[:, :, s].reshape(d, 1, g * n))


def _prep_all(w_in, w_out, ln_v_g, ln_v_b, w_s, b_s, a_re, a_im, log_dt, b_re, b_im, c_re, c_im,
              d_skip, w_glu, b_glu, ln1_g, ln1_b):
    d = w_in.shape[0]
    dt = jnp.exp(log_dt)[..., None]
    decay = jnp.exp(a_re * dt)
    lb_re, lb_im = decay * jnp.cos(a_im * dt), decay * jnp.sin(a_im * dt)
    den = a_re * a_re + a_im * a_im
    nr, ni = lb_re - 1.0, lb_im
    zr = (nr * a_re + ni * a_im) / den
    zi = (ni * a_re - nr * a_im) / den
    bb_re = zr[..., None] * b_re - zi[..., None] * b_im
    bb_im = zr[..., None] * b_im + zi[..., None] * b_re

    def bd(a):
        blocks = _block_diag(a.reshape((d * 2, HALF_GROUPS) + a.shape[2:]))
        return blocks.reshape((d, 2) + blocks.shape[1:])

    bdb = jnp.concatenate([bd(bb_re.transpose(0, 1, 3, 2)), bd(bb_im.transpose(0, 1, 3, 2))], axis=3)
    bdc = jnp.concatenate([bd(c_re.transpose(0, 1, 3, 2)), bd(-c_im.transpose(0, 1, 3, 2))], axis=2)
    m1, m2, lr8, li8 = _prep_s5_folded(lb_re, lb_im, bb_re, bb_im, c_re, c_im)
    return dict(
        m1=m1, m2=m2, lr8=lr8, li8=li8,
        win=w_in.astype(BF16), wout=w_out.astype(BF16),
        lnvg=ln_v_g[:, None], lnvb=ln_v_b[:, None],
        wtril=jnp.tril(w_s).astype(BF16),
        bsb=jnp.broadcast_to(b_s[..., None], (d, H_A, CHUNK, LANES)),
        ws0=jnp.repeat(w_s[:, :, 0, 0], P_A, axis=1)[:, None], bs0=jnp.repeat(b_s[:, :, 0], P_A, axis=1)[:, None],
        bdb=bdb.astype(BF16), bdc=bdc.astype(BF16),
        lr=lb_re.reshape(d, 2, 1, HALF_STATE), li=lb_im.reshape(d, 2, 1, HALF_STATE),
        dskip=d_skip.reshape(d, 1, W_B), glu=bd(w_glu).astype(BF16),
        bglu=b_glu.reshape(d, 1, W_B), ln1g=ln1_g[:, None], ln1b=ln1_b[:, None])


def _state_to_cols(h_re, h_im):
    d, b = h_re.shape[:2]
    re = h_re.reshape(d, b, 2, HALF_STATE)
    im = h_im.reshape(d, b, 2, HALF_STATE)
    return jnp.concatenate([re, im], axis=3).transpose(0, 2, 1, 3)


def _cols_to_state(h):
    d, _, b, _ = h.shape
    re = h[..., :HALF_STATE].transpose(0, 2, 1, 3).reshape(d, b, G_B, N_STATE)
    im = h[..., HALF_STATE:].transpose(0, 2, 1, 3).reshape(d, b, G_B, N_STATE)
    return re, im


def _pairs_to_state(h):
    d, b, _ = h.shape
    h = h.reshape(d, b, G_B // 2, 2, 2, N_STATE)
    return h[:, :, :, 0].reshape(d, b, G_B, N_STATE), h[:, :, :, 1].reshape(d, b, G_B, N_STATE)


def kernel(x_prompt, x_sample, state_ssm_re, state_ssm_im, w_in, w_out, ln_v_g, ln_v_b, w_s, b_s, ssm_a_re, ssm_a_im, ssm_log_dt, ssm_b_re, ssm_b_im, ssm_c_re, ssm_c_im, ssm_d, w_glu, b_glu, ln1_g, ln1_b, ln2_g, ln2_b, w_router, router_bias, w_gate, w_up, w_down):
    depth = w_in.shape[0]
    alpha = float((2 * depth) ** 0.25)
    nb, seq, _ = x_prompt.shape
    ns = x_sample.shape[0]
    tokens = nb * seq
    nt = tokens // TILE_M + N_CLASSES
    rb = router_bias[None]
    wrt = w_router.T
    wrp = jnp.pad(w_router, ((0, 0), (0, LANES - N_EXPERTS)))
    rbcol = router_bias[:, None]
    tok = np.arange(nb * CHUNK)
    tri = jnp.asarray(tok[:, None] < tok[None, :], BF16)
    lw = _prep_all(w_in, w_out, ln_v_g, ln_v_b, w_s, b_s, ssm_a_re, ssm_a_im, ssm_log_dt,
                   ssm_b_re, ssm_b_im, ssm_c_re, ssm_c_im, ssm_d, w_glu, b_glu, ln1_g, ln1_b)
    shared = (wrt, rbcol, tri)
    ln2g, ln2b = ln2_g[:, None], ln2_b[:, None]
    h0s = _state_to_cols(state_ssm_re, state_ssm_im)
    xp = x_prompt
    pos = None
    xs = x_sample.reshape(ns, D_MODEL)
    pr_h, sm_h, sm_v = [], [], []
    for l in range(depth):
        x1t, hfin, cls, rank, counts = _mixer_prompt(xp, lw, shared, l, alpha, nb, seq, pos)
        pos, zstart, zlen, tail, tables = _plan(cls, rank, counts, nt)
        x_sorted = _dispatch(x1t, pos, zstart, zlen, tail, nb * CHUNK, nt * TILE_M)
        xp = _pair_experts(x_sorted, l, tables, wrp, w_gate, w_up, w_down, ln2g, ln2b, alpha)
        pr_h.append(hfin)

        x1s, hnew, v_new = _mixer_sample(xs, h0s, lw, l, alpha)
        xs = _moe_dense(x1s, l, w_router, rb, w_gate, w_up, w_down, ln2g, ln2b, alpha, tm=ns)
        sm_h.append(hnew)
        sm_v.append(v_new.reshape(ns, 1, W_A))
    y_prompt = _ungather(xp, pos, nb, seq)
    pr_re, pr_im = _pairs_to_state(jnp.stack(pr_h))
    sm_re, sm_im = _cols_to_state(jnp.stack(sm_h))
    return (y_prompt, xs.reshape(ns, 1, D_MODEL), pr_re, pr_im, sm_re, sm_im, jnp.stack(sm_v))
```

```python
import functools

import jax
import jax.numpy as jnp
import numpy as np
from jax import lax
from jax.experimental import pallas as pl
from jax.experimental.pallas import tpu as pltpu

D_MODEL = 1024
W_A = 512
W_B = 512
CHUNK = 128
H_A = 4
P_A = W_A // H_A
GROUP_B = 16
G_B = W_B // GROUP_B
N_STATE = 64
N_EXPERTS = 16
N_EXPERT_GROUPS = 4
EXPERTS_PER_GROUP = N_EXPERTS // N_EXPERT_GROUPS
D_FF_EXPERT = D_MODEL // 4
LN_EPS = 1e-5

LANES = 128
SUBLANES = 8
HALF_GROUPS = 16
HALF_W = HALF_GROUPS * GROUP_B
HALF_STATE = HALF_GROUPS * N_STATE
VMEM_LIMIT = 56 * 1024 * 1024

PAIRS = ((0, 1), (0, 2), (0, 3), (1, 3), (1, 2), (3, 2))
N_CLASSES = N_EXPERT_GROUPS * len(PAIRS)
CLASS_ROWS = 32
TILE_M = 256
TOK_ROWS = D_MODEL // LANES
ZERO_TOKENS = TILE_M // 2
DMA_UNROLL = 16
DENSE_EXPERTS_PER_STEP = 4
S5_FOLD = 8
S5_PAIRS = G_B // 2
EA_TABLE = np.array([EXPERTS_PER_GROUP * g + a for g in range(N_EXPERT_GROUPS) for a, _ in PAIRS], np.int32)
EB_TABLE = np.array([EXPERTS_PER_GROUP * g + b for g in range(N_EXPERT_GROUPS) for _, b in PAIRS], np.int32)

F32 = jnp.float32
BF16 = jnp.bfloat16
I32 = jnp.int32
HIGHEST = lax.Precision.HIGHEST


def _layer_norm(x, g, b):
    mu = jnp.mean(x, axis=-1, keepdims=True)
    xc = x - mu
    var = jnp.mean(xc * xc, axis=-1, keepdims=True)
    return xc * lax.rsqrt(var + LN_EPS) * g + b


def _dot(a, b):
    return jnp.dot(a, b, preferred_element_type=F32)


def _route_classes(x1, wrt, rbcol):
    def split(a):
        hi = a.astype(BF16)
        return hi, (a - hi.astype(F32)).astype(BF16)

    def dot_t(a, b):
        return lax.dot_general(a, b, (((1,), (1,)), ((), ())), preferred_element_type=F32)
    w_hi, w_lo = split(wrt)
    x_hi, x_lo = split(x1)
    logits_t = dot_t(w_hi, x_hi) + (dot_t(w_hi, x_lo) + dot_t(w_lo, x_hi))
    biased = jax.nn.sigmoid(logits_t) + rbcol
    rows = [biased[e:e + 1, :] for e in range(N_EXPERTS)]
    n = EXPERTS_PER_GROUP

    best = sel = None
    for g in range(N_EXPERT_GROUPS):
        v = rows[n * g:n * (g + 1)]
        gs = None
        for a, b in PAIRS:
            s = v[a] + v[b]
            gs = s if gs is None else jnp.maximum(gs, s)
        if g == 0:
            best, sel = gs, jnp.zeros(gs.shape, I32)
        else:
            upd = gs > best
            sel = jnp.where(upd, g, sel)
            best = jnp.where(upd, gs, best)

    cls = jnp.zeros(sel.shape, I32)
    for g in range(N_EXPERT_GROUPS):
        v = rows[n * g:n * (g + 1)]
        lo = jnp.full(sel.shape, n, I32)
        hi = jnp.full(sel.shape, -1, I32)
        for i in range(n):
            before = jnp.zeros(sel.shape, I32)
            for j in range(n):
                if j < i:
                    before = before + (v[j] >= v[i]).astype(I32)
                elif j > i:
                    before = before + (v[j] > v[i]).astype(I32)
            member = before < 2
            lo = jnp.where(member, jnp.minimum(lo, i), lo)
            hi = jnp.where(member, jnp.maximum(hi, i), hi)
        pidx = jnp.zeros(sel.shape, I32)
        for k, (a, b) in enumerate(PAIRS):
            pidx = jnp.where((lo == min(a, b)) & (hi == max(a, b)), k, pidx)
        cls = jnp.where(sel == g, g * len(PAIRS) + pidx, cls)
    return cls


def _to_token_tiles(ref, row0, x):
    n = x.shape[0]
    for c in range(TOK_ROWS):
        ref[pl.ds(row0 * TOK_ROWS + c, n, stride=TOK_ROWS), :] = x[:, c * LANES:(c + 1) * LANES]


def _from_token_tiles(ref, row0, n):
    return jnp.concatenate(
        [ref[pl.ds(row0 * TOK_ROWS + c, n, stride=TOK_ROWS), :] for c in range(TOK_ROWS)], axis=1)


def _gathered_tokens(step, nsteps, pos8_ref, src_hbm, bufs, sems, rows, consume):
    def start(s, base):
        def body(g, c):
            idx0 = base + g * DMA_UNROLL
            row0 = pl.multiple_of(g * (DMA_UNROLL * TOK_ROWS), DMA_UNROLL * TOK_ROWS)
            for i in range(DMA_UNROLL):
                p8 = pl.multiple_of(pos8_ref[idx0 + i], TOK_ROWS)
                pltpu.make_async_copy(src_hbm.at[pl.ds(p8, TOK_ROWS), :],
                                      bufs[s].at[pl.ds(row0 + i * TOK_ROWS, TOK_ROWS), :],
                                      sems.at[s]).start(priority=i % 2)
            return c
        lax.fori_loop(0, rows // DMA_UNROLL, body, 0)

    @pl.when(step == 0)
    def _():
        start(0, 0)

    for s in range(2):
        @pl.when(lax.rem(step, 2) == s)
        def _(s=s):
            pltpu.make_async_copy(src_hbm.at[pl.ds(0, rows * TOK_ROWS), :], bufs[s], sems.at[s]).wait()

            @pl.when(step + 1 < nsteps)
            def _():
                start(1 - s, (step + 1) * rows)
            consume(bufs[s])


def _mixer_kernel(alpha, nb, gather_in, *refs):
    if gather_in:
        pos8_ref, zs_hbm = refs[:2]
        refs = refs[2:]
    else:
        x_ref = refs[0]
        refs = refs[1:]
    (win_ref, wout_ref, lnvg_ref, lnvb_ref, wtril_ref, bsb_ref, m1_ref, m2_ref, lr8_ref, li8_ref,
     dskip_ref, glu_ref, bglu_ref, ln1g_ref, ln1b_ref, wrt_ref, rbcol_ref, tri_ref,
     x1t_ref, hfin_ref, cls_ref, rank_ref, cnt_ref,
     xb_ref, xs_slab, xs_scb, xst_ref, ht_ref, yt_ref, mix_ref, hstate_ref, carry_ref) = refs[:32]
    lt = CHUNK
    rows = nb * lt
    pitch = lt + SUBLANES
    step = pl.program_id(0)
    nsteps = pl.num_programs(0)

    @pl.when(step == 0)
    def _():
        hstate_ref[...] = jnp.zeros_like(hstate_ref)
        carry_ref[...] = jnp.zeros_like(carry_ref)

    if gather_in:
        xin_ref, xbuf0, xbuf1, gsem = refs[32:36]

        def consume(buf):
            xin_ref[...] = _from_token_tiles(buf, 0, rows)
        _gathered_tokens(step, nsteps, pos8_ref, zs_hbm, (xbuf0, xbuf1), gsem, rows, consume)

        def load_x():
            return xin_ref[...]
    else:
        def load_x():
            return x_ref[...].reshape(rows, D_MODEL)

    xb_ref[...] = load_x().astype(BF16)

    fold = S5_FOLD
    nchunk = lt // fold
    cb = nchunk * nb
    xs = _dot(xb_ref[...], win_ref[:, 2 * W_A:])
    for j in range(W_B // LANES):
        for b in range(nb):
            xs_slab[j, b * pitch:b * pitch + lt, :] = xs[b * lt:(b + 1) * lt, j * LANES:(j + 1) * LANES]

    def to_scb(t, c):
        s_, ch_ = lax.bitwise_and(t, fold - 1), lax.shift_right_logical(t, fold.bit_length() - 1)
        r0 = pl.multiple_of(s_ * cb + ch_ * nb, SUBLANES)
        for j in range(W_B // LANES):
            xs_scb[pl.ds(r0, nb), j * LANES:(j + 1) * LANES] = xs_slab[j, pl.ds(t, nb, stride=pitch), :]
        return c
    lax.fori_loop(0, lt, to_scb, 0, unroll=4)
    xst_ref[...] = xs_scb[...].T.astype(BF16)

    vg = jax.nn.gelu(_dot(xb_ref[...], win_ref[:, W_A:2 * W_A]))
    v = _layer_norm(vg, lnvg_ref[...], lnvb_ref[...]).astype(BF16)
    u = jax.nn.gelu(_dot(xb_ref[...], win_ref[:, :W_A]))
    for h in range(H_A):
        hs = slice(h * P_A, (h + 1) * P_A)
        vcat = jnp.concatenate([v[b * lt:(b + 1) * lt, hs] for b in range(nb)], axis=1)
        o = _dot(wtril_ref[h], vcat)
        for b in range(nb):
            rs = slice(b * lt, (b + 1) * lt)
            mix_ref[rs, hs] = u[rs, hs] * (o[:, b * LANES:(b + 1) * LANES] + bsb_ref[h])

    def chunk_inputs(pr):
        return jnp.concatenate(
            [xst_ref[(2 * pr + gi) * GROUP_B:(2 * pr + gi + 1) * GROUP_B, s * cb:(s + 1) * cb]
             for gi in range(2) for s in range(fold)], axis=0)

    pw = 2 * 2 * N_STATE
    for pr in range(S5_PAIRS):
        ht_ref[:, pr * pw:(pr + 1) * pw] = _dot(m1_ref[pr], chunk_inputs(pr)).T

    half_pairs = S5_PAIRS // 2
    for half in range(2):
        c0 = half * half_pairs * pw
        lr8 = [jnp.broadcast_to(lr8_ref[:, (half * half_pairs + p) * LANES:(half * half_pairs + p + 1) * LANES],
                                (nb, LANES)) for p in range(half_pairs)]
        li8 = [jnp.broadcast_to(li8_ref[:, (half * half_pairs + p) * LANES:(half * half_pairs + p + 1) * LANES],
                                (nb, LANES)) for p in range(half_pairs)]
        h = [hstate_ref[:, c0 + q * LANES:c0 + (q + 1) * LANES] for q in range(2 * half_pairs)]
        for c in range(nchunk):
            rs = slice(c * nb, (c + 1) * nb)
            for p in range(half_pairs):
                re_sl = slice(c0 + p * pw, c0 + p * pw + LANES)
                im_sl = slice(c0 + p * pw + LANES, c0 + (p + 1) * pw)
                hr, hi = h[2 * p], h[2 * p + 1]
                ur, ui = ht_ref[rs, re_sl], ht_ref[rs, im_sl]
                ht_ref[rs, re_sl] = hr
                ht_ref[rs, im_sl] = hi
                h[2 * p] = lr8[p] * hr - li8[p] * hi + ur
                h[2 * p + 1] = lr8[p] * hi + li8[p] * hr + ui
        for q in range(2 * half_pairs):
            hstate_ref[:, c0 + q * LANES:c0 + (q + 1) * LANES] = h[q]
    hfin_ref[...] = hstate_ref[...]

    for pr in range(S5_PAIRS):
        rhs = jnp.concatenate([ht_ref[:, pr * pw:(pr + 1) * pw].T.astype(BF16), chunk_inputs(pr)], axis=0)
        yt = _dot(m2_ref[pr], rhs)
        for gi in range(2):
            for s in range(fold):
                r0 = (gi * fold + s) * GROUP_B
                yt_ref[(2 * pr + gi) * GROUP_B:(2 * pr + gi + 1) * GROUP_B, s * cb:(s + 1) * cb] = (
                    yt[r0:r0 + GROUP_B, :])

    y = jax.nn.gelu(yt_ref[...].T + dskip_ref[...] * xs_scb[...])
    yb = y.astype(BF16)
    spitch = cb + SUBLANES
    for k in range(2):
        sl = slice(k * HALF_W, (k + 1) * HALF_W)
        gl = _dot(yb[:, sl], glu_ref[k]) + bglu_ref[:, sl]
        z = y[:, sl] * jax.nn.sigmoid(gl)
        for j in range(HALF_W // LANES):
            for s in range(fold):
                xs_slab[k * (HALF_W // LANES) + j, s * spitch:s * spitch + cb, :] = (
                    z[s * cb:(s + 1) * cb, j * LANES:(j + 1) * LANES])

    def to_bt(ch, c):
        for b in range(nb):
            dst = pl.multiple_of(b * lt + ch * fold, SUBLANES)
            for j in range(W_B // LANES):
                mix_ref[pl.ds(dst, fold), W_A + j * LANES:W_A + (j + 1) * LANES] = (
                    xs_slab[j, pl.ds(ch * nb + b, fold, stride=spitch), :])
        return c
    lax.fori_loop(0, nchunk, to_bt, 0)

    mix = _dot(mix_ref[...].astype(BF16), wout_ref[...])
    x1 = _layer_norm(alpha * load_x() + mix, ln1g_ref[...], ln1b_ref[...])
    _to_token_tiles(x1t_ref, 0, x1)

    cls = _route_classes(x1, wrt_ref[...], rbcol_ref[...])
    crow = lax.broadcasted_iota(I32, (CLASS_ROWS, rows), 0)
    onehot = jnp.where(crow == cls, 1.0, 0.0)
    prefix = _dot(onehot.astype(BF16), tri_ref[...])
    carry = carry_ref[:, 0:1]
    rank = jnp.sum(onehot * (prefix + carry), axis=0, keepdims=True)
    cls_ref[...] = cls.reshape(1, 1, rows)
    rank_ref[...] = rank.astype(I32).reshape(1, 1, rows)
    carry_ref[...] = carry_ref[...] + jnp.sum(onehot, axis=1, keepdims=True)
    cnt_ref[...] = carry_ref[...]


def _const_spec(shape):
    nd = len(shape)
    return pl.BlockSpec(shape, lambda *_: (0,) * nd, pipeline_mode=pl.Buffered(1))


def _layer_spec(shape, l):
    nd = len(shape)
    return pl.BlockSpec((None,) + tuple(shape[1:]), lambda *_: (l,) + (0,) * (nd - 1),
                        pipeline_mode=pl.Buffered(1))


MIXER_WEIGHTS = ("win", "wout", "lnvg", "lnvb", "wtril", "bsb", "m1", "m2", "lr8", "li8", "dskip", "glu",
                 "bglu", "ln1g", "ln1b")
SAMPLE_WEIGHTS = ("win", "wout", "lnvg", "lnvb", "ws0", "bs0", "bdb", "bdc", "lr", "li", "dskip", "glu",
                  "bglu", "ln1g", "ln1b")


def _mixer_prompt(x, lw, shared, l, alpha, nb, seq, pos_prev=None):
    lt = CHUNK
    rows = nb * lt
    nsteps = seq // lt
    gather_in = pos_prev is not None
    weights = tuple(lw[k] for k in MIXER_WEIGHTS) + tuple(shared)
    wspecs = [_layer_spec(lw[k].shape, l) for k in MIXER_WEIGHTS] + [_const_spec(w.shape) for w in shared]
    if gather_in:
        x_spec = pl.BlockSpec(memory_space=pl.ANY)
    else:
        x_spec = pl.BlockSpec((nb, lt, D_MODEL), lambda i, *_: (0, i, 0))
    scratch = [
        pltpu.VMEM((rows, D_MODEL), BF16),
        pltpu.VMEM((W_B // LANES, nb * (lt + SUBLANES), LANES), F32),
        pltpu.VMEM((rows, W_B), F32),
        pltpu.VMEM((W_B, rows), BF16),
        pltpu.VMEM((rows // S5_FOLD, 2 * G_B * N_STATE), F32),
        pltpu.VMEM((W_B, rows), F32),
        pltpu.VMEM((rows, D_MODEL), F32),
        pltpu.VMEM((nb, 2 * G_B * N_STATE), F32),
        pltpu.VMEM((CLASS_ROWS, LANES), F32),
    ]
    if gather_in:
        scratch += [pltpu.VMEM((rows, D_MODEL), F32),
                    pltpu.VMEM((rows * TOK_ROWS, LANES), F32), pltpu.VMEM((rows * TOK_ROWS, LANES), F32),
                    pltpu.SemaphoreType.DMA((2,))]
    grid_spec = pltpu.PrefetchScalarGridSpec(
        num_scalar_prefetch=1 if gather_in else 0,
        grid=(nsteps,),
        in_specs=[x_spec] + wspecs,
        out_specs=[pl.BlockSpec((rows * TOK_ROWS, LANES), lambda i, *_: (i, 0)),
                   pl.BlockSpec((nb, 2 * G_B * N_STATE), lambda i, *_: (0, 0)),
                   pl.BlockSpec((1, 1, rows), lambda i, *_: (i, 0, 0)),
                   pl.BlockSpec((1, 1, rows), lambda i, *_: (i, 0, 0)),
                   pl.BlockSpec((CLASS_ROWS, LANES), lambda i, *_: (0, 0))],
        scratch_shapes=scratch)
    args = ((pos_prev, x) if gather_in else (x,)) + weights
    return pl.pallas_call(
        functools.partial(_mixer_kernel, alpha, nb, gather_in),
        grid_spec=grid_spec,
        out_shape=[jax.ShapeDtypeStruct((nb * seq * TOK_ROWS, LANES), F32),
                   jax.ShapeDtypeStruct((nb, 2 * G_B * N_STATE), F32),
                   jax.ShapeDtypeStruct((nsteps, 1, rows), I32),
                   jax.ShapeDtypeStruct((nsteps, 1, rows), I32),
                   jax.ShapeDtypeStruct((CLASS_ROWS, LANES), F32)],
        compiler_params=pltpu.CompilerParams(dimension_semantics=("arbitrary",),
                                             vmem_limit_bytes=VMEM_LIMIT),
    )(*args)


def _dispatch_kernel(rows, pos8_ref, zstart_ref, zlen_ref, tail_ref, x_ref, xs_hbm, zero_ref, sem, zsem):
    step = pl.program_id(0)
    ztok = ZERO_TOKENS

    @pl.when(step == 0)
    def _():
        zero_ref[...] = jnp.zeros_like(zero_ref)
        pieces = []
        for c in range(N_CLASSES):
            start = zstart_ref[c]
            zlen = zlen_ref[c]
            p = TILE_M // 2
            while p >= 1:
                hit = (zlen & p) != 0
                pieces.append((hit, pltpu.make_async_copy(
                    zero_ref.at[pl.ds(0, p * TOK_ROWS), :],
                    xs_hbm.at[pl.ds(pl.multiple_of(start * TOK_ROWS, TOK_ROWS), p * TOK_ROWS), :], zsem)))
                start = start + jnp.where(hit, p, 0)
                p //= 2
        for hit, cp in pieces:
            pl.when(hit)(cp.start)
        for hit, cp in pieces:
            pl.when(hit)(cp.wait)

        zrows = ztok * TOK_ROWS
        first = tail_ref[0] // ztok

        def tail_copy(q):
            return pltpu.make_async_copy(
                zero_ref, xs_hbm.at[pl.ds(pl.multiple_of(q * zrows, zrows), zrows), :], zsem)

        def tail_start(q, c):
            tail_copy(q).start()
            return c

        def tail_wait(q, c):
            tail_copy(q).wait()
            return c
        lax.fori_loop(first, xs_hbm.shape[0] // zrows, tail_start, 0)
        lax.fori_loop(first, xs_hbm.shape[0] // zrows, tail_wait, 0)

    base = step * rows

    def body(g, c):
        idx0 = base + g * DMA_UNROLL
        row0 = pl.multiple_of(g * (DMA_UNROLL * TOK_ROWS), DMA_UNROLL * TOK_ROWS)
        for i in range(DMA_UNROLL):
            p8 = pl.multiple_of(pos8_ref[idx0 + i], TOK_ROWS)
            pltpu.make_async_copy(x_ref.at[pl.ds(row0 + i * TOK_ROWS, TOK_ROWS), :],
                                  xs_hbm.at[pl.ds(p8, TOK_ROWS), :], sem).start(priority=i % 2)
        return c
    lax.fori_loop(0, rows // DMA_UNROLL, body, 0)
    pltpu.make_async_copy(x_ref, xs_hbm.at[pl.ds(0, rows * TOK_ROWS), :], sem).wait()


def _dispatch(x1t, pos8, zstart, zlen, tail, rows, ns_tokens):
    return pl.pallas_call(
        functools.partial(_dispatch_kernel, rows),
        grid_spec=pltpu.PrefetchScalarGridSpec(
            num_scalar_prefetch=4,
            grid=(x1t.shape[0] // (rows * TOK_ROWS),),
            in_specs=[pl.BlockSpec((rows * TOK_ROWS, LANES), lambda i, *_: (i, 0))],
            out_specs=pl.BlockSpec(memory_space=pl.ANY),
            scratch_shapes=[pltpu.VMEM((ZERO_TOKENS * TOK_ROWS, LANES), F32),
                            pltpu.SemaphoreType.DMA(()), pltpu.SemaphoreType.DMA(())]),
        out_shape=jax.ShapeDtypeStruct((ns_tokens * TOK_ROWS, LANES), F32),
        compiler_params=pltpu.CompilerParams(dimension_semantics=("arbitrary",),
                                             vmem_limit_bytes=VMEM_LIMIT),
    )(pos8, zstart, zlen, tail, x1t)


def _pair_kernel(alpha, l, tidx_ref, tcls_ref, nused_ref,
                 ea_ref, cha_ref, bufa_ref, nxa_ref, hna_ref, eb_ref, chb_ref, bufb_ref, nxb_ref, hnb_ref,
                 x_ref, wrp_ref, wg_hbm, wu_hbm, wd_hbm,
                 ln2g_ref, ln2b_ref, z_ref, w1_ref, w2_ref, xprev_ref, moe_ref, sg_ref, su_ref, sd_ref, wsem):
    j = pl.program_id(0)
    f = D_FF_EXPERT
    nused = nused_ref[0]
    slots = ((ea_ref, cha_ref, bufa_ref, nxa_ref, hna_ref), (eb_ref, chb_ref, bufb_ref, nxb_ref, hnb_ref))

    def weight_copies(k, e, buf):
        return [pltpu.make_async_copy(wg_hbm.at[l, e], sg_ref.at[k, buf], wsem.at[k, buf]),
                pltpu.make_async_copy(wu_hbm.at[l, e], su_ref.at[k, buf], wsem.at[k, buf]),
                pltpu.make_async_copy(wd_hbm.at[l, e], sd_ref.at[k, buf], wsem.at[k, buf])]

    @pl.when(j == 0)
    def _():
        xprev_ref[...] = jnp.zeros_like(xprev_ref)
        moe_ref[...] = jnp.zeros_like(moe_ref)
        w1_ref[:, 4 * f:] = wrp_ref[...].astype(BF16)
        for k, (e_ref, _, buf_ref, _, _) in enumerate(slots):
            for cp in weight_copies(k, e_ref[0], buf_ref[0]):
                cp.start()

    @pl.when(j > nused)
    def _():
        z_ref[...] = jnp.zeros_like(z_ref)

    @pl.when(j <= nused)
    def _():
        class_start = jnp.logical_or(j == 0, tcls_ref[j] != tcls_ref[jnp.maximum(j - 1, 0)])

        for k, (e_ref, ch_ref, buf_ref, nx_ref, hn_ref) in enumerate(slots):
            @pl.when(jnp.logical_and(class_start, ch_ref[j] != 0))
            def _(k=k, e_ref=e_ref, buf_ref=buf_ref, nx_ref=nx_ref, hn_ref=hn_ref):
                buf = buf_ref[j]
                for cp in weight_copies(k, e_ref[j], buf):
                    cp.wait()

                @pl.when(hn_ref[j] != 0)
                def _():
                    for cp in weight_copies(k, nx_ref[j], 1 - buf):
                        cp.start()
                w1_ref[:, 2 * k * f:(2 * k + 1) * f] = sg_ref[k, buf].astype(BF16)
                w1_ref[:, (2 * k + 1) * f:(2 * k + 2) * f] = su_ref[k, buf].astype(BF16)
                w2_ref[k * f:(k + 1) * f, :] = sd_ref[k, buf].astype(BF16)

        _to_token_tiles(z_ref, 0, _layer_norm(alpha * xprev_ref[...] + moe_ref[...], ln2g_ref[...], ln2b_ref[...]))

        x = _from_token_tiles(x_ref, 0, TILE_M)
        gu = _dot(x.astype(BF16), w1_ref[...])
        scores = jax.nn.sigmoid(gu[:, 4 * f:])
        lane = lax.broadcasted_iota(I32, scores.shape, 1)
        sa = jnp.sum(jnp.where(lane == ea_ref[j], scores, 0.0), axis=-1, keepdims=True)
        sb = jnp.sum(jnp.where(lane == eb_ref[j], scores, 0.0), axis=-1, keepdims=True)
        tot = sa + sb
        ha = jax.nn.silu(gu[:, 0 * f:1 * f]) * gu[:, 1 * f:2 * f] * (sa / tot)
        hb = jax.nn.silu(gu[:, 2 * f:3 * f]) * gu[:, 3 * f:4 * f] * (sb / tot)
        xprev_ref[...] = x
        moe_ref[...] = _dot(jnp.concatenate([ha, hb], axis=1).astype(BF16), w2_ref[...])


def _pair_experts(xs, l, tables, wrp, w_gate, w_up, w_down, ln2g, ln2b, alpha):
    nsteps = tables[0].shape[0]
    cst = lambda j, *_: (0, 0)
    lsel = lambda j, *_: (l, 0, 0)
    hbm = pl.BlockSpec(memory_space=pl.ANY)
    return pl.pallas_call(
        functools.partial(_pair_kernel, alpha, l),
        grid_spec=pltpu.PrefetchScalarGridSpec(
            num_scalar_prefetch=len(tables),
            grid=(nsteps,),
            in_specs=[pl.BlockSpec((TILE_M * TOK_ROWS, LANES), lambda j, ti, *_: (ti[j], 0)),
                      pl.BlockSpec(wrp.shape, cst), hbm, hbm, hbm,
                      pl.BlockSpec((None,) + ln2g.shape[1:], lsel), pl.BlockSpec((None,) + ln2b.shape[1:], lsel)],
            out_specs=pl.BlockSpec((TILE_M * TOK_ROWS, LANES), lambda j, *_: (jnp.maximum(j - 1, 0), 0)),
            scratch_shapes=[pltpu.VMEM((D_MODEL, 4 * D_FF_EXPERT + LANES), BF16),
                            pltpu.VMEM((2 * D_FF_EXPERT, D_MODEL), BF16),
                            pltpu.VMEM((TILE_M, D_MODEL), F32), pltpu.VMEM((TILE_M, D_MODEL), F32),
                            pltpu.VMEM((2, 2, D_MODEL, D_FF_EXPERT), F32),
                            pltpu.VMEM((2, 2, D_MODEL, D_FF_EXPERT), F32),
                            pltpu.VMEM((2, 2, D_FF_EXPERT, D_MODEL), F32),
                            pltpu.SemaphoreType.DMA((2, 2))]),
        out_shape=jax.ShapeDtypeStruct(xs.shape, F32),
        compiler_params=pltpu.CompilerParams(dimension_semantics=("arbitrary",),
                                             vmem_limit_bytes=VMEM_LIMIT),
    )(*tables, xs, wrp, w_gate, w_up, w_down, ln2g, ln2b)


def _ungather_kernel(nb, pos8_ref, zs_hbm, out_ref, buf0, buf1, sems):
    rows = nb * CHUNK

    def consume(buf):
        out_ref[...] = _from_token_tiles(buf, 0, rows).reshape(nb, CHUNK, D_MODEL)
    _gathered_tokens(pl.program_id(0), pl.num_programs(0), pos8_ref, zs_hbm, (buf0, buf1), sems, rows, consume)


def _ungather(zs, pos8, nb, seq):
    rows = nb * CHUNK
    return pl.pallas_call(
        functools.partial(_ungather_kernel, nb),
        grid_spec=pltpu.PrefetchScalarGridSpec(
            num_scalar_prefetch=1,
            grid=(seq // CHUNK,),
            in_specs=[pl.BlockSpec(memory_space=pl.ANY)],
            out_specs=pl.BlockSpec((nb, CHUNK, D_MODEL), lambda i, *_: (0, i, 0)),
            scratch_shapes=[pltpu.VMEM((rows * TOK_ROWS, LANES), F32), pltpu.VMEM((rows * TOK_ROWS, LANES), F32),
                            pltpu.SemaphoreType.DMA((2,))]),
        out_shape=jax.ShapeDtypeStruct((nb, seq, D_MODEL), F32),
        compiler_params=pltpu.CompilerParams(dimension_semantics=("arbitrary",),
                                             vmem_limit_bytes=VMEM_LIMIT),
    )(pos8, zs)


def _plan(cls, rank, counts, nt):
    cnt = counts[:N_CLASSES, 0].astype(I32)
    ntile = (cnt + TILE_M - 1) // TILE_M
    padded = ntile * TILE_M
    off = jnp.cumsum(padded) - padded
    classes = jnp.arange(N_CLASSES, dtype=I32)
    pos8 = (rank.reshape(-1) + jnp.sum(jnp.where(cls.reshape(-1, 1) == classes, off, 0), axis=1)) * TOK_ROWS
    tile_end = jnp.cumsum(ntile)
    nused = tile_end[-1:].astype(I32)
    tidx = jnp.minimum(jnp.arange(nt + 1, dtype=I32), nused - 1)
    tsel = tile_end[None, :] <= tidx[:, None]
    tcls = jnp.sum(tsel.astype(I32), axis=1)
    used = ntile > 0
    earlier = used[None, :] & (classes[None, :] < classes[:, None])
    later = classes[None, :] > classes[:, None]
    prev_used = jnp.max(jnp.where(earlier, classes[None, :], -1), axis=1)
    per_class = []
    for table in (EA_TABLE, EB_TABLE):
        e_c = jnp.asarray(table)
        change = used & ((prev_used < 0) | (e_c[jnp.maximum(prev_used, 0)] != e_c))
        nxt = jnp.min(jnp.where(change[None, :] & later, classes[None, :], N_CLASSES), axis=1)
        per_class += [e_c, change.astype(I32), (jnp.cumsum(change.astype(I32)) - 1) & 1,
                      e_c[jnp.minimum(nxt, N_CLASSES - 1)], (nxt < N_CLASSES).astype(I32)]
    onehot = tcls[:, None] == classes
    per_tile = jnp.sum(jnp.where(onehot[None], jnp.stack(per_class)[:, None, :], 0), axis=2)
    tables = (tidx, tcls, nused) + tuple(per_tile)
    return pos8.astype(I32), off + cnt, padded - cnt, nused * TILE_M, tables


def _mixer_sample_kernel(alpha, x_ref, h0_ref, win_ref, wout_ref, lnvg_ref, lnvb_ref, ws0_ref, bs0_ref,
                         bdb_ref, bdc_ref, lr_ref, li_ref, dskip_ref, glu_ref, bglu_ref,
                         ln1g_ref, ln1b_ref,
                         x1_ref, hnew_ref, v_ref):
    x = x_ref[...]
    proj = _dot(x.astype(BF16), win_ref[...])
    u = jax.nn.gelu(proj[:, :W_A])
    v = _layer_norm(jax.nn.gelu(proj[:, W_A:2 * W_A]), lnvg_ref[...], lnvb_ref[...])
    v_ref[...] = v
    y_a = u * (ws0_ref[...] * v + bs0_ref[...])
    xs = proj[:, 2 * W_A:]
    zs = []
    for k in range(2):
        sl = slice(k * HALF_W, (k + 1) * HALF_W)
        bu = _dot(xs[:, sl].astype(BF16), bdb_ref[k])
        h0r = h0_ref[k, :, :HALF_STATE]
        h0i = h0_ref[k, :, HALF_STATE:]
        lr = lr_ref[k]
        li = li_ref[k]
        hr = lr * h0r - li * h0i + bu[:, :HALF_STATE]
        hi = lr * h0i + li * h0r + bu[:, HALF_STATE:]
        hnew_ref[k, :, :HALF_STATE] = hr
        hnew_ref[k, :, HALF_STATE:] = hi
        hcat = jnp.concatenate([hr, hi], axis=1).astype(BF16)
        y = jax.nn.gelu(_dot(hcat, bdc_ref[k]) + dskip_ref[:, sl] * xs[:, sl])
        gl = _dot(y.astype(BF16), glu_ref[k]) + bglu_ref[:, sl]
        zs.append(y * jax.nn.sigmoid(gl))
    cat = jnp.concatenate([y_a] + zs, axis=1).astype(BF16)
    mix = _dot(cat, wout_ref[...])
    x1_ref[...] = _layer_norm(alpha * x + mix, ln1g_ref[...], ln1b_ref[...])


def _mixer_sample(x, h0, lw, l, alpha):
    n = x.shape[0]
    full = lambda shape: pl.BlockSpec(shape, lambda i: (0,) * len(shape))
    return pl.pallas_call(
        functools.partial(_mixer_sample_kernel, alpha),
        grid=(1,),
        in_specs=[full(x.shape), _layer_spec(h0.shape, l)] + [_layer_spec(lw[k].shape, l) for k in SAMPLE_WEIGHTS],
        out_specs=[full((n, D_MODEL)), full((2, n, 2 * HALF_STATE)), full((n, W_A))],
        out_shape=[jax.ShapeDtypeStruct((n, D_MODEL), F32),
                   jax.ShapeDtypeStruct((2, n, 2 * HALF_STATE), F32),
                   jax.ShapeDtypeStruct((n, W_A), F32)],
        compiler_params=pltpu.CompilerParams(dimension_semantics=("arbitrary",), vmem_limit_bytes=VMEM_LIMIT),
    )(x, h0, *[lw[k] for k in SAMPLE_WEIGHTS])


def _route(x, wr, rbias):
    logits = jnp.dot(x, wr, preferred_element_type=F32, precision=HIGHEST)
    scores = jax.nn.sigmoid(logits)
    biased = scores + rbias
    lane = lax.broadcasted_iota(I32, biased.shape, 1)
    grp = lane // EXPERTS_PER_GROUP
    neg = jnp.float32(-jnp.inf)

    def top2(vals):
        m1 = jnp.max(vals, axis=-1, keepdims=True)
        i1 = jnp.min(jnp.where(vals == m1, lane, N_EXPERTS), axis=-1, keepdims=True)
        rest = jnp.where(lane == i1, neg, vals)
        m2 = jnp.max(rest, axis=-1, keepdims=True)
        i2 = jnp.min(jnp.where(rest == m2, lane, N_EXPERTS), axis=-1, keepdims=True)
        return m1, i1, m2, i2

    best = sel = None
    for g in range(N_EXPERT_GROUPS):
        m1, _, m2, _ = top2(jnp.where(grp == g, biased, neg))
        gs = m1 + m2
        if g == 0:
            best, sel = gs, jnp.zeros(gs.shape, I32)
        else:
            upd = gs > best
            sel = jnp.where(upd, g, sel)
            best = jnp.where(upd, gs, best)
    _, i1, _, i2 = top2(jnp.where(grp == sel, biased, neg))
    s1 = jnp.sum(jnp.where(lane == i1, scores, 0.0), axis=-1, keepdims=True)
    s2 = jnp.sum(jnp.where(lane == i2, scores, 0.0), axis=-1, keepdims=True)
    tot = s1 + s2
    return jnp.where(lane == i1, s1 / tot, 0.0) + jnp.where(lane == i2, s2 / tot, 0.0)


def _moe_kernel(alpha, x_ref, wr_ref, rb_ref, wg_ref, wu_ref, wd_ref, ln2g_ref, ln2b_ref,
                out_ref, xb_ref, comb_ref, acc_ref):
    step = pl.program_id(1)

    @pl.when(step == 0)
    def _():
        x = x_ref[...]
        xb_ref[...] = x.astype(BF16)
        comb_ref[...] = _route(x, wr_ref[...], rb_ref[...])
        acc_ref[...] = jnp.zeros_like(acc_ref)

    xb = xb_ref[...]
    comb = comb_ref[...]
    lane = lax.broadcasted_iota(I32, comb.shape, 1)
    acc = acc_ref[...]
    for k in range(DENSE_EXPERTS_PER_STEP):
        g = _dot(xb, wg_ref[k].astype(BF16))
        u = _dot(xb, wu_ref[k].astype(BF16))
        ce = jnp.sum(jnp.where(lane == step * DENSE_EXPERTS_PER_STEP + k, comb, 0.0), axis=-1, keepdims=True)
        h = (jax.nn.silu(g) * u * ce).astype(BF16)
        acc = acc + _dot(h, wd_ref[k].astype(BF16))
    acc_ref[...] = acc

    @pl.when(step == pl.num_programs(1) - 1)
    def _():
        out_ref[...] = _layer_norm(alpha * x_ref[...] + acc_ref[...], ln2g_ref[...], ln2b_ref[...])


def _moe_dense(x, l, wr, rb, w_gate, w_up, w_down, ln2g, ln2b, alpha, tm):
    t = x.shape[0]
    cst = lambda i, e: (0, 0)
    wsel = lambda i, e: (l, e, 0, 0)
    return pl.pallas_call(
        functools.partial(_moe_kernel, alpha),
        grid=(t // tm, N_EXPERTS // DENSE_EXPERTS_PER_STEP),
        in_specs=[pl.BlockSpec((tm, D_MODEL), lambda i, e: (i, 0)),
                  pl.BlockSpec(wr.shape, cst), pl.BlockSpec(rb.shape, cst),
                  pl.BlockSpec((None, DENSE_EXPERTS_PER_STEP, D_MODEL, D_FF_EXPERT), wsel),
                  pl.BlockSpec((None, DENSE_EXPERTS_PER_STEP, D_MODEL, D_FF_EXPERT), wsel),
                  pl.BlockSpec((None, DENSE_EXPERTS_PER_STEP, D_FF_EXPERT, D_MODEL), wsel),
                  pl.BlockSpec((None,) + ln2g.shape[1:], lambda i, e: (l, 0, 0)),
                  pl.BlockSpec((None,) + ln2b.shape[1:], lambda i, e: (l, 0, 0))],
        out_specs=pl.BlockSpec((tm, D_MODEL), lambda i, e: (i, 0)),
        out_shape=jax.ShapeDtypeStruct((t, D_MODEL), F32),
        scratch_shapes=[pltpu.VMEM((tm, D_MODEL), BF16),
                        pltpu.VMEM((tm, N_EXPERTS), F32),
                        pltpu.VMEM((tm, D_MODEL), F32)],
        compiler_params=pltpu.CompilerParams(dimension_semantics=("arbitrary", "arbitrary"),
                                             vmem_limit_bytes=VMEM_LIMIT),
    )(x, wr, rb, w_gate, w_up, w_down, ln2g, ln2b)


def _prep_s5_folded(lb_re, lb_im, bb_re, bb_im, c_re, c_im):
    d, g, n = lb_re.shape
    s = S5_FOLD
    pairs = g // 2
    pr, pi = [jnp.ones_like(lb_re)], [jnp.zeros_like(lb_re)]
    for _ in range(s):
        pr, pi = pr + [pr[-1] * lb_re - pi[-1] * lb_im], pi + [pr[-1] * lb_im + pi[-1] * lb_re]
    p_re, p_im = jnp.stack(pr, axis=2), jnp.stack(pi, axis=2)

    def pair_cols(re, im):
        z = jnp.zeros_like(re[:, :, 0])
        g0 = jnp.concatenate([re[:, :, 0], z, im[:, :, 0], z], axis=-1)
        g1 = jnp.concatenate([z, re[:, :, 1], z, im[:, :, 1]], axis=-1)
        return jnp.concatenate([g0, g1], axis=2)

    def pair_diag(a):
        z = jnp.zeros_like(a[:, :, 0])
        return jnp.concatenate([jnp.concatenate([a[:, :, 0], z], axis=-1),
                                jnp.concatenate([z, a[:, :, 1]], axis=-1)], axis=2)

    bt_re, bt_im = bb_re.transpose(0, 1, 3, 2)[:, :, None], bb_im.transpose(0, 1, 3, 2)[:, :, None]
    k_re, k_im = p_re[:, :, s - 1::-1, None, :], p_im[:, :, s - 1::-1, None, :]
    m1t = pair_cols((k_re * bt_re - k_im * bt_im).reshape(d, pairs, 2, -1, n),
                    (k_re * bt_im + k_im * bt_re).reshape(d, pairs, 2, -1, n))
    m1 = jnp.swapaxes(m1t, -1, -2)

    cq_re, cq_im = c_re[:, :, None], c_im[:, :, None]
    j_re, j_im = p_re[:, :, 1:, None, :], p_im[:, :, 1:, None, :]
    hpart = pair_cols((cq_re * j_re - cq_im * j_im).reshape(d, pairs, 2, -1, n),
                      (-(cq_re * j_im + cq_im * j_re)).reshape(d, pairs, 2, -1, n))

    t_re, t_im = p_re[:, :, :s, None, :], p_im[:, :, :s, None, :]
    kern = (jnp.einsum("dgtqn,dgnp->dgtqp", cq_re * t_re - cq_im * t_im, bb_re, precision=HIGHEST)
            - jnp.einsum("dgtqn,dgnp->dgtqp", cq_re * t_im + cq_im * t_re, bb_im, precision=HIGHEST))
    zero = jnp.zeros_like(kern[:, :, 0])
    kx = jnp.stack([jnp.concatenate([kern[:, :, j - q] if q <= j else zero for q in range(s)], axis=-1)
                    for j in range(s)], axis=2)
    apart = pair_diag(kx.reshape(d, pairs, 2, -1, kx.shape[-1]))

    m2 = jnp.concatenate([hpart, apart], axis=-1)
    return (m1.astype(BF16), m2.astype(BF16), p_re[:, :, s].reshape(d, 1, g * n), p_im[:, :, s].reshape(d, 1, g * n))


def _prep_all(w_in, w_out, ln_v_g, ln_v_b, w_s, b_s, a_re, a_im, log_dt, b_re, b_im, c_re, c_im,
              d_skip, w_glu, b_glu, ln1_g, ln1_b):
    d = w_in.shape[0]
    dt = jnp.exp(log_dt)[..., None]
    decay = jnp.exp(a_re * dt)
    lb_re, lb_im = decay * jnp.cos(a_im * dt), decay * jnp.sin(a_im * dt)
    den = a_re * a_re + a_im * a_im
    nr, ni = lb_re - 1.0, lb_im
    zr = (nr * a_re + ni * a_im) / den
    zi = (ni * a_re - nr * a_im) / den
    bb_re = zr[..., None] * b_re - zi[..., None] * b_im
    bb_im = zr[..., None] * b_im + zi[..., None] * b_re

    def bd(a):
        r, c = a.shape[2:]
        rep = jnp.asarray(np.tile(np.eye(c, dtype=np.float32), (1, HALF_GROUPS)))
        diag = jnp.asarray(np.kron(np.eye(HALF_GROUPS), np.ones((r, c))) > 0)
        tiled = jnp.einsum("dkrc,cn->dkrn", a.reshape(d, 2, HALF_GROUPS * r, c), rep)
        return jnp.where(diag, tiled, 0.0).astype(BF16)

    bdb = jnp.concatenate([bd(bb_re.transpose(0, 1, 3, 2)), bd(bb_im.transpose(0, 1, 3, 2))], axis=3)
    bdc = jnp.concatenate([bd(c_re.transpose(0, 1, 3, 2)), bd(-c_im.transpose(0, 1, 3, 2))], axis=2)
    m1, m2, lr8, li8 = _prep_s5_folded(lb_re, lb_im, bb_re, bb_im, c_re, c_im)
    return dict(
        m1=m1, m2=m2, lr8=lr8, li8=li8,
        win=w_in.astype(BF16), wout=w_out.astype(BF16),
        lnvg=ln_v_g[:, None], lnvb=ln_v_b[:, None],
        wtril=jnp.tril(w_s).astype(BF16),
        bsb=jnp.broadcast_to(b_s[..., None], (d, H_A, CHUNK, LANES)),
        ws0=jnp.repeat(w_s[:, :, 0, 0], P_A, axis=1)[:, None], bs0=jnp.repeat(b_s[:, :, 0], P_A, axis=1)[:, None],
        bdb=bdb, bdc=bdc,
        lr=lb_re.reshape(d, 2, 1, HALF_STATE), li=lb_im.reshape(d, 2, 1, HALF_STATE),
        dskip=d_skip.reshape(d, 1, W_B), glu=bd(w_glu),
        bglu=b_glu.reshape(d, 1, W_B), ln1g=ln1_g[:, None], ln1b=ln1_b[:, None])


def _state_to_cols(h_re, h_im):
    d, b = h_re.shape[:2]
    re = h_re.reshape(d, b, 2, HALF_STATE)
    im = h_im.reshape(d, b, 2, HALF_STATE)
    return jnp.concatenate([re, im], axis=3).transpose(0, 2, 1, 3)


def _cols_to_state(h):
    d, _, b, _ = h.shape
    re = h[..., :HALF_STATE].transpose(0, 2, 1, 3).reshape(d, b, G_B, N_STATE)
    im = h[..., HALF_STATE:].transpose(0, 2, 1, 3).reshape(d, b, G_B, N_STATE)
    return re, im


def _pairs_to_state(h):
    d, b, _ = h.shape
    h = h.reshape(d, b, G_B // 2, 2, 2, N_STATE)
    return h[:, :, :, 0].reshape(d, b, G_B, N_STATE), h[:, :, :, 1].reshape(d, b, G_B, N_STATE)


def kernel(x_prompt, x_sample, state_ssm_re, state_ssm_im, w_in, w_out, ln_v_g, ln_v_b, w_s, b_s, ssm_a_re, ssm_a_im, ssm_log_dt, ssm_b_re, ssm_b_im, ssm_c_re, ssm_c_im, ssm_d, w_glu, b_glu, ln1_g, ln1_b, ln2_g, ln2_b, w_router, router_bias, w_gate, w_up, w_down):
    depth = w_in.shape[0]
    alpha = float((2 * depth) ** 0.25)
    nb, seq, _ = x_prompt.shape
    ns = x_sample.shape[0]
    tokens = nb * seq
    nt = tokens // TILE_M + N_CLASSES
    rb = router_bias[None]
    wrt = w_router.T
    wrp = jnp.pad(w_router, ((0, 0), (0, LANES - N_EXPERTS)))
    rbcol = router_bias[:, None]
    tok = np.arange(nb * CHUNK)
    tri = jnp.asarray(tok[:, None] < tok[None, :], BF16)
    lw = _prep_all(w_in, w_out, ln_v_g, ln_v_b, w_s, b_s, ssm_a_re, ssm_a_im, ssm_log_dt,
                   ssm_b_re, ssm_b_im, ssm_c_re, ssm_c_im, ssm_d, w_glu, b_glu, ln1_g, ln1_b)
    shared = (wrt, rbcol, tri)
    ln2g, ln2b = ln2_g[:, None], ln2_b[:, None]
    h0s = _state_to_cols(state_ssm_re, state_ssm_im)
    xp = x_prompt
    pos = None
    xs = x_sample.reshape(ns, D_MODEL)
    pr_h, sm_h, sm_v = [], [], []
    for l in range(depth):
        x1t, hfin, cls, rank, counts = _mixer_prompt(xp, lw, shared, l, alpha, nb, seq, pos)
        pos, zstart, zlen, tail, tables = _plan(cls, rank, counts, nt)
        x_sorted = _dispatch(x1t, pos, zstart, zlen, tail, nb * CHUNK, nt * TILE_M)
        xp = _pair_experts(x_sorted, l, tables, wrp, w_gate, w_up, w_down, ln2g, ln2b, alpha)
        pr_h.append(hfin)

        x1s, hnew, v_new = _mixer_sample(xs, h0s, lw, l, alpha)
        xs = _moe_dense(x1s, l, w_router, rb, w_gate, w_up, w_down, ln2g, ln2b, alpha, tm=ns)
        sm_h.append(hnew)
        sm_v.append(v_new.reshape(ns, 1, W_A))
    y_prompt = _ungather(xp, pos, nb, seq)
    pr_re, pr_im = _pairs_to_state(jnp.stack(pr_h))
    sm_re, sm_im = _cols_to_state(jnp.stack(sm_h))
    return (y_prompt, xs.reshape(ns, 1, D_MODEL), pr_re, pr_im, sm_re, sm_im, jnp.stack(sm_v))
```

```python
import functools

import jax
import jax.numpy as jnp
import numpy as np
from jax import lax
from jax.experimental import pallas as pl
from jax.experimental.pallas import tpu as pltpu

D_MODEL = 1024
W_A = 512
W_B = 512
CHUNK = 128
H_A = 4
P_A = W_A // H_A
GROUP_B = 16
G_B = W_B // GROUP_B
N_STATE = 64
N_EXPERTS = 16
N_EXPERT_GROUPS = 4
EXPERTS_PER_GROUP = N_EXPERTS // N_EXPERT_GROUPS
D_FF_EXPERT = D_MODEL // 4
LN_EPS = 1e-5

LANES = 128
SUBLANES = 8
HALF_GROUPS = 16
HALF_W = HALF_GROUPS * GROUP_B
HALF_STATE = HALF_GROUPS * N_STATE
VMEM_LIMIT = 56 * 1024 * 1024

PAIRS = ((0, 1), (0, 2), (0, 3), (1, 3), (1, 2), (3, 2))
N_CLASSES = N_EXPERT_GROUPS * len(PAIRS)
CLASS_ROWS = 32
TILE_M = 256
TOK_ROWS = D_MODEL // LANES
ZERO_TOKENS = TILE_M // 2
DMA_UNROLL = 16
DENSE_EXPERTS_PER_STEP = 4
S5_FOLD = 8
S5_PAIRS = G_B // 2
EA_TABLE = np.array([EXPERTS_PER_GROUP * g + a for g in range(N_EXPERT_GROUPS) for a, _ in PAIRS], np.int32)
EB_TABLE = np.array([EXPERTS_PER_GROUP * g + b for g in range(N_EXPERT_GROUPS) for _, b in PAIRS], np.int32)

F32 = jnp.float32
BF16 = jnp.bfloat16
I32 = jnp.int32
HIGHEST = lax.Precision.HIGHEST


def _layer_norm(x, g, b):
    mu = jnp.mean(x, axis=-1, keepdims=True)
    xc = x - mu
    var = jnp.mean(xc * xc, axis=-1, keepdims=True)
    return xc * lax.rsqrt(var + LN_EPS) * g + b


def _dot(a, b):
    return jnp.dot(a, b, preferred_element_type=F32)


def _route_classes(x1, wrt, rbcol):
    def split(a):
        hi = a.astype(BF16)
        return hi, (a - hi.astype(F32)).astype(BF16)

    def dot_t(a, b):
        return lax.dot_general(a, b, (((1,), (1,)), ((), ())), preferred_element_type=F32)
    w_hi, w_lo = split(wrt)
    x_hi, x_lo = split(x1)
    logits_t = dot_t(w_hi, x_hi) + (dot_t(w_hi, x_lo) + dot_t(w_lo, x_hi))
    biased = jax.nn.sigmoid(logits_t) + rbcol
    rows = [biased[e:e + 1, :] for e in range(N_EXPERTS)]
    n = EXPERTS_PER_GROUP

    best = sel = None
    for g in range(N_EXPERT_GROUPS):
        v = rows[n * g:n * (g + 1)]
        gs = None
        for a, b in PAIRS:
            s = v[a] + v[b]
            gs = s if gs is None else jnp.maximum(gs, s)
        if g == 0:
            best, sel = gs, jnp.zeros(gs.shape, I32)
        else:
            upd = gs > best
            sel = jnp.where(upd, g, sel)
            best = jnp.where(upd, gs, best)

    cls = jnp.zeros(sel.shape, I32)
    for g in range(N_EXPERT_GROUPS):
        v = rows[n * g:n * (g + 1)]
        lo = jnp.full(sel.shape, n, I32)
        hi = jnp.full(sel.shape, -1, I32)
        for i in range(n):
            before = jnp.zeros(sel.shape, I32)
            for j in range(n):
                if j < i:
                    before = before + (v[j] >= v[i]).astype(I32)
                elif j > i:
                    before = before + (v[j] > v[i]).astype(I32)
            member = before < 2
            lo = jnp.where(member, jnp.minimum(lo, i), lo)
            hi = jnp.where(member, jnp.maximum(hi, i), hi)
        pidx = jnp.zeros(sel.shape, I32)
        for k, (a, b) in enumerate(PAIRS):
            pidx = jnp.where((lo == min(a, b)) & (hi == max(a, b)), k, pidx)
        cls = jnp.where(sel == g, g * len(PAIRS) + pidx, cls)
    return cls


def _to_token_tiles(ref, row0, x):
    n = x.shape[0]
    for c in range(TOK_ROWS):
        ref[pl.ds(row0 * TOK_ROWS + c, n, stride=TOK_ROWS), :] = x[:, c * LANES:(c + 1) * LANES]


def _from_token_tiles(ref, row0, n):
    return jnp.concatenate(
        [ref[pl.ds(row0 * TOK_ROWS + c, n, stride=TOK_ROWS), :] for c in range(TOK_ROWS)], axis=1)


def _gathered_tokens(step, nsteps, pos8_ref, src_hbm, bufs, sems, rows, consume):
    def start(s, base):
        def body(g, c):
            idx0 = base + g * DMA_UNROLL
            row0 = pl.multiple_of(g * (DMA_UNROLL * TOK_ROWS), DMA_UNROLL * TOK_ROWS)
            for i in range(DMA_UNROLL):
                p8 = pl.multiple_of(pos8_ref[idx0 + i], TOK_ROWS)
                pltpu.make_async_copy(src_hbm.at[pl.ds(p8, TOK_ROWS), :],
                                      bufs[s].at[pl.ds(row0 + i * TOK_ROWS, TOK_ROWS), :],
                                      sems.at[s]).start(priority=i % 2)
            return c
        lax.fori_loop(0, rows // DMA_UNROLL, body, 0)

    @pl.when(step == 0)
    def _():
        start(0, 0)

    for s in range(2):
        @pl.when(lax.rem(step, 2) == s)
        def _(s=s):
            pltpu.make_async_copy(src_hbm.at[pl.ds(0, rows * TOK_ROWS), :], bufs[s], sems.at[s]).wait()

            @pl.when(step + 1 < nsteps)
            def _():
                start(1 - s, (step + 1) * rows)
            consume(bufs[s])


def _mixer_kernel(alpha, nb, gather_in, *refs):
    if gather_in:
        pos8_ref, zs_hbm = refs[:2]
        refs = refs[2:]
    else:
        x_ref = refs[0]
        refs = refs[1:]
    (win_ref, wout_ref, lnvg_ref, lnvb_ref, wtril_ref, bsb_ref, m1_ref, m2_ref, lr8_ref, li8_ref,
     dskip_ref, glu_ref, bglu_ref, ln1g_ref, ln1b_ref, wrt_ref, rbcol_ref, tri_ref,
     x1t_ref, hfin_ref, cls_ref, rank_ref, cnt_ref,
     xb_ref, xs_slab, xs_scb, xst_ref, ht_ref, yt_ref, mix_ref, hstate_ref, carry_ref) = refs[:32]
    lt = CHUNK
    rows = nb * lt
    pitch = lt + SUBLANES
    step = pl.program_id(0)
    nsteps = pl.num_programs(0)

    @pl.when(step == 0)
    def _():
        hstate_ref[...] = jnp.zeros_like(hstate_ref)
        carry_ref[...] = jnp.zeros_like(carry_ref)

    if gather_in:
        xin_ref, xbuf0, xbuf1, gsem = refs[32:36]

        def consume(buf):
            xin_ref[...] = _from_token_tiles(buf, 0, rows)
        _gathered_tokens(step, nsteps, pos8_ref, zs_hbm, (xbuf0, xbuf1), gsem, rows, consume)

        def load_x():
            return xin_ref[...]
    else:
        def load_x():
            return x_ref[...].reshape(rows, D_MODEL)

    xb_ref[...] = load_x().astype(BF16)

    fold = S5_FOLD
    nchunk = lt // fold
    cb = nchunk * nb
    xs = _dot(xb_ref[...], win_ref[:, 2 * W_A:])
    for j in range(W_B // LANES):
        for b in range(nb):
            xs_slab[j, b * pitch:b * pitch + lt, :] = xs[b * lt:(b + 1) * lt, j * LANES:(j + 1) * LANES]

    def to_scb(t, c):
        s_, ch_ = lax.bitwise_and(t, fold - 1), lax.shift_right_logical(t, fold.bit_length() - 1)
        r0 = pl.multiple_of(s_ * cb + ch_ * nb, SUBLANES)
        for j in range(W_B // LANES):
            xs_scb[pl.ds(r0, nb), j * LANES:(j + 1) * LANES] = xs_slab[j, pl.ds(t, nb, stride=pitch), :]
        return c
    lax.fori_loop(0, lt, to_scb, 0, unroll=4)
    xst_ref[...] = xs_scb[...].T.astype(BF16)

    vg = jax.nn.gelu(_dot(xb_ref[...], win_ref[:, W_A:2 * W_A]))
    v = _layer_norm(vg, lnvg_ref[...], lnvb_ref[...]).astype(BF16)
    u = jax.nn.gelu(_dot(xb_ref[...], win_ref[:, :W_A]))
    for h in range(H_A):
        hs = slice(h * P_A, (h + 1) * P_A)
        vcat = jnp.concatenate([v[b * lt:(b + 1) * lt, hs] for b in range(nb)], axis=1)
        o = _dot(wtril_ref[h], vcat)
        for b in range(nb):
            rs = slice(b * lt, (b + 1) * lt)
            mix_ref[rs, hs] = u[rs, hs] * (o[:, b * LANES:(b + 1) * LANES] + bsb_ref[h])

    def chunk_inputs(pr):
        return jnp.concatenate(
            [xst_ref[(2 * pr + gi) * GROUP_B:(2 * pr + gi + 1) * GROUP_B, s * cb:(s + 1) * cb]
             for gi in range(2) for s in range(fold)], axis=0)

    pw = 2 * 2 * N_STATE
    for pr in range(S5_PAIRS):
        ht_ref[:, pr * pw:(pr + 1) * pw] = _dot(m1_ref[pr], chunk_inputs(pr)).T

    half_pairs = S5_PAIRS // 2
    for half in range(2):
        c0 = half * half_pairs * pw
        lr8 = [jnp.broadcast_to(lr8_ref[:, (half * half_pairs + p) * LANES:(half * half_pairs + p + 1) * LANES],
                                (nb, LANES)) for p in range(half_pairs)]
        li8 = [jnp.broadcast_to(li8_ref[:, (half * half_pairs + p) * LANES:(half * half_pairs + p + 1) * LANES],
                                (nb, LANES)) for p in range(half_pairs)]
        h = [hstate_ref[:, c0 + q * LANES:c0 + (q + 1) * LANES] for q in range(2 * half_pairs)]
        for c in range(nchunk):
            rs = slice(c * nb, (c + 1) * nb)
            for p in range(half_pairs):
                re_sl = slice(c0 + p * pw, c0 + p * pw + LANES)
                im_sl = slice(c0 + p * pw + LANES, c0 + (p + 1) * pw)
                hr, hi = h[2 * p], h[2 * p + 1]
                ur, ui = ht_ref[rs, re_sl], ht_ref[rs, im_sl]
                ht_ref[rs, re_sl] = hr
                ht_ref[rs, im_sl] = hi
                h[2 * p] = lr8[p] * hr - li8[p] * hi + ur
                h[2 * p + 1] = lr8[p] * hi + li8[p] * hr + ui
        for q in range(2 * half_pairs):
            hstate_ref[:, c0 + q * LANES:c0 + (q + 1) * LANES] = h[q]
    hfin_ref[...] = hstate_ref[...]

    for pr in range(S5_PAIRS):
        rhs = jnp.concatenate([ht_ref[:, pr * pw:(pr + 1) * pw].T.astype(BF16), chunk_inputs(pr)], axis=0)
        yt = _dot(m2_ref[pr], rhs)
        for gi in range(2):
            for s in range(fold):
                r0 = (gi * fold + s) * GROUP_B
                yt_ref[(2 * pr + gi) * GROUP_B:(2 * pr + gi + 1) * GROUP_B, s * cb:(s + 1) * cb] = (
                    yt[r0:r0 + GROUP_B, :])

    y = jax.nn.gelu(yt_ref[...].T + dskip_ref[...] * xs_scb[...])
    yb = y.astype(BF16)
    spitch = cb + SUBLANES
    for k in range(2):
        sl = slice(k * HALF_W, (k + 1) * HALF_W)
        gl = _dot(yb[:, sl], glu_ref[k]) + bglu_ref[:, sl]
        z = y[:, sl] * jax.nn.sigmoid(gl)
        for j in range(HALF_W // LANES):
            for s in range(fold):
                xs_slab[k * (HALF_W // LANES) + j, s * spitch:s * spitch + cb, :] = (
                    z[s * cb:(s + 1) * cb, j * LANES:(j + 1) * LANES])

    def to_bt(ch, c):
        for b in range(nb):
            dst = pl.multiple_of(b * lt + ch * fold, SUBLANES)
            for j in range(W_B // LANES):
                mix_ref[pl.ds(dst, fold), W_A + j * LANES:W_A + (j + 1) * LANES] = (
                    xs_slab[j, pl.ds(ch * nb + b, fold, stride=spitch), :])
        return c
    lax.fori_loop(0, nchunk, to_bt, 0)

    mix = _dot(mix_ref[...].astype(BF16), wout_ref[...])
    x1 = _layer_norm(alpha * load_x() + mix, ln1g_ref[...], ln1b_ref[...])
    _to_token_tiles(x1t_ref, 0, x1)

    cls = _route_classes(x1, wrt_ref[...], rbcol_ref[...])
    crow = lax.broadcasted_iota(I32, (CLASS_ROWS, rows), 0)
    onehot = jnp.where(crow == cls, 1.0, 0.0)
    prefix = _dot(onehot.astype(BF16), tri_ref[...])
    carry = carry_ref[:, 0:1]
    rank = jnp.sum(onehot * (prefix + carry), axis=0, keepdims=True)
    cls_ref[...] = cls.reshape(1, 1, rows)
    rank_ref[...] = rank.astype(I32).reshape(1, 1, rows)
    carry_ref[...] = carry_ref[...] + jnp.sum(onehot, axis=1, keepdims=True)
    cnt_ref[...] = carry_ref[...]


def _const_spec(shape):
    nd = len(shape)
    return pl.BlockSpec(shape, lambda *_: (0,) * nd, pipeline_mode=pl.Buffered(1))


def _layer_spec(shape, l):
    nd = len(shape)
    return pl.BlockSpec((None,) + tuple(shape[1:]), lambda *_: (l,) + (0,) * (nd - 1),
                        pipeline_mode=pl.Buffered(1))


MIXER_WEIGHTS = ("win", "wout", "lnvg", "lnvb", "wtril", "bsb", "m1", "m2", "lr8", "li8", "dskip", "glu",
                 "bglu", "ln1g", "ln1b")
SAMPLE_WEIGHTS = ("win", "wout", "lnvg", "lnvb", "ws0", "bs0", "bdb", "bdc", "lr", "li", "dskip", "glu",
                  "bglu", "ln1g", "ln1b")


def _mixer_prompt(x, lw, shared, l, alpha, nb, seq, pos_prev=None):
    lt = CHUNK
    rows = nb * lt
    nsteps = seq // lt
    gather_in = pos_prev is not None
    weights = tuple(lw[k] for k in MIXER_WEIGHTS) + tuple(shared)
    wspecs = [_layer_spec(lw[k].shape, l) for k in MIXER_WEIGHTS] + [_const_spec(w.shape) for w in shared]
    if gather_in:
        x_spec = pl.BlockSpec(memory_space=pl.ANY)
    else:
        x_spec = pl.BlockSpec((nb, lt, D_MODEL), lambda i, *_: (0, i, 0))
    scratch = [
        pltpu.VMEM((rows, D_MODEL), BF16),
        pltpu.VMEM((W_B // LANES, nb * (lt + SUBLANES), LANES), F32),
        pltpu.VMEM((rows, W_B), F32),
        pltpu.VMEM((W_B, rows), BF16),
        pltpu.VMEM((rows // S5_FOLD, 2 * G_B * N_STATE), F32),
        pltpu.VMEM((W_B, rows), F32),
        pltpu.VMEM((rows, D_MODEL), F32),
        pltpu.VMEM((nb, 2 * G_B * N_STATE), F32),
        pltpu.VMEM((CLASS_ROWS, LANES), F32),
    ]
    if gather_in:
        scratch += [pltpu.VMEM((rows, D_MODEL), F32),
                    pltpu.VMEM((rows * TOK_ROWS, LANES), F32), pltpu.VMEM((rows * TOK_ROWS, LANES), F32),
                    pltpu.SemaphoreType.DMA((2,))]
    grid_spec = pltpu.PrefetchScalarGridSpec(
        num_scalar_prefetch=1 if gather_in else 0,
        grid=(nsteps,),
        in_specs=[x_spec] + wspecs,
        out_specs=[pl.BlockSpec((rows * TOK_ROWS, LANES), lambda i, *_: (i, 0)),
                   pl.BlockSpec((nb, 2 * G_B * N_STATE), lambda i, *_: (0, 0)),
                   pl.BlockSpec((1, 1, rows), lambda i, *_: (i, 0, 0)),
                   pl.BlockSpec((1, 1, rows), lambda i, *_: (i, 0, 0)),
                   pl.BlockSpec((CLASS_ROWS, LANES), lambda i, *_: (0, 0))],
        scratch_shapes=scratch)
    args = ((pos_prev, x) if gather_in else (x,)) + weights
    return pl.pallas_call(
        functools.partial(_mixer_kernel, alpha, nb, gather_in),
        grid_spec=grid_spec,
        out_shape=[jax.ShapeDtypeStruct((nb * seq * TOK_ROWS, LANES), F32),
                   jax.ShapeDtypeStruct((nb, 2 * G_B * N_STATE), F32),
                   jax.ShapeDtypeStruct((nsteps, 1, rows), I32),
                   jax.ShapeDtypeStruct((nsteps, 1, rows), I32),
                   jax.ShapeDtypeStruct((CLASS_ROWS, LANES), F32)],
        compiler_params=pltpu.CompilerParams(dimension_semantics=("arbitrary",),
                                             vmem_limit_bytes=VMEM_LIMIT),
    )(*args)


def _dispatch_kernel(rows, pos8_ref, zstart_ref, zlen_ref, tail_ref, x_ref, xs_hbm, zero_ref, sem, zsem):
    step = pl.program_id(0)
    ztok = ZERO_TOKENS

    @pl.when(step == 0)
    def _():
        zero_ref[...] = jnp.zeros_like(zero_ref)
        pieces = []
        for c in range(N_CLASSES):
            start = zstart_ref[c]
            zlen = zlen_ref[c]
            p = TILE_M // 2
            while p >= 1:
                hit = (zlen & p) != 0
                pieces.append((hit, pltpu.make_async_copy(
                    zero_ref.at[pl.ds(0, p * TOK_ROWS), :],
                    xs_hbm.at[pl.ds(pl.multiple_of(start * TOK_ROWS, TOK_ROWS), p * TOK_ROWS), :], zsem)))
                start = start + jnp.where(hit, p, 0)
                p //= 2
        for hit, cp in pieces:
            pl.when(hit)(cp.start)
        for hit, cp in pieces:
            pl.when(hit)(cp.wait)

        zrows = ztok * TOK_ROWS
        first = tail_ref[0] // ztok

        def tail_copy(q):
            return pltpu.make_async_copy(
                zero_ref, xs_hbm.at[pl.ds(pl.multiple_of(q * zrows, zrows), zrows), :], zsem)

        def tail_start(q, c):
            tail_copy(q).start()
            return c

        def tail_wait(q, c):
            tail_copy(q).wait()
            return c
        lax.fori_loop(first, xs_hbm.shape[0] // zrows, tail_start, 0)
        lax.fori_loop(first, xs_hbm.shape[0] // zrows, tail_wait, 0)

    base = step * rows

    def body(g, c):
        idx0 = base + g * DMA_UNROLL
        row0 = pl.multiple_of(g * (DMA_UNROLL * TOK_ROWS), DMA_UNROLL * TOK_ROWS)
        for i in range(DMA_UNROLL):
            p8 = pl.multiple_of(pos8_ref[idx0 + i], TOK_ROWS)
            pltpu.make_async_copy(x_ref.at[pl.ds(row0 + i * TOK_ROWS, TOK_ROWS), :],
                                  xs_hbm.at[pl.ds(p8, TOK_ROWS), :], sem).start(priority=i % 2)
        return c
    lax.fori_loop(0, rows // DMA_UNROLL, body, 0)
    pltpu.make_async_copy(x_ref, xs_hbm.at[pl.ds(0, rows * TOK_ROWS), :], sem).wait()


def _dispatch(x1t, pos8, zstart, zlen, tail, rows, ns_tokens):
    return pl.pallas_call(
        functools.partial(_dispatch_kernel, rows),
        grid_spec=pltpu.PrefetchScalarGridSpec(
            num_scalar_prefetch=4,
            grid=(x1t.shape[0] // (rows * TOK_ROWS),),
            in_specs=[pl.BlockSpec((rows * TOK_ROWS, LANES), lambda i, *_: (i, 0))],
            out_specs=pl.BlockSpec(memory_space=pl.ANY),
            scratch_shapes=[pltpu.VMEM((ZERO_TOKENS * TOK_ROWS, LANES), F32),
                            pltpu.SemaphoreType.DMA(()), pltpu.SemaphoreType.DMA(())]),
        out_shape=jax.ShapeDtypeStruct((ns_tokens * TOK_ROWS, LANES), F32),
        compiler_params=pltpu.CompilerParams(dimension_semantics=("arbitrary",),
                                             vmem_limit_bytes=VMEM_LIMIT),
    )(pos8, zstart, zlen, tail, x1t)


def _pair_kernel(alpha, l, tidx_ref, tcls_ref, nused_ref,
                 ea_ref, cha_ref, bufa_ref, nxa_ref, hna_ref, eb_ref, chb_ref, bufb_ref, nxb_ref, hnb_ref,
                 x_ref, wrp_ref, wg_hbm, wu_hbm, wd_hbm,
                 ln2g_ref, ln2b_ref, z_ref, w1_ref, w2_ref, xprev_ref, moe_ref, sg_ref, su_ref, sd_ref, wsem):
    j = pl.program_id(0)
    f = D_FF_EXPERT
    nused = nused_ref[0]
    slots = ((ea_ref, cha_ref, bufa_ref, nxa_ref, hna_ref), (eb_ref, chb_ref, bufb_ref, nxb_ref, hnb_ref))

    def weight_copies(k, e, buf):
        return [pltpu.make_async_copy(wg_hbm.at[l, e], sg_ref.at[k, buf], wsem.at[k, buf]),
                pltpu.make_async_copy(wu_hbm.at[l, e], su_ref.at[k, buf], wsem.at[k, buf]),
                pltpu.make_async_copy(wd_hbm.at[l, e], sd_ref.at[k, buf], wsem.at[k, buf])]

    @pl.when(j == 0)
    def _():
        xprev_ref[...] = jnp.zeros_like(xprev_ref)
        moe_ref[...] = jnp.zeros_like(moe_ref)
        w1_ref[:, 4 * f:] = wrp_ref[...].astype(BF16)
        for k, (e_ref, _, buf_ref, _, _) in enumerate(slots):
            for cp in weight_copies(k, e_ref[0], buf_ref[0]):
                cp.start()

    @pl.when(j > nused)
    def _():
        z_ref[...] = jnp.zeros_like(z_ref)

    @pl.when(j <= nused)
    def _():
        class_start = jnp.logical_or(j == 0, tcls_ref[j] != tcls_ref[jnp.maximum(j - 1, 0)])

        for k, (e_ref, ch_ref, buf_ref, nx_ref, hn_ref) in enumerate(slots):
            @pl.when(jnp.logical_and(class_start, ch_ref[j] != 0))
            def _(k=k, e_ref=e_ref, buf_ref=buf_ref, nx_ref=nx_ref, hn_ref=hn_ref):
                buf = buf_ref[j]
                for cp in weight_copies(k, e_ref[j], buf):
                    cp.wait()

                @pl.when(hn_ref[j] != 0)
                def _():
                    for cp in weight_copies(k, nx_ref[j], 1 - buf):
                        cp.start()
                w1_ref[:, 2 * k * f:(2 * k + 1) * f] = sg_ref[k, buf].astype(BF16)
                w1_ref[:, (2 * k + 1) * f:(2 * k + 2) * f] = su_ref[k, buf].astype(BF16)
                w2_ref[k * f:(k + 1) * f, :] = sd_ref[k, buf].astype(BF16)

        _to_token_tiles(z_ref, 0, _layer_norm(alpha * xprev_ref[...] + moe_ref[...], ln2g_ref[...], ln2b_ref[...]))

        x = _from_token_tiles(x_ref, 0, TILE_M)
        gu = _dot(x.astype(BF16), w1_ref[...])
        scores = jax.nn.sigmoid(gu[:, 4 * f:])
        lane = lax.broadcasted_iota(I32, scores.shape, 1)
        sa = jnp.sum(jnp.where(lane == ea_ref[j], scores, 0.0), axis=-1, keepdims=True)
        sb = jnp.sum(jnp.where(lane == eb_ref[j], scores, 0.0), axis=-1, keepdims=True)
        tot = sa + sb
        ha = jax.nn.silu(gu[:, 0 * f:1 * f]) * gu[:, 1 * f:2 * f] * (sa / tot)
        hb = jax.nn.silu(gu[:, 2 * f:3 * f]) * gu[:, 3 * f:4 * f] * (sb / tot)
        xprev_ref[...] = x
        moe_ref[...] = _dot(jnp.concatenate([ha, hb], axis=1).astype(BF16), w2_ref[...])


def _pair_experts(xs, l, tables, wrp, w_gate, w_up, w_down, ln2g, ln2b, alpha):
    nsteps = tables[0].shape[0]
    cst = lambda j, *_: (0, 0)
    lsel = lambda j, *_: (l, 0, 0)
    hbm = pl.BlockSpec(memory_space=pl.ANY)
    return pl.pallas_call(
        functools.partial(_pair_kernel, alpha, l),
        grid_spec=pltpu.PrefetchScalarGridSpec(
            num_scalar_prefetch=len(tables),
            grid=(nsteps,),
            in_specs=[pl.BlockSpec((TILE_M * TOK_ROWS, LANES), lambda j, ti, *_: (ti[j], 0)),
                      pl.BlockSpec(wrp.shape, cst), hbm, hbm, hbm,
                      pl.BlockSpec((None,) + ln2g.shape[1:], lsel), pl.BlockSpec((None,) + ln2b.shape[1:], lsel)],
            out_specs=pl.BlockSpec((TILE_M * TOK_ROWS, LANES), lambda j, *_: (jnp.maximum(j - 1, 0), 0)),
            scratch_shapes=[pltpu.VMEM((D_MODEL, 4 * D_FF_EXPERT + LANES), BF16),
                            pltpu.VMEM((2 * D_FF_EXPERT, D_MODEL), BF16),
                            pltpu.VMEM((TILE_M, D_MODEL), F32), pltpu.VMEM((TILE_M, D_MODEL), F32),
                            pltpu.VMEM((2, 2, D_MODEL, D_FF_EXPERT), F32),
                            pltpu.VMEM((2, 2, D_MODEL, D_FF_EXPERT), F32),
                            pltpu.VMEM((2, 2, D_FF_EXPERT, D_MODEL), F32),
                            pltpu.SemaphoreType.DMA((2, 2))]),
        out_shape=jax.ShapeDtypeStruct(xs.shape, F32),
        compiler_params=pltpu.CompilerParams(dimension_semantics=("arbitrary",),
                                             vmem_limit_bytes=VMEM_LIMIT),
    )(*tables, xs, wrp, w_gate, w_up, w_down, ln2g, ln2b)


def _ungather_kernel(nb, pos8_ref, zs_hbm, out_ref, buf0, buf1, sems):
    rows = nb * CHUNK

    def consume(buf):
        out_ref[...] = _from_token_tiles(buf, 0, rows).reshape(nb, CHUNK, D_MODEL)
    _gathered_tokens(pl.program_id(0), pl.num_programs(0), pos8_ref, zs_hbm, (buf0, buf1), sems, rows, consume)


def _ungather(zs, pos8, nb, seq):
    rows = nb * CHUNK
    return pl.pallas_call(
        functools.partial(_ungather_kernel, nb),
        grid_spec=pltpu.PrefetchScalarGridSpec(
            num_scalar_prefetch=1,
            grid=(seq // CHUNK,),
            in_specs=[pl.BlockSpec(memory_space=pl.ANY)],
            out_specs=pl.BlockSpec((nb, CHUNK, D_MODEL), lambda i, *_: (0, i, 0)),
            scratch_shapes=[pltpu.VMEM((rows * TOK_ROWS, LANES), F32), pltpu.VMEM((rows * TOK_ROWS, LANES), F32),
                            pltpu.SemaphoreType.DMA((2,))]),
        out_shape=jax.ShapeDtypeStruct((nb, seq, D_MODEL), F32),
        compiler_params=pltpu.CompilerParams(dimension_semantics=("arbitrary",),
                                             vmem_limit_bytes=VMEM_LIMIT),
    )(pos8, zs)


def _plan(cls, rank, counts, nt):
    cnt = counts[:N_CLASSES, 0].astype(I32)
    ntile = (cnt + TILE_M - 1) // TILE_M
    padded = ntile * TILE_M
    off = jnp.cumsum(padded) - padded
    classes = jnp.arange(N_CLASSES, dtype=I32)
    pos8 = (rank.reshape(-1) + jnp.sum(jnp.where(cls.reshape(-1, 1) == classes, off, 0), axis=1)) * TOK_ROWS
    tile_end = jnp.cumsum(ntile)
    nused = tile_end[-1:].astype(I32)
    tidx = jnp.minimum(jnp.arange(nt + 1, dtype=I32), nused - 1)
    tsel = tile_end[None, :] <= tidx[:, None]
    tcls = jnp.sum(tsel.astype(I32), axis=1)
    used = ntile > 0
    earlier = used[None, :] & (classes[None, :] < classes[:, None])
    later = classes[None, :] > classes[:, None]
    prev_used = jnp.max(jnp.where(earlier, classes[None, :], -1), axis=1)
    per_class = []
    for table in (EA_TABLE, EB_TABLE):
        e_c = jnp.asarray(table)
        change = used & ((prev_used < 0) | (e_c[jnp.maximum(prev_used, 0)] != e_c))
        nxt = jnp.min(jnp.where(change[None, :] & later, classes[None, :], N_CLASSES), axis=1)
        per_class += [e_c, change.astype(I32), (jnp.cumsum(change.astype(I32)) - 1) & 1,
                      e_c[jnp.minimum(nxt, N_CLASSES - 1)], (nxt < N_CLASSES).astype(I32)]
    onehot = tcls[:, None] == classes
    per_tile = jnp.sum(jnp.where(onehot[None], jnp.stack(per_class)[:, None, :], 0), axis=2)
    tables = (tidx, tcls, nused) + tuple(per_tile)
    return pos8.astype(I32), off + cnt, padded - cnt, nused * TILE_M, tables


def _mixer_sample_kernel(alpha, x_ref, h0_ref, win_ref, wout_ref, lnvg_ref, lnvb_ref, ws0_ref, bs0_ref,
                         bdb_ref, bdc_ref, lr_ref, li_ref, dskip_ref, glu_ref, bglu_ref,
                         ln1g_ref, ln1b_ref,
                         x1_ref, hnew_ref, v_ref):
    x = x_ref[...]
    proj = _dot(x.astype(BF16), win_ref[...])
    u = jax.nn.gelu(proj[:, :W_A])
    v = _layer_norm(jax.nn.gelu(proj[:, W_A:2 * W_A]), lnvg_ref[...], lnvb_ref[...])
    v_ref[...] = v
    y_a = u * (ws0_ref[...] * v + bs0_ref[...])
    xs = proj[:, 2 * W_A:]
    zs = []
    for k in range(2):
        sl = slice(k * HALF_W, (k + 1) * HALF_W)
        bu = _dot(xs[:, sl].astype(BF16), bdb_ref[k])
        h0r = h0_ref[k, :, :HALF_STATE]
        h0i = h0_ref[k, :, HALF_STATE:]
        lr = lr_ref[k]
        li = li_ref[k]
        hr = lr * h0r - li * h0i + bu[:, :HALF_STATE]
        hi = lr * h0i + li * h0r + bu[:, HALF_STATE:]
        hnew_ref[k, :, :HALF_STATE] = hr
        hnew_ref[k, :, HALF_STATE:] = hi
        hcat = jnp.concatenate([hr, hi], axis=1).astype(BF16)
        y = jax.nn.gelu(_dot(hcat, bdc_ref[k]) + dskip_ref[:, sl] * xs[:, sl])
        gl = _dot(y.astype(BF16), glu_ref[k]) + bglu_ref[:, sl]
        zs.append(y * jax.nn.sigmoid(gl))
    cat = jnp.concatenate([y_a] + zs, axis=1).astype(BF16)
    mix = _dot(cat, wout_ref[...])
    x1_ref[...] = _layer_norm(alpha * x + mix, ln1g_ref[...], ln1b_ref[...])


def _mixer_sample(x, h0, lw, l, alpha):
    n = x.shape[0]
    full = lambda shape: pl.BlockSpec(shape, lambda i: (0,) * len(shape))
    return pl.pallas_call(
        functools.partial(_mixer_sample_kernel, alpha),
        grid=(1,),
        in_specs=[full(x.shape), _layer_spec(h0.shape, l)] + [_layer_spec(lw[k].shape, l) for k in SAMPLE_WEIGHTS],
        out_specs=[full((n, D_MODEL)), full((2, n, 2 * HALF_STATE)), full((n, W_A))],
        out_shape=[jax.ShapeDtypeStruct((n, D_MODEL), F32),
                   jax.ShapeDtypeStruct((2, n, 2 * HALF_STATE), F32),
                   jax.ShapeDtypeStruct((n, W_A), F32)],
        compiler_params=pltpu.CompilerParams(dimension_semantics=("arbitrary",), vmem_limit_bytes=VMEM_LIMIT),
    )(x, h0, *[lw[k] for k in SAMPLE_WEIGHTS])


def _route(x, wr, rbias):
    logits = jnp.dot(x, wr, preferred_element_type=F32, precision=HIGHEST)
    scores = jax.nn.sigmoid(logits)
    biased = scores + rbias
    lane = lax.broadcasted_iota(I32, biased.shape, 1)
    grp = lane // EXPERTS_PER_GROUP
    neg = jnp.float32(-jnp.inf)

    def top2(vals):
        m1 = jnp.max(vals, axis=-1, keepdims=True)
        i1 = jnp.min(jnp.where(vals == m1, lane, N_EXPERTS), axis=-1, keepdims=True)
        rest = jnp.where(lane == i1, neg, vals)
        m2 = jnp.max(rest, axis=-1, keepdims=True)
        i2 = jnp.min(jnp.where(rest == m2, lane, N_EXPERTS), axis=-1, keepdims=True)
        return m1, i1, m2, i2

    best = sel = None
    for g in range(N_EXPERT_GROUPS):
        m1, _, m2, _ = top2(jnp.where(grp == g, biased, neg))
        gs = m1 + m2
        if g == 0:
            best, sel = gs, jnp.zeros(gs.shape, I32)
        else:
            upd = gs > best
            sel = jnp.where(upd, g, sel)
            best = jnp.where(upd, gs, best)
    _, i1, _, i2 = top2(jnp.where(grp == sel, biased, neg))
    s1 = jnp.sum(jnp.where(lane == i1, scores, 0.0), axis=-1, keepdims=True)
    s2 = jnp.sum(jnp.where(lane == i2, scores, 0.0), axis=-1, keepdims=True)
    tot = s1 + s2
    return jnp.where(lane == i1, s1 / tot, 0.0) + jnp.where(lane == i2, s2 / tot, 0.0)


def _moe_kernel(alpha, x_ref, wr_ref, rb_ref, wg_ref, wu_ref, wd_ref, ln2g_ref, ln2b_ref,
                out_ref, xb_ref, comb_ref, acc_ref):
    step = pl.program_id(1)

    @pl.when(step == 0)
    def _():
        x = x_ref[...]
        xb_ref[...] = x.astype(BF16)
        comb_ref[...] = _route(x, wr_ref[...], rb_ref[...])
        acc_ref[...] = jnp.zeros_like(acc_ref)

    xb = xb_ref[...]
    comb = comb_ref[...]
    lane = lax.broadcasted_iota(I32, comb.shape, 1)
    acc = acc_ref[...]
    for k in range(DENSE_EXPERTS_PER_STEP):
        g = _dot(xb, wg_ref[k].astype(BF16))
        u = _dot(xb, wu_ref[k].astype(BF16))
        ce = jnp.sum(jnp.where(lane == step * DENSE_EXPERTS_PER_STEP + k, comb, 0.0), axis=-1, keepdims=True)
        h = (jax.nn.silu(g) * u * ce).astype(BF16)
        acc = acc + _dot(h, wd_ref[k].astype(BF16))
    acc_ref[...] = acc

    @pl.when(step == pl.num_programs(1) - 1)
    def _():
        out_ref[...] = _layer_norm(alpha * x_ref[...] + acc_ref[...], ln2g_ref[...], ln2b_ref[...])


def _moe_dense(x, l, wr, rb, w_gate, w_up, w_down, ln2g, ln2b, alpha, tm):
    t = x.shape[0]
    cst = lambda i, e: (0, 0)
    wsel = lambda i, e: (l, e, 0, 0)
    return pl.pallas_call(
        functools.partial(_moe_kernel, alpha),
        grid=(t // tm, N_EXPERTS // DENSE_EXPERTS_PER_STEP),
        in_specs=[pl.BlockSpec((tm, D_MODEL), lambda i, e: (i, 0)),
                  pl.BlockSpec(wr.shape, cst), pl.BlockSpec(rb.shape, cst),
                  pl.BlockSpec((None, DENSE_EXPERTS_PER_STEP, D_MODEL, D_FF_EXPERT), wsel),
                  pl.BlockSpec((None, DENSE_EXPERTS_PER_STEP, D_MODEL, D_FF_EXPERT), wsel),
                  pl.BlockSpec((None, DENSE_EXPERTS_PER_STEP, D_FF_EXPERT, D_MODEL), wsel),
                  pl.BlockSpec((None,) + ln2g.shape[1:], lambda i, e: (l, 0, 0)),
                  pl.BlockSpec((None,) + ln2b.shape[1:], lambda i, e: (l, 0, 0))],
        out_specs=pl.BlockSpec((tm, D_MODEL), lambda i, e: (i, 0)),
        out_shape=jax.ShapeDtypeStruct((t, D_MODEL), F32),
        scratch_shapes=[pltpu.VMEM((tm, D_MODEL), BF16),
                        pltpu.VMEM((tm, N_EXPERTS), F32),
                        pltpu.VMEM((tm, D_MODEL), F32)],
        compiler_params=pltpu.CompilerParams(dimension_semantics=("arbitrary", "arbitrary"),
                                             vmem_limit_bytes=VMEM_LIMIT),
    )(x, wr, rb, w_gate, w_up, w_down, ln2g, ln2b)


def _prep_s5_folded(lb_re, lb_im, bb_re, bb_im, c_re, c_im):
    d, g, n = lb_re.shape
    s = S5_FOLD
    pairs = g // 2
    pr, pi = [jnp.ones_like(lb_re)], [jnp.zeros_like(lb_re)]
    for _ in range(s):
        pr, pi = pr + [pr[-1] * lb_re - pi[-1] * lb_im], pi + [pr[-1] * lb_im + pi[-1] * lb_re]
    p_re, p_im = jnp.stack(pr, axis=2), jnp.stack(pi, axis=2)

    def pair_cols(re, im):
        re, im = re.astype(BF16), im.astype(BF16)
        z = jnp.zeros_like(re[:, :, 0])
        g0 = jnp.concatenate([re[:, :, 0], z, im[:, :, 0], z], axis=-1)
        g1 = jnp.concatenate([z, re[:, :, 1], z, im[:, :, 1]], axis=-1)
        return jnp.concatenate([g0, g1], axis=2)

    def pair_diag(a):
        a = a.astype(BF16)
        z = jnp.zeros_like(a[:, :, 0])
        return jnp.concatenate([jnp.concatenate([a[:, :, 0], z], axis=-1),
                                jnp.concatenate([z, a[:, :, 1]], axis=-1)], axis=2)

    bt_re, bt_im = bb_re.transpose(0, 1, 3, 2)[:, :, None], bb_im.transpose(0, 1, 3, 2)[:, :, None]
    k_re, k_im = p_re[:, :, s - 1::-1, None, :], p_im[:, :, s - 1::-1, None, :]
    m1t = pair_cols((k_re * bt_re - k_im * bt_im).reshape(d, pairs, 2, -1, n),
                    (k_re * bt_im + k_im * bt_re).reshape(d, pairs, 2, -1, n))
    m1 = jnp.swapaxes(m1t, -1, -2)

    cq_re, cq_im = c_re[:, :, None], c_im[:, :, None]
    j_re, j_im = p_re[:, :, 1:, None, :], p_im[:, :, 1:, None, :]
    hpart = pair_cols((cq_re * j_re - cq_im * j_im).reshape(d, pairs, 2, -1, n),
                      (-(cq_re * j_im + cq_im * j_re)).reshape(d, pairs, 2, -1, n))

    t_re, t_im = p_re[:, :, :s, None, :], p_im[:, :, :s, None, :]
    kern = (jnp.einsum("dgtqn,dgnp->dgtqp", cq_re * t_re - cq_im * t_im, bb_re, precision=HIGHEST)
            - jnp.einsum("dgtqn,dgnp->dgtqp", cq_re * t_im + cq_im * t_re, bb_im, precision=HIGHEST))
    zero = jnp.zeros_like(kern[:, :, 0])
    kx = jnp.stack([jnp.concatenate([kern[:, :, j - q] if q <= j else zero for q in range(s)], axis=-1)
                    for j in range(s)], axis=2)
    apart = pair_diag(kx.reshape(d, pairs, 2, -1, kx.shape[-1]))

    m2 = jnp.concatenate([hpart, apart], axis=-1)
    return m1, m2, p_re[:, :, s].reshape(d, 1, g * n), p_im[:, :, s].reshape(d, 1, g * n)


def _prep_all(w_in, w_out, ln_v_g, ln_v_b, w_s, b_s, a_re, a_im, log_dt, b_re, b_im, c_re, c_im,
              d_skip, w_glu, b_glu, ln1_g, ln1_b):
    d = w_in.shape[0]
    dt = jnp.exp(log_dt)[..., None]
    decay = jnp.exp(a_re * dt)
    lb_re, lb_im = decay * jnp.cos(a_im * dt), decay * jnp.sin(a_im * dt)
    den = a_re * a_re + a_im * a_im
    nr, ni = lb_re - 1.0, lb_im
    zr = (nr * a_re + ni * a_im) / den
    zi = (ni * a_re - nr * a_im) / den
    bb_re = zr[..., None] * b_re - zi[..., None] * b_im
    bb_im = zr[..., None] * b_im + zi[..., None] * b_re

    def bd(a):
        r, c = a.shape[2:]
        rep = jnp.asarray(np.tile(np.eye(c, dtype=np.float32), (1, HALF_GROUPS)))
        diag = jnp.asarray(np.kron(np.eye(HALF_GROUPS), np.ones((r, c))) > 0)
        tiled = jnp.einsum("dkrc,cn->dkrn", a.reshape(d, 2, HALF_GROUPS * r, c), rep)
        return jnp.where(diag, tiled, 0.0).astype(BF16)

    bdb = jnp.concatenate([bd(bb_re.transpose(0, 1, 3, 2)), bd(bb_im.transpose(0, 1, 3, 2))], axis=3)
    bdc = jnp.concatenate([bd(c_re.transpose(0, 1, 3, 2)), bd(-c_im.transpose(0, 1, 3, 2))], axis=2)
    m1, m2, lr8, li8 = _prep_s5_folded(lb_re, lb_im, bb_re, bb_im, c_re, c_im)
    return dict(
        m1=m1, m2=m2, lr8=lr8, li8=li8,
        win=w_in.astype(BF16), wout=w_out.astype(BF16),
        lnvg=ln_v_g[:, None], lnvb=ln_v_b[:, None],
        wtril=jnp.tril(w_s).astype(BF16),
        bsb=jnp.broadcast_to(b_s[..., None], (d, H_A, CHUNK, LANES)),
        ws0=jnp.repeat(w_s[:, :, 0, 0], P_A, axis=1)[:, None], bs0=jnp.repeat(b_s[:, :, 0], P_A, axis=1)[:, None],
        bdb=bdb, bdc=bdc,
        lr=lb_re.reshape(d, 2, 1, HALF_STATE), li=lb_im.reshape(d, 2, 1, HALF_STATE),
        dskip=d_skip.reshape(d, 1, W_B), glu=bd(w_glu),
        bglu=b_glu.reshape(d, 1, W_B), ln1g=ln1_g[:, None], ln1b=ln1_b[:, None])


def _state_to_cols(h_re, h_im):
    d, b = h_re.shape[:2]
    re = h_re.reshape(d, b, 2, HALF_STATE)
    im = h_im.reshape(d, b, 2, HALF_STATE)
    return jnp.concatenate([re, im], axis=3).transpose(0, 2, 1, 3)


def _cols_to_state(h):
    d, _, b, _ = h.shape
    re = h[..., :HALF_STATE].transpose(0, 2, 1, 3).reshape(d, b, G_B, N_STATE)
    im = h[..., HALF_STATE:].transpose(0, 2, 1, 3).reshape(d, b, G_B, N_STATE)
    return re, im


def _pairs_to_state(h):
    d, b, _ = h.shape
    h = h.reshape(d, b, G_B // 2, 2, 2, N_STATE)
    return h[:, :, :, 0].reshape(d, b, G_B, N_STATE), h[:, :, :, 1].reshape(d, b, G_B, N_STATE)


def kernel(x_prompt, x_sample, state_ssm_re, state_ssm_im, w_in, w_out, ln_v_g, ln_v_b, w_s, b_s, ssm_a_re, ssm_a_im, ssm_log_dt, ssm_b_re, ssm_b_im, ssm_c_re, ssm_c_im, ssm_d, w_glu, b_glu, ln1_g, ln1_b, ln2_g, ln2_b, w_router, router_bias, w_gate, w_up, w_down):
    depth = w_in.shape[0]
    alpha = float((2 * depth) ** 0.25)
    nb, seq, _ = x_prompt.shape
    ns = x_sample.shape[0]
    tokens = nb * seq
    nt = tokens // TILE_M + N_CLASSES
    rb = router_bias[None]
    wrt = w_router.T
    wrp = jnp.pad(w_router, ((0, 0), (0, LANES - N_EXPERTS)))
    rbcol = router_bias[:, None]
    tok = np.arange(nb * CHUNK)
    tri = jnp.asarray(tok[:, None] < tok[None, :], BF16)
    lw = _prep_all(w_in, w_out, ln_v_g, ln_v_b, w_s, b_s, ssm_a_re, ssm_a_im, ssm_log_dt,
                   ssm_b_re, ssm_b_im, ssm_c_re, ssm_c_im, ssm_d, w_glu, b_glu, ln1_g, ln1_b)
    shared = (wrt, rbcol, tri)
    ln2g, ln2b = ln2_g[:, None], ln2_b[:, None]
    h0s = _state_to_cols(state_ssm_re, state_ssm_im)
    xp = x_prompt
    pos = None
    xs = x_sample.reshape(ns, D_MODEL)
    pr_h, sm_h, sm_v = [], [], []
    for l in range(depth):
        x1t, hfin, cls, rank, counts = _mixer_prompt(xp, lw, shared, l, alpha, nb, seq, pos)
        pos, zstart, zlen, tail, tables = _plan(cls, rank, counts, nt)
        x_sorted = _dispatch(x1t, pos, zstart, zlen, tail, nb * CHUNK, nt * TILE_M)
        xp = _pair_experts(x_sorted, l, tables, wrp, w_gate, w_up, w_down, ln2g, ln2b, alpha)
        pr_h.append(hfin)

        x1s, hnew, v_new = _mixer_sample(xs, h0s, lw, l, alpha)
        xs = _moe_dense(x1s, l, w_router, rb, w_gate, w_up, w_down, ln2g, ln2b, alpha, tm=ns)
        sm_h.append(hnew)
        sm_v.append(v_new.reshape(ns, 1, W_A))
    y_prompt = _ungather(xp, pos, nb, seq)
    pr_re, pr_im = _pairs_to_state(jnp.stack(pr_h))
    sm_re, sm_im = _cols_to_state(jnp.stack(sm_h))
    return (y_prompt, xs.reshape(ns, 1, D_MODEL), pr_re, pr_im, sm_re, sm_im, jnp.stack(sm_v))
```

```python
import functools

import jax
import jax.numpy as jnp
import numpy as np
from jax import lax
from jax.experimental import pallas as pl
from jax.experimental.pallas import tpu as pltpu

D_MODEL = 1024
W_A = 512
W_B = 512
CHUNK = 128
H_A = 4
P_A = W_A // H_A
GROUP_B = 16
G_B = W_B // GROUP_B
N_STATE = 64
N_EXPERTS = 16
N_EXPERT_GROUPS = 4
EXPERTS_PER_GROUP = N_EXPERTS // N_EXPERT_GROUPS
D_FF_EXPERT = D_MODEL // 4
LN_EPS = 1e-5

LANES = 128
SUBLANES = 8
HALF_GROUPS = 16
HALF_W = HALF_GROUPS * GROUP_B
HALF_STATE = HALF_GROUPS * N_STATE
VMEM_LIMIT = 56 * 1024 * 1024

PAIRS = ((0, 1), (0, 2), (0, 3), (1, 3), (1, 2), (3, 2))
N_CLASSES = N_EXPERT_GROUPS * len(PAIRS)
CLASS_ROWS = 32
TILE_M = 256
TOK_ROWS = D_MODEL // LANES
ZERO_TOKENS = TILE_M // 2
DMA_UNROLL = 16
DENSE_EXPERTS_PER_STEP = 4
S5_FOLD = 8
S5_PAIRS = G_B // 2
EA_TABLE = np.array([EXPERTS_PER_GROUP * g + a for g in range(N_EXPERT_GROUPS) for a, _ in PAIRS], np.int32)
EB_TABLE = np.array([EXPERTS_PER_GROUP * g + b for g in range(N_EXPERT_GROUPS) for _, b in PAIRS], np.int32)

F32 = jnp.float32
BF16 = jnp.bfloat16
I32 = jnp.int32
HIGHEST = lax.Precision.HIGHEST


def _layer_norm(x, g, b):
    mu = jnp.mean(x, axis=-1, keepdims=True)
    xc = x - mu
    var = jnp.mean(xc * xc, axis=-1, keepdims=True)
    return xc * lax.rsqrt(var + LN_EPS) * g + b


def _dot(a, b):
    return jnp.dot(a, b, preferred_element_type=F32)


def _route_classes(x1, wrt, rbcol):
    def split(a):
        hi = a.astype(BF16)
        return hi, (a - hi.astype(F32)).astype(BF16)

    def dot_t(a, b):
        return lax.dot_general(a, b, (((1,), (1,)), ((), ())), preferred_element_type=F32)
    w_hi, w_lo = split(wrt)
    x_hi, x_lo = split(x1)
    logits_t = dot_t(w_hi, x_hi) + (dot_t(w_hi, x_lo) + dot_t(w_lo, x_hi))
    biased = jax.nn.sigmoid(logits_t) + rbcol
    rows = [biased[e:e + 1, :] for e in range(N_EXPERTS)]
    n = EXPERTS_PER_GROUP

    best = sel = None
    for g in range(N_EXPERT_GROUPS):
        v = rows[n * g:n * (g + 1)]
        gs = None
        for a, b in PAIRS:
            s = v[a] + v[b]
            gs = s if gs is None else jnp.maximum(gs, s)
        if g == 0:
            best, sel = gs, jnp.zeros(gs.shape, I32)
        else:
            upd = gs > best
            sel = jnp.where(upd, g, sel)
            best = jnp.where(upd, gs, best)

    cls = jnp.zeros(sel.shape, I32)
    for g in range(N_EXPERT_GROUPS):
        v = rows[n * g:n * (g + 1)]
        lo = jnp.full(sel.shape, n, I32)
        hi = jnp.full(sel.shape, -1, I32)
        for i in range(n):
            before = jnp.zeros(sel.shape, I32)
            for j in range(n):
                if j < i:
                    before = before + (v[j] >= v[i]).astype(I32)
                elif j > i:
                    before = before + (v[j] > v[i]).astype(I32)
            member = before < 2
            lo = jnp.where(member, jnp.minimum(lo, i), lo)
            hi = jnp.where(member, jnp.maximum(hi, i), hi)
        pidx = jnp.zeros(sel.shape, I32)
        for k, (a, b) in enumerate(PAIRS):
            pidx = jnp.where((lo == min(a, b)) & (hi == max(a, b)), k, pidx)
        cls = jnp.where(sel == g, g * len(PAIRS) + pidx, cls)
    return cls


def _to_token_tiles(ref, row0, x):
    n = x.shape[0]
    for c in range(TOK_ROWS):
        ref[pl.ds(row0 * TOK_ROWS + c, n, stride=TOK_ROWS), :] = x[:, c * LANES:(c + 1) * LANES]


def _from_token_tiles(ref, row0, n):
    return jnp.concatenate(
        [ref[pl.ds(row0 * TOK_ROWS + c, n, stride=TOK_ROWS), :] for c in range(TOK_ROWS)], axis=1)


def _gathered_tokens(step, nsteps, pos8_ref, src_hbm, bufs, sems, rows, consume):
    def start(s, base):
        def body(g, c):
            idx0 = base + g * DMA_UNROLL
            row0 = pl.multiple_of(g * (DMA_UNROLL * TOK_ROWS), DMA_UNROLL * TOK_ROWS)
            for i in range(DMA_UNROLL):
                p8 = pl.multiple_of(pos8_ref[idx0 + i], TOK_ROWS)
                pltpu.make_async_copy(src_hbm.at[pl.ds(p8, TOK_ROWS), :],
                                      bufs[s].at[pl.ds(row0 + i * TOK_ROWS, TOK_ROWS), :],
                                      sems.at[s]).start(priority=i % 2)
            return c
        lax.fori_loop(0, rows // DMA_UNROLL, body, 0)

    @pl.when(step == 0)
    def _():
        start(0, 0)

    for s in range(2):
        @pl.when(lax.rem(step, 2) == s)
        def _(s=s):
            pltpu.make_async_copy(src_hbm.at[pl.ds(0, rows * TOK_ROWS), :], bufs[s], sems.at[s]).wait()

            @pl.when(step + 1 < nsteps)
            def _():
                start(1 - s, (step + 1) * rows)
            consume(bufs[s])


def _mixer_kernel(alpha, nb, gather_in, *refs):
    if gather_in:
        pos8_ref, zs_hbm = refs[:2]
        refs = refs[2:]
    else:
        x_ref = refs[0]
        refs = refs[1:]
    (win_ref, wout_ref, lnvg_ref, lnvb_ref, wtril_ref, bsb_ref, m1_ref, m2_ref, lr8_ref, li8_ref,
     dskip_ref, glu_ref, bglu_ref, ln1g_ref, ln1b_ref, wrt_ref, rbcol_ref, tri_ref,
     x1t_ref, hfin_ref, cls_ref, rank_ref, cnt_ref,
     xb_ref, xs_slab, xs_scb, xst_ref, ht_ref, yt_ref, mix_ref, hstate_ref, carry_ref) = refs[:32]
    lt = CHUNK
    rows = nb * lt
    pitch = lt + SUBLANES
    step = pl.program_id(0)
    nsteps = pl.num_programs(0)

    @pl.when(step == 0)
    def _():
        hstate_ref[...] = jnp.zeros_like(hstate_ref)
        carry_ref[...] = jnp.zeros_like(carry_ref)

    if gather_in:
        xin_ref, xbuf0, xbuf1, gsem = refs[32:36]

        def consume(buf):
            xin_ref[...] = _from_token_tiles(buf, 0, rows)
        _gathered_tokens(step, nsteps, pos8_ref, zs_hbm, (xbuf0, xbuf1), gsem, rows, consume)

        def load_x():
            return xin_ref[...]
    else:
        def load_x():
            return x_ref[...].reshape(rows, D_MODEL)

    xb_ref[...] = load_x().astype(BF16)

    fold = S5_FOLD
    nchunk = lt // fold
    cb = nchunk * nb
    xs = _dot(xb_ref[...], win_ref[:, 2 * W_A:])
    for j in range(W_B // LANES):
        for b in range(nb):
            xs_slab[j, b * pitch:b * pitch + lt, :] = xs[b * lt:(b + 1) * lt, j * LANES:(j + 1) * LANES]

    def to_scb(t, c):
        s_, ch_ = lax.bitwise_and(t, fold - 1), lax.shift_right_logical(t, fold.bit_length() - 1)
        r0 = pl.multiple_of(s_ * cb + ch_ * nb, SUBLANES)
        for j in range(W_B // LANES):
            xs_scb[pl.ds(r0, nb), j * LANES:(j + 1) * LANES] = xs_slab[j, pl.ds(t, nb, stride=pitch), :]
        return c
    lax.fori_loop(0, lt, to_scb, 0, unroll=4)
    xst_ref[...] = xs_scb[...].T.astype(BF16)

    vg = jax.nn.gelu(_dot(xb_ref[...], win_ref[:, W_A:2 * W_A]))
    v = _layer_norm(vg, lnvg_ref[...], lnvb_ref[...]).astype(BF16)
    u = jax.nn.gelu(_dot(xb_ref[...], win_ref[:, :W_A]))
    for h in range(H_A):
        hs = slice(h * P_A, (h + 1) * P_A)
        vcat = jnp.concatenate([v[b * lt:(b + 1) * lt, hs] for b in range(nb)], axis=1)
        o = _dot(wtril_ref[h], vcat)
        for b in range(nb):
            rs = slice(b * lt, (b + 1) * lt)
            mix_ref[rs, hs] = u[rs, hs] * (o[:, b * LANES:(b + 1) * LANES] + bsb_ref[h])

    def chunk_inputs(pr):
        return jnp.concatenate(
            [xst_ref[(2 * pr + gi) * GROUP_B:(2 * pr + gi + 1) * GROUP_B, s * cb:(s + 1) * cb]
             for gi in range(2) for s in range(fold)], axis=0)

    pw = 2 * 2 * N_STATE
    for pr in range(S5_PAIRS):
        ht_ref[:, pr * pw:(pr + 1) * pw] = _dot(m1_ref[pr], chunk_inputs(pr)).T

    half_pairs = S5_PAIRS // 2
    for half in range(2):
        c0 = half * half_pairs * pw
        lr8 = [jnp.broadcast_to(lr8_ref[:, (half * half_pairs + p) * LANES:(half * half_pairs + p + 1) * LANES],
                                (nb, LANES)) for p in range(half_pairs)]
        li8 = [jnp.broadcast_to(li8_ref[:, (half * half_pairs + p) * LANES:(half * half_pairs + p + 1) * LANES],
                                (nb, LANES)) for p in range(half_pairs)]
        h = [hstate_ref[:, c0 + q * LANES:c0 + (q + 1) * LANES] for q in range(2 * half_pairs)]
        for c in range(nchunk):
            rs = slice(c * nb, (c + 1) * nb)
            for p in range(half_pairs):
                re_sl = slice(c0 + p * pw, c0 + p * pw + LANES)
                im_sl = slice(c0 + p * pw + LANES, c0 + (p + 1) * pw)
                hr, hi = h[2 * p], h[2 * p + 1]
                ur, ui = ht_ref[rs, re_sl], ht_ref[rs, im_sl]
                ht_ref[rs, re_sl] = hr
                ht_ref[rs, im_sl] = hi
                h[2 * p] = lr8[p] * hr - li8[p] * hi + ur
                h[2 * p + 1] = lr8[p] * hi + li8[p] * hr + ui
        for q in range(2 * half_pairs):
            hstate_ref[:, c0 + q * LANES:c0 + (q + 1) * LANES] = h[q]
    hfin_ref[...] = hstate_ref[...]

    for pr in range(S5_PAIRS):
        rhs = jnp.concatenate([ht_ref[:, pr * pw:(pr + 1) * pw].T.astype(BF16), chunk_inputs(pr)], axis=0)
        yt = _dot(m2_ref[pr], rhs)
        for gi in range(2):
            for s in range(fold):
                r0 = (gi * fold + s) * GROUP_B
                yt_ref[(2 * pr + gi) * GROUP_B:(2 * pr + gi + 1) * GROUP_B, s * cb:(s + 1) * cb] = (
                    yt[r0:r0 + GROUP_B, :])

    y = jax.nn.gelu(yt_ref[...].T + dskip_ref[...] * xs_scb[...])
    yb = y.astype(BF16)
    spitch = cb + SUBLANES
    for k in range(2):
        sl = slice(k * HALF_W, (k + 1) * HALF_W)
        gl = _dot(yb[:, sl], glu_ref[k]) + bglu_ref[:, sl]
        z = y[:, sl] * jax.nn.sigmoid(gl)
        for j in range(HALF_W // LANES):
            for s in range(fold):
                xs_slab[k * (HALF_W // LANES) + j, s * spitch:s * spitch + cb, :] = (
                    z[s * cb:(s + 1) * cb, j * LANES:(j + 1) * LANES])

    def to_bt(ch, c):
        for b in range(nb):
            dst = pl.multiple_of(b * lt + ch * fold, SUBLANES)
            for j in range(W_B // LANES):
                mix_ref[pl.ds(dst, fold), W_A + j * LANES:W_A + (j + 1) * LANES] = (
                    xs_slab[j, pl.ds(ch * nb + b, fold, stride=spitch), :])
        return c
    lax.fori_loop(0, nchunk, to_bt, 0)

    mix = _dot(mix_ref[...].astype(BF16), wout_ref[...])
    x1 = _layer_norm(alpha * load_x() + mix, ln1g_ref[...], ln1b_ref[...])
    _to_token_tiles(x1t_ref, 0, x1)

    cls = _route_classes(x1, wrt_ref[...], rbcol_ref[...])
    crow = lax.broadcasted_iota(I32, (CLASS_ROWS, rows), 0)
    onehot = jnp.where(crow == cls, 1.0, 0.0)
    prefix = _dot(onehot.astype(BF16), tri_ref[...])
    carry = carry_ref[:, 0:1]
    rank = jnp.sum(onehot * (prefix + carry), axis=0, keepdims=True)
    cls_ref[...] = cls.reshape(1, 1, rows)
    rank_ref[...] = rank.astype(I32).reshape(1, 1, rows)
    carry_ref[...] = carry_ref[...] + jnp.sum(onehot, axis=1, keepdims=True)
    cnt_ref[...] = carry_ref[...]


def _const_spec(shape):
    nd = len(shape)
    return pl.BlockSpec(shape, lambda *_: (0,) * nd, pipeline_mode=pl.Buffered(1))


def _layer_spec(shape, l):
    nd = len(shape)
    return pl.BlockSpec((None,) + tuple(shape[1:]), lambda *_: (l,) + (0,) * (nd - 1),
                        pipeline_mode=pl.Buffered(1))


MIXER_WEIGHTS = ("win", "wout", "lnvg", "lnvb", "wtril", "bsb", "m1", "m2", "lr8", "li8", "dskip", "glu",
                 "bglu", "ln1g", "ln1b")
SAMPLE_WEIGHTS = ("win", "wout", "lnvg", "lnvb", "ws0", "bs0", "bdb", "bdc", "lr", "li", "dskip", "glu",
                  "bglu", "ln1g", "ln1b")


def _mixer_prompt(x, lw, shared, l, alpha, nb, seq, pos_prev=None):
    lt = CHUNK
    rows = nb * lt
    nsteps = seq // lt
    gather_in = pos_prev is not None
    weights = tuple(lw[k] for k in MIXER_WEIGHTS) + tuple(shared)
    wspecs = [_layer_spec(lw[k].shape, l) for k in MIXER_WEIGHTS] + [_const_spec(w.shape) for w in shared]
    if gather_in:
        x_spec = pl.BlockSpec(memory_space=pl.ANY)
    else:
        x_spec = pl.BlockSpec((nb, lt, D_MODEL), lambda i, *_: (0, i, 0))
    scratch = [
        pltpu.VMEM((rows, D_MODEL), BF16),
        pltpu.VMEM((W_B // LANES, nb * (lt + SUBLANES), LANES), F32),
        pltpu.VMEM((rows, W_B), F32),
        pltpu.VMEM((W_B, rows), BF16),
        pltpu.VMEM((rows // S5_FOLD, 2 * G_B * N_STATE), F32),
        pltpu.VMEM((W_B, rows), F32),
        pltpu.VMEM((rows, D_MODEL), F32),
        pltpu.VMEM((nb, 2 * G_B * N_STATE), F32),
        pltpu.VMEM((CLASS_ROWS, LANES), F32),
    ]
    if gather_in:
        scratch += [pltpu.VMEM((rows, D_MODEL), F32),
                    pltpu.VMEM((rows * TOK_ROWS, LANES), F32), pltpu.VMEM((rows * TOK_ROWS, LANES), F32),
                    pltpu.SemaphoreType.DMA((2,))]
    grid_spec = pltpu.PrefetchScalarGridSpec(
        num_scalar_prefetch=1 if gather_in else 0,
        grid=(nsteps,),
        in_specs=[x_spec] + wspecs,
        out_specs=[pl.BlockSpec((rows * TOK_ROWS, LANES), lambda i, *_: (i, 0)),
                   pl.BlockSpec((nb, 2 * G_B * N_STATE), lambda i, *_: (0, 0)),
                   pl.BlockSpec((1, 1, rows), lambda i, *_: (i, 0, 0)),
                   pl.BlockSpec((1, 1, rows), lambda i, *_: (i, 0, 0)),
                   pl.BlockSpec((CLASS_ROWS, LANES), lambda i, *_: (0, 0))],
        scratch_shapes=scratch)
    args = ((pos_prev, x) if gather_in else (x,)) + weights
    return pl.pallas_call(
        functools.partial(_mixer_kernel, alpha, nb, gather_in),
        grid_spec=grid_spec,
        out_shape=[jax.ShapeDtypeStruct((nb * seq * TOK_ROWS, LANES), F32),
                   jax.ShapeDtypeStruct((nb, 2 * G_B * N_STATE), F32),
                   jax.ShapeDtypeStruct((nsteps, 1, rows), I32),
                   jax.ShapeDtypeStruct((nsteps, 1, rows), I32),
                   jax.ShapeDtypeStruct((CLASS_ROWS, LANES), F32)],
        compiler_params=pltpu.CompilerParams(dimension_semantics=("arbitrary",),
                                             vmem_limit_bytes=VMEM_LIMIT),
    )(*args)


def _dispatch_kernel(rows, pos8_ref, zstart_ref, zlen_ref, tail_ref, x_hbm, xs_hbm,
                     buf0, buf1, buf2, zero_ref, insem, outsem, zsem):
    step = pl.program_id(0)
    nsteps = pl.num_programs(0)
    ztok = ZERO_TOKENS
    blk = rows * TOK_ROWS
    bufs = (buf0, buf1, buf2)

    def block_in(i, b):
        return pltpu.make_async_copy(x_hbm.at[pl.ds(pl.multiple_of(i * blk, blk), blk), :], bufs[b], insem.at[b])

    def scatter_done(b):
        return pltpu.make_async_copy(bufs[b], xs_hbm.at[pl.ds(0, blk), :], outsem.at[b])

    @pl.when(step == 0)
    def _():
        block_in(0, 0).start()

        @pl.when(nsteps > 1)
        def _():
            block_in(1, 1).start()

        zero_ref[...] = jnp.zeros_like(zero_ref)
        pieces = []
        for c in range(N_CLASSES):
            start = zstart_ref[c]
            zlen = zlen_ref[c]
            p = TILE_M // 2
            while p >= 1:
                hit = (zlen & p) != 0
                pieces.append((hit, pltpu.make_async_copy(
                    zero_ref.at[pl.ds(0, p * TOK_ROWS), :],
                    xs_hbm.at[pl.ds(pl.multiple_of(start * TOK_ROWS, TOK_ROWS), p * TOK_ROWS), :], zsem)))
                start = start + jnp.where(hit, p, 0)
                p //= 2
        for hit, cp in pieces:
            pl.when(hit)(cp.start)
        for hit, cp in pieces:
            pl.when(hit)(cp.wait)

        zrows = ztok * TOK_ROWS
        first = tail_ref[0] // ztok

        def tail_copy(q):
            return pltpu.make_async_copy(
                zero_ref, xs_hbm.at[pl.ds(pl.multiple_of(q * zrows, zrows), zrows), :], zsem)

        def tail_start(q, c):
            tail_copy(q).start()
            return c

        def tail_wait(q, c):
            tail_copy(q).wait()
            return c
        lax.fori_loop(first, xs_hbm.shape[0] // zrows, tail_start, 0)
        lax.fori_loop(first, xs_hbm.shape[0] // zrows, tail_wait, 0)

    base = step * rows
    for b in range(3):
        @pl.when(lax.rem(step, 3) == b)
        def _(b=b):
            block_in(step, b).wait()

            def body(g, c):
                idx0 = base + g * DMA_UNROLL
                row0 = pl.multiple_of(g * (DMA_UNROLL * TOK_ROWS), DMA_UNROLL * TOK_ROWS)
                for i in range(DMA_UNROLL):
                    p8 = pl.multiple_of(pos8_ref[idx0 + i], TOK_ROWS)
                    pltpu.make_async_copy(bufs[b].at[pl.ds(row0 + i * TOK_ROWS, TOK_ROWS), :],
                                          xs_hbm.at[pl.ds(p8, TOK_ROWS), :], outsem.at[b]).start(priority=i % 2)
                return c
            lax.fori_loop(0, rows // DMA_UNROLL, body, 0)

            prev = (b + 2) % 3

            @pl.when(step >= 1)
            def _():
                scatter_done(prev).wait()

            @pl.when(step + 2 < nsteps)
            def _():
                block_in(step + 2, prev).start()

            @pl.when(step == nsteps - 1)
            def _():
                scatter_done(b).wait()


def _dispatch(x1t, pos8, zstart, zlen, tail, rows, ns_tokens):
    blk = (rows * TOK_ROWS, LANES)
    return pl.pallas_call(
        functools.partial(_dispatch_kernel, rows),
        grid_spec=pltpu.PrefetchScalarGridSpec(
            num_scalar_prefetch=4,
            grid=(x1t.shape[0] // (rows * TOK_ROWS),),
            in_specs=[pl.BlockSpec(memory_space=pl.ANY)],
            out_specs=pl.BlockSpec(memory_space=pl.ANY),
            scratch_shapes=[pltpu.VMEM(blk, F32), pltpu.VMEM(blk, F32), pltpu.VMEM(blk, F32),
                            pltpu.VMEM((ZERO_TOKENS * TOK_ROWS, LANES), F32),
                            pltpu.SemaphoreType.DMA((3,)), pltpu.SemaphoreType.DMA((3,)),
                            pltpu.SemaphoreType.DMA(())]),
        out_shape=jax.ShapeDtypeStruct((ns_tokens * TOK_ROWS, LANES), F32),
        compiler_params=pltpu.CompilerParams(dimension_semantics=("arbitrary",),
                                             vmem_limit_bytes=VMEM_LIMIT),
    )(pos8, zstart, zlen, tail, x1t)


def _pair_kernel(alpha, l, tidx_ref, tcls_ref, nused_ref,
                 ea_ref, cha_ref, bufa_ref, nxa_ref, hna_ref, eb_ref, chb_ref, bufb_ref, nxb_ref, hnb_ref,
                 x_ref, wrp_ref, wg_hbm, wu_hbm, wd_hbm,
                 ln2g_ref, ln2b_ref, z_ref, w1_ref, w2_ref, xprev_ref, moe_ref, sg_ref, su_ref, sd_ref, wsem):
    j = pl.program_id(0)
    f = D_FF_EXPERT
    nused = nused_ref[0]
    slots = ((ea_ref, cha_ref, bufa_ref, nxa_ref, hna_ref), (eb_ref, chb_ref, bufb_ref, nxb_ref, hnb_ref))

    def weight_copies(k, e, buf):
        return [pltpu.make_async_copy(wg_hbm.at[l, e], sg_ref.at[k, buf], wsem.at[k, buf]),
                pltpu.make_async_copy(wu_hbm.at[l, e], su_ref.at[k, buf], wsem.at[k, buf]),
                pltpu.make_async_copy(wd_hbm.at[l, e], sd_ref.at[k, buf], wsem.at[k, buf])]

    @pl.when(j == 0)
    def _():
        xprev_ref[...] = jnp.zeros_like(xprev_ref)
        moe_ref[...] = jnp.zeros_like(moe_ref)
        w1_ref[:, 4 * f:] = wrp_ref[...].astype(BF16)
        for k, (e_ref, _, buf_ref, _, _) in enumerate(slots):
            for cp in weight_copies(k, e_ref[0], buf_ref[0]):
                cp.start()

    @pl.when(j > nused)
    def _():
        z_ref[...] = jnp.zeros_like(z_ref)

    @pl.when(j <= nused)
    def _():
        class_start = jnp.logical_or(j == 0, tcls_ref[j] != tcls_ref[jnp.maximum(j - 1, 0)])

        for k, (e_ref, ch_ref, buf_ref, nx_ref, hn_ref) in enumerate(slots):
            @pl.when(jnp.logical_and(class_start, ch_ref[j] != 0))
            def _(k=k, e_ref=e_ref, buf_ref=buf_ref, nx_ref=nx_ref, hn_ref=hn_ref):
                buf = buf_ref[j]
                for cp in weight_copies(k, e_ref[j], buf):
                    cp.wait()

                @pl.when(hn_ref[j] != 0)
                def _():
                    for cp in weight_copies(k, nx_ref[j], 1 - buf):
                        cp.start()
                w1_ref[:, 2 * k * f:(2 * k + 1) * f] = sg_ref[k, buf].astype(BF16)
                w1_ref[:, (2 * k + 1) * f:(2 * k + 2) * f] = su_ref[k, buf].astype(BF16)
                w2_ref[k * f:(k + 1) * f, :] = sd_ref[k, buf].astype(BF16)

        _to_token_tiles(z_ref, 0, _layer_norm(alpha * xprev_ref[...] + moe_ref[...], ln2g_ref[...], ln2b_ref[...]))

        x = _from_token_tiles(x_ref, 0, TILE_M)
        gu = _dot(x.astype(BF16), w1_ref[...])
        scores = jax.nn.sigmoid(gu[:, 4 * f:])
        lane = lax.broadcasted_iota(I32, scores.shape, 1)
        sa = jnp.sum(jnp.where(lane == ea_ref[j], scores, 0.0), axis=-1, keepdims=True)
        sb = jnp.sum(jnp.where(lane == eb_ref[j], scores, 0.0), axis=-1, keepdims=True)
        tot = sa + sb
        ha = jax.nn.silu(gu[:, 0 * f:1 * f]) * gu[:, 1 * f:2 * f] * (sa / tot)
        hb = jax.nn.silu(gu[:, 2 * f:3 * f]) * gu[:, 3 * f:4 * f] * (sb / tot)
        xprev_ref[...] = x
        moe_ref[...] = _dot(jnp.concatenate([ha, hb], axis=1).astype(BF16), w2_ref[...])


def _pair_experts(xs, l, tables, wrp, w_gate, w_up, w_down, ln2g, ln2b, alpha):
    nsteps = tables[0].shape[0]
    cst = lambda j, *_: (0, 0)
    lsel = lambda j, *_: (l, 0, 0)
    hbm = pl.BlockSpec(memory_space=pl.ANY)
    return pl.pallas_call(
        functools.partial(_pair_kernel, alpha, l),
        grid_spec=pltpu.PrefetchScalarGridSpec(
            num_scalar_prefetch=len(tables),
            grid=(nsteps,),
            in_specs=[pl.BlockSpec((TILE_M * TOK_ROWS, LANES), lambda j, ti, *_: (ti[j], 0)),
                      pl.BlockSpec(wrp.shape, cst), hbm, hbm, hbm,
                      pl.BlockSpec((None,) + ln2g.shape[1:], lsel), pl.BlockSpec((None,) + ln2b.shape[1:], lsel)],
            out_specs=pl.BlockSpec((TILE_M * TOK_ROWS, LANES), lambda j, *_: (jnp.maximum(j - 1, 0), 0)),
            scratch_shapes=[pltpu.VMEM((D_MODEL, 4 * D_FF_EXPERT + LANES), BF16),
                            pltpu.VMEM((2 * D_FF_EXPERT, D_MODEL), BF16),
                            pltpu.VMEM((TILE_M, D_MODEL), F32), pltpu.VMEM((TILE_M, D_MODEL), F32),
                            pltpu.VMEM((2, 2, D_MODEL, D_FF_EXPERT), F32),
                            pltpu.VMEM((2, 2, D_MODEL, D_FF_EXPERT), F32),
                            pltpu.VMEM((2, 2, D_FF_EXPERT, D_MODEL), F32),
                            pltpu.SemaphoreType.DMA((2, 2))]),
        out_shape=jax.ShapeDtypeStruct(xs.shape, F32),
        compiler_params=pltpu.CompilerParams(dimension_semantics=("arbitrary",),
                                             vmem_limit_bytes=VMEM_LIMIT),
    )(*tables, xs, wrp, w_gate, w_up, w_down, ln2g, ln2b)


def _ungather_kernel(nb, pos8_ref, zs_hbm, out_ref, buf0, buf1, sems):
    rows = nb * CHUNK

    def consume(buf):
        out_ref[...] = _from_token_tiles(buf, 0, rows).reshape(nb, CHUNK, D_MODEL)
    _gathered_tokens(pl.program_id(0), pl.num_programs(0), pos8_ref, zs_hbm, (buf0, buf1), sems, rows, consume)


def _ungather(zs, pos8, nb, seq):
    rows = nb * CHUNK
    return pl.pallas_call(
        functools.partial(_ungather_kernel, nb),
        grid_spec=pltpu.PrefetchScalarGridSpec(
            num_scalar_prefetch=1,
            grid=(seq // CHUNK,),
            in_specs=[pl.BlockSpec(memory_space=pl.ANY)],
            out_specs=pl.BlockSpec((nb, CHUNK, D_MODEL), lambda i, *_: (0, i, 0)),
            scratch_shapes=[pltpu.VMEM((rows * TOK_ROWS, LANES), F32), pltpu.VMEM((rows * TOK_ROWS, LANES), F32),
                            pltpu.SemaphoreType.DMA((2,))]),
        out_shape=jax.ShapeDtypeStruct((nb, seq, D_MODEL), F32),
        compiler_params=pltpu.CompilerParams(dimension_semantics=("arbitrary",),
                                             vmem_limit_bytes=VMEM_LIMIT),
    )(pos8, zs)


def _plan(cls, rank, counts, nt):
    cnt = counts[:N_CLASSES, 0].astype(I32)
    ntile = (cnt + TILE_M - 1) // TILE_M
    padded = ntile * TILE_M
    off = jnp.cumsum(padded) - padded
    classes = jnp.arange(N_CLASSES, dtype=I32)
    pos8 = (rank.reshape(-1) + jnp.sum(jnp.where(cls.reshape(-1, 1) == classes, off, 0), axis=1)) * TOK_ROWS
    tile_end = jnp.cumsum(ntile)
    nused = tile_end[-1:].astype(I32)
    tidx = jnp.minimum(jnp.arange(nt + 1, dtype=I32), nused - 1)
    tsel = tile_end[None, :] <= tidx[:, None]
    tcls = jnp.sum(tsel.astype(I32), axis=1)
    used = ntile > 0
    earlier = used[None, :] & (classes[None, :] < classes[:, None])
    later = classes[None, :] > classes[:, None]
    prev_used = jnp.max(jnp.where(earlier, classes[None, :], -1), axis=1)
    per_class = []
    for table in (EA_TABLE, EB_TABLE):
        e_c = jnp.asarray(table)
        change = used & ((prev_used < 0) | (e_c[jnp.maximum(prev_used, 0)] != e_c))
        nxt = jnp.min(jnp.where(change[None, :] & later, classes[None, :], N_CLASSES), axis=1)
        per_class += [e_c, change.astype(I32), (jnp.cumsum(change.astype(I32)) - 1) & 1,
                      e_c[jnp.minimum(nxt, N_CLASSES - 1)], (nxt < N_CLASSES).astype(I32)]
    onehot = tcls[:, None] == classes
    per_tile = jnp.sum(jnp.where(onehot[None], jnp.stack(per_class)[:, None, :], 0), axis=2)
    tables = (tidx, tcls, nused) + tuple(per_tile)
    return pos8.astype(I32), off + cnt, padded - cnt, nused * TILE_M, tables


def _mixer_sample_kernel(alpha, x_ref, h0_ref, win_ref, wout_ref, lnvg_ref, lnvb_ref, ws0_ref, bs0_ref,
                         bdb_ref, bdc_ref, lr_ref, li_ref, dskip_ref, glu_ref, bglu_ref,
                         ln1g_ref, ln1b_ref,
                         x1_ref, hnew_ref, v_ref):
    x = x_ref[...]
    proj = _dot(x.astype(BF16), win_ref[...])
    u = jax.nn.gelu(proj[:, :W_A])
    v = _layer_norm(jax.nn.gelu(proj[:, W_A:2 * W_A]), lnvg_ref[...], lnvb_ref[...])
    v_ref[...] = v
    y_a = u * (ws0_ref[...] * v + bs0_ref[...])
    xs = proj[:, 2 * W_A:]
    zs = []
    for k in range(2):
        sl = slice(k * HALF_W, (k + 1) * HALF_W)
        bu = _dot(xs[:, sl].astype(BF16), bdb_ref[k])
        h0r = h0_ref[k, :, :HALF_STATE]
        h0i = h0_ref[k, :, HALF_STATE:]
        lr = lr_ref[k]
        li = li_ref[k]
        hr = lr * h0r - li * h0i + bu[:, :HALF_STATE]
        hi = lr * h0i + li * h0r + bu[:, HALF_STATE:]
        hnew_ref[k, :, :HALF_STATE] = hr
        hnew_ref[k, :, HALF_STATE:] = hi
        hcat = jnp.concatenate([hr, hi], axis=1).astype(BF16)
        y = jax.nn.gelu(_dot(hcat, bdc_ref[k]) + dskip_ref[:, sl] * xs[:, sl])
        gl = _dot(y.astype(BF16), glu_ref[k]) + bglu_ref[:, sl]
        zs.append(y * jax.nn.sigmoid(gl))
    cat = jnp.concatenate([y_a] + zs, axis=1).astype(BF16)
    mix = _dot(cat, wout_ref[...])
    x1_ref[...] = _layer_norm(alpha * x + mix, ln1g_ref[...], ln1b_ref[...])


def _mixer_sample(x, h0, lw, l, alpha):
    n = x.shape[0]
    full = lambda shape: pl.BlockSpec(shape, lambda i: (0,) * len(shape))
    return pl.pallas_call(
        functools.partial(_mixer_sample_kernel, alpha),
        grid=(1,),
        in_specs=[full(x.shape), _layer_spec(h0.shape, l)] + [_layer_spec(lw[k].shape, l) for k in SAMPLE_WEIGHTS],
        out_specs=[full((n, D_MODEL)), full((2, n, 2 * HALF_STATE)), full((n, W_A))],
        out_shape=[jax.ShapeDtypeStruct((n, D_MODEL), F32),
                   jax.ShapeDtypeStruct((2, n, 2 * HALF_STATE), F32),
                   jax.ShapeDtypeStruct((n, W_A), F32)],
        compiler_params=pltpu.CompilerParams(dimension_semantics=("arbitrary",), vmem_limit_bytes=VMEM_LIMIT),
    )(x, h0, *[lw[k] for k in SAMPLE_WEIGHTS])


def _route(x, wr, rbias):
    logits = jnp.dot(x, wr, preferred_element_type=F32, precision=HIGHEST)
    scores = jax.nn.sigmoid(logits)
    biased = scores + rbias
    lane = lax.broadcasted_iota(I32, biased.shape, 1)
    grp = lane // EXPERTS_PER_GROUP
    neg = jnp.float32(-jnp.inf)

    def top2(vals):
        m1 = jnp.max(vals, axis=-1, keepdims=True)
        i1 = jnp.min(jnp.where(vals == m1, lane, N_EXPERTS), axis=-1, keepdims=True)
        rest = jnp.where(lane == i1, neg, vals)
        m2 = jnp.max(rest, axis=-1, keepdims=True)
        i2 = jnp.min(jnp.where(rest == m2, lane, N_EXPERTS), axis=-1, keepdims=True)
        return m1, i1, m2, i2

    best = sel = None
    for g in range(N_EXPERT_GROUPS):
        m1, _, m2, _ = top2(jnp.where(grp == g, biased, neg))
        gs = m1 + m2
        if g == 0:
            best, sel = gs, jnp.zeros(gs.shape, I32)
        else:
            upd = gs > best
            sel = jnp.where(upd, g, sel)
            best = jnp.where(upd, gs, best)
    _, i1, _, i2 = top2(jnp.where(grp == sel, biased, neg))
    s1 = jnp.sum(jnp.where(lane == i1, scores, 0.0), axis=-1, keepdims=True)
    s2 = jnp.sum(jnp.where(lane == i2, scores, 0.0), axis=-1, keepdims=True)
    tot = s1 + s2
    return jnp.where(lane == i1, s1 / tot, 0.0) + jnp.where(lane == i2, s2 / tot, 0.0)


def _moe_kernel(alpha, x_ref, wr_ref, rb_ref, wg_ref, wu_ref, wd_ref, ln2g_ref, ln2b_ref,
                out_ref, xb_ref, comb_ref, acc_ref):
    step = pl.program_id(1)

    @pl.when(step == 0)
    def _():
        x = x_ref[...]
        xb_ref[...] = x.astype(BF16)
        comb_ref[...] = _route(x, wr_ref[...], rb_ref[...])
        acc_ref[...] = jnp.zeros_like(acc_ref)

    xb = xb_ref[...]
    comb = comb_ref[...]
    lane = lax.broadcasted_iota(I32, comb.shape, 1)
    acc = acc_ref[...]
    for k in range(DENSE_EXPERTS_PER_STEP):
        g = _dot(xb, wg_ref[k].astype(BF16))
        u = _dot(xb, wu_ref[k].astype(BF16))
        ce = jnp.sum(jnp.where(lane == step * DENSE_EXPERTS_PER_STEP + k, comb, 0.0), axis=-1, keepdims=True)
        h = (jax.nn.silu(g) * u * ce).astype(BF16)
        acc = acc + _dot(h, wd_ref[k].astype(BF16))
    acc_ref[...] = acc

    @pl.when(step == pl.num_programs(1) - 1)
    def _():
        out_ref[...] = _layer_norm(alpha * x_ref[...] + acc_ref[...], ln2g_ref[...], ln2b_ref[...])


def _moe_dense(x, l, wr, rb, w_gate, w_up, w_down, ln2g, ln2b, alpha, tm):
    t = x.shape[0]
    cst = lambda i, e: (0, 0)
    wsel = lambda i, e: (l, e, 0, 0)
    return pl.pallas_call(
        functools.partial(_moe_kernel, alpha),
        grid=(t // tm, N_EXPERTS // DENSE_EXPERTS_PER_STEP),
        in_specs=[pl.BlockSpec((tm, D_MODEL), lambda i, e: (i, 0)),
                  pl.BlockSpec(wr.shape, cst), pl.BlockSpec(rb.shape, cst),
                  pl.BlockSpec((None, DENSE_EXPERTS_PER_STEP, D_MODEL, D_FF_EXPERT), wsel),
                  pl.BlockSpec((None, DENSE_EXPERTS_PER_STEP, D_MODEL, D_FF_EXPERT), wsel),
                  pl.BlockSpec((None, DENSE_EXPERTS_PER_STEP, D_FF_EXPERT, D_MODEL), wsel),
                  pl.BlockSpec((None,) + ln2g.shape[1:], lambda i, e: (l, 0, 0)),
                  pl.BlockSpec((None,) + ln2b.shape[1:], lambda i, e: (l, 0, 0))],
        out_specs=pl.BlockSpec((tm, D_MODEL), lambda i, e: (i, 0)),
        out_shape=jax.ShapeDtypeStruct((t, D_MODEL), F32),
        scratch_shapes=[pltpu.VMEM((tm, D_MODEL), BF16),
                        pltpu.VMEM((tm, N_EXPERTS), F32),
                        pltpu.VMEM((tm, D_MODEL), F32)],
        compiler_params=pltpu.CompilerParams(dimension_semantics=("arbitrary", "arbitrary"),
                                             vmem_limit_bytes=VMEM_LIMIT),
    )(x, wr, rb, w_gate, w_up, w_down, ln2g, ln2b)


def _prep_s5_folded(lb_re, lb_im, bb_re, bb_im, c_re, c_im):
    d, g, n = lb_re.shape
    s = S5_FOLD
    pairs = g // 2
    pr, pi = [jnp.ones_like(lb_re)], [jnp.zeros_like(lb_re)]
    for _ in range(s):
        pr, pi = pr + [pr[-1] * lb_re - pi[-1] * lb_im], pi + [pr[-1] * lb_im + pi[-1] * lb_re]
    p_re, p_im = jnp.stack(pr, axis=2), jnp.stack(pi, axis=2)

    def pair_cols(re, im):
        z = jnp.zeros_like(re[:, :, 0])
        g0 = jnp.concatenate([re[:, :, 0], z, im[:, :, 0], z], axis=-1)
        g1 = jnp.concatenate([z, re[:, :, 1], z, im[:, :, 1]], axis=-1)
        return jnp.concatenate([g0, g1], axis=2)

    def pair_diag(a):
        z = jnp.zeros_like(a[:, :, 0])
        return jnp.concatenate([jnp.concatenate([a[:, :, 0], z], axis=-1),
                                jnp.concatenate([z, a[:, :, 1]], axis=-1)], axis=2)

    bt_re, bt_im = bb_re.transpose(0, 1, 3, 2)[:, :, None], bb_im.transpose(0, 1, 3, 2)[:, :, None]
    k_re, k_im = p_re[:, :, s - 1::-1, None, :], p_im[:, :, s - 1::-1, None, :]
    m1t = pair_cols((k_re * bt_re - k_im * bt_im).reshape(d, pairs, 2, -1, n),
                    (k_re * bt_im + k_im * bt_re).reshape(d, pairs, 2, -1, n))
    m1 = jnp.swapaxes(m1t, -1, -2)

    cq_re, cq_im = c_re[:, :, None], c_im[:, :, None]
    j_re, j_im = p_re[:, :, 1:, None, :], p_im[:, :, 1:, None, :]
    hpart = pair_cols((cq_re * j_re - cq_im * j_im).reshape(d, pairs, 2, -1, n),
                      (-(cq_re * j_im + cq_im * j_re)).reshape(d, pairs, 2, -1, n))

    t_re, t_im = p_re[:, :, :s, None, :], p_im[:, :, :s, None, :]
    kern = (jnp.einsum("dgtqn,dgnp->dgtqp", cq_re * t_re - cq_im * t_im, bb_re, precision=HIGHEST)
            - jnp.einsum("dgtqn,dgnp->dgtqp", cq_re * t_im + cq_im * t_re, bb_im, precision=HIGHEST))
    zero = jnp.zeros_like(kern[:, :, 0])
    kx = jnp.stack([jnp.concatenate([kern[:, :, j - q] if q <= j else zero for q in range(s)], axis=-1)
                    for j in range(s)], axis=2)
    apart = pair_diag(kx.reshape(d, pairs, 2, -1, kx.shape[-1]))

    m2 = jnp.concatenate([hpart, apart], axis=-1)
    return (m1.astype(BF16), m2.astype(BF16), p_re[:, :, s].reshape(d, 1, g * n), p_im[:, :, s].reshape(d, 1, g * n))


def _prep_all(w_in, w_out, ln_v_g, ln_v_b, w_s, b_s, a_re, a_im, log_dt, b_re, b_im, c_re, c_im,
              d_skip, w_glu, b_glu, ln1_g, ln1_b):
    d = w_in.shape[0]
    dt = jnp.exp(log_dt)[..., None]
    decay = jnp.exp(a_re * dt)
    lb_re, lb_im = decay * jnp.cos(a_im * dt), decay * jnp.sin(a_im * dt)
    den = a_re * a_re + a_im * a_im
    nr, ni = lb_re - 1.0, lb_im
    zr = (nr * a_re + ni * a_im) / den
    zi = (ni * a_re - nr * a_im) / den
    bb_re = zr[..., None] * b_re - zi[..., None] * b_im
    bb_im = zr[..., None] * b_im + zi[..., None] * b_re

    def bd(a):
        r, c = a.shape[2:]
        rep = jnp.asarray(np.tile(np.eye(c, dtype=np.float32), (1, HALF_GROUPS)))
        diag = jnp.asarray(np.kron(np.eye(HALF_GROUPS), np.ones((r, c))) > 0)
        tiled = jnp.einsum("dkrc,cn->dkrn", a.reshape(d, 2, HALF_GROUPS * r, c), rep)
        return jnp.where(diag, tiled, 0.0).astype(BF16)

    bdb = jnp.concatenate([bd(bb_re.transpose(0, 1, 3, 2)), bd(bb_im.transpose(0, 1, 3, 2))], axis=3)
    bdc = jnp.concatenate([bd(c_re.transpose(0, 1, 3, 2)), bd(-c_im.transpose(0, 1, 3, 2))], axis=2)
    m1, m2, lr8, li8 = _prep_s5_folded(lb_re, lb_im, bb_re, bb_im, c_re, c_im)
    return dict(
        m1=m1, m2=m2, lr8=lr8, li8=li8,
        win=w_in.astype(BF16), wout=w_out.astype(BF16),
        lnvg=ln_v_g[:, None], lnvb=ln_v_b[:, None],
        wtril=jnp.tril(w_s).astype(BF16),
        bsb=jnp.broadcast_to(b_s[..., None], (d, H_A, CHUNK, LANES)),
        ws0=jnp.repeat(w_s[:, :, 0, 0], P_A, axis=1)[:, None], bs0=jnp.repeat(b_s[:, :, 0], P_A, axis=1)[:, None],
        bdb=bdb, bdc=bdc,
        lr=lb_re.reshape(d, 2, 1, HALF_STATE), li=lb_im.reshape(d, 2, 1, HALF_STATE),
        dskip=d_skip.reshape(d, 1, W_B), glu=bd(w_glu),
        bglu=b_glu.reshape(d, 1, W_B), ln1g=ln1_g[:, None], ln1b=ln1_b[:, None])


def _state_to_cols(h_re, h_im):
    d, b = h_re.shape[:2]
    re = h_re.reshape(d, b, 2, HALF_STATE)
    im = h_im.reshape(d, b, 2, HALF_STATE)
    return jnp.concatenate([re, im], axis=3).transpose(0, 2, 1, 3)


def _cols_to_state(h):
    d, _, b, _ = h.shape
    re = h[..., :HALF_STATE].transpose(0, 2, 1, 3).reshape(d, b, G_B, N_STATE)
    im = h[..., HALF_STATE:].transpose(0, 2, 1, 3).reshape(d, b, G_B, N_STATE)
    return re, im


def _pairs_to_state(h):
    d, b, _ = h.shape
    h = h.reshape(d, b, G_B // 2, 2, 2, N_STATE)
    return h[:, :, :, 0].reshape(d, b, G_B, N_STATE), h[:, :, :, 1].reshape(d, b, G_B, N_STATE)


def kernel(x_prompt, x_sample, state_ssm_re, state_ssm_im, w_in, w_out, ln_v_g, ln_v_b, w_s, b_s, ssm_a_re, ssm_a_im, ssm_log_dt, ssm_b_re, ssm_b_im, ssm_c_re, ssm_c_im, ssm_d, w_glu, b_glu, ln1_g, ln1_b, ln2_g, ln2_b, w_router, router_bias, w_gate, w_up, w_down):
    depth = w_in.shape[0]
    alpha = float((2 * depth) ** 0.25)
    nb, seq, _ = x_prompt.shape
    ns = x_sample.shape[0]
    tokens = nb * seq
    nt = tokens // TILE_M + N_CLASSES
    rb = router_bias[None]
    wrt = w_router.T
    wrp = jnp.pad(w_router, ((0, 0), (0, LANES - N_EXPERTS)))
    rbcol = router_bias[:, None]
    tok = np.arange(nb * CHUNK)
    tri = jnp.asarray(tok[:, None] < tok[None, :], BF16)
    lw = _prep_all(w_in, w_out, ln_v_g, ln_v_b, w_s, b_s, ssm_a_re, ssm_a_im, ssm_log_dt,
                   ssm_b_re, ssm_b_im, ssm_c_re, ssm_c_im, ssm_d, w_glu, b_glu, ln1_g, ln1_b)
    shared = (wrt, rbcol, tri)
    ln2g, ln2b = ln2_g[:, None], ln2_b[:, None]
    h0s = _state_to_cols(state_ssm_re, state_ssm_im)
    xp = x_prompt
    pos = None
    xs = x_sample.reshape(ns, D_MODEL)
    pr_h, sm_h, sm_v = [], [], []
    for l in range(depth):
        x1t, hfin, cls, rank, counts = _mixer_prompt(xp, lw, shared, l, alpha, nb, seq, pos)
        pos, zstart, zlen, tail, tables = _plan(cls, rank, counts, nt)
        x_sorted = _dispatch(x1t, pos, zstart, zlen, tail, nb * CHUNK, nt * TILE_M)
        xp = _pair_experts(x_sorted, l, tables, wrp, w_gate, w_up, w_down, ln2g, ln2b, alpha)
        pr_h.append(hfin)

        x1s, hnew, v_new = _mixer_sample(xs, h0s, lw, l, alpha)
        xs = _moe_dense(x1s, l, w_router, rb, w_gate, w_up, w_down, ln2g, ln2b, alpha, tm=ns)
        sm_h.append(hnew)
        sm_v.append(v_new.reshape(ns, 1, W_A))
    y_prompt = _ungather(xp, pos, nb, seq)
    pr_re, pr_im = _pairs_to_state(jnp.stack(pr_h))
    sm_re, sm_im = _cols_to_state(jnp.stack(sm_h))
    return (y_prompt, xs.reshape(ns, 1, D_MODEL), pr_re, pr_im, sm_re, sm_im, jnp.stack(sm_v))
```

```python
import functools

import jax
import jax.numpy as jnp
import numpy as np
from jax import lax
from jax.experimental import pallas as pl
from jax.experimental.pallas import tpu as pltpu

D_MODEL = 1024
W_A = 512
W_B = 512
CHUNK = 128
H_A = 4
P_A = W_A // H_A
GROUP_B = 16
G_B = W_B // GROUP_B
N_STATE = 64
N_EXPERTS = 16
N_EXPERT_GROUPS = 4
EXPERTS_PER_GROUP = N_EXPERTS // N_EXPERT_GROUPS
D_FF_EXPERT = D_MODEL // 4
LN_EPS = 1e-5

LANES = 128
SUBLANES = 8
HALF_GROUPS = 16
HALF_W = HALF_GROUPS * GROUP_B
HALF_STATE = HALF_GROUPS * N_STATE
VMEM_LIMIT = 56 * 1024 * 1024

PAIRS = ((0, 1), (0, 2), (0, 3), (1, 3), (1, 2), (3, 2))
N_CLASSES = N_EXPERT_GROUPS * len(PAIRS)
CLASS_ROWS = 32
TILE_M = 256
TOK_ROWS = D_MODEL // LANES
ZERO_TOKENS = TILE_M // 2
DMA_UNROLL = 16
DENSE_EXPERTS_PER_STEP = 4
S5_FOLD = 8
S5_PAIRS = G_B // 2
EA_TABLE = np.array([EXPERTS_PER_GROUP * g + a for g in range(N_EXPERT_GROUPS) for a, _ in PAIRS], np.int32)
EB_TABLE = np.array([EXPERTS_PER_GROUP * g + b for g in range(N_EXPERT_GROUPS) for _, b in PAIRS], np.int32)

F32 = jnp.float32
BF16 = jnp.bfloat16
I32 = jnp.int32
HIGHEST = lax.Precision.HIGHEST


def _layer_norm(x, g, b):
    mu = jnp.mean(x, axis=-1, keepdims=True)
    xc = x - mu
    var = jnp.mean(xc * xc, axis=-1, keepdims=True)
    return xc * lax.rsqrt(var + LN_EPS) * g + b


def _dot(a, b):
    return jnp.dot(a, b, preferred_element_type=F32)


def _route_classes(x1, wrt, rbcol):
    def split(a):
        hi = a.astype(BF16)
        return hi, (a - hi.astype(F32)).astype(BF16)

    def dot_t(a, b):
        return lax.dot_general(a, b, (((1,), (1,)), ((), ())), preferred_element_type=F32)
    w_hi, w_lo = split(wrt)
    x_hi, x_lo = split(x1)
    logits_t = dot_t(w_hi, x_hi) + (dot_t(w_hi, x_lo) + dot_t(w_lo, x_hi))
    biased = jax.nn.sigmoid(logits_t) + rbcol
    rows = [biased[e:e + 1, :] for e in range(N_EXPERTS)]
    n = EXPERTS_PER_GROUP

    best = sel = None
    for g in range(N_EXPERT_GROUPS):
        v = rows[n * g:n * (g + 1)]
        gs = None
        for a, b in PAIRS:
            s = v[a] + v[b]
            gs = s if gs is None else jnp.maximum(gs, s)
        if g == 0:
            best, sel = gs, jnp.zeros(gs.shape, I32)
        else:
            upd = gs > best
            sel = jnp.where(upd, g, sel)
            best = jnp.where(upd, gs, best)

    cls = jnp.zeros(sel.shape, I32)
    for g in range(N_EXPERT_GROUPS):
        v = rows[n * g:n * (g + 1)]
        lo = jnp.full(sel.shape, n, I32)
        hi = jnp.full(sel.shape, -1, I32)
        for i in range(n):
            before = jnp.zeros(sel.shape, I32)
            for j in range(n):
                if j < i:
                    before = before + (v[j] >= v[i]).astype(I32)
                elif j > i:
                    before = before + (v[j] > v[i]).astype(I32)
            member = before < 2
            lo = jnp.where(member, jnp.minimum(lo, i), lo)
            hi = jnp.where(member, jnp.maximum(hi, i), hi)
        pidx = jnp.zeros(sel.shape, I32)
        for k, (a, b) in enumerate(PAIRS):
            pidx = jnp.where((lo == min(a, b)) & (hi == max(a, b)), k, pidx)
        cls = jnp.where(sel == g, g * len(PAIRS) + pidx, cls)
    return cls


def _to_token_tiles(ref, row0, x):
    n = x.shape[0]
    for c in range(TOK_ROWS):
        ref[pl.ds(row0 * TOK_ROWS + c, n, stride=TOK_ROWS), :] = x[:, c * LANES:(c + 1) * LANES]


def _from_token_tiles(ref, row0, n):
    return jnp.concatenate(
        [ref[pl.ds(row0 * TOK_ROWS + c, n, stride=TOK_ROWS), :] for c in range(TOK_ROWS)], axis=1)


def _gathered_tokens(step, nsteps, pos8_ref, src_hbm, bufs, sems, rows, consume):
    def start(s, base):
        def body(g, c):
            idx0 = base + g * DMA_UNROLL
            row0 = pl.multiple_of(g * (DMA_UNROLL * TOK_ROWS), DMA_UNROLL * TOK_ROWS)
            for i in range(DMA_UNROLL):
                p8 = pl.multiple_of(pos8_ref[idx0 + i], TOK_ROWS)
                pltpu.make_async_copy(src_hbm.at[pl.ds(p8, TOK_ROWS), :],
                                      bufs[s].at[pl.ds(row0 + i * TOK_ROWS, TOK_ROWS), :],
                                      sems.at[s]).start(priority=i % 2)
            return c
        lax.fori_loop(0, rows // DMA_UNROLL, body, 0)

    @pl.when(step == 0)
    def _():
        start(0, 0)

    for s in range(2):
        @pl.when(lax.rem(step, 2) == s)
        def _(s=s):
            pltpu.make_async_copy(src_hbm.at[pl.ds(0, rows * TOK_ROWS), :], bufs[s], sems.at[s]).wait()

            @pl.when(step + 1 < nsteps)
            def _():
                start(1 - s, (step + 1) * rows)
            consume(bufs[s])


def _mixer_kernel(alpha, nb, gather_in, *refs):
    if gather_in:
        pos8_ref, zs_hbm = refs[:2]
        refs = refs[2:]
    else:
        x_ref = refs[0]
        refs = refs[1:]
    (win_ref, wout_ref, lnvg_ref, lnvb_ref, wtril_ref, bsb_ref, m1_ref, m2_ref, lr8_ref, li8_ref,
     dskip_ref, glu_ref, bglu_ref, ln1g_ref, ln1b_ref, wrt_ref, rbcol_ref, tri_ref,
     x1t_ref, hfin_ref, cls_ref, rank_ref, cnt_ref,
     xb_ref, xs_slab, xs_scb, xst_ref, ht_ref, yt_ref, mix_ref, hstate_ref, carry_ref) = refs[:32]
    lt = CHUNK
    rows = nb * lt
    pitch = lt + SUBLANES
    step = pl.program_id(0)
    nsteps = pl.num_programs(0)

    @pl.when(step == 0)
    def _():
        hstate_ref[...] = jnp.zeros_like(hstate_ref)
        carry_ref[...] = jnp.zeros_like(carry_ref)

    if gather_in:
        xin_ref, xbuf0, xbuf1, gsem = refs[32:36]

        def consume(buf):
            xin_ref[...] = _from_token_tiles(buf, 0, rows)
        _gathered_tokens(step, nsteps, pos8_ref, zs_hbm, (xbuf0, xbuf1), gsem, rows, consume)

        def load_x():
            return xin_ref[...]
    else:
        def load_x():
            return x_ref[...].reshape(rows, D_MODEL)

    xb_ref[...] = load_x().astype(BF16)

    fold = S5_FOLD
    nchunk = lt // fold
    cb = nchunk * nb
    xs = _dot(xb_ref[...], win_ref[:, 2 * W_A:])
    for j in range(W_B // LANES):
        for b in range(nb):
            xs_slab[j, b * pitch:b * pitch + lt, :] = xs[b * lt:(b + 1) * lt, j * LANES:(j + 1) * LANES]

    for t in range(lt):
        r0 = (t % fold) * cb + (t // fold) * nb
        for j in range(W_B // LANES):
            xs_scb[r0:r0 + nb, j * LANES:(j + 1) * LANES] = xs_slab[j, pl.ds(t, nb, stride=pitch), :]
    xst_ref[...] = xs_scb[...].T.astype(BF16)

    vg = jax.nn.gelu(_dot(xb_ref[...], win_ref[:, W_A:2 * W_A]))
    v = _layer_norm(vg, lnvg_ref[...], lnvb_ref[...]).astype(BF16)
    u = jax.nn.gelu(_dot(xb_ref[...], win_ref[:, :W_A]))
    for h in range(H_A):
        hs = slice(h * P_A, (h + 1) * P_A)
        vcat = jnp.concatenate([v[b * lt:(b + 1) * lt, hs] for b in range(nb)], axis=1)
        o = _dot(wtril_ref[h], vcat)
        for b in range(nb):
            rs = slice(b * lt, (b + 1) * lt)
            mix_ref[rs, hs] = u[rs, hs] * (o[:, b * LANES:(b + 1) * LANES] + bsb_ref[h])

    def chunk_inputs(pr):
        return jnp.concatenate(
            [xst_ref[(2 * pr + gi) * GROUP_B:(2 * pr + gi + 1) * GROUP_B, s * cb:(s + 1) * cb]
             for gi in range(2) for s in range(fold)], axis=0)

    pw = 2 * 2 * N_STATE
    for pr in range(S5_PAIRS):
        ht_ref[:, pr * pw:(pr + 1) * pw] = _dot(m1_ref[pr], chunk_inputs(pr)).T

    half_pairs = S5_PAIRS // 2
    for half in range(2):
        c0 = half * half_pairs * pw
        lr8 = [jnp.broadcast_to(lr8_ref[:, (half * half_pairs + p) * LANES:(half * half_pairs + p + 1) * LANES],
                                (nb, LANES)) for p in range(half_pairs)]
        li8 = [jnp.broadcast_to(li8_ref[:, (half * half_pairs + p) * LANES:(half * half_pairs + p + 1) * LANES],
                                (nb, LANES)) for p in range(half_pairs)]
        h = [hstate_ref[:, c0 + q * LANES:c0 + (q + 1) * LANES] for q in range(2 * half_pairs)]
        for c in range(nchunk):
            rs = slice(c * nb, (c + 1) * nb)
            for p in range(half_pairs):
                re_sl = slice(c0 + p * pw, c0 + p * pw + LANES)
                im_sl = slice(c0 + p * pw + LANES, c0 + (p + 1) * pw)
                hr, hi = h[2 * p], h[2 * p + 1]
                ur, ui = ht_ref[rs, re_sl], ht_ref[rs, im_sl]
                ht_ref[rs, re_sl] = hr
                ht_ref[rs, im_sl] = hi
                h[2 * p] = lr8[p] * hr - li8[p] * hi + ur
                h[2 * p + 1] = lr8[p] * hi + li8[p] * hr + ui
        for q in range(2 * half_pairs):
            hstate_ref[:, c0 + q * LANES:c0 + (q + 1) * LANES] = h[q]
    hfin_ref[...] = hstate_ref[...]

    for pr in range(S5_PAIRS):
        rhs = jnp.concatenate([ht_ref[:, pr * pw:(pr + 1) * pw].T.astype(BF16), chunk_inputs(pr)], axis=0)
        yt = _dot(m2_ref[pr], rhs)
        for gi in range(2):
            for s in range(fold):
                r0 = (gi * fold + s) * GROUP_B
                yt_ref[(2 * pr + gi) * GROUP_B:(2 * pr + gi + 1) * GROUP_B, s * cb:(s + 1) * cb] = (
                    yt[r0:r0 + GROUP_B, :])

    y = jax.nn.gelu(yt_ref[...].T + dskip_ref[...] * xs_scb[...])
    yb = y.astype(BF16)
    spitch = cb + SUBLANES
    for k in range(2):
        sl = slice(k * HALF_W, (k + 1) * HALF_W)
        gl = _dot(yb[:, sl], glu_ref[k]) + bglu_ref[:, sl]
        z = y[:, sl] * jax.nn.sigmoid(gl)
        for j in range(HALF_W // LANES):
            for s in range(fold):
                xs_slab[k * (HALF_W // LANES) + j, s * spitch:s * spitch + cb, :] = (
                    z[s * cb:(s + 1) * cb, j * LANES:(j + 1) * LANES])

    for ch in range(nchunk):
        for b in range(nb):
            dst = b * lt + ch * fold
            for j in range(W_B // LANES):
                mix_ref[dst:dst + fold, W_A + j * LANES:W_A + (j + 1) * LANES] = (
                    xs_slab[j, pl.ds(ch * nb + b, fold, stride=spitch), :])

    mix = _dot(mix_ref[...].astype(BF16), wout_ref[...])
    x1 = _layer_norm(alpha * load_x() + mix, ln1g_ref[...], ln1b_ref[...])
    _to_token_tiles(x1t_ref, 0, x1)

    cls = _route_classes(x1, wrt_ref[...], rbcol_ref[...])
    crow = lax.broadcasted_iota(I32, (CLASS_ROWS, rows), 0)
    onehot = jnp.where(crow == cls, 1.0, 0.0)
    prefix = _dot(onehot.astype(BF16), tri_ref[...])
    carry = carry_ref[:, 0:1]
    rank = jnp.sum(onehot * (prefix + carry), axis=0, keepdims=True)
    cls_ref[...] = cls.reshape(1, 1, rows)
    rank_ref[...] = rank.astype(I32).reshape(1, 1, rows)
    carry_ref[...] = carry_ref[...] + jnp.sum(onehot, axis=1, keepdims=True)
    cnt_ref[...] = carry_ref[...]


def _const_spec(shape):
    nd = len(shape)
    return pl.BlockSpec(shape, lambda *_: (0,) * nd, pipeline_mode=pl.Buffered(1))


def _layer_spec(shape, l):
    nd = len(shape)
    return pl.BlockSpec((None,) + tuple(shape[1:]), lambda *_: (l,) + (0,) * (nd - 1),
                        pipeline_mode=pl.Buffered(1))


MIXER_WEIGHTS = ("win", "wout", "lnvg", "lnvb", "wtril", "bsb", "m1", "m2", "lr8", "li8", "dskip", "glu",
                 "bglu", "ln1g", "ln1b")
SAMPLE_WEIGHTS = ("win", "wout", "lnvg", "lnvb", "ws0", "bs0", "bdb", "bdc", "lr", "li", "dskip", "glu",
                  "bglu", "ln1g", "ln1b")


def _mixer_prompt(x, lw, shared, l, alpha, nb, seq, pos_prev=None):
    lt = CHUNK
    rows = nb * lt
    nsteps = seq // lt
    gather_in = pos_prev is not None
    weights = tuple(lw[k] for k in MIXER_WEIGHTS) + tuple(shared)
    wspecs = [_layer_spec(lw[k].shape, l) for k in MIXER_WEIGHTS] + [_const_spec(w.shape) for w in shared]
    if gather_in:
        x_spec = pl.BlockSpec(memory_space=pl.ANY)
    else:
        x_spec = pl.BlockSpec((nb, lt, D_MODEL), lambda i, *_: (0, i, 0))
    scratch = [
        pltpu.VMEM((rows, D_MODEL), BF16),
        pltpu.VMEM((W_B // LANES, nb * (lt + SUBLANES), LANES), F32),
        pltpu.VMEM((rows, W_B), F32),
        pltpu.VMEM((W_B, rows), BF16),
        pltpu.VMEM((rows // S5_FOLD, 2 * G_B * N_STATE), F32),
        pltpu.VMEM((W_B, rows), F32),
        pltpu.VMEM((rows, D_MODEL), F32),
        pltpu.VMEM((nb, 2 * G_B * N_STATE), F32),
        pltpu.VMEM((CLASS_ROWS, LANES), F32),
    ]
    if gather_in:
        scratch += [pltpu.VMEM((rows, D_MODEL), F32),
                    pltpu.VMEM((rows * TOK_ROWS, LANES), F32), pltpu.VMEM((rows * TOK_ROWS, LANES), F32),
                    pltpu.SemaphoreType.DMA((2,))]
    grid_spec = pltpu.PrefetchScalarGridSpec(
        num_scalar_prefetch=1 if gather_in else 0,
        grid=(nsteps,),
        in_specs=[x_spec] + wspecs,
        out_specs=[pl.BlockSpec((rows * TOK_ROWS, LANES), lambda i, *_: (i, 0)),
                   pl.BlockSpec((nb, 2 * G_B * N_STATE), lambda i, *_: (0, 0)),
                   pl.BlockSpec((1, 1, rows), lambda i, *_: (i, 0, 0)),
                   pl.BlockSpec((1, 1, rows), lambda i, *_: (i, 0, 0)),
                   pl.BlockSpec((CLASS_ROWS, LANES), lambda i, *_: (0, 0))],
        scratch_shapes=scratch)
    args = ((pos_prev, x) if gather_in else (x,)) + weights
    return pl.pallas_call(
        functools.partial(_mixer_kernel, alpha, nb, gather_in),
        grid_spec=grid_spec,
        out_shape=[jax.ShapeDtypeStruct((nb * seq * TOK_ROWS, LANES), F32),
                   jax.ShapeDtypeStruct((nb, 2 * G_B * N_STATE), F32),
                   jax.ShapeDtypeStruct((nsteps, 1, rows), I32),
                   jax.ShapeDtypeStruct((nsteps, 1, rows), I32),
                   jax.ShapeDtypeStruct((CLASS_ROWS, LANES), F32)],
        compiler_params=pltpu.CompilerParams(dimension_semantics=("arbitrary",),
                                             vmem_limit_bytes=VMEM_LIMIT),
    )(*args)


def _dispatch_kernel(rows, pos8_ref, zstart_ref, zlen_ref, tail_ref, x_hbm, xs_hbm,
                     buf0, buf1, buf2, zero_ref, insem, outsem, zsem):
    step = pl.program_id(0)
    nsteps = pl.num_programs(0)
    ztok = ZERO_TOKENS
    blk = rows * TOK_ROWS
    bufs = (buf0, buf1, buf2)

    def block_in(i, b):
        return pltpu.make_async_copy(x_hbm.at[pl.ds(pl.multiple_of(i * blk, blk), blk), :], bufs[b], insem.at[b])

    def scatter_done(b):
        return pltpu.make_async_copy(bufs[b], xs_hbm.at[pl.ds(0, blk), :], outsem.at[b])

    @pl.when(step == 0)
    def _():
        block_in(0, 0).start()

        @pl.when(nsteps > 1)
        def _():
            block_in(1, 1).start()

        zero_ref[...] = jnp.zeros_like(zero_ref)
        pieces = []
        for c in range(N_CLASSES):
            start = zstart_ref[c]
            zlen = zlen_ref[c]
            p = TILE_M // 2
            while p >= 1:
                hit = (zlen & p) != 0
                pieces.append((hit, pltpu.make_async_copy(
                    zero_ref.at[pl.ds(0, p * TOK_ROWS), :],
                    xs_hbm.at[pl.ds(pl.multiple_of(start * TOK_ROWS, TOK_ROWS), p * TOK_ROWS), :], zsem)))
                start = start + jnp.where(hit, p, 0)
                p //= 2
        for hit, cp in pieces:
            pl.when(hit)(cp.start)
        for hit, cp in pieces:
            pl.when(hit)(cp.wait)

        zrows = ztok * TOK_ROWS
        first = tail_ref[0] // ztok

        def tail_copy(q):
            return pltpu.make_async_copy(
                zero_ref, xs_hbm.at[pl.ds(pl.multiple_of(q * zrows, zrows), zrows), :], zsem)

        def tail_start(q, c):
            tail_copy(q).start()
            return c

        def tail_wait(q, c):
            tail_copy(q).wait()
            return c
        lax.fori_loop(first, xs_hbm.shape[0] // zrows, tail_start, 0)
        lax.fori_loop(first, xs_hbm.shape[0] // zrows, tail_wait, 0)

    base = step * rows
    for b in range(3):
        @pl.when(lax.rem(step, 3) == b)
        def _(b=b):
            block_in(step, b).wait()

            def body(g, c):
                idx0 = base + g * DMA_UNROLL
                row0 = pl.multiple_of(g * (DMA_UNROLL * TOK_ROWS), DMA_UNROLL * TOK_ROWS)
                for i in range(DMA_UNROLL):
                    p8 = pl.multiple_of(pos8_ref[idx0 + i], TOK_ROWS)
                    pltpu.make_async_copy(bufs[b].at[pl.ds(row0 + i * TOK_ROWS, TOK_ROWS), :],
                                          xs_hbm.at[pl.ds(p8, TOK_ROWS), :], outsem.at[b]).start(priority=i % 2)
                return c
            lax.fori_loop(0, rows // DMA_UNROLL, body, 0)

            prev = (b + 2) % 3

            @pl.when(step >= 1)
            def _():
                scatter_done(prev).wait()

            @pl.when(step + 2 < nsteps)
            def _():
                block_in(step + 2, prev).start()

            @pl.when(step == nsteps - 1)
            def _():
                scatter_done(b).wait()


def _dispatch(x1t, pos8, zstart, zlen, tail, rows, ns_tokens):
    blk = (rows * TOK_ROWS, LANES)
    return pl.pallas_call(
        functools.partial(_dispatch_kernel, rows),
        grid_spec=pltpu.PrefetchScalarGridSpec(
            num_scalar_prefetch=4,
            grid=(x1t.shape[0] // (rows * TOK_ROWS),),
            in_specs=[pl.BlockSpec(memory_space=pl.ANY)],
            out_specs=pl.BlockSpec(memory_space=pl.ANY),
            scratch_shapes=[pltpu.VMEM(blk, F32), pltpu.VMEM(blk, F32), pltpu.VMEM(blk, F32),
                            pltpu.VMEM((ZERO_TOKENS * TOK_ROWS, LANES), F32),
                            pltpu.SemaphoreType.DMA((3,)), pltpu.SemaphoreType.DMA((3,)),
                            pltpu.SemaphoreType.DMA(())]),
        out_shape=jax.ShapeDtypeStruct((ns_tokens * TOK_ROWS, LANES), F32),
        compiler_params=pltpu.CompilerParams(dimension_semantics=("arbitrary",),
                                             vmem_limit_bytes=VMEM_LIMIT),
    )(pos8, zstart, zlen, tail, x1t)


def _pair_kernel(alpha, l, tidx_ref, tcls_ref, nused_ref,
                 ea_ref, cha_ref, bufa_ref, nxa_ref, hna_ref, eb_ref, chb_ref, bufb_ref, nxb_ref, hnb_ref,
                 x_ref, wrp_ref, wg_hbm, wu_hbm, wd_hbm,
                 ln2g_ref, ln2b_ref, z_ref, w1_ref, w2_ref, xprev_ref, moe_ref, sg_ref, su_ref, sd_ref, wsem):
    j = pl.program_id(0)
    f = D_FF_EXPERT
    nused = nused_ref[0]
    slots = ((ea_ref, cha_ref, bufa_ref, nxa_ref, hna_ref), (eb_ref, chb_ref, bufb_ref, nxb_ref, hnb_ref))

    def weight_copies(k, e, buf):
        return [pltpu.make_async_copy(wg_hbm.at[l, e], sg_ref.at[k, buf], wsem.at[k, buf]),
                pltpu.make_async_copy(wu_hbm.at[l, e], su_ref.at[k, buf], wsem.at[k, buf]),
                pltpu.make_async_copy(wd_hbm.at[l, e], sd_ref.at[k, buf], wsem.at[k, buf])]

    @pl.when(j == 0)
    def _():
        xprev_ref[...] = jnp.zeros_like(xprev_ref)
        moe_ref[...] = jnp.zeros_like(moe_ref)
        w1_ref[:, 4 * f:] = wrp_ref[...].astype(BF16)
        for k, (e_ref, _, buf_ref, _, _) in enumerate(slots):
            for cp in weight_copies(k, e_ref[0], buf_ref[0]):
                cp.start()

    @pl.when(j > nused)
    def _():
        z_ref[...] = jnp.zeros_like(z_ref)

    @pl.when(j <= nused)
    def _():
        class_start = jnp.logical_or(j == 0, tcls_ref[j] != tcls_ref[jnp.maximum(j - 1, 0)])

        for k, (e_ref, ch_ref, buf_ref, nx_ref, hn_ref) in enumerate(slots):
            @pl.when(jnp.logical_and(class_start, ch_ref[j] != 0))
            def _(k=k, e_ref=e_ref, buf_ref=buf_ref, nx_ref=nx_ref, hn_ref=hn_ref):
                buf = buf_ref[j]
                for cp in weight_copies(k, e_ref[j], buf):
                    cp.wait()

                @pl.when(hn_ref[j] != 0)
                def _():
                    for cp in weight_copies(k, nx_ref[j], 1 - buf):
                        cp.start()
                w1_ref[:, 2 * k * f:(2 * k + 1) * f] = sg_ref[k, buf].astype(BF16)
                w1_ref[:, (2 * k + 1) * f:(2 * k + 2) * f] = su_ref[k, buf].astype(BF16)
                w2_ref[k * f:(k + 1) * f, :] = sd_ref[k, buf].astype(BF16)

        _to_token_tiles(z_ref, 0, _layer_norm(alpha * xprev_ref[...] + moe_ref[...], ln2g_ref[...], ln2b_ref[...]))

        x = _from_token_tiles(x_ref, 0, TILE_M)
        gu = _dot(x.astype(BF16), w1_ref[...])
        scores = jax.nn.sigmoid(gu[:, 4 * f:])
        lane = lax.broadcasted_iota(I32, scores.shape, 1)
        sa = jnp.sum(jnp.where(lane == ea_ref[j], scores, 0.0), axis=-1, keepdims=True)
        sb = jnp.sum(jnp.where(lane == eb_ref[j], scores, 0.0), axis=-1, keepdims=True)
        tot = sa + sb
        ha = jax.nn.silu(gu[:, 0 * f:1 * f]) * gu[:, 1 * f:2 * f] * (sa / tot)
        hb = jax.nn.silu(gu[:, 2 * f:3 * f]) * gu[:, 3 * f:4 * f] * (sb / tot)
        xprev_ref[...] = x
        moe_ref[...] = _dot(jnp.concatenate([ha, hb], axis=1).astype(BF16), w2_ref[...])


def _pair_experts(xs, l, tables, wrp, w_gate, w_up, w_down, ln2g, ln2b, alpha):
    nsteps = tables[0].shape[0]
    cst = lambda j, *_: (0, 0)
    lsel = lambda j, *_: (l, 0, 0)
    hbm = pl.BlockSpec(memory_space=pl.ANY)
    return pl.pallas_call(
        functools.partial(_pair_kernel, alpha, l),
        grid_spec=pltpu.PrefetchScalarGridSpec(
            num_scalar_prefetch=len(tables),
            grid=(nsteps,),
            in_specs=[pl.BlockSpec((TILE_M * TOK_ROWS, LANES), lambda j, ti, *_: (ti[j], 0)),
                      pl.BlockSpec(wrp.shape, cst), hbm, hbm, hbm,
                      pl.BlockSpec((None,) + ln2g.shape[1:], lsel), pl.BlockSpec((None,) + ln2b.shape[1:], lsel)],
            out_specs=pl.BlockSpec((TILE_M * TOK_ROWS, LANES), lambda j, *_: (jnp.maximum(j - 1, 0), 0)),
            scratch_shapes=[pltpu.VMEM((D_MODEL, 4 * D_FF_EXPERT + LANES), BF16),
                            pltpu.VMEM((2 * D_FF_EXPERT, D_MODEL), BF16),
                            pltpu.VMEM((TILE_M, D_MODEL), F32), pltpu.VMEM((TILE_M, D_MODEL), F32),
                            pltpu.VMEM((2, 2, D_MODEL, D_FF_EXPERT), F32),
                            pltpu.VMEM((2, 2, D_MODEL, D_FF_EXPERT), F32),
                            pltpu.VMEM((2, 2, D_FF_EXPERT, D_MODEL), F32),
                            pltpu.SemaphoreType.DMA((2, 2))]),
        out_shape=jax.ShapeDtypeStruct(xs.shape, F32),
        compiler_params=pltpu.CompilerParams(dimension_semantics=("arbitrary",),
                                             vmem_limit_bytes=VMEM_LIMIT),
    )(*tables, xs, wrp, w_gate, w_up, w_down, ln2g, ln2b)


def _ungather_kernel(nb, pos8_ref, zs_hbm, out_ref, buf0, buf1, sems):
    rows = nb * CHUNK

    def consume(buf):
        out_ref[...] = _from_token_tiles(buf, 0, rows).reshape(nb, CHUNK, D_MODEL)
    _gathered_tokens(pl.program_id(0), pl.num_programs(0), pos8_ref, zs_hbm, (buf0, buf1), sems, rows, consume)


def _ungather(zs, pos8, nb, seq):
    rows = nb * CHUNK
    return pl.pallas_call(
        functools.partial(_ungather_kernel, nb),
        grid_spec=pltpu.PrefetchScalarGridSpec(
            num_scalar_prefetch=1,
            grid=(seq // CHUNK,),
            in_specs=[pl.BlockSpec(memory_space=pl.ANY)],
            out_specs=pl.BlockSpec((nb, CHUNK, D_MODEL), lambda i, *_: (0, i, 0)),
            scratch_shapes=[pltpu.VMEM((rows * TOK_ROWS, LANES), F32), pltpu.VMEM((rows * TOK_ROWS, LANES), F32),
                            pltpu.SemaphoreType.DMA((2,))]),
        out_shape=jax.ShapeDtypeStruct((nb, seq, D_MODEL), F32),
        compiler_params=pltpu.CompilerParams(dimension_semantics=("arbitrary",),
                                             vmem_limit_bytes=VMEM_LIMIT),
    )(pos8, zs)


def _plan(cls, rank, counts, nt):
    cnt = counts[:N_CLASSES, 0].astype(I32)
    ntile = (cnt + TILE_M - 1) // TILE_M
    padded = ntile * TILE_M
    off = jnp.cumsum(padded) - padded
    classes = jnp.arange(N_CLASSES, dtype=I32)
    pos8 = (rank.reshape(-1) + jnp.sum(jnp.where(cls.reshape(-1, 1) == classes, off, 0), axis=1)) * TOK_ROWS
    tile_end = jnp.cumsum(ntile)
    nused = tile_end[-1:].astype(I32)
    tidx = jnp.minimum(jnp.arange(nt + 1, dtype=I32), nused - 1)
    tsel = tile_end[None, :] <= tidx[:, None]
    tcls = jnp.sum(tsel.astype(I32), axis=1)
    used = ntile > 0
    earlier = used[None, :] & (classes[None, :] < classes[:, None])
    later = classes[None, :] > classes[:, None]
    prev_used = jnp.max(jnp.where(earlier, classes[None, :], -1), axis=1)
    per_class = []
    for table in (EA_TABLE, EB_TABLE):
        e_c = jnp.asarray(table)
        change = used & ((prev_used < 0) | (e_c[jnp.maximum(prev_used, 0)] != e_c))
        nxt = jnp.min(jnp.where(change[None, :] & later, classes[None, :], N_CLASSES), axis=1)
        per_class += [e_c, change.astype(I32), (jnp.cumsum(change.astype(I32)) - 1) & 1,
                      e_c[jnp.minimum(nxt, N_CLASSES - 1)], (nxt < N_CLASSES).astype(I32)]
    onehot = tcls[:, None] == classes
    per_tile = jnp.sum(jnp.where(onehot[None], jnp.stack(per_class)[:, None, :], 0), axis=2)
    tables = (tidx, tcls, nused) + tuple(per_tile)
    return pos8.astype(I32), off + cnt, padded - cnt, nused * TILE_M, tables


def _mixer_sample_kernel(alpha, x_ref, h0_ref, win_ref, wout_ref, lnvg_ref, lnvb_ref, ws0_ref, bs0_ref,
                         bdb_ref, bdc_ref, lr_ref, li_ref, dskip_ref, glu_ref, bglu_ref,
                         ln1g_ref, ln1b_ref,
                         x1_ref, hnew_ref, v_ref):
    x = x_ref[...]
    proj = _dot(x.astype(BF16), win_ref[...])
    u = jax.nn.gelu(proj[:, :W_A])
    v = _layer_norm(jax.nn.gelu(proj[:, W_A:2 * W_A]), lnvg_ref[...], lnvb_ref[...])
    v_ref[...] = v
    y_a = u * (ws0_ref[...] * v + bs0_ref[...])
    xs = proj[:, 2 * W_A:]
    zs = []
    for k in range(2):
        sl = slice(k * HALF_W, (k + 1) * HALF_W)
        bu = _dot(xs[:, sl].astype(BF16), bdb_ref[k])
        h0r = h0_ref[k, :, :HALF_STATE]
        h0i = h0_ref[k, :, HALF_STATE:]
        lr = lr_ref[k]
        li = li_ref[k]
        hr = lr * h0r - li * h0i + bu[:, :HALF_STATE]
        hi = lr * h0i + li * h0r + bu[:, HALF_STATE:]
        hnew_ref[k, :, :HALF_STATE] = hr
        hnew_ref[k, :, HALF_STATE:] = hi
        hcat = jnp.concatenate([hr, hi], axis=1).astype(BF16)
        y = jax.nn.gelu(_dot(hcat, bdc_ref[k]) + dskip_ref[:, sl] * xs[:, sl])
        gl = _dot(y.astype(BF16), glu_ref[k]) + bglu_ref[:, sl]
        zs.append(y * jax.nn.sigmoid(gl))
    cat = jnp.concatenate([y_a] + zs, axis=1).astype(BF16)
    mix = _dot(cat, wout_ref[...])
    x1_ref[...] = _layer_norm(alpha * x + mix, ln1g_ref[...], ln1b_ref[...])


def _mixer_sample(x, h0, lw, l, alpha):
    n = x.shape[0]
    full = lambda shape: pl.BlockSpec(shape, lambda i: (0,) * len(shape))
    return pl.pallas_call(
        functools.partial(_mixer_sample_kernel, alpha),
        grid=(1,),
        in_specs=[full(x.shape), _layer_spec(h0.shape, l)] + [_layer_spec(lw[k].shape, l) for k in SAMPLE_WEIGHTS],
        out_specs=[full((n, D_MODEL)), full((2, n, 2 * HALF_STATE)), full((n, W_A))],
        out_shape=[jax.ShapeDtypeStruct((n, D_MODEL), F32),
                   jax.ShapeDtypeStruct((2, n, 2 * HALF_STATE), F32),
                   jax.ShapeDtypeStruct((n, W_A), F32)],
        compiler_params=pltpu.CompilerParams(dimension_semantics=("arbitrary",), vmem_limit_bytes=VMEM_LIMIT),
    )(x, h0, *[lw[k] for k in SAMPLE_WEIGHTS])


def _route(x, wr, rbias):
    logits = jnp.dot(x, wr, preferred_element_type=F32, precision=HIGHEST)
    scores = jax.nn.sigmoid(logits)
    biased = scores + rbias
    lane = lax.broadcasted_iota(I32, biased.shape, 1)
    grp = lane // EXPERTS_PER_GROUP
    neg = jnp.float32(-jnp.inf)

    def top2(vals):
        m1 = jnp.max(vals, axis=-1, keepdims=True)
        i1 = jnp.min(jnp.where(vals == m1, lane, N_EXPERTS), axis=-1, keepdims=True)
        rest = jnp.where(lane == i1, neg, vals)
        m2 = jnp.max(rest, axis=-1, keepdims=True)
        i2 = jnp.min(jnp.where(rest == m2, lane, N_EXPERTS), axis=-1, keepdims=True)
        return m1, i1, m2, i2

    best = sel = None
    for g in range(N_EXPERT_GROUPS):
        m1, _, m2, _ = top2(jnp.where(grp == g, biased, neg))
        gs = m1 + m2
        if g == 0:
            best, sel = gs, jnp.zeros(gs.shape, I32)
        else:
            upd = gs > best
            sel = jnp.where(upd, g, sel)
            best = jnp.where(upd, gs, best)
    _, i1, _, i2 = top2(jnp.where(grp == sel, biased, neg))
    s1 = jnp.sum(jnp.where(lane == i1, scores, 0.0), axis=-1, keepdims=True)
    s2 = jnp.sum(jnp.where(lane == i2, scores, 0.0), axis=-1, keepdims=True)
    tot = s1 + s2
    return jnp.where(lane == i1, s1 / tot, 0.0) + jnp.where(lane == i2, s2 / tot, 0.0)


def _moe_kernel(alpha, x_ref, wr_ref, rb_ref, wg_ref, wu_ref, wd_ref, ln2g_ref, ln2b_ref,
                out_ref, xb_ref, comb_ref, acc_ref):
    step = pl.program_id(1)

    @pl.when(step == 0)
    def _():
        x = x_ref[...]
        xb_ref[...] = x.astype(BF16)
        comb_ref[...] = _route(x, wr_ref[...], rb_ref[...])
        acc_ref[...] = jnp.zeros_like(acc_ref)

    xb = xb_ref[...]
    comb = comb_ref[...]
    lane = lax.broadcasted_iota(I32, comb.shape, 1)
    acc = acc_ref[...]
    for k in range(DENSE_EXPERTS_PER_STEP):
        g = _dot(xb, wg_ref[k].astype(BF16))
        u = _dot(xb, wu_ref[k].astype(BF16))
        ce = jnp.sum(jnp.where(lane == step * DENSE_EXPERTS_PER_STEP + k, comb, 0.0), axis=-1, keepdims=True)
        h = (jax.nn.silu(g) * u * ce).astype(BF16)
        acc = acc + _dot(h, wd_ref[k].astype(BF16))
    acc_ref[...] = acc

    @pl.when(step == pl.num_programs(1) - 1)
    def _():
        out_ref[...] = _layer_norm(alpha * x_ref[...] + acc_ref[...], ln2g_ref[...], ln2b_ref[...])


def _moe_dense(x, l, wr, rb, w_gate, w_up, w_down, ln2g, ln2b, alpha, tm):
    t = x.shape[0]
    cst = lambda i, e: (0, 0)
    wsel = lambda i, e: (l, e, 0, 0)
    return pl.pallas_call(
        functools.partial(_moe_kernel, alpha),
        grid=(t // tm, N_EXPERTS // DENSE_EXPERTS_PER_STEP),
        in_specs=[pl.BlockSpec((tm, D_MODEL), lambda i, e: (i, 0)),
                  pl.BlockSpec(wr.shape, cst), pl.BlockSpec(rb.shape, cst),
                  pl.BlockSpec((None, DENSE_EXPERTS_PER_STEP, D_MODEL, D_FF_EXPERT), wsel),
                  pl.BlockSpec((None, DENSE_EXPERTS_PER_STEP, D_MODEL, D_FF_EXPERT), wsel),
                  pl.BlockSpec((None, DENSE_EXPERTS_PER_STEP, D_FF_EXPERT, D_MODEL), wsel),
                  pl.BlockSpec((None,) + ln2g.shape[1:], lambda i, e: (l, 0, 0)),
                  pl.BlockSpec((None,) + ln2b.shape[1:], lambda i, e: (l, 0, 0))],
        out_specs=pl.BlockSpec((tm, D_MODEL), lambda i, e: (i, 0)),
        out_shape=jax.ShapeDtypeStruct((t, D_MODEL), F32),
        scratch_shapes=[pltpu.VMEM((tm, D_MODEL), BF16),
                        pltpu.VMEM((tm, N_EXPERTS), F32),
                        pltpu.VMEM((tm, D_MODEL), F32)],
        compiler_params=pltpu.CompilerParams(dimension_semantics=("arbitrary", "arbitrary"),
                                             vmem_limit_bytes=VMEM_LIMIT),
    )(x, wr, rb, w_gate, w_up, w_down, ln2g, ln2b)


def _prep_s5_folded(lb_re, lb_im, bb_re, bb_im, c_re, c_im):
    d, g, n = lb_re.shape
    s = S5_FOLD
    pairs = g // 2
    pr, pi = [jnp.ones_like(lb_re)], [jnp.zeros_like(lb_re)]
    for _ in range(s):
        pr, pi = pr + [pr[-1] * lb_re - pi[-1] * lb_im], pi + [pr[-1] * lb_im + pi[-1] * lb_re]
    p_re, p_im = jnp.stack(pr, axis=2), jnp.stack(pi, axis=2)

    def pair_cols(re, im):
        z = jnp.zeros_like(re[:, :, 0])
        g0 = jnp.concatenate([re[:, :, 0], z, im[:, :, 0], z], axis=-1)
        g1 = jnp.concatenate([z, re[:, :, 1], z, im[:, :, 1]], axis=-1)
        return jnp.concatenate([g0, g1], axis=2)

    def pair_diag(a):
        z = jnp.zeros_like(a[:, :, 0])
        return jnp.concatenate([jnp.concatenate([a[:, :, 0], z], axis=-1),
                                jnp.concatenate([z, a[:, :, 1]], axis=-1)], axis=2)

    bt_re, bt_im = bb_re.transpose(0, 1, 3, 2)[:, :, None], bb_im.transpose(0, 1, 3, 2)[:, :, None]
    k_re, k_im = p_re[:, :, s - 1::-1, None, :], p_im[:, :, s - 1::-1, None, :]
    m1t = pair_cols((k_re * bt_re - k_im * bt_im).reshape(d, pairs, 2, -1, n),
                    (k_re * bt_im + k_im * bt_re).reshape(d, pairs, 2, -1, n))
    m1 = jnp.swapaxes(m1t, -1, -2)

    cq_re, cq_im = c_re[:, :, None], c_im[:, :, None]
    j_re, j_im = p_re[:, :, 1:, None, :], p_im[:, :, 1:, None, :]
    hpart = pair_cols((cq_re * j_re - cq_im * j_im).reshape(d, pairs, 2, -1, n),
                      (-(cq_re * j_im + cq_im * j_re)).reshape(d, pairs, 2, -1, n))

    t_re, t_im = p_re[:, :, :s, None, :], p_im[:, :, :s, None, :]
    kern = (jnp.einsum("dgtqn,dgnp->dgtqp", cq_re * t_re - cq_im * t_im, bb_re, precision=HIGHEST)
            - jnp.einsum("dgtqn,dgnp->dgtqp", cq_re * t_im + cq_im * t_re, bb_im, precision=HIGHEST))
    zero = jnp.zeros_like(kern[:, :, 0])
    kx = jnp.stack([jnp.concatenate([kern[:, :, j - q] if q <= j else zero for q in range(s)], axis=-1)
                    for j in range(s)], axis=2)
    apart = pair_diag(kx.reshape(d, pairs, 2, -1, kx.shape[-1]))

    m2 = jnp.concatenate([hpart, apart], axis=-1)
    return (m1.astype(BF16), m2.astype(BF16), p_re[:, :, s].reshape(d, 1, g * n), p_im[:, :, s].reshape(d, 1, g * n))


def _prep_all(w_in, w_out, ln_v_g, ln_v_b, w_s, b_s, a_re, a_im, log_dt, b_re, b_im, c_re, c_im,
              d_skip, w_glu, b_glu, ln1_g, ln1_b):
    d = w_in.shape[0]
    dt = jnp.exp(log_dt)[..., None]
    decay = jnp.exp(a_re * dt)
    lb_re, lb_im = decay * jnp.cos(a_im * dt), decay * jnp.sin(a_im * dt)
    den = a_re * a_re + a_im * a_im
    nr, ni = lb_re - 1.0, lb_im
    zr = (nr * a_re + ni * a_im) / den
    zi = (ni * a_re - nr * a_im) / den
    bb_re = zr[..., None] * b_re - zi[..., None] * b_im
    bb_im = zr[..., None] * b_im + zi[..., None] * b_re

    def bd(a):
        r, c = a.shape[2:]
        rep = jnp.asarray(np.tile(np.eye(c, dtype=np.float32), (1, HALF_GROUPS)))
        diag = jnp.asarray(np.kron(np.eye(HALF_GROUPS), np.ones((r, c))) > 0)
        tiled = jnp.einsum("dkrc,cn->dkrn", a.reshape(d, 2, HALF_GROUPS * r, c), rep)
        return jnp.where(diag, tiled, 0.0).astype(BF16)

    bdb = jnp.concatenate([bd(bb_re.transpose(0, 1, 3, 2)), bd(bb_im.transpose(0, 1, 3, 2))], axis=3)
    bdc = jnp.concatenate([bd(c_re.transpose(0, 1, 3, 2)), bd(-c_im.transpose(0, 1, 3, 2))], axis=2)
    m1, m2, lr8, li8 = _prep_s5_folded(lb_re, lb_im, bb_re, bb_im, c_re, c_im)
    return dict(
        m1=m1, m2=m2, lr8=lr8, li8=li8,
        win=w_in.astype(BF16), wout=w_out.astype(BF16),
        lnvg=ln_v_g[:, None], lnvb=ln_v_b[:, None],
        wtril=jnp.tril(w_s).astype(BF16),
        bsb=jnp.broadcast_to(b_s[..., None], (d, H_A, CHUNK, LANES)),
        ws0=jnp.repeat(w_s[:, :, 0, 0], P_A, axis=1)[:, None], bs0=jnp.repeat(b_s[:, :, 0], P_A, axis=1)[:, None],
        bdb=bdb, bdc=bdc,
        lr=lb_re.reshape(d, 2, 1, HALF_STATE), li=lb_im.reshape(d, 2, 1, HALF_STATE),
        dskip=d_skip.reshape(d, 1, W_B), glu=bd(w_glu),
        bglu=b_glu.reshape(d, 1, W_B), ln1g=ln1_g[:, None], ln1b=ln1_b[:, None])


def _state_to_cols(h_re, h_im):
    d, b = h_re.shape[:2]
    re = h_re.reshape(d, b, 2, HALF_STATE)
    im = h_im.reshape(d, b, 2, HALF_STATE)
    return jnp.concatenate([re, im], axis=3).transpose(0, 2, 1, 3)


def _cols_to_state(h):
    d, _, b, _ = h.shape
    re = h[..., :HALF_STATE].transpose(0, 2, 1, 3).reshape(d, b, G_B, N_STATE)
    im = h[..., HALF_STATE:].transpose(0, 2, 1, 3).reshape(d, b, G_B, N_STATE)
    return re, im


def _pairs_to_state(h):
    d, b, _ = h.shape
    h = h.reshape(d, b, G_B // 2, 2, 2, N_STATE)
    return h[:, :, :, 0].reshape(d, b, G_B, N_STATE), h[:, :, :, 1].reshape(d, b, G_B, N_STATE)


def kernel(x_prompt, x_sample, state_ssm_re, state_ssm_im, w_in, w_out, ln_v_g, ln_v_b, w_s, b_s, ssm_a_re, ssm_a_im, ssm_log_dt, ssm_b_re, ssm_b_im, ssm_c_re, ssm_c_im, ssm_d, w_glu, b_glu, ln1_g, ln1_b, ln2_g, ln2_b, w_router, router_bias, w_gate, w_up, w_down):
    depth = w_in.shape[0]
    alpha = float((2 * depth) ** 0.25)
    nb, seq, _ = x_prompt.shape
    ns = x_sample.shape[0]
    tokens = nb * seq
    nt = tokens // TILE_M + N_CLASSES
    rb = router_bias[None]
    wrt = w_router.T
    wrp = jnp.pad(w_router, ((0, 0), (0, LANES - N_EXPERTS)))
    rbcol = router_bias[:, None]
    tok = np.arange(nb * CHUNK)
    tri = jnp.asarray(tok[:, None] < tok[None, :], BF16)
    lw = _prep_all(w_in, w_out, ln_v_g, ln_v_b, w_s, b_s, ssm_a_re, ssm_a_im, ssm_log_dt,
                   ssm_b_re, ssm_b_im, ssm_c_re, ssm_c_im, ssm_d, w_glu, b_glu, ln1_g, ln1_b)
    shared = (wrt, rbcol, tri)
    ln2g, ln2b = ln2_g[:, None], ln2_b[:, None]
    h0s = _state_to_cols(state_ssm_re, state_ssm_im)
    xp = x_prompt
    pos = None
    xs = x_sample.reshape(ns, D_MODEL)
    pr_h, sm_h, sm_v = [], [], []
    for l in range(depth):
        x1t, hfin, cls, rank, counts = _mixer_prompt(xp, lw, shared, l, alpha, nb, seq, pos)
        pos, zstart, zlen, tail, tables = _plan(cls, rank, counts, nt)
        x_sorted = _dispatch(x1t, pos, zstart, zlen, tail, nb * CHUNK, nt * TILE_M)
        xp = _pair_experts(x_sorted, l, tables, wrp, w_gate, w_up, w_down, ln2g, ln2b, alpha)
        pr_h.append(hfin)

        x1s, hnew, v_new = _mixer_sample(xs, h0s, lw, l, alpha)
        xs = _moe_dense(x1s, l, w_router, rb, w_gate, w_up, w_down, ln2g, ln2b, alpha, tm=ns)
        sm_h.append(hnew)
        sm_v.append(v_new.reshape(ns, 1, W_A))
    y_prompt = _ungather(xp, pos, nb, seq)
    pr_re, pr_im = _pairs_to_state(jnp.stack(pr_h))
    sm_re, sm_im = _cols_to_state(jnp.stack(sm_h))
    return (y_prompt, xs.reshape(ns, 1, D_MODEL), pr_re, pr_im, sm_re, sm_im, jnp.stack(sm_v))
```

```python
import functools

import jax
import jax.numpy as jnp
import numpy as np
from jax import lax
from jax.experimental import pallas as pl
from jax.experimental.pallas import tpu as pltpu

D_MODEL = 1024
W_A = 512
W_B = 512
CHUNK = 128
H_A = 4
P_A = W_A // H_A
GROUP_B = 16
G_B = W_B // GROUP_B
N_STATE = 64
N_EXPERTS = 16
N_EXPERT_GROUPS = 4
EXPERTS_PER_GROUP = N_EXPERTS // N_EXPERT_GROUPS
D_FF_EXPERT = D_MODEL // 4
LN_EPS = 1e-5

LANES = 128
SUBLANES = 8
HALF_GROUPS = 16
HALF_W = HALF_GROUPS * GROUP_B
HALF_STATE = HALF_GROUPS * N_STATE
VMEM_LIMIT = 56 * 1024 * 1024

PAIRS = ((0, 1), (0, 2), (0, 3), (1, 3), (1, 2), (3, 2))
N_CLASSES = N_EXPERT_GROUPS * len(PAIRS)
CLASS_ROWS = 32
TILE_M = 256
TOK_ROWS = D_MODEL // LANES
ZERO_TOKENS = TILE_M // 2
DMA_UNROLL = 16
DENSE_EXPERTS_PER_STEP = 4
S5_FOLD = 8
S5_PAIRS = G_B // 2
EA_TABLE = np.array([EXPERTS_PER_GROUP * g + a for g in range(N_EXPERT_GROUPS) for a, _ in PAIRS], np.int32)
EB_TABLE = np.array([EXPERTS_PER_GROUP * g + b for g in range(N_EXPERT_GROUPS) for _, b in PAIRS], np.int32)

F32 = jnp.float32
BF16 = jnp.bfloat16
I32 = jnp.int32
HIGHEST = lax.Precision.HIGHEST


def _layer_norm(x, g, b):
    mu = jnp.mean(x, axis=-1, keepdims=True)
    xc = x - mu
    var = jnp.mean(xc * xc, axis=-1, keepdims=True)
    return xc * lax.rsqrt(var + LN_EPS) * g + b


def _dot(a, b):
    return jnp.dot(a, b, preferred_element_type=F32)


def _route_classes(x1, wrt, rbcol):
    def split(a):
        hi = a.astype(BF16)
        return hi, (a - hi.astype(F32)).astype(BF16)

    def dot_t(a, b):
        return lax.dot_general(a, b, (((1,), (1,)), ((), ())), preferred_element_type=F32)
    w_hi, w_lo = split(wrt)
    x_hi, x_lo = split(x1)
    logits_t = dot_t(w_hi, x_hi) + (dot_t(w_hi, x_lo) + dot_t(w_lo, x_hi))
    biased = jax.nn.sigmoid(logits_t) + rbcol
    rows = [biased[e:e + 1, :] for e in range(N_EXPERTS)]
    n = EXPERTS_PER_GROUP

    best = sel = None
    for g in range(N_EXPERT_GROUPS):
        v = rows[n * g:n * (g + 1)]
        gs = None
        for a, b in PAIRS:
            s = v[a] + v[b]
            gs = s if gs is None else jnp.maximum(gs, s)
        if g == 0:
            best, sel = gs, jnp.zeros(gs.shape, I32)
        else:
            upd = gs > best
            sel = jnp.where(upd, g, sel)
            best = jnp.where(upd, gs, best)

    cls = jnp.zeros(sel.shape, I32)
    for g in range(N_EXPERT_GROUPS):
        v = rows[n * g:n * (g + 1)]
        lo = jnp.full(sel.shape, n, I32)
        hi = jnp.full(sel.shape, -1, I32)
        for i in range(n):
            before = jnp.zeros(sel.shape, I32)
            for j in range(n):
                if j < i:
                    before = before + (v[j] >= v[i]).astype(I32)
                elif j > i:
                    before = before + (v[j] > v[i]).astype(I32)
            member = before < 2
            lo = jnp.where(member, jnp.minimum(lo, i), lo)
            hi = jnp.where(member, jnp.maximum(hi, i), hi)
        pidx = jnp.zeros(sel.shape, I32)
        for k, (a, b) in enumerate(PAIRS):
            pidx = jnp.where((lo == min(a, b)) & (hi == max(a, b)), k, pidx)
        cls = jnp.where(sel == g, g * len(PAIRS) + pidx, cls)
    return cls


def _to_token_tiles(ref, row0, x):
    n = x.shape[0]
    for c in range(TOK_ROWS):
        ref[pl.ds(row0 * TOK_ROWS + c, n, stride=TOK_ROWS), :] = x[:, c * LANES:(c + 1) * LANES]


def _from_token_tiles(ref, row0, n):
    return jnp.concatenate(
        [ref[pl.ds(row0 * TOK_ROWS + c, n, stride=TOK_ROWS), :] for c in range(TOK_ROWS)], axis=1)


def _gathered_tokens(step, nsteps, pos8_ref, src_hbm, bufs, sems, rows, consume):
    def start(s, base):
        def body(g, c):
            idx0 = base + g * DMA_UNROLL
            row0 = pl.multiple_of(g * (DMA_UNROLL * TOK_ROWS), DMA_UNROLL * TOK_ROWS)
            for i in range(DMA_UNROLL):
                p8 = pl.multiple_of(pos8_ref[idx0 + i], TOK_ROWS)
                pltpu.make_async_copy(src_hbm.at[pl.ds(p8, TOK_ROWS), :],
                                      bufs[s].at[pl.ds(row0 + i * TOK_ROWS, TOK_ROWS), :],
                                      sems.at[s]).start(priority=i % 2)
            return c
        lax.fori_loop(0, rows // DMA_UNROLL, body, 0)

    @pl.when(step == 0)
    def _():
        start(0, 0)

    for s in range(2):
        @pl.when(lax.rem(step, 2) == s)
        def _(s=s):
            pltpu.make_async_copy(src_hbm.at[pl.ds(0, rows * TOK_ROWS), :], bufs[s], sems.at[s]).wait()

            @pl.when(step + 1 < nsteps)
            def _():
                start(1 - s, (step + 1) * rows)
            consume(bufs[s])


def _mixer_kernel(alpha, nb, gather_in, *refs):
    if gather_in:
        pos8_ref, zs_hbm = refs[:2]
        refs = refs[2:]
    else:
        x_ref = refs[0]
        refs = refs[1:]
    (win_ref, wout_ref, lnvg_ref, lnvb_ref, wtril_ref, bsb_ref, m1_ref, m2_ref, lr8_ref, li8_ref,
     dskip_ref, glu_ref, bglu_ref, ln1g_ref, ln1b_ref, wrt_ref, rbcol_ref, tri_ref,
     x1t_ref, hfin_ref, cls_ref, rank_ref, cnt_ref,
     xb_ref, xs_slab, xs_scb, xst_ref, ht_ref, yt_ref, mix_ref, hstate_ref, carry_ref) = refs[:32]
    lt = CHUNK
    rows = nb * lt
    pitch = lt + SUBLANES
    step = pl.program_id(0)
    nsteps = pl.num_programs(0)

    @pl.when(step == 0)
    def _():
        hstate_ref[...] = jnp.zeros_like(hstate_ref)
        carry_ref[...] = jnp.zeros_like(carry_ref)

    if gather_in:
        xin_ref, xbuf0, xbuf1, gsem = refs[32:36]

        def consume(buf):
            xg = _from_token_tiles(buf, 0, rows)
            xin_ref[...] = xg
            xb_ref[...] = xg.astype(BF16)
        _gathered_tokens(step, nsteps, pos8_ref, zs_hbm, (xbuf0, xbuf1), gsem, rows, consume)

        def load_x():
            return xin_ref[...]
    else:
        def load_x():
            return x_ref[...].reshape(rows, D_MODEL)

    if not gather_in:
        xb_ref[...] = load_x().astype(BF16)

    fold = S5_FOLD
    nchunk = lt // fold
    cb = nchunk * nb
    xs = _dot(xb_ref[...], win_ref[:, 2 * W_A:])
    for j in range(W_B // LANES):
        for b in range(nb):
            xs_slab[j, b * pitch:b * pitch + lt, :] = xs[b * lt:(b + 1) * lt, j * LANES:(j + 1) * LANES]

    for t in range(lt):
        r0 = (t % fold) * cb + (t // fold) * nb
        for j in range(W_B // LANES):
            xs_scb[r0:r0 + nb, j * LANES:(j + 1) * LANES] = xs_slab[j, pl.ds(t, nb, stride=pitch), :]
    xst_ref[...] = xs_scb[...].T.astype(BF16)

    vg = jax.nn.gelu(_dot(xb_ref[...], win_ref[:, W_A:2 * W_A]))
    v = _layer_norm(vg, lnvg_ref[...], lnvb_ref[...]).astype(BF16)
    u = jax.nn.gelu(_dot(xb_ref[...], win_ref[:, :W_A]))
    for h in range(H_A):
        hs = slice(h * P_A, (h + 1) * P_A)
        vcat = jnp.concatenate([v[b * lt:(b + 1) * lt, hs] for b in range(nb)], axis=1)
        o = _dot(wtril_ref[h], vcat)
        for b in range(nb):
            rs = slice(b * lt, (b + 1) * lt)
            mix_ref[rs, hs] = u[rs, hs] * (o[:, b * LANES:(b + 1) * LANES] + bsb_ref[h])

    def chunk_inputs(pr):
        return jnp.concatenate(
            [xst_ref[(2 * pr + gi) * GROUP_B:(2 * pr + gi + 1) * GROUP_B, s * cb:(s + 1) * cb]
             for gi in range(2) for s in range(fold)], axis=0)

    pw = 2 * 2 * N_STATE
    for pr in range(S5_PAIRS):
        ht_ref[:, pr * pw:(pr + 1) * pw] = _dot(m1_ref[pr], chunk_inputs(pr)).T

    half_pairs = S5_PAIRS // 2
    for half in range(2):
        c0 = half * half_pairs * pw
        lr8 = [jnp.broadcast_to(lr8_ref[:, (half * half_pairs + p) * LANES:(half * half_pairs + p + 1) * LANES],
                                (nb, LANES)) for p in range(half_pairs)]
        li8 = [jnp.broadcast_to(li8_ref[:, (half * half_pairs + p) * LANES:(half * half_pairs + p + 1) * LANES],
                                (nb, LANES)) for p in range(half_pairs)]
        h = [hstate_ref[:, c0 + q * LANES:c0 + (q + 1) * LANES] for q in range(2 * half_pairs)]
        for c in range(nchunk):
            rs = slice(c * nb, (c + 1) * nb)
            for p in range(half_pairs):
                re_sl = slice(c0 + p * pw, c0 + p * pw + LANES)
                im_sl = slice(c0 + p * pw + LANES, c0 + (p + 1) * pw)
                hr, hi = h[2 * p], h[2 * p + 1]
                ur, ui = ht_ref[rs, re_sl], ht_ref[rs, im_sl]
                ht_ref[rs, re_sl] = hr
                ht_ref[rs, im_sl] = hi
                h[2 * p] = lr8[p] * hr - li8[p] * hi + ur
                h[2 * p + 1] = lr8[p] * hi + li8[p] * hr + ui
        for q in range(2 * half_pairs):
            hstate_ref[:, c0 + q * LANES:c0 + (q + 1) * LANES] = h[q]
    hfin_ref[...] = hstate_ref[...]

    for pr in range(S5_PAIRS):
        rhs = jnp.concatenate([ht_ref[:, pr * pw:(pr + 1) * pw].T.astype(BF16), chunk_inputs(pr)], axis=0)
        yt = _dot(m2_ref[pr], rhs)
        for gi in range(2):
            for s in range(fold):
                r0 = (gi * fold + s) * GROUP_B
                yt_ref[(2 * pr + gi) * GROUP_B:(2 * pr + gi + 1) * GROUP_B, s * cb:(s + 1) * cb] = (
                    yt[r0:r0 + GROUP_B, :])

    y = jax.nn.gelu(yt_ref[...].T + dskip_ref[...] * xs_scb[...])
    yb = y.astype(BF16)
    spitch = cb + SUBLANES
    for k in range(2):
        sl = slice(k * HALF_W, (k + 1) * HALF_W)
        gl = _dot(yb[:, sl], glu_ref[k]) + bglu_ref[:, sl]
        z = y[:, sl] * jax.nn.sigmoid(gl)
        for j in range(HALF_W // LANES):
            for s in range(fold):
                xs_slab[k * (HALF_W // LANES) + j, s * spitch:s * spitch + cb, :] = (
                    z[s * cb:(s + 1) * cb, j * LANES:(j + 1) * LANES])

    for ch in range(nchunk):
        for b in range(nb):
            dst = b * lt + ch * fold
            for j in range(W_B // LANES):
                mix_ref[dst:dst + fold, W_A + j * LANES:W_A + (j + 1) * LANES] = (
                    xs_slab[j, pl.ds(ch * nb + b, fold, stride=spitch), :])

    mix = _dot(mix_ref[...].astype(BF16), wout_ref[...])
    x1 = _layer_norm(alpha * load_x() + mix, ln1g_ref[...], ln1b_ref[...])
    _to_token_tiles(x1t_ref, 0, x1)

    cls = _route_classes(x1, wrt_ref[...], rbcol_ref[...])
    crow = lax.broadcasted_iota(I32, (CLASS_ROWS, rows), 0)
    onehot = jnp.where(crow == cls, 1.0, 0.0)
    prefix = _dot(onehot.astype(BF16), tri_ref[...])
    carry = carry_ref[:, 0:1]
    rank = jnp.sum(onehot * (prefix + carry), axis=0, keepdims=True)
    cls_ref[...] = cls.reshape(1, 1, rows)
    rank_ref[...] = rank.astype(I32).reshape(1, 1, rows)
    carry_ref[...] = carry_ref[...] + jnp.sum(onehot, axis=1, keepdims=True)
    cnt_ref[...] = carry_ref[...]


def _const_spec(shape):
    nd = len(shape)
    return pl.BlockSpec(shape, lambda *_: (0,) * nd, pipeline_mode=pl.Buffered(1))


def _layer_spec(shape, l):
    nd = len(shape)
    return pl.BlockSpec((None,) + tuple(shape[1:]), lambda *_: (l,) + (0,) * (nd - 1),
                        pipeline_mode=pl.Buffered(1))


MIXER_WEIGHTS = ("win", "wout", "lnvg", "lnvb", "wtril", "bsb", "m1", "m2", "lr8", "li8", "dskip", "glu",
                 "bglu", "ln1g", "ln1b")
SAMPLE_WEIGHTS = ("win", "wout", "lnvg", "lnvb", "ws0", "bs0", "bdb", "bdc", "lr", "li", "dskip", "glu",
                  "bglu", "ln1g", "ln1b")


def _mixer_prompt(x, lw, shared, l, alpha, nb, seq, pos_prev=None):
    lt = CHUNK
    rows = nb * lt
    nsteps = seq // lt
    gather_in = pos_prev is not None
    weights = tuple(lw[k] for k in MIXER_WEIGHTS) + tuple(shared)
    wspecs = [_layer_spec(lw[k].shape, l) for k in MIXER_WEIGHTS] + [_const_spec(w.shape) for w in shared]
    if gather_in:
        x_spec = pl.BlockSpec(memory_space=pl.ANY)
    else:
        x_spec = pl.BlockSpec((nb, lt, D_MODEL), lambda i, *_: (0, i, 0))
    scratch = [
        pltpu.VMEM((rows, D_MODEL), BF16),
        pltpu.VMEM((W_B // LANES, nb * (lt + SUBLANES), LANES), F32),
        pltpu.VMEM((rows, W_B), F32),
        pltpu.VMEM((W_B, rows), BF16),
        pltpu.VMEM((rows // S5_FOLD, 2 * G_B * N_STATE), F32),
        pltpu.VMEM((W_B, rows), F32),
        pltpu.VMEM((rows, D_MODEL), F32),
        pltpu.VMEM((nb, 2 * G_B * N_STATE), F32),
        pltpu.VMEM((CLASS_ROWS, LANES), F32),
    ]
    if gather_in:
        scratch += [pltpu.VMEM((rows, D_MODEL), F32),
                    pltpu.VMEM((rows * TOK_ROWS, LANES), F32), pltpu.VMEM((rows * TOK_ROWS, LANES), F32),
                    pltpu.SemaphoreType.DMA((2,))]
    grid_spec = pltpu.PrefetchScalarGridSpec(
        num_scalar_prefetch=1 if gather_in else 0,
        grid=(nsteps,),
        in_specs=[x_spec] + wspecs,
        out_specs=[pl.BlockSpec((rows * TOK_ROWS, LANES), lambda i, *_: (i, 0)),
                   pl.BlockSpec((nb, 2 * G_B * N_STATE), lambda i, *_: (0, 0)),
                   pl.BlockSpec((1, 1, rows), lambda i, *_: (i, 0, 0)),
                   pl.BlockSpec((1, 1, rows), lambda i, *_: (i, 0, 0)),
                   pl.BlockSpec((CLASS_ROWS, LANES), lambda i, *_: (0, 0))],
        scratch_shapes=scratch)
    args = ((pos_prev, x) if gather_in else (x,)) + weights
    return pl.pallas_call(
        functools.partial(_mixer_kernel, alpha, nb, gather_in),
        grid_spec=grid_spec,
        out_shape=[jax.ShapeDtypeStruct((nb * seq * TOK_ROWS, LANES), F32),
                   jax.ShapeDtypeStruct((nb, 2 * G_B * N_STATE), F32),
                   jax.ShapeDtypeStruct((nsteps, 1, rows), I32),
                   jax.ShapeDtypeStruct((nsteps, 1, rows), I32),
                   jax.ShapeDtypeStruct((CLASS_ROWS, LANES), F32)],
        compiler_params=pltpu.CompilerParams(dimension_semantics=("arbitrary",),
                                             vmem_limit_bytes=VMEM_LIMIT),
    )(*args)


def _dispatch_kernel(rows, pos8_ref, zstart_ref, zlen_ref, tail_ref, x_hbm, xs_hbm,
                     buf0, buf1, buf2, zero_ref, insem, outsem, zsem):
    step = pl.program_id(0)
    nsteps = pl.num_programs(0)
    ztok = ZERO_TOKENS
    blk = rows * TOK_ROWS
    bufs = (buf0, buf1, buf2)

    def block_in(i, b):
        return pltpu.make_async_copy(x_hbm.at[pl.ds(pl.multiple_of(i * blk, blk), blk), :], bufs[b], insem.at[b])

    def scatter_done(b):
        return pltpu.make_async_copy(bufs[b], xs_hbm.at[pl.ds(0, blk), :], outsem.at[b])

    @pl.when(step == 0)
    def _():
        block_in(0, 0).start()

        @pl.when(nsteps > 1)
        def _():
            block_in(1, 1).start()

        zero_ref[...] = jnp.zeros_like(zero_ref)
        pieces = []
        for c in range(N_CLASSES):
            start = zstart_ref[c]
            zlen = zlen_ref[c]
            p = TILE_M // 2
            while p >= 1:
                hit = (zlen & p) != 0
                pieces.append((hit, pltpu.make_async_copy(
                    zero_ref.at[pl.ds(0, p * TOK_ROWS), :],
                    xs_hbm.at[pl.ds(pl.multiple_of(start * TOK_ROWS, TOK_ROWS), p * TOK_ROWS), :], zsem)))
                start = start + jnp.where(hit, p, 0)
                p //= 2
        for hit, cp in pieces:
            pl.when(hit)(cp.start)
        for hit, cp in pieces:
            pl.when(hit)(cp.wait)

        zrows = ztok * TOK_ROWS
        first = tail_ref[0] // ztok

        def tail_copy(q):
            return pltpu.make_async_copy(
                zero_ref, xs_hbm.at[pl.ds(pl.multiple_of(q * zrows, zrows), zrows), :], zsem)

        def tail_start(q, c):
            tail_copy(q).start()
            return c

        def tail_wait(q, c):
            tail_copy(q).wait()
            return c
        lax.fori_loop(first, xs_hbm.shape[0] // zrows, tail_start, 0)
        lax.fori_loop(first, xs_hbm.shape[0] // zrows, tail_wait, 0)

    base = step * rows
    for b in range(3):
        @pl.when(lax.rem(step, 3) == b)
        def _(b=b):
            block_in(step, b).wait()

            def body(g, c):
                idx0 = base + g * DMA_UNROLL
                row0 = pl.multiple_of(g * (DMA_UNROLL * TOK_ROWS), DMA_UNROLL * TOK_ROWS)
                for i in range(DMA_UNROLL):
                    p8 = pl.multiple_of(pos8_ref[idx0 + i], TOK_ROWS)
                    pltpu.make_async_copy(bufs[b].at[pl.ds(row0 + i * TOK_ROWS, TOK_ROWS), :],
                                          xs_hbm.at[pl.ds(p8, TOK_ROWS), :], outsem.at[b]).start(priority=i % 2)
                return c
            lax.fori_loop(0, rows // DMA_UNROLL, body, 0)

            prev = (b + 2) % 3

            @pl.when(step >= 1)
            def _():
                scatter_done(prev).wait()

            @pl.when(step + 2 < nsteps)
            def _():
                block_in(step + 2, prev).start()

            @pl.when(step == nsteps - 1)
            def _():
                scatter_done(b).wait()


def _dispatch(x1t, pos8, zstart, zlen, tail, rows, ns_tokens):
    blk = (rows * TOK_ROWS, LANES)
    return pl.pallas_call(
        functools.partial(_dispatch_kernel, rows),
        grid_spec=pltpu.PrefetchScalarGridSpec(
            num_scalar_prefetch=4,
            grid=(x1t.shape[0] // (rows * TOK_ROWS),),
            in_specs=[pl.BlockSpec(memory_space=pl.ANY)],
            out_specs=pl.BlockSpec(memory_space=pl.ANY),
            scratch_shapes=[pltpu.VMEM(blk, F32), pltpu.VMEM(blk, F32), pltpu.VMEM(blk, F32),
                            pltpu.VMEM((ZERO_TOKENS * TOK_ROWS, LANES), F32),
                            pltpu.SemaphoreType.DMA((3,)), pltpu.SemaphoreType.DMA((3,)),
                            pltpu.SemaphoreType.DMA(())]),
        out_shape=jax.ShapeDtypeStruct((ns_tokens * TOK_ROWS, LANES), F32),
        compiler_params=pltpu.CompilerParams(dimension_semantics=("arbitrary",),
                                             vmem_limit_bytes=VMEM_LIMIT),
    )(pos8, zstart, zlen, tail, x1t)


def _pair_kernel(alpha, l, tidx_ref, tcls_ref, nused_ref,
                 ea_ref, cha_ref, bufa_ref, nxa_ref, hna_ref, eb_ref, chb_ref, bufb_ref, nxb_ref, hnb_ref,
                 x_ref, wrp_ref, wg_hbm, wu_hbm, wd_hbm,
                 ln2g_ref, ln2b_ref, z_ref, w1_ref, w2_ref, xprev_ref, moe_ref, sg_ref, su_ref, sd_ref, wsem):
    j = pl.program_id(0)
    f = D_FF_EXPERT
    nused = nused_ref[0]
    slots = ((ea_ref, cha_ref, bufa_ref, nxa_ref, hna_ref), (eb_ref, chb_ref, bufb_ref, nxb_ref, hnb_ref))

    def weight_copies(k, e, buf):
        return [pltpu.make_async_copy(wg_hbm.at[l, e], sg_ref.at[k, buf], wsem.at[k, buf]),
                pltpu.make_async_copy(wu_hbm.at[l, e], su_ref.at[k, buf], wsem.at[k, buf]),
                pltpu.make_async_copy(wd_hbm.at[l, e], sd_ref.at[k, buf], wsem.at[k, buf])]

    @pl.when(j == 0)
    def _():
        xprev_ref[...] = jnp.zeros_like(xprev_ref)
        moe_ref[...] = jnp.zeros_like(moe_ref)
        w1_ref[:, 4 * f:] = wrp_ref[...].astype(BF16)
        for k, (e_ref, _, buf_ref, _, _) in enumerate(slots):
            for cp in weight_copies(k, e_ref[0], buf_ref[0]):
                cp.start()

    @pl.when(j <= nused)
    def _():
        class_start = jnp.logical_or(j == 0, tcls_ref[j] != tcls_ref[jnp.maximum(j - 1, 0)])

        for k, (e_ref, ch_ref, buf_ref, nx_ref, hn_ref) in enumerate(slots):
            @pl.when(jnp.logical_and(class_start, ch_ref[j] != 0))
            def _(k=k, e_ref=e_ref, buf_ref=buf_ref, nx_ref=nx_ref, hn_ref=hn_ref):
                buf = buf_ref[j]
                for cp in weight_copies(k, e_ref[j], buf):
                    cp.wait()

                @pl.when(hn_ref[j] != 0)
                def _():
                    for cp in weight_copies(k, nx_ref[j], 1 - buf):
                        cp.start()
                w1_ref[:, 2 * k * f:(2 * k + 1) * f] = sg_ref[k, buf].astype(BF16)
                w1_ref[:, (2 * k + 1) * f:(2 * k + 2) * f] = su_ref[k, buf].astype(BF16)
                w2_ref[k * f:(k + 1) * f, :] = sd_ref[k, buf].astype(BF16)

        _to_token_tiles(z_ref, 0, _layer_norm(alpha * xprev_ref[...] + moe_ref[...], ln2g_ref[...], ln2b_ref[...]))

        x = _from_token_tiles(x_ref, 0, TILE_M)
        gu = _dot(x.astype(BF16), w1_ref[...])
        scores = jax.nn.sigmoid(gu[:, 4 * f:])
        lane = lax.broadcasted_iota(I32, scores.shape, 1)
        sa = jnp.sum(jnp.where(lane == ea_ref[j], scores, 0.0), axis=-1, keepdims=True)
        sb = jnp.sum(jnp.where(lane == eb_ref[j], scores, 0.0), axis=-1, keepdims=True)
        tot = sa + sb
        ha = jax.nn.silu(gu[:, 0 * f:1 * f]) * gu[:, 1 * f:2 * f] * (sa / tot)
        hb = jax.nn.silu(gu[:, 2 * f:3 * f]) * gu[:, 3 * f:4 * f] * (sb / tot)
        xprev_ref[...] = x
        moe_ref[...] = _dot(jnp.concatenate([ha, hb], axis=1).astype(BF16), w2_ref[...])


def _pair_experts(xs, l, tables, wrp, w_gate, w_up, w_down, ln2g, ln2b, alpha):
    nsteps = tables[0].shape[0]
    cst = lambda j, *_: (0, 0)
    lsel = lambda j, *_: (l, 0, 0)
    hbm = pl.BlockSpec(memory_space=pl.ANY)
    return pl.pallas_call(
        functools.partial(_pair_kernel, alpha, l),
        grid_spec=pltpu.PrefetchScalarGridSpec(
            num_scalar_prefetch=len(tables),
            grid=(nsteps,),
            in_specs=[pl.BlockSpec((TILE_M * TOK_ROWS, LANES), lambda j, ti, *_: (ti[j], 0)),
                      pl.BlockSpec(wrp.shape, cst), hbm, hbm, hbm,
                      pl.BlockSpec((None,) + ln2g.shape[1:], lsel), pl.BlockSpec((None,) + ln2b.shape[1:], lsel)],
            out_specs=pl.BlockSpec((TILE_M * TOK_ROWS, LANES),
                                   lambda j, ti, tc, nu, *_: (jnp.minimum(jnp.maximum(j - 1, 0), nu[0] - 1), 0)),
            scratch_shapes=[pltpu.VMEM((D_MODEL, 4 * D_FF_EXPERT + LANES), BF16),
                            pltpu.VMEM((2 * D_FF_EXPERT, D_MODEL), BF16),
                            pltpu.VMEM((TILE_M, D_MODEL), F32), pltpu.VMEM((TILE_M, D_MODEL), F32),
                            pltpu.VMEM((2, 2, D_MODEL, D_FF_EXPERT), F32),
                            pltpu.VMEM((2, 2, D_MODEL, D_FF_EXPERT), F32),
                            pltpu.VMEM((2, 2, D_FF_EXPERT, D_MODEL), F32),
                            pltpu.SemaphoreType.DMA((2, 2))]),
        out_shape=jax.ShapeDtypeStruct(xs.shape, F32),
        input_output_aliases={len(tables): 0},
        compiler_params=pltpu.CompilerParams(dimension_semantics=("arbitrary",),
                                             vmem_limit_bytes=VMEM_LIMIT),
    )(*tables, xs, wrp, w_gate, w_up, w_down, ln2g, ln2b)


def _ungather_kernel(nb, pos8_ref, zs_hbm, out_ref, buf0, buf1, sems):
    rows = nb * CHUNK

    def consume(buf):
        out_ref[...] = _from_token_tiles(buf, 0, rows).reshape(nb, CHUNK, D_MODEL)
    _gathered_tokens(pl.program_id(0), pl.num_programs(0), pos8_ref, zs_hbm, (buf0, buf1), sems, rows, consume)


def _ungather(zs, pos8, nb, seq):
    rows = nb * CHUNK
    return pl.pallas_call(
        functools.partial(_ungather_kernel, nb),
        grid_spec=pltpu.PrefetchScalarGridSpec(
            num_scalar_prefetch=1,
            grid=(seq // CHUNK,),
            in_specs=[pl.BlockSpec(memory_space=pl.ANY)],
            out_specs=pl.BlockSpec((nb, CHUNK, D_MODEL), lambda i, *_: (0, i, 0)),
            scratch_shapes=[pltpu.VMEM((rows * TOK_ROWS, LANES), F32), pltpu.VMEM((rows * TOK_ROWS, LANES), F32),
                            pltpu.SemaphoreType.DMA((2,))]),
        out_shape=jax.ShapeDtypeStruct((nb, seq, D_MODEL), F32),
        compiler_params=pltpu.CompilerParams(dimension_semantics=("arbitrary",),
                                             vmem_limit_bytes=VMEM_LIMIT),
    )(pos8, zs)


def _plan(cls, rank, counts, nt):
    cnt = counts[:N_CLASSES, 0].astype(I32)
    ntile = (cnt + TILE_M - 1) // TILE_M
    padded = ntile * TILE_M
    off = jnp.cumsum(padded) - padded
    classes = jnp.arange(N_CLASSES, dtype=I32)
    pos8 = (rank.reshape(-1) + jnp.sum(jnp.where(cls.reshape(-1, 1) == classes, off, 0), axis=1)) * TOK_ROWS
    tile_end = jnp.cumsum(ntile)
    nused = tile_end[-1:].astype(I32)
    tidx = jnp.minimum(jnp.arange(nt + 1, dtype=I32), nused - 1)
    tsel = tile_end[None, :] <= tidx[:, None]
    tcls = jnp.sum(tsel.astype(I32), axis=1)
    used = ntile > 0
    earlier = used[None, :] & (classes[None, :] < classes[:, None])
    later = classes[None, :] > classes[:, None]
    prev_used = jnp.max(jnp.where(earlier, classes[None, :], -1), axis=1)
    per_class = []
    for table in (EA_TABLE, EB_TABLE):
        e_c = jnp.asarray(table)
        change = used & ((prev_used < 0) | (e_c[jnp.maximum(prev_used, 0)] != e_c))
        nxt = jnp.min(jnp.where(change[None, :] & later, classes[None, :], N_CLASSES), axis=1)
        per_class += [e_c, change.astype(I32), (jnp.cumsum(change.astype(I32)) - 1) & 1,
                      e_c[jnp.minimum(nxt, N_CLASSES - 1)], (nxt < N_CLASSES).astype(I32)]
    onehot = tcls[:, None] == classes
    per_tile = jnp.sum(jnp.where(onehot[None], jnp.stack(per_class)[:, None, :], 0), axis=2)
    tables = (tidx, tcls, nused) + tuple(per_tile)
    return pos8.astype(I32), off + cnt, padded - cnt, nused * TILE_M, tables


def _mixer_sample_kernel(alpha, x_ref, h0_ref, win_ref, wout_ref, lnvg_ref, lnvb_ref, ws0_ref, bs0_ref,
                         bdb_ref, bdc_ref, lr_ref, li_ref, dskip_ref, glu_ref, bglu_ref,
                         ln1g_ref, ln1b_ref,
                         x1_ref, hnew_ref, v_ref):
    x = x_ref[...]
    proj = _dot(x.astype(BF16), win_ref[...])
    u = jax.nn.gelu(proj[:, :W_A])
    v = _layer_norm(jax.nn.gelu(proj[:, W_A:2 * W_A]), lnvg_ref[...], lnvb_ref[...])
    v_ref[...] = v
    y_a = u * (ws0_ref[...] * v + bs0_ref[...])
    xs = proj[:, 2 * W_A:]
    zs = []
    for k in range(2):
        sl = slice(k * HALF_W, (k + 1) * HALF_W)
        bu = _dot(xs[:, sl].astype(BF16), bdb_ref[k])
        h0r = h0_ref[k, :, :HALF_STATE]
        h0i = h0_ref[k, :, HALF_STATE:]
        lr = lr_ref[k]
        li = li_ref[k]
        hr = lr * h0r - li * h0i + bu[:, :HALF_STATE]
        hi = lr * h0i + li * h0r + bu[:, HALF_STATE:]
        hnew_ref[k, :, :HALF_STATE] = hr
        hnew_ref[k, :, HALF_STATE:] = hi
        hcat = jnp.concatenate([hr, hi], axis=1).astype(BF16)
        y = jax.nn.gelu(_dot(hcat, bdc_ref[k]) + dskip_ref[:, sl] * xs[:, sl])
        gl = _dot(y.astype(BF16), glu_ref[k]) + bglu_ref[:, sl]
        zs.append(y * jax.nn.sigmoid(gl))
    cat = jnp.concatenate([y_a] + zs, axis=1).astype(BF16)
    mix = _dot(cat, wout_ref[...])
    x1_ref[...] = _layer_norm(alpha * x + mix, ln1g_ref[...], ln1b_ref[...])


def _mixer_sample(x, h0, lw, l, alpha):
    n = x.shape[0]
    full = lambda shape: pl.BlockSpec(shape, lambda i: (0,) * len(shape))
    return pl.pallas_call(
        functools.partial(_mixer_sample_kernel, alpha),
        grid=(1,),
        in_specs=[full(x.shape), _layer_spec(h0.shape, l)] + [_layer_spec(lw[k].shape, l) for k in SAMPLE_WEIGHTS],
        out_specs=[full((n, D_MODEL)), full((2, n, 2 * HALF_STATE)), full((n, W_A))],
        out_shape=[jax.ShapeDtypeStruct((n, D_MODEL), F32),
                   jax.ShapeDtypeStruct((2, n, 2 * HALF_STATE), F32),
                   jax.ShapeDtypeStruct((n, W_A), F32)],
        compiler_params=pltpu.CompilerParams(dimension_semantics=("arbitrary",), vmem_limit_bytes=VMEM_LIMIT),
    )(x, h0, *[lw[k] for k in SAMPLE_WEIGHTS])


def _route(x, wr, rbias):
    logits = jnp.dot(x, wr, preferred_element_type=F32, precision=HIGHEST)
    scores = jax.nn.sigmoid(logits)
    biased = scores + rbias
    lane = lax.broadcasted_iota(I32, biased.shape, 1)
    grp = lane // EXPERTS_PER_GROUP
    neg = jnp.float32(-jnp.inf)

    def top2(vals):
        m1 = jnp.max(vals, axis=-1, keepdims=True)
        i1 = jnp.min(jnp.where(vals == m1, lane, N_EXPERTS), axis=-1, keepdims=True)
        rest = jnp.where(lane == i1, neg, vals)
        m2 = jnp.max(rest, axis=-1, keepdims=True)
        i2 = jnp.min(jnp.where(rest == m2, lane, N_EXPERTS), axis=-1, keepdims=True)
        return m1, i1, m2, i2

    best = sel = None
    for g in range(N_EXPERT_GROUPS):
        m1, _, m2, _ = top2(jnp.where(grp == g, biased, neg))
        gs = m1 + m2
        if g == 0:
            best, sel = gs, jnp.zeros(gs.shape, I32)
        else:
            upd = gs > best
            sel = jnp.where(upd, g, sel)
            best = jnp.where(upd, gs, best)
    _, i1, _, i2 = top2(jnp.where(grp == sel, biased, neg))
    s1 = jnp.sum(jnp.where(lane == i1, scores, 0.0), axis=-1, keepdims=True)
    s2 = jnp.sum(jnp.where(lane == i2, scores, 0.0), axis=-1, keepdims=True)
    tot = s1 + s2
    return jnp.where(lane == i1, s1 / tot, 0.0) + jnp.where(lane == i2, s2 / tot, 0.0)


def _moe_kernel(alpha, x_ref, wr_ref, rb_ref, wg_ref, wu_ref, wd_ref, ln2g_ref, ln2b_ref,
                out_ref, xb_ref, comb_ref, acc_ref):
    step = pl.program_id(1)

    @pl.when(step == 0)
    def _():
        x = x_ref[...]
        xb_ref[...] = x.astype(BF16)
        comb_ref[...] = _route(x, wr_ref[...], rb_ref[...])
        acc_ref[...] = jnp.zeros_like(acc_ref)

    xb = xb_ref[...]
    comb = comb_ref[...]
    lane = lax.broadcasted_iota(I32, comb.shape, 1)
    acc = acc_ref[...]
    for k in range(DENSE_EXPERTS_PER_STEP):
        g = _dot(xb, wg_ref[k].astype(BF16))
        u = _dot(xb, wu_ref[k].astype(BF16))
        ce = jnp.sum(jnp.where(lane == step * DENSE_EXPERTS_PER_STEP + k, comb, 0.0), axis=-1, keepdims=True)
        h = (jax.nn.silu(g) * u * ce).astype(BF16)
        acc = acc + _dot(h, wd_ref[k].astype(BF16))
    acc_ref[...] = acc

    @pl.when(step == pl.num_programs(1) - 1)
    def _():
        out_ref[...] = _layer_norm(alpha * x_ref[...] + acc_ref[...], ln2g_ref[...], ln2b_ref[...])


def _moe_dense(x, l, wr, rb, w_gate, w_up, w_down, ln2g, ln2b, alpha, tm):
    t = x.shape[0]
    cst = lambda i, e: (0, 0)
    wsel = lambda i, e: (l, e, 0, 0)
    return pl.pallas_call(
        functools.partial(_moe_kernel, alpha),
        grid=(t // tm, N_EXPERTS // DENSE_EXPERTS_PER_STEP),
        in_specs=[pl.BlockSpec((tm, D_MODEL), lambda i, e: (i, 0)),
                  pl.BlockSpec(wr.shape, cst), pl.BlockSpec(rb.shape, cst),
                  pl.BlockSpec((None, DENSE_EXPERTS_PER_STEP, D_MODEL, D_FF_EXPERT), wsel),
                  pl.BlockSpec((None, DENSE_EXPERTS_PER_STEP, D_MODEL, D_FF_EXPERT), wsel),
                  pl.BlockSpec((None, DENSE_EXPERTS_PER_STEP, D_FF_EXPERT, D_MODEL), wsel),
                  pl.BlockSpec((None,) + ln2g.shape[1:], lambda i, e: (l, 0, 0)),
                  pl.BlockSpec((None,) + ln2b.shape[1:], lambda i, e: (l, 0, 0))],
        out_specs=pl.BlockSpec((tm, D_MODEL), lambda i, e: (i, 0)),
        out_shape=jax.ShapeDtypeStruct((t, D_MODEL), F32),
        scratch_shapes=[pltpu.VMEM((tm, D_MODEL), BF16),
                        pltpu.VMEM((tm, N_EXPERTS), F32),
                        pltpu.VMEM((tm, D_MODEL), F32)],
        compiler_params=pltpu.CompilerParams(dimension_semantics=("arbitrary", "arbitrary"),
                                             vmem_limit_bytes=VMEM_LIMIT),
    )(x, wr, rb, w_gate, w_up, w_down, ln2g, ln2b)


def _prep_s5_folded(lb_re, lb_im, bb_re, bb_im, c_re, c_im):
    d, g, n = lb_re.shape
    s = S5_FOLD
    pairs = g // 2
    pr, pi = [jnp.ones_like(lb_re)], [jnp.zeros_like(lb_re)]
    for _ in range(s):
        pr, pi = pr + [pr[-1] * lb_re - pi[-1] * lb_im], pi + [pr[-1] * lb_im + pi[-1] * lb_re]
    p_re, p_im = jnp.stack(pr, axis=2), jnp.stack(pi, axis=2)

    def pair_cols(re, im):
        z = jnp.zeros_like(re[:, :, 0])
        g0 = jnp.concatenate([re[:, :, 0], z, im[:, :, 0], z], axis=-1)
        g1 = jnp.concatenate([z, re[:, :, 1], z, im[:, :, 1]], axis=-1)
        return jnp.concatenate([g0, g1], axis=2)

    def pair_diag(a):
        z = jnp.zeros_like(a[:, :, 0])
        return jnp.concatenate([jnp.concatenate([a[:, :, 0], z], axis=-1),
                                jnp.concatenate([z, a[:, :, 1]], axis=-1)], axis=2)

    bt_re, bt_im = bb_re.transpose(0, 1, 3, 2)[:, :, None], bb_im.transpose(0, 1, 3, 2)[:, :, None]
    k_re, k_im = p_re[:, :, s - 1::-1, None, :], p_im[:, :, s - 1::-1, None, :]
    m1t = pair_cols((k_re * bt_re - k_im * bt_im).reshape(d, pairs, 2, -1, n),
                    (k_re * bt_im + k_im * bt_re).reshape(d, pairs, 2, -1, n))
    m1 = jnp.swapaxes(m1t, -1, -2)

    cq_re, cq_im = c_re[:, :, None], c_im[:, :, None]
    j_re, j_im = p_re[:, :, 1:, None, :], p_im[:, :, 1:, None, :]
    hpart = pair_cols((cq_re * j_re - cq_im * j_im).reshape(d, pairs, 2, -1, n),
                      (-(cq_re * j_im + cq_im * j_re)).reshape(d, pairs, 2, -1, n))

    t_re, t_im = p_re[:, :, :s, None, :], p_im[:, :, :s, None, :]
    kern = (jnp.einsum("dgtqn,dgnp->dgtqp", cq_re * t_re - cq_im * t_im, bb_re, precision=HIGHEST)
            - jnp.einsum("dgtqn,dgnp->dgtqp", cq_re * t_im + cq_im * t_re, bb_im, precision=HIGHEST))
    zero = jnp.zeros_like(kern[:, :, 0])
    kx = jnp.stack([jnp.concatenate([kern[:, :, j - q] if q <= j else zero for q in range(s)], axis=-1)
                    for j in range(s)], axis=2)
    apart = pair_diag(kx.reshape(d, pairs, 2, -1, kx.shape[-1]))

    m2 = jnp.concatenate([hpart, apart], axis=-1)
    return (m1.astype(BF16), m2.astype(BF16), p_re[:, :, s].reshape(d, 1, g * n), p_im[:, :, s].reshape(d, 1, g * n))


def _prep_all(w_in, w_out, ln_v_g, ln_v_b, w_s, b_s, a_re, a_im, log_dt, b_re, b_im, c_re, c_im,
              d_skip, w_glu, b_glu, ln1_g, ln1_b):
    d = w_in.shape[0]
    dt = jnp.exp(log_dt)[..., None]
    decay = jnp.exp(a_re * dt)
    lb_re, lb_im = decay * jnp.cos(a_im * dt), decay * jnp.sin(a_im * dt)
    den = a_re * a_re + a_im * a_im
    nr, ni = lb_re - 1.0, lb_im
    zr = (nr * a_re + ni * a_im) / den
    zi = (ni * a_re - nr * a_im) / den
    bb_re = zr[..., None] * b_re - zi[..., None] * b_im
    bb_im = zr[..., None] * b_im + zi[..., None] * b_re

    def bd(a):
        r, c = a.shape[2:]
        rep = jnp.asarray(np.tile(np.eye(c, dtype=np.float32), (1, HALF_GROUPS)))
        diag = jnp.asarray(np.kron(np.eye(HALF_GROUPS), np.ones((r, c))) > 0)
        tiled = jnp.einsum("dkrc,cn->dkrn", a.reshape(d, 2, HALF_GROUPS * r, c), rep)
        return jnp.where(diag, tiled, 0.0).astype(BF16)

    bdb = jnp.concatenate([bd(bb_re.transpose(0, 1, 3, 2)), bd(bb_im.transpose(0, 1, 3, 2))], axis=3)
    bdc = jnp.concatenate([bd(c_re.transpose(0, 1, 3, 2)), bd(-c_im.transpose(0, 1, 3, 2))], axis=2)
    m1, m2, lr8, li8 = _prep_s5_folded(lb_re, lb_im, bb_re, bb_im, c_re, c_im)
    return dict(
        m1=m1, m2=m2, lr8=lr8, li8=li8,
        win=w_in.astype(BF16), wout=w_out.astype(BF16),
        lnvg=ln_v_g[:, None], lnvb=ln_v_b[:, None],
        wtril=jnp.tril(w_s).astype(BF16),
        bsb=jnp.broadcast_to(b_s[..., None], (d, H_A, CHUNK, LANES)),
        ws0=jnp.repeat(w_s[:, :, 0, 0], P_A, axis=1)[:, None], bs0=jnp.repeat(b_s[:, :, 0], P_A, axis=1)[:, None],
        bdb=bdb, bdc=bdc,
        lr=lb_re.reshape(d, 2, 1, HALF_STATE), li=lb_im.reshape(d, 2, 1, HALF_STATE),
        dskip=d_skip.reshape(d, 1, W_B), glu=bd(w_glu),
        bglu=b_glu.reshape(d, 1, W_B), ln1g=ln1_g[:, None], ln1b=ln1_b[:, None])


def _state_to_cols(h_re, h_im):
    d, b = h_re.shape[:2]
    re = h_re.reshape(d, b, 2, HALF_STATE)
    im = h_im.reshape(d, b, 2, HALF_STATE)
    return jnp.concatenate([re, im], axis=3).transpose(0, 2, 1, 3)


def _cols_to_state(h):
    d, _, b, _ = h.shape
    re = h[..., :HALF_STATE].transpose(0, 2, 1, 3).reshape(d, b, G_B, N_STATE)
    im = h[..., HALF_STATE:].transpose(0, 2, 1, 3).reshape(d, b, G_B, N_STATE)
    return re, im


def _pairs_to_state(h):
    d, b, _ = h.shape
    h = h.reshape(d, b, G_B // 2, 2, 2, N_STATE)
    return h[:, :, :, 0].reshape(d, b, G_B, N_STATE), h[:, :, :, 1].reshape(d, b, G_B, N_STATE)


def kernel(x_prompt, x_sample, state_ssm_re, state_ssm_im, w_in, w_out, ln_v_g, ln_v_b, w_s, b_s, ssm_a_re, ssm_a_im, ssm_log_dt, ssm_b_re, ssm_b_im, ssm_c_re, ssm_c_im, ssm_d, w_glu, b_glu, ln1_g, ln1_b, ln2_g, ln2_b, w_router, router_bias, w_gate, w_up, w_down):
    depth = w_in.shape[0]
    alpha = float((2 * depth) ** 0.25)
    nb, seq, _ = x_prompt.shape
    ns = x_sample.shape[0]
    tokens = nb * seq
    nt = tokens // TILE_M + N_CLASSES
    rb = router_bias[None]
    wrt = w_router.T
    wrp = jnp.pad(w_router, ((0, 0), (0, LANES - N_EXPERTS)))
    rbcol = router_bias[:, None]
    tok = np.arange(nb * CHUNK)
    tri = jnp.asarray(tok[:, None] < tok[None, :], BF16)
    lw = _prep_all(w_in, w_out, ln_v_g, ln_v_b, w_s, b_s, ssm_a_re, ssm_a_im, ssm_log_dt,
                   ssm_b_re, ssm_b_im, ssm_c_re, ssm_c_im, ssm_d, w_glu, b_glu, ln1_g, ln1_b)
    shared = (wrt, rbcol, tri)
    ln2g, ln2b = ln2_g[:, None], ln2_b[:, None]
    h0s = _state_to_cols(state_ssm_re, state_ssm_im)
    xp = x_prompt
    pos = None
    xs = x_sample.reshape(ns, D_MODEL)
    pr_h, sm_h, sm_v = [], [], []
    for l in range(depth):
        x1t, hfin, cls, rank, counts = _mixer_prompt(xp, lw, shared, l, alpha, nb, seq, pos)
        pos, zstart, zlen, tail, tables = _plan(cls, rank, counts, nt)
        x_sorted = _dispatch(x1t, pos, zstart, zlen, tail, nb * CHUNK, nt * TILE_M)
        xp = _pair_experts(x_sorted, l, tables, wrp, w_gate, w_up, w_down, ln2g, ln2b, alpha)
        pr_h.append(hfin)

        x1s, hnew, v_new = _mixer_sample(xs, h0s, lw, l, alpha)
        xs = _moe_dense(x1s, l, w_router, rb, w_gate, w_up, w_down, ln2g, ln2b, alpha, tm=ns)
        sm_h.append(hnew)
        sm_v.append(v_new.reshape(ns, 1, W_A))
    y_prompt = _ungather(xp, pos, nb, seq)
    pr_re, pr_im = _pairs_to_state(jnp.stack(pr_h))
    sm_re, sm_im = _cols_to_state(jnp.stack(sm_h))
    return (y_prompt, xs.reshape(ns, 1, D_MODEL), pr_re, pr_im, sm_re, sm_im, jnp.stack(sm_v))
```

```python
import functools

import jax
import jax.numpy as jnp
import numpy as np
from jax import lax
from jax.experimental import pallas as pl
from jax.experimental.pallas import tpu as pltpu

D_MODEL = 1024
W_A = 512
W_B = 512
CHUNK = 128
H_A = 4
P_A = W_A // H_A
GROUP_B = 16
G_B = W_B // GROUP_B
N_STATE = 64
N_EXPERTS = 16
N_EXPERT_GROUPS = 4
EXPERTS_PER_GROUP = N_EXPERTS // N_EXPERT_GROUPS
D_FF_EXPERT = D_MODEL // 4
LN_EPS = 1e-5

LANES = 128
SUBLANES = 8
HALF_GROUPS = 16
HALF_W = HALF_GROUPS * GROUP_B
HALF_STATE = HALF_GROUPS * N_STATE
VMEM_LIMIT = 56 * 1024 * 1024

PAIRS = ((0, 1), (0, 2), (0, 3), (1, 3), (1, 2), (3, 2))
N_CLASSES = N_EXPERT_GROUPS * len(PAIRS)
CLASS_ROWS = 32
TILE_M = 256
TOK_ROWS = D_MODEL // LANES
ZERO_TOKENS = TILE_M // 2
DMA_UNROLL = 16
DENSE_EXPERTS_PER_STEP = 4
S5_FOLD = 8
S5_PAIRS = G_B // 2
EA_TABLE = np.array([EXPERTS_PER_GROUP * g + a for g in range(N_EXPERT_GROUPS) for a, _ in PAIRS], np.int32)
EB_TABLE = np.array([EXPERTS_PER_GROUP * g + b for g in range(N_EXPERT_GROUPS) for _, b in PAIRS], np.int32)

F32 = jnp.float32
BF16 = jnp.bfloat16
I32 = jnp.int32
HIGHEST = lax.Precision.HIGHEST


def _layer_norm(x, g, b):
    mu = jnp.mean(x, axis=-1, keepdims=True)
    xc = x - mu
    var = jnp.mean(xc * xc, axis=-1, keepdims=True)
    return xc * lax.rsqrt(var + LN_EPS) * g + b


def _dot(a, b):
    return jnp.dot(a, b, preferred_element_type=F32)


def _route_classes(x1, wrt, rbcol):
    def split(a):
        hi = a.astype(BF16)
        return hi, (a - hi.astype(F32)).astype(BF16)

    def dot_t(a, b):
        return lax.dot_general(a, b, (((1,), (1,)), ((), ())), preferred_element_type=F32)
    w_hi, w_lo = split(wrt)
    x_hi, x_lo = split(x1)
    logits_t = dot_t(w_hi, x_hi) + (dot_t(w_hi, x_lo) + dot_t(w_lo, x_hi))
    biased = jax.nn.sigmoid(logits_t) + rbcol
    rows = [biased[e:e + 1, :] for e in range(N_EXPERTS)]
    n = EXPERTS_PER_GROUP

    best = sel = None
    for g in range(N_EXPERT_GROUPS):
        v = rows[n * g:n * (g + 1)]
        gs = None
        for a, b in PAIRS:
            s = v[a] + v[b]
            gs = s if gs is None else jnp.maximum(gs, s)
        if g == 0:
            best, sel = gs, jnp.zeros(gs.shape, I32)
        else:
            upd = gs > best
            sel = jnp.where(upd, g, sel)
            best = jnp.where(upd, gs, best)

    cls = jnp.zeros(sel.shape, I32)
    for g in range(N_EXPERT_GROUPS):
        v = rows[n * g:n * (g + 1)]
        lo = jnp.full(sel.shape, n, I32)
        hi = jnp.full(sel.shape, -1, I32)
        for i in range(n):
            before = jnp.zeros(sel.shape, I32)
            for j in range(n):
                if j < i:
                    before = before + (v[j] >= v[i]).astype(I32)
                elif j > i:
                    before = before + (v[j] > v[i]).astype(I32)
            member = before < 2
            lo = jnp.where(member, jnp.minimum(lo, i), lo)
            hi = jnp.where(member, jnp.maximum(hi, i), hi)
        pidx = jnp.zeros(sel.shape, I32)
        for k, (a, b) in enumerate(PAIRS):
            pidx = jnp.where((lo == min(a, b)) & (hi == max(a, b)), k, pidx)
        cls = jnp.where(sel == g, g * len(PAIRS) + pidx, cls)
    return cls


def _to_token_tiles(ref, row0, x):
    n = x.shape[0]
    for c in range(TOK_ROWS):
        ref[pl.ds(row0 * TOK_ROWS + c, n, stride=TOK_ROWS), :] = x[:, c * LANES:(c + 1) * LANES]


def _from_token_tiles(ref, row0, n):
    return jnp.concatenate(
        [ref[pl.ds(row0 * TOK_ROWS + c, n, stride=TOK_ROWS), :] for c in range(TOK_ROWS)], axis=1)


def _gathered_tokens(step, nsteps, pos8_ref, src_hbm, bufs, sems, rows, consume):
    def start(s, base):
        def body(g, c):
            idx0 = base + g * DMA_UNROLL
            row0 = pl.multiple_of(g * (DMA_UNROLL * TOK_ROWS), DMA_UNROLL * TOK_ROWS)
            for i in range(DMA_UNROLL):
                p8 = pl.multiple_of(pos8_ref[idx0 + i], TOK_ROWS)
                pltpu.make_async_copy(src_hbm.at[pl.ds(p8, TOK_ROWS), :],
                                      bufs[s].at[pl.ds(row0 + i * TOK_ROWS, TOK_ROWS), :],
                                      sems.at[s]).start(priority=i % 2)
            return c
        lax.fori_loop(0, rows // DMA_UNROLL, body, 0)

    @pl.when(step == 0)
    def _():
        start(0, 0)

    for s in range(2):
        @pl.when(lax.rem(step, 2) == s)
        def _(s=s):
            pltpu.make_async_copy(src_hbm.at[pl.ds(0, rows * TOK_ROWS), :], bufs[s], sems.at[s]).wait()
            consume(bufs[s])

            @pl.when(step + 1 < nsteps)
            def _():
                start(1 - s, (step + 1) * rows)


def _mixer_kernel(alpha, nb, gather_in, *refs):
    if gather_in:
        pos8_ref, zs_hbm = refs[:2]
        refs = refs[2:]
    else:
        x_ref = refs[0]
        refs = refs[1:]
    (win_ref, wout_ref, lnvg_ref, lnvb_ref, wtril_ref, bsb_ref, m1_ref, m2_ref, lr8_ref, li8_ref,
     dskip_ref, glu_ref, bglu_ref, ln1g_ref, ln1b_ref, wrt_ref, rbcol_ref, tri_ref,
     x1t_ref, hfin_ref, cls_ref, rank_ref, cnt_ref,
     xb_ref, xs_slab, xs_scb, xst_ref, ht_ref, yt_ref, mix_ref, hstate_ref, carry_ref) = refs[:32]
    lt = CHUNK
    rows = nb * lt
    pitch = lt + SUBLANES
    step = pl.program_id(0)
    nsteps = pl.num_programs(0)

    @pl.when(step == 0)
    def _():
        hstate_ref[...] = jnp.zeros_like(hstate_ref)
        carry_ref[...] = jnp.zeros_like(carry_ref)

    if gather_in:
        xin_ref, xbuf0, xbuf1, gsem = refs[32:36]

        def consume(buf):
            xg = _from_token_tiles(buf, 0, rows)
            xin_ref[...] = xg
            xb_ref[...] = xg.astype(BF16)
        _gathered_tokens(step, nsteps, pos8_ref, zs_hbm, (xbuf0, xbuf1), gsem, rows, consume)

        def load_x():
            return xin_ref[...]
    else:
        def load_x():
            return x_ref[...].reshape(rows, D_MODEL)

    if not gather_in:
        xb_ref[...] = load_x().astype(BF16)

    fold = S5_FOLD
    nchunk = lt // fold
    cb = nchunk * nb
    xs = _dot(xb_ref[...], win_ref[:, 2 * W_A:])
    for j in range(W_B // LANES):
        for b in range(nb):
            xs_slab[j, b * pitch:b * pitch + lt, :] = xs[b * lt:(b + 1) * lt, j * LANES:(j + 1) * LANES]

    for t in range(lt):
        r0 = (t % fold) * cb + (t // fold) * nb
        for j in range(W_B // LANES):
            xs_scb[r0:r0 + nb, j * LANES:(j + 1) * LANES] = xs_slab[j, pl.ds(t, nb, stride=pitch), :]
    xst_ref[...] = xs_scb[...].T.astype(BF16)

    vg = jax.nn.gelu(_dot(xb_ref[...], win_ref[:, W_A:2 * W_A]))
    v = _layer_norm(vg, lnvg_ref[...], lnvb_ref[...]).astype(BF16)
    u = jax.nn.gelu(_dot(xb_ref[...], win_ref[:, :W_A]))
    for h in range(H_A):
        hs = slice(h * P_A, (h + 1) * P_A)
        vcat = jnp.concatenate([v[b * lt:(b + 1) * lt, hs] for b in range(nb)], axis=1)
        o = _dot(wtril_ref[h], vcat)
        for b in range(nb):
            rs = slice(b * lt, (b + 1) * lt)
            mix_ref[rs, hs] = u[rs, hs] * (o[:, b * LANES:(b + 1) * LANES] + bsb_ref[h])

    def chunk_inputs(pr):
        return jnp.concatenate(
            [xst_ref[(2 * pr + gi) * GROUP_B:(2 * pr + gi + 1) * GROUP_B, s * cb:(s + 1) * cb]
             for gi in range(2) for s in range(fold)], axis=0)

    pw = 2 * 2 * N_STATE
    for pr in range(S5_PAIRS):
        ht_ref[:, pr * pw:(pr + 1) * pw] = _dot(m1_ref[pr], chunk_inputs(pr)).T

    half_pairs = S5_PAIRS // 2
    for half in range(2):
        c0 = half * half_pairs * pw
        lr8 = [jnp.broadcast_to(lr8_ref[:, (half * half_pairs + p) * LANES:(half * half_pairs + p + 1) * LANES],
                                (nb, LANES)) for p in range(half_pairs)]
        li8 = [jnp.broadcast_to(li8_ref[:, (half * half_pairs + p) * LANES:(half * half_pairs + p + 1) * LANES],
                                (nb, LANES)) for p in range(half_pairs)]
        h = [hstate_ref[:, c0 + q * LANES:c0 + (q + 1) * LANES] for q in range(2 * half_pairs)]
        for c in range(nchunk):
            rs = slice(c * nb, (c + 1) * nb)
            for p in range(half_pairs):
                re_sl = slice(c0 + p * pw, c0 + p * pw + LANES)
                im_sl = slice(c0 + p * pw + LANES, c0 + (p + 1) * pw)
                hr, hi = h[2 * p], h[2 * p + 1]
                ur, ui = ht_ref[rs, re_sl], ht_ref[rs, im_sl]
                ht_ref[rs, re_sl] = hr
                ht_ref[rs, im_sl] = hi
                h[2 * p] = lr8[p] * hr - li8[p] * hi + ur
                h[2 * p + 1] = lr8[p] * hi + li8[p] * hr + ui
        for q in range(2 * half_pairs):
            hstate_ref[:, c0 + q * LANES:c0 + (q + 1) * LANES] = h[q]
    hfin_ref[...] = hstate_ref[...]

    for pr in range(S5_PAIRS):
        rhs = jnp.concatenate([ht_ref[:, pr * pw:(pr + 1) * pw].T.astype(BF16), chunk_inputs(pr)], axis=0)
        yt = _dot(m2_ref[pr], rhs)
        for gi in range(2):
            for s in range(fold):
                r0 = (gi * fold + s) * GROUP_B
                yt_ref[(2 * pr + gi) * GROUP_B:(2 * pr + gi + 1) * GROUP_B, s * cb:(s + 1) * cb] = (
                    yt[r0:r0 + GROUP_B, :])

    y = jax.nn.gelu(yt_ref[...].T + dskip_ref[...] * xs_scb[...])
    yb = y.astype(BF16)
    spitch = cb + SUBLANES
    for k in range(2):
        sl = slice(k * HALF_W, (k + 1) * HALF_W)
        gl = _dot(yb[:, sl], glu_ref[k]) + bglu_ref[:, sl]
        z = y[:, sl] * jax.nn.sigmoid(gl)
        for j in range(HALF_W // LANES):
            for s in range(fold):
                xs_slab[k * (HALF_W // LANES) + j, s * spitch:s * spitch + cb, :] = (
                    z[s * cb:(s + 1) * cb, j * LANES:(j + 1) * LANES])

    for ch in range(nchunk):
        for b in range(nb):
            dst = b * lt + ch * fold
            for j in range(W_B // LANES):
                mix_ref[dst:dst + fold, W_A + j * LANES:W_A + (j + 1) * LANES] = (
                    xs_slab[j, pl.ds(ch * nb + b, fold, stride=spitch), :])

    mix = _dot(mix_ref[...].astype(BF16), wout_ref[...])
    x1 = _layer_norm(alpha * load_x() + mix, ln1g_ref[...], ln1b_ref[...])
    _to_token_tiles(x1t_ref, 0, x1)

    cls = _route_classes(x1, wrt_ref[...], rbcol_ref[...])
    crow = lax.broadcasted_iota(I32, (CLASS_ROWS, rows), 0)
    onehot = jnp.where(crow == cls, 1.0, 0.0)
    prefix = _dot(onehot.astype(BF16), tri_ref[...])
    carry = carry_ref[:, 0:1]
    rank = jnp.sum(onehot * (prefix + carry), axis=0, keepdims=True)
    cls_ref[...] = cls.reshape(1, 1, rows)
    rank_ref[...] = rank.astype(I32).reshape(1, 1, rows)
    carry_ref[...] = carry_ref[...] + jnp.sum(onehot, axis=1, keepdims=True)
    cnt_ref[...] = carry_ref[...]


def _const_spec(shape):
    nd = len(shape)
    return pl.BlockSpec(shape, lambda *_: (0,) * nd, pipeline_mode=pl.Buffered(1))


def _layer_spec(shape, l):
    nd = len(shape)
    return pl.BlockSpec((None,) + tuple(shape[1:]), lambda *_: (l,) + (0,) * (nd - 1),
                        pipeline_mode=pl.Buffered(1))


MIXER_WEIGHTS = ("win", "wout", "lnvg", "lnvb", "wtril", "bsb", "m1", "m2", "lr8", "li8", "dskip", "glu",
                 "bglu", "ln1g", "ln1b")
SAMPLE_WEIGHTS = ("win", "wout", "lnvg", "lnvb", "ws0", "bs0", "bdb", "bdc", "lr", "li", "dskip", "glu",
                  "bglu", "ln1g", "ln1b")


def _mixer_prompt(x, lw, shared, l, alpha, nb, seq, pos_prev=None):
    lt = CHUNK
    rows = nb * lt
    nsteps = seq // lt
    gather_in = pos_prev is not None
    weights = tuple(lw[k] for k in MIXER_WEIGHTS) + tuple(shared)
    wspecs = [_layer_spec(lw[k].shape, l) for k in MIXER_WEIGHTS] + [_const_spec(w.shape) for w in shared]
    if gather_in:
        x_spec = pl.BlockSpec(memory_space=pl.ANY)
    else:
        x_spec = pl.BlockSpec((nb, lt, D_MODEL), lambda i, *_: (0, i, 0))
    scratch = [
        pltpu.VMEM((rows, D_MODEL), BF16),
        pltpu.VMEM((W_B // LANES, nb * (lt + SUBLANES), LANES), F32),
        pltpu.VMEM((rows, W_B), F32),
        pltpu.VMEM((W_B, rows), BF16),
        pltpu.VMEM((rows // S5_FOLD, 2 * G_B * N_STATE), F32),
        pltpu.VMEM((W_B, rows), F32),
        pltpu.VMEM((rows, D_MODEL), F32),
        pltpu.VMEM((nb, 2 * G_B * N_STATE), F32),
        pltpu.VMEM((CLASS_ROWS, LANES), F32),
    ]
    if gather_in:
        scratch += [pltpu.VMEM((rows, D_MODEL), F32),
                    pltpu.VMEM((rows * TOK_ROWS, LANES), F32), pltpu.VMEM((rows * TOK_ROWS, LANES), F32),
                    pltpu.SemaphoreType.DMA((2,))]
    grid_spec = pltpu.PrefetchScalarGridSpec(
        num_scalar_prefetch=1 if gather_in else 0,
        grid=(nsteps,),
        in_specs=[x_spec] + wspecs,
        out_specs=[pl.BlockSpec((rows * TOK_ROWS, LANES), lambda i, *_: (i, 0)),
                   pl.BlockSpec((nb, 2 * G_B * N_STATE), lambda i, *_: (0, 0)),
                   pl.BlockSpec((1, 1, rows), lambda i, *_: (i, 0, 0)),
                   pl.BlockSpec((1, 1, rows), lambda i, *_: (i, 0, 0)),
                   pl.BlockSpec((CLASS_ROWS, LANES), lambda i, *_: (0, 0))],
        scratch_shapes=scratch)
    args = ((pos_prev, x) if gather_in else (x,)) + weights
    return pl.pallas_call(
        functools.partial(_mixer_kernel, alpha, nb, gather_in),
        grid_spec=grid_spec,
        out_shape=[jax.ShapeDtypeStruct((nb * seq * TOK_ROWS, LANES), F32),
                   jax.ShapeDtypeStruct((nb, 2 * G_B * N_STATE), F32),
                   jax.ShapeDtypeStruct((nsteps, 1, rows), I32),
                   jax.ShapeDtypeStruct((nsteps, 1, rows), I32),
                   jax.ShapeDtypeStruct((CLASS_ROWS, LANES), F32)],
        compiler_params=pltpu.CompilerParams(dimension_semantics=("arbitrary",),
                                             vmem_limit_bytes=VMEM_LIMIT),
    )(*args)


def _dispatch_kernel(rows, pos8_ref, zstart_ref, zlen_ref, tail_ref, x_hbm, xs_hbm,
                     buf0, buf1, buf2, zero_ref, insem, outsem, zsem):
    step = pl.program_id(0)
    nsteps = pl.num_programs(0)
    ztok = ZERO_TOKENS
    blk = rows * TOK_ROWS
    bufs = (buf0, buf1, buf2)

    def block_in(i, b):
        return pltpu.make_async_copy(x_hbm.at[pl.ds(pl.multiple_of(i * blk, blk), blk), :], bufs[b], insem.at[b])

    def scatter_done(b):
        return pltpu.make_async_copy(bufs[b], xs_hbm.at[pl.ds(0, blk), :], outsem.at[b])

    @pl.when(step == 0)
    def _():
        block_in(0, 0).start()

        @pl.when(nsteps > 1)
        def _():
            block_in(1, 1).start()

        zero_ref[...] = jnp.zeros_like(zero_ref)
        pieces = []
        for c in range(N_CLASSES):
            start = zstart_ref[c]
            zlen = zlen_ref[c]
            p = TILE_M // 2
            while p >= 1:
                hit = (zlen & p) != 0
                pieces.append((hit, pltpu.make_async_copy(
                    zero_ref.at[pl.ds(0, p * TOK_ROWS), :],
                    xs_hbm.at[pl.ds(pl.multiple_of(start * TOK_ROWS, TOK_ROWS), p * TOK_ROWS), :], zsem)))
                start = start + jnp.where(hit, p, 0)
                p //= 2
        for hit, cp in pieces:
            pl.when(hit)(cp.start)
        for hit, cp in pieces:
            pl.when(hit)(cp.wait)

        zrows = ztok * TOK_ROWS
        first = tail_ref[0] // ztok

        def tail_copy(q):
            return pltpu.make_async_copy(
                zero_ref, xs_hbm.at[pl.ds(pl.multiple_of(q * zrows, zrows), zrows), :], zsem)

        def tail_start(q, c):
            tail_copy(q).start()
            return c

        def tail_wait(q, c):
            tail_copy(q).wait()
            return c
        lax.fori_loop(first, xs_hbm.shape[0] // zrows, tail_start, 0)
        lax.fori_loop(first, xs_hbm.shape[0] // zrows, tail_wait, 0)

    base = step * rows
    for b in range(3):
        @pl.when(lax.rem(step, 3) == b)
        def _(b=b):
            block_in(step, b).wait()

            def body(g, c):
                idx0 = base + g * DMA_UNROLL
                row0 = pl.multiple_of(g * (DMA_UNROLL * TOK_ROWS), DMA_UNROLL * TOK_ROWS)
                for i in range(DMA_UNROLL):
                    p8 = pl.multiple_of(pos8_ref[idx0 + i], TOK_ROWS)
                    pltpu.make_async_copy(bufs[b].at[pl.ds(row0 + i * TOK_ROWS, TOK_ROWS), :],
                                          xs_hbm.at[pl.ds(p8, TOK_ROWS), :], outsem.at[b]).start(priority=i % 2)
                return c
            lax.fori_loop(0, rows // DMA_UNROLL, body, 0)

            prev = (b + 2) % 3

            @pl.when(step >= 1)
            def _():
                scatter_done(prev).wait()

            @pl.when(step + 2 < nsteps)
            def _():
                block_in(step + 2, prev).start()

            @pl.when(step == nsteps - 1)
            def _():
                scatter_done(b).wait()


def _dispatch(x1t, pos8, zstart, zlen, tail, rows, ns_tokens):
    blk = (rows * TOK_ROWS, LANES)
    return pl.pallas_call(
        functools.partial(_dispatch_kernel, rows),
        grid_spec=pltpu.PrefetchScalarGridSpec(
            num_scalar_prefetch=4,
            grid=(x1t.shape[0] // (rows * TOK_ROWS),),
            in_specs=[pl.BlockSpec(memory_space=pl.ANY)],
            out_specs=pl.BlockSpec(memory_space=pl.ANY),
            scratch_shapes=[pltpu.VMEM(blk, F32), pltpu.VMEM(blk, F32), pltpu.VMEM(blk, F32),
                            pltpu.VMEM((ZERO_TOKENS * TOK_ROWS, LANES), F32),
                            pltpu.SemaphoreType.DMA((3,)), pltpu.SemaphoreType.DMA((3,)),
                            pltpu.SemaphoreType.DMA(())]),
        out_shape=jax.ShapeDtypeStruct((ns_tokens * TOK_ROWS, LANES), F32),
        compiler_params=pltpu.CompilerParams(dimension_semantics=("arbitrary",),
                                             vmem_limit_bytes=VMEM_LIMIT),
    )(pos8, zstart, zlen, tail, x1t)


def _pair_kernel(alpha, l, tidx_ref, tcls_ref, nused_ref,
                 ea_ref, cha_ref, bufa_ref, nxa_ref, hna_ref, eb_ref, chb_ref, bufb_ref, nxb_ref, hnb_ref,
                 x_ref, wrp_ref, wg_hbm, wu_hbm, wd_hbm,
                 ln2g_ref, ln2b_ref, z_ref, w1_ref, w2_ref, xprev_ref, moe_ref, sg_ref, su_ref, sd_ref, wsem):
    j = pl.program_id(0)
    f = D_FF_EXPERT
    nused = nused_ref[0]
    slots = ((ea_ref, cha_ref, bufa_ref, nxa_ref, hna_ref), (eb_ref, chb_ref, bufb_ref, nxb_ref, hnb_ref))

    def weight_copies(k, e, buf):
        return [pltpu.make_async_copy(wg_hbm.at[l, e], sg_ref.at[k, buf], wsem.at[k, buf]),
                pltpu.make_async_copy(wu_hbm.at[l, e], su_ref.at[k, buf], wsem.at[k, buf]),
                pltpu.make_async_copy(wd_hbm.at[l, e], sd_ref.at[k, buf], wsem.at[k, buf])]

    @pl.when(j == 0)
    def _():
        xprev_ref[...] = jnp.zeros_like(xprev_ref)
        moe_ref[...] = jnp.zeros_like(moe_ref)
        w1_ref[:, 4 * f:] = wrp_ref[...].astype(BF16)
        for k, (e_ref, _, buf_ref, _, _) in enumerate(slots):
            for cp in weight_copies(k, e_ref[0], buf_ref[0]):
                cp.start()

    @pl.when(j <= nused)
    def _():
        class_start = jnp.logical_or(j == 0, tcls_ref[j] != tcls_ref[jnp.maximum(j - 1, 0)])

        for k, (e_ref, ch_ref, buf_ref, nx_ref, hn_ref) in enumerate(slots):
            @pl.when(jnp.logical_and(class_start, ch_ref[j] != 0))
            def _(k=k, e_ref=e_ref, buf_ref=buf_ref, nx_ref=nx_ref, hn_ref=hn_ref):
                buf = buf_ref[j]
                for cp in weight_copies(k, e_ref[j], buf):
                    cp.wait()

                w1_ref[:, 2 * k * f:(2 * k + 1) * f] = sg_ref[k, buf].astype(BF16)
                w1_ref[:, (2 * k + 1) * f:(2 * k + 2) * f] = su_ref[k, buf].astype(BF16)
                w2_ref[k * f:(k + 1) * f, :] = sd_ref[k, buf].astype(BF16)

                @pl.when(hn_ref[j] != 0)
                def _():
                    for cp in weight_copies(k, nx_ref[j], 1 - buf):
                        cp.start()

        _to_token_tiles(z_ref, 0, _layer_norm(alpha * xprev_ref[...] + moe_ref[...], ln2g_ref[...], ln2b_ref[...]))

        x = _from_token_tiles(x_ref, 0, TILE_M)
        gu = _dot(x.astype(BF16), w1_ref[...])
        scores = jax.nn.sigmoid(gu[:, 4 * f:])
        lane = lax.broadcasted_iota(I32, scores.shape, 1)
        sa = jnp.sum(jnp.where(lane == ea_ref[j], scores, 0.0), axis=-1, keepdims=True)
        sb = jnp.sum(jnp.where(lane == eb_ref[j], scores, 0.0), axis=-1, keepdims=True)
        tot = sa + sb
        ha = jax.nn.silu(gu[:, 0 * f:1 * f]) * gu[:, 1 * f:2 * f] * (sa / tot)
        hb = jax.nn.silu(gu[:, 2 * f:3 * f]) * gu[:, 3 * f:4 * f] * (sb / tot)
        xprev_ref[...] = x
        moe_ref[...] = _dot(jnp.concatenate([ha, hb], axis=1).astype(BF16), w2_ref[...])


def _pair_experts(xs, l, tables, wrp, w_gate, w_up, w_down, ln2g, ln2b, alpha):
    nsteps = tables[0].shape[0]
    cst = lambda j, *_: (0, 0)
    lsel = lambda j, *_: (l, 0, 0)
    hbm = pl.BlockSpec(memory_space=pl.ANY)
    return pl.pallas_call(
        functools.partial(_pair_kernel, alpha, l),
        grid_spec=pltpu.PrefetchScalarGridSpec(
            num_scalar_prefetch=len(tables),
            grid=(nsteps,),
            in_specs=[pl.BlockSpec((TILE_M * TOK_ROWS, LANES), lambda j, ti, *_: (ti[j], 0)),
                      pl.BlockSpec(wrp.shape, cst), hbm, hbm, hbm,
                      pl.BlockSpec((None,) + ln2g.shape[1:], lsel), pl.BlockSpec((None,) + ln2b.shape[1:], lsel)],
            out_specs=pl.BlockSpec((TILE_M * TOK_ROWS, LANES),
                                   lambda j, ti, tc, nu, *_: (jnp.minimum(jnp.maximum(j - 1, 0), nu[0] - 1), 0)),
            scratch_shapes=[pltpu.VMEM((D_MODEL, 4 * D_FF_EXPERT + LANES), BF16),
                            pltpu.VMEM((2 * D_FF_EXPERT, D_MODEL), BF16),
                            pltpu.VMEM((TILE_M, D_MODEL), F32), pltpu.VMEM((TILE_M, D_MODEL), F32),
                            pltpu.VMEM((2, 2, D_MODEL, D_FF_EXPERT), F32),
                            pltpu.VMEM((2, 2, D_MODEL, D_FF_EXPERT), F32),
                            pltpu.VMEM((2, 2, D_FF_EXPERT, D_MODEL), F32),
                            pltpu.SemaphoreType.DMA((2, 2))]),
        out_shape=jax.ShapeDtypeStruct(xs.shape, F32),
        input_output_aliases={len(tables): 0},
        compiler_params=pltpu.CompilerParams(dimension_semantics=("arbitrary",),
                                             vmem_limit_bytes=VMEM_LIMIT),
    )(*tables, xs, wrp, w_gate, w_up, w_down, ln2g, ln2b)


def _ungather_kernel(nb, pos8_ref, zs_hbm, out_ref, buf0, buf1, sems):
    rows = nb * CHUNK

    def consume(buf):
        out_ref[...] = _from_token_tiles(buf, 0, rows).reshape(nb, CHUNK, D_MODEL)
    _gathered_tokens(pl.program_id(0), pl.num_programs(0), pos8_ref, zs_hbm, (buf0, buf1), sems, rows, consume)


def _ungather(zs, pos8, nb, seq):
    rows = nb * CHUNK
    return pl.pallas_call(
        functools.partial(_ungather_kernel, nb),
        grid_spec=pltpu.PrefetchScalarGridSpec(
            num_scalar_prefetch=1,
            grid=(seq // CHUNK,),
            in_specs=[pl.BlockSpec(memory_space=pl.ANY)],
            out_specs=pl.BlockSpec((nb, CHUNK, D_MODEL), lambda i, *_: (0, i, 0)),
            scratch_shapes=[pltpu.VMEM((rows * TOK_ROWS, LANES), F32), pltpu.VMEM((rows * TOK_ROWS, LANES), F32),
                            pltpu.SemaphoreType.DMA((2,))]),
        out_shape=jax.ShapeDtypeStruct((nb, seq, D_MODEL), F32),
        compiler_params=pltpu.CompilerParams(dimension_semantics=("arbitrary",),
                                             vmem_limit_bytes=VMEM_LIMIT),
    )(pos8, zs)


def _plan(cls, rank, counts, nt):
    cnt = counts[:N_CLASSES, 0].astype(I32)
    ntile = (cnt + TILE_M - 1) // TILE_M
    padded = ntile * TILE_M
    off = jnp.cumsum(padded) - padded
    classes = jnp.arange(N_CLASSES, dtype=I32)
    pos8 = (rank.reshape(-1) + jnp.sum(jnp.where(cls.reshape(-1, 1) == classes, off, 0), axis=1)) * TOK_ROWS
    tile_end = jnp.cumsum(ntile)
    nused = tile_end[-1:].astype(I32)
    tidx = jnp.minimum(jnp.arange(nt + 1, dtype=I32), nused - 1)
    tsel = tile_end[None, :] <= tidx[:, None]
    tcls = jnp.sum(tsel.astype(I32), axis=1)
    used = ntile > 0
    earlier = used[None, :] & (classes[None, :] < classes[:, None])
    later = classes[None, :] > classes[:, None]
    prev_used = jnp.max(jnp.where(earlier, classes[None, :], -1), axis=1)
    per_class = []
    for table in (EA_TABLE, EB_TABLE):
        e_c = jnp.asarray(table)
        change = used & ((prev_used < 0) | (e_c[jnp.maximum(prev_used, 0)] != e_c))
        nxt = jnp.min(jnp.where(change[None, :] & later, classes[None, :], N_CLASSES), axis=1)
        per_class += [e_c, change.astype(I32), (jnp.cumsum(change.astype(I32)) - 1) & 1,
                      e_c[jnp.minimum(nxt, N_CLASSES - 1)], (nxt < N_CLASSES).astype(I32)]
    onehot = tcls[:, None] == classes
    per_tile = jnp.sum(jnp.where(onehot[None], jnp.stack(per_class)[:, None, :], 0), axis=2)
    tables = (tidx, tcls, nused) + tuple(per_tile)
    return pos8.astype(I32), off + cnt, padded - cnt, nused * TILE_M, tables


def _mixer_sample_kernel(alpha, x_ref, h0_ref, win_ref, wout_ref, lnvg_ref, lnvb_ref, ws0_ref, bs0_ref,
                         bdb_ref, bdc_ref, lr_ref, li_ref, dskip_ref, glu_ref, bglu_ref,
                         ln1g_ref, ln1b_ref,
                         x1_ref, hnew_ref, v_ref):
    x = x_ref[...]
    proj = _dot(x.astype(BF16), win_ref[...])
    u = jax.nn.gelu(proj[:, :W_A])
    v = _layer_norm(jax.nn.gelu(proj[:, W_A:2 * W_A]), lnvg_ref[...], lnvb_ref[...])
    v_ref[...] = v
    y_a = u * (ws0_ref[...] * v + bs0_ref[...])
    xs = proj[:, 2 * W_A:]
    zs = []
    for k in range(2):
        sl = slice(k * HALF_W, (k + 1) * HALF_W)
        bu = _dot(xs[:, sl].astype(BF16), bdb_ref[k])
        h0r = h0_ref[k, :, :HALF_STATE]
        h0i = h0_ref[k, :, HALF_STATE:]
        lr = lr_ref[k]
        li = li_ref[k]
        hr = lr * h0r - li * h0i + bu[:, :HALF_STATE]
        hi = lr * h0i + li * h0r + bu[:, HALF_STATE:]
        hnew_ref[k, :, :HALF_STATE] = hr
        hnew_ref[k, :, HALF_STATE:] = hi
        hcat = jnp.concatenate([hr, hi], axis=1).astype(BF16)
        y = jax.nn.gelu(_dot(hcat, bdc_ref[k]) + dskip_ref[:, sl] * xs[:, sl])
        gl = _dot(y.astype(BF16), glu_ref[k]) + bglu_ref[:, sl]
        zs.append(y * jax.nn.sigmoid(gl))
    cat = jnp.concatenate([y_a] + zs, axis=1).astype(BF16)
    mix = _dot(cat, wout_ref[...])
    x1_ref[...] = _layer_norm(alpha * x + mix, ln1g_ref[...], ln1b_ref[...])


def _mixer_sample(x, h0, lw, l, alpha):
    n = x.shape[0]
    full = lambda shape: pl.BlockSpec(shape, lambda i: (0,) * len(shape))
    return pl.pallas_call(
        functools.partial(_mixer_sample_kernel, alpha),
        grid=(1,),
        in_specs=[full(x.shape), _layer_spec(h0.shape, l)] + [_layer_spec(lw[k].shape, l) for k in SAMPLE_WEIGHTS],
        out_specs=[full((n, D_MODEL)), full((2, n, 2 * HALF_STATE)), full((n, W_A))],
        out_shape=[jax.ShapeDtypeStruct((n, D_MODEL), F32),
                   jax.ShapeDtypeStruct((2, n, 2 * HALF_STATE), F32),
                   jax.ShapeDtypeStruct((n, W_A), F32)],
        compiler_params=pltpu.CompilerParams(dimension_semantics=("arbitrary",), vmem_limit_bytes=VMEM_LIMIT),
    )(x, h0, *[lw[k] for k in SAMPLE_WEIGHTS])


def _route(x, wr, rbias):
    logits = jnp.dot(x, wr, preferred_element_type=F32, precision=HIGHEST)
    scores = jax.nn.sigmoid(logits)
    biased = scores + rbias
    lane = lax.broadcasted_iota(I32, biased.shape, 1)
    grp = lane // EXPERTS_PER_GROUP
    neg = jnp.float32(-jnp.inf)

    def top2(vals):
        m1 = jnp.max(vals, axis=-1, keepdims=True)
        i1 = jnp.min(jnp.where(vals == m1, lane, N_EXPERTS), axis=-1, keepdims=True)
        rest = jnp.where(lane == i1, neg, vals)
        m2 = jnp.max(rest, axis=-1, keepdims=True)
        i2 = jnp.min(jnp.where(rest == m2, lane, N_EXPERTS), axis=-1, keepdims=True)
        return m1, i1, m2, i2

    best = sel = None
    for g in range(N_EXPERT_GROUPS):
        m1, _, m2, _ = top2(jnp.where(grp == g, biased, neg))
        gs = m1 + m2
        if g == 0:
            best, sel = gs, jnp.zeros(gs.shape, I32)
        else:
            upd = gs > best
            sel = jnp.where(upd, g, sel)
            best = jnp.where(upd, gs, best)
    _, i1, _, i2 = top2(jnp.where(grp == sel, biased, neg))
    s1 = jnp.sum(jnp.where(lane == i1, scores, 0.0), axis=-1, keepdims=True)
    s2 = jnp.sum(jnp.where(lane == i2, scores, 0.0), axis=-1, keepdims=True)
    tot = s1 + s2
    return jnp.where(lane == i1, s1 / tot, 0.0) + jnp.where(lane == i2, s2 / tot, 0.0)


def _moe_kernel(alpha, x_ref, wr_ref, rb_ref, wg_ref, wu_ref, wd_ref, ln2g_ref, ln2b_ref,
                out_ref, xb_ref, comb_ref, acc_ref):
    step = pl.program_id(1)

    @pl.when(step == 0)
    def _():
        x = x_ref[...]
        xb_ref[...] = x.astype(BF16)
        comb_ref[...] = _route(x, wr_ref[...], rb_ref[...])
        acc_ref[...] = jnp.zeros_like(acc_ref)

    xb = xb_ref[...]
    comb = comb_ref[...]
    lane = lax.broadcasted_iota(I32, comb.shape, 1)
    acc = acc_ref[...]
    for k in range(DENSE_EXPERTS_PER_STEP):
        g = _dot(xb, wg_ref[k].astype(BF16))
        u = _dot(xb, wu_ref[k].astype(BF16))
        ce = jnp.sum(jnp.where(lane == step * DENSE_EXPERTS_PER_STEP + k, comb, 0.0), axis=-1, keepdims=True)
        h = (jax.nn.silu(g) * u * ce).astype(BF16)
        acc = acc + _dot(h, wd_ref[k].astype(BF16))
    acc_ref[...] = acc

    @pl.when(step == pl.num_programs(1) - 1)
    def _():
        out_ref[...] = _layer_norm(alpha * x_ref[...] + acc_ref[...], ln2g_ref[...], ln2b_ref[...])


def _moe_dense(x, l, wr, rb, w_gate, w_up, w_down, ln2g, ln2b, alpha, tm):
    t = x.shape[0]
    cst = lambda i, e: (0, 0)
    wsel = lambda i, e: (l, e, 0, 0)
    return pl.pallas_call(
        functools.partial(_moe_kernel, alpha),
        grid=(t // tm, N_EXPERTS // DENSE_EXPERTS_PER_STEP),
        in_specs=[pl.BlockSpec((tm, D_MODEL), lambda i, e: (i, 0)),
                  pl.BlockSpec(wr.shape, cst), pl.BlockSpec(rb.shape, cst),
                  pl.BlockSpec((None, DENSE_EXPERTS_PER_STEP, D_MODEL, D_FF_EXPERT), wsel),
                  pl.BlockSpec((None, DENSE_EXPERTS_PER_STEP, D_MODEL, D_FF_EXPERT), wsel),
                  pl.BlockSpec((None, DENSE_EXPERTS_PER_STEP, D_FF_EXPERT, D_MODEL), wsel),
                  pl.BlockSpec((None,) + ln2g.shape[1:], lambda i, e: (l, 0, 0)),
                  pl.BlockSpec((None,) + ln2b.shape[1:], lambda i, e: (l, 0, 0))],
        out_specs=pl.BlockSpec((tm, D_MODEL), lambda i, e: (i, 0)),
        out_shape=jax.ShapeDtypeStruct((t, D_MODEL), F32),
        scratch_shapes=[pltpu.VMEM((tm, D_MODEL), BF16),
                        pltpu.VMEM((tm, N_EXPERTS), F32),
                        pltpu.VMEM((tm, D_MODEL), F32)],
        compiler_params=pltpu.CompilerParams(dimension_semantics=("arbitrary", "arbitrary"),
                                             vmem_limit_bytes=VMEM_LIMIT),
    )(x, wr, rb, w_gate, w_up, w_down, ln2g, ln2b)


def _prep_s5_folded(lb_re, lb_im, bb_re, bb_im, c_re, c_im):
    d, g, n = lb_re.shape
    s = S5_FOLD
    pairs = g // 2
    pr, pi = [jnp.ones_like(lb_re)], [jnp.zeros_like(lb_re)]
    for _ in range(s):
        pr, pi = pr + [pr[-1] * lb_re - pi[-1] * lb_im], pi + [pr[-1] * lb_im + pi[-1] * lb_re]
    p_re, p_im = jnp.stack(pr, axis=2), jnp.stack(pi, axis=2)

    def pair_cols(re, im):
        z = jnp.zeros_like(re[:, :, 0])
        g0 = jnp.concatenate([re[:, :, 0], z, im[:, :, 0], z], axis=-1)
        g1 = jnp.concatenate([z, re[:, :, 1], z, im[:, :, 1]], axis=-1)
        return jnp.concatenate([g0, g1], axis=2)

    def pair_diag(a):
        z = jnp.zeros_like(a[:, :, 0])
        return jnp.concatenate([jnp.concatenate([a[:, :, 0], z], axis=-1),
                                jnp.concatenate([z, a[:, :, 1]], axis=-1)], axis=2)

    bt_re, bt_im = bb_re.transpose(0, 1, 3, 2)[:, :, None], bb_im.transpose(0, 1, 3, 2)[:, :, None]
    k_re, k_im = p_re[:, :, s - 1::-1, None, :], p_im[:, :, s - 1::-1, None, :]
    m1t = pair_cols((k_re * bt_re - k_im * bt_im).reshape(d, pairs, 2, -1, n),
                    (k_re * bt_im + k_im * bt_re).reshape(d, pairs, 2, -1, n))
    m1 = jnp.swapaxes(m1t, -1, -2)

    cq_re, cq_im = c_re[:, :, None], c_im[:, :, None]
    j_re, j_im = p_re[:, :, 1:, None, :], p_im[:, :, 1:, None, :]
    hpart = pair_cols((cq_re * j_re - cq_im * j_im).reshape(d, pairs, 2, -1, n),
                      (-(cq_re * j_im + cq_im * j_re)).reshape(d, pairs, 2, -1, n))

    t_re, t_im = p_re[:, :, :s, None, :], p_im[:, :, :s, None, :]
    kern = (jnp.einsum("dgtqn,dgnp->dgtqp", cq_re * t_re - cq_im * t_im, bb_re, precision=HIGHEST)
            - jnp.einsum("dgtqn,dgnp->dgtqp", cq_re * t_im + cq_im * t_re, bb_im, precision=HIGHEST))
    zero = jnp.zeros_like(kern[:, :, 0])
    kx = jnp.stack([jnp.concatenate([kern[:, :, j - q] if q <= j else zero for q in range(s)], axis=-1)
                    for j in range(s)], axis=2)
    apart = pair_diag(kx.reshape(d, pairs, 2, -1, kx.shape[-1]))

    m2 = jnp.concatenate([hpart, apart], axis=-1)
    return (m1.astype(BF16), m2.astype(BF16), p_re[:, :, s].reshape(d, 1, g * n), p_im[:, :, s].reshape(d, 1, g * n))


def _prep_all(w_in, w_out, ln_v_g, ln_v_b, w_s, b_s, a_re, a_im, log_dt, b_re, b_im, c_re, c_im,
              d_skip, w_glu, b_glu, ln1_g, ln1_b):
    d = w_in.shape[0]
    dt = jnp.exp(log_dt)[..., None]
    decay = jnp.exp(a_re * dt)
    lb_re, lb_im = decay * jnp.cos(a_im * dt), decay * jnp.sin(a_im * dt)
    den = a_re * a_re + a_im * a_im
    nr, ni = lb_re - 1.0, lb_im
    zr = (nr * a_re + ni * a_im) / den
    zi = (ni * a_re - nr * a_im) / den
    bb_re = zr[..., None] * b_re - zi[..., None] * b_im
    bb_im = zr[..., None] * b_im + zi[..., None] * b_re

    def bd(a):
        r, c = a.shape[2:]
        rep = jnp.asarray(np.tile(np.eye(c, dtype=np.float32), (1, HALF_GROUPS)))
        diag = jnp.asarray(np.kron(np.eye(HALF_GROUPS), np.ones((r, c))) > 0)
        tiled = jnp.einsum("dkrc,cn->dkrn", a.reshape(d, 2, HALF_GROUPS * r, c), rep)
        return jnp.where(diag, tiled, 0.0).astype(BF16)

    bdb = jnp.concatenate([bd(bb_re.transpose(0, 1, 3, 2)), bd(bb_im.transpose(0, 1, 3, 2))], axis=3)
    bdc = jnp.concatenate([bd(c_re.transpose(0, 1, 3, 2)), bd(-c_im.transpose(0, 1, 3, 2))], axis=2)
    m1, m2, lr8, li8 = _prep_s5_folded(lb_re, lb_im, bb_re, bb_im, c_re, c_im)
    return dict(
        m1=m1, m2=m2, lr8=lr8, li8=li8,
        win=w_in.astype(BF16), wout=w_out.astype(BF16),
        lnvg=ln_v_g[:, None], lnvb=ln_v_b[:, None],
        wtril=jnp.tril(w_s).astype(BF16),
        bsb=jnp.broadcast_to(b_s[..., None], (d, H_A, CHUNK, LANES)),
        ws0=jnp.repeat(w_s[:, :, 0, 0], P_A, axis=1)[:, None], bs0=jnp.repeat(b_s[:, :, 0], P_A, axis=1)[:, None],
        bdb=bdb, bdc=bdc,
        lr=lb_re.reshape(d, 2, 1, HALF_STATE), li=lb_im.reshape(d, 2, 1, HALF_STATE),
        dskip=d_skip.reshape(d, 1, W_B), glu=bd(w_glu),
        bglu=b_glu.reshape(d, 1, W_B), ln1g=ln1_g[:, None], ln1b=ln1_b[:, None])


def _state_to_cols(h_re, h_im):
    d, b = h_re.shape[:2]
    re = h_re.reshape(d, b, 2, HALF_STATE)
    im = h_im.reshape(d, b, 2, HALF_STATE)
    return jnp.concatenate([re, im], axis=3).transpose(0, 2, 1, 3)


def _cols_to_state(h):
    d, _, b, _ = h.shape
    re = h[..., :HALF_STATE].transpose(0, 2, 1, 3).reshape(d, b, G_B, N_STATE)
    im = h[..., HALF_STATE:].transpose(0, 2, 1, 3).reshape(d, b, G_B, N_STATE)
    return re, im


def _pairs_to_state(h):
    d, b, _ = h.shape
    h = h.reshape(d, b, G_B // 2, 2, 2, N_STATE)
    return h[:, :, :, 0].reshape(d, b, G_B, N_STATE), h[:, :, :, 1].reshape(d, b, G_B, N_STATE)


def kernel(x_prompt, x_sample, state_ssm_re, state_ssm_im, w_in, w_out, ln_v_g, ln_v_b, w_s, b_s, ssm_a_re, ssm_a_im, ssm_log_dt, ssm_b_re, ssm_b_im, ssm_c_re, ssm_c_im, ssm_d, w_glu, b_glu, ln1_g, ln1_b, ln2_g, ln2_b, w_router, router_bias, w_gate, w_up, w_down):
    depth = w_in.shape[0]
    alpha = float((2 * depth) ** 0.25)
    nb, seq, _ = x_prompt.shape
    ns = x_sample.shape[0]
    tokens = nb * seq
    nt = tokens // TILE_M + N_CLASSES
    rb = router_bias[None]
    wrt = w_router.T
    wrp = jnp.pad(w_router, ((0, 0), (0, LANES - N_EXPERTS)))
    rbcol = router_bias[:, None]
    tok = np.arange(nb * CHUNK)
    tri = jnp.asarray(tok[:, None] < tok[None, :], BF16)
    lw = _prep_all(w_in, w_out, ln_v_g, ln_v_b, w_s, b_s, ssm_a_re, ssm_a_im, ssm_log_dt,
                   ssm_b_re, ssm_b_im, ssm_c_re, ssm_c_im, ssm_d, w_glu, b_glu, ln1_g, ln1_b)
    shared = (wrt, rbcol, tri)
    ln2g, ln2b = ln2_g[:, None], ln2_b[:, None]
    h0s = _state_to_cols(state_ssm_re, state_ssm_im)
    xp = x_prompt
    pos = None
    xs = x_sample.reshape(ns, D_MODEL)
    pr_h, sm_h, sm_v = [], [], []
    for l in range(depth):
        x1t, hfin, cls, rank, counts = _mixer_prompt(xp, lw, shared, l, alpha, nb, seq, pos)
        pos, zstart, zlen, tail, tables = _plan(cls, rank, counts, nt)
        x_sorted = _dispatch(x1t, pos, zstart, zlen, tail, nb * CHUNK, nt * TILE_M)
        xp = _pair_experts(x_sorted, l, tables, wrp, w_gate, w_up, w_down, ln2g, ln2b, alpha)
        pr_h.append(hfin)

        x1s, hnew, v_new = _mixer_sample(xs, h0s, lw, l, alpha)
        xs = _moe_dense(x1s, l, w_router, rb, w_gate, w_up, w_down, ln2g, ln2b, alpha, tm=ns)
        sm_h.append(hnew)
        sm_v.append(v_new.reshape(ns, 1, W_A))
    y_prompt = _ungather(xp, pos, nb, seq)
    pr_re, pr_im = _pairs_to_state(jnp.stack(pr_h))
    sm_re, sm_im = _cols_to_state(jnp.stack(sm_h))
    return (y_prompt, xs.reshape(ns, 1, D_MODEL), pr_re, pr_im, sm_re, sm_im, jnp.stack(sm_v))
```

```python
import functools

import jax
import jax.numpy as jnp
import numpy as np
from jax import lax
from jax.experimental import pallas as pl
from jax.experimental.pallas import tpu as pltpu

D_MODEL = 1024
W_A = 512
W_B = 512
CHUNK = 128
H_A = 4
P_A = W_A // H_A
GROUP_B = 16
G_B = W_B // GROUP_B
N_STATE = 64
N_EXPERTS = 16
N_EXPERT_GROUPS = 4
EXPERTS_PER_GROUP = N_EXPERTS // N_EXPERT_GROUPS
D_FF_EXPERT = D_MODEL // 4
LN_EPS = 1e-5

LANES = 128
SUBLANES = 8
HALF_GROUPS = 16
HALF_W = HALF_GROUPS * GROUP_B
HALF_STATE = HALF_GROUPS * N_STATE
VMEM_LIMIT = 56 * 1024 * 1024

PAIRS = ((0, 1), (0, 2), (0, 3), (1, 3), (1, 2), (3, 2))
N_CLASSES = N_EXPERT_GROUPS * len(PAIRS)
CLASS_ROWS = 32
TILE_M = 256
TOK_ROWS = D_MODEL // LANES
ZERO_TOKENS = TILE_M // 2
DMA_UNROLL = 16
DENSE_EXPERTS_PER_STEP = 4
S5_FOLD = 8
S5_PAIRS = G_B // 2
EA_TABLE = np.array([EXPERTS_PER_GROUP * g + a for g in range(N_EXPERT_GROUPS) for a, _ in PAIRS], np.int32)
EB_TABLE = np.array([EXPERTS_PER_GROUP * g + b for g in range(N_EXPERT_GROUPS) for _, b in PAIRS], np.int32)

F32 = jnp.float32
BF16 = jnp.bfloat16
I32 = jnp.int32
HIGHEST = lax.Precision.HIGHEST


def _layer_norm(x, g, b):
    mu = jnp.mean(x, axis=-1, keepdims=True)
    xc = x - mu
    var = jnp.mean(xc * xc, axis=-1, keepdims=True)
    return xc * lax.rsqrt(var + LN_EPS) * g + b


def _dot(a, b):
    return jnp.dot(a, b, preferred_element_type=F32)


def _route_classes(x1, wrt, rbcol):
    def split(a):
        hi = a.astype(BF16)
        return hi, (a - hi.astype(F32)).astype(BF16)

    def dot_t(a, b):
        return lax.dot_general(a, b, (((1,), (1,)), ((), ())), preferred_element_type=F32)
    w_hi, w_lo = split(wrt)
    x_hi, x_lo = split(x1)
    logits_t = dot_t(w_hi, x_hi) + (dot_t(w_hi, x_lo) + dot_t(w_lo, x_hi))
    biased = jax.nn.sigmoid(logits_t) + rbcol
    rows = [biased[e:e + 1, :] for e in range(N_EXPERTS)]
    n = EXPERTS_PER_GROUP

    best = sel = None
    for g in range(N_EXPERT_GROUPS):
        v = rows[n * g:n * (g + 1)]
        gs = None
        for a, b in PAIRS:
            s = v[a] + v[b]
            gs = s if gs is None else jnp.maximum(gs, s)
        if g == 0:
            best, sel = gs, jnp.zeros(gs.shape, I32)
        else:
            upd = gs > best
            sel = jnp.where(upd, g, sel)
            best = jnp.where(upd, gs, best)

    cls = jnp.zeros(sel.shape, I32)
    for g in range(N_EXPERT_GROUPS):
        v = rows[n * g:n * (g + 1)]
        lo = jnp.full(sel.shape, n, I32)
        hi = jnp.full(sel.shape, -1, I32)
        for i in range(n):
            before = jnp.zeros(sel.shape, I32)
            for j in range(n):
                if j < i:
                    before = before + (v[j] >= v[i]).astype(I32)
                elif j > i:
                    before = before + (v[j] > v[i]).astype(I32)
            member = before < 2
            lo = jnp.where(member, jnp.minimum(lo, i), lo)
            hi = jnp.where(member, jnp.maximum(hi, i), hi)
        pidx = jnp.zeros(sel.shape, I32)
        for k, (a, b) in enumerate(PAIRS):
            pidx = jnp.where((lo == min(a, b)) & (hi == max(a, b)), k, pidx)
        cls = jnp.where(sel == g, g * len(PAIRS) + pidx, cls)
    return cls


def _to_token_tiles(ref, row0, x):
    n = x.shape[0]
    for c in range(TOK_ROWS):
        ref[pl.ds(row0 * TOK_ROWS + c, n, stride=TOK_ROWS), :] = x[:, c * LANES:(c + 1) * LANES]


def _from_token_tiles(ref, row0, n):
    return jnp.concatenate(
        [ref[pl.ds(row0 * TOK_ROWS + c, n, stride=TOK_ROWS), :] for c in range(TOK_ROWS)], axis=1)


def _gathered_tokens(step, nsteps, pos8_ref, src_hbm, bufs, sems, rows, consume):
    def start(s, base):
        def body(g, c):
            idx0 = base + g * DMA_UNROLL
            row0 = pl.multiple_of(g * (DMA_UNROLL * TOK_ROWS), DMA_UNROLL * TOK_ROWS)
            for i in range(DMA_UNROLL):
                p8 = pl.multiple_of(pos8_ref[idx0 + i], TOK_ROWS)
                pltpu.make_async_copy(src_hbm.at[pl.ds(p8, TOK_ROWS), :],
                                      bufs[s].at[pl.ds(row0 + i * TOK_ROWS, TOK_ROWS), :],
                                      sems.at[s]).start(priority=i % 2)
            return c
        lax.fori_loop(0, rows // DMA_UNROLL, body, 0)

    @pl.when(step == 0)
    def _():
        start(0, 0)

    for s in range(2):
        @pl.when(lax.rem(step, 2) == s)
        def _(s=s):
            pltpu.make_async_copy(src_hbm.at[pl.ds(0, rows * TOK_ROWS), :], bufs[s], sems.at[s]).wait()

            @pl.when(step + 1 < nsteps)
            def _():
                start(1 - s, (step + 1) * rows)
            consume(bufs[s])


def _mixer_kernel(alpha, nb, gather_in, *refs):
    if gather_in:
        pos8_ref, zs_hbm = refs[:2]
        refs = refs[2:]
    else:
        x_ref = refs[0]
        refs = refs[1:]
    (win_ref, wout_ref, lnvg_ref, lnvb_ref, wtril_ref, bsb_ref, m1_ref, m2_ref, lr8_ref, li8_ref,
     dskip_ref, glu_ref, bglu_ref, ln1g_ref, ln1b_ref, wrt_ref, rbcol_ref, tri_ref,
     x1t_ref, hfin_ref, cls_ref, rank_ref, cnt_ref,
     xb_ref, xs_slab, xs_scb, xst_ref, ht_ref, yt_ref, mix_ref, hstate_ref, carry_ref) = refs[:32]
    lt = CHUNK
    rows = nb * lt
    pitch = lt + SUBLANES
    step = pl.program_id(0)
    nsteps = pl.num_programs(0)

    @pl.when(step == 0)
    def _():
        hstate_ref[...] = jnp.zeros_like(hstate_ref)
        carry_ref[...] = jnp.zeros_like(carry_ref)

    if gather_in:
        xin_ref, xbuf0, xbuf1, gsem = refs[32:36]

        def consume(buf):
            xg = _from_token_tiles(buf, 0, rows)
            xin_ref[...] = xg
            xb_ref[...] = xg.astype(BF16)
        _gathered_tokens(step, nsteps, pos8_ref, zs_hbm, (xbuf0, xbuf1), gsem, rows, consume)

        def load_x():
            return xin_ref[...]
    else:
        def load_x():
            return x_ref[...].reshape(rows, D_MODEL)

    if not gather_in:
        xb_ref[...] = load_x().astype(BF16)

    fold = S5_FOLD
    nchunk = lt // fold
    cb = nchunk * nb
    xs = _dot(xb_ref[...], win_ref[:, 2 * W_A:])
    for j in range(W_B // LANES):
        for b in range(nb):
            xs_slab[j, b * pitch:b * pitch + lt, :] = xs[b * lt:(b + 1) * lt, j * LANES:(j + 1) * LANES]

    for t in range(lt):
        r0 = (t % fold) * cb + (t // fold) * nb
        for j in range(W_B // LANES):
            xs_scb[r0:r0 + nb, j * LANES:(j + 1) * LANES] = xs_slab[j, pl.ds(t, nb, stride=pitch), :]
    xst_ref[...] = xs_scb[...].T.astype(BF16)

    vg = jax.nn.gelu(_dot(xb_ref[...], win_ref[:, W_A:2 * W_A]))
    v = _layer_norm(vg, lnvg_ref[...], lnvb_ref[...]).astype(BF16)
    u = jax.nn.gelu(_dot(xb_ref[...], win_ref[:, :W_A]))
    for h in range(H_A):
        hs = slice(h * P_A, (h + 1) * P_A)
        vcat = jnp.concatenate([v[b * lt:(b + 1) * lt, hs] for b in range(nb)], axis=1)
        o = _dot(wtril_ref[h], vcat)
        for b in range(nb):
            rs = slice(b * lt, (b + 1) * lt)
            mix_ref[rs, hs] = u[rs, hs] * (o[:, b * LANES:(b + 1) * LANES] + bsb_ref[h])

    def chunk_inputs(pr):
        return jnp.concatenate(
            [xst_ref[(2 * pr + gi) * GROUP_B:(2 * pr + gi + 1) * GROUP_B, s * cb:(s + 1) * cb]
             for gi in range(2) for s in range(fold)], axis=0)

    pw = 2 * 2 * N_STATE
    for pr in range(S5_PAIRS):
        ht_ref[:, pr * pw:(pr + 1) * pw] = _dot(m1_ref[pr], chunk_inputs(pr)).T

    half_pairs = S5_PAIRS // 2
    for half in range(2):
        c0 = half * half_pairs * pw
        lr8 = [jnp.broadcast_to(lr8_ref[:, (half * half_pairs + p) * LANES:(half * half_pairs + p + 1) * LANES],
                                (nb, LANES)) for p in range(half_pairs)]
        li8 = [jnp.broadcast_to(li8_ref[:, (half * half_pairs + p) * LANES:(half * half_pairs + p + 1) * LANES],
                                (nb, LANES)) for p in range(half_pairs)]
        h = [hstate_ref[:, c0 + q * LANES:c0 + (q + 1) * LANES] for q in range(2 * half_pairs)]
        for c in range(nchunk):
            rs = slice(c * nb, (c + 1) * nb)
            for p in range(half_pairs):
                re_sl = slice(c0 + p * pw, c0 + p * pw + LANES)
                im_sl = slice(c0 + p * pw + LANES, c0 + (p + 1) * pw)
                hr, hi = h[2 * p], h[2 * p + 1]
                ur, ui = ht_ref[rs, re_sl], ht_ref[rs, im_sl]
                ht_ref[rs, re_sl] = hr
                ht_ref[rs, im_sl] = hi
                h[2 * p] = lr8[p] * hr - li8[p] * hi + ur
                h[2 * p + 1] = lr8[p] * hi + li8[p] * hr + ui
        for q in range(2 * half_pairs):
            hstate_ref[:, c0 + q * LANES:c0 + (q + 1) * LANES] = h[q]
    hfin_ref[...] = hstate_ref[...]

    for pr in range(S5_PAIRS):
        rhs = jnp.concatenate([ht_ref[:, pr * pw:(pr + 1) * pw].T.astype(BF16), chunk_inputs(pr)], axis=0)
        yt = _dot(m2_ref[pr], rhs)
        for gi in range(2):
            for s in range(fold):
                r0 = (gi * fold + s) * GROUP_B
                yt_ref[(2 * pr + gi) * GROUP_B:(2 * pr + gi + 1) * GROUP_B, s * cb:(s + 1) * cb] = (
                    yt[r0:r0 + GROUP_B, :])

    y = jax.nn.gelu(yt_ref[...].T + dskip_ref[...] * xs_scb[...])
    yb = y.astype(BF16)
    spitch = cb + SUBLANES
    for k in range(2):
        sl = slice(k * HALF_W, (k + 1) * HALF_W)
        gl = _dot(yb[:, sl], glu_ref[k]) + bglu_ref[:, sl]
        z = y[:, sl] * jax.nn.sigmoid(gl)
        for j in range(HALF_W // LANES):
            for s in range(fold):
                xs_slab[k * (HALF_W // LANES) + j, s * spitch:s * spitch + cb, :] = (
                    z[s * cb:(s + 1) * cb, j * LANES:(j + 1) * LANES])

    for ch in range(nchunk):
        for b in range(nb):
            dst = b * lt + ch * fold
            for j in range(W_B // LANES):
                mix_ref[dst:dst + fold, W_A + j * LANES:W_A + (j + 1) * LANES] = (
                    xs_slab[j, pl.ds(ch * nb + b, fold, stride=spitch), :])

    mix = _dot(mix_ref[...].astype(BF16), wout_ref[...])
    x1 = _layer_norm(alpha * load_x() + mix, ln1g_ref[...], ln1b_ref[...])
    _to_token_tiles(x1t_ref, 0, x1)

    cls = _route_classes(x1, wrt_ref[...], rbcol_ref[...])
    crow = lax.broadcasted_iota(I32, (CLASS_ROWS, rows), 0)
    onehot = jnp.where(crow == cls, 1.0, 0.0)
    prefix = _dot(onehot.astype(BF16), tri_ref[...])
    carry = carry_ref[:, 0:1]
    rank = jnp.sum(onehot * (prefix + carry), axis=0, keepdims=True)
    cls_ref[...] = cls.reshape(1, 1, rows)
    rank_ref[...] = rank.astype(I32).reshape(1, 1, rows)
    carry_ref[...] = carry_ref[...] + jnp.sum(onehot, axis=1, keepdims=True)
    cnt_ref[...] = carry_ref[...]


def _const_spec(shape):
    nd = len(shape)
    return pl.BlockSpec(shape, lambda *_: (0,) * nd, pipeline_mode=pl.Buffered(1))


def _layer_spec(shape, l):
    nd = len(shape)
    return pl.BlockSpec((None,) + tuple(shape[1:]), lambda *_: (l,) + (0,) * (nd - 1),
                        pipeline_mode=pl.Buffered(1))


MIXER_WEIGHTS = ("win", "wout", "lnvg", "lnvb", "wtril", "bsb", "m1", "m2", "lr8", "li8", "dskip", "glu",
                 "bglu", "ln1g", "ln1b")
SAMPLE_WEIGHTS = ("win", "wout", "lnvg", "lnvb", "ws0", "bs0", "bdb", "bdc", "lr", "li", "dskip", "glu",
                  "bglu", "ln1g", "ln1b")


def _mixer_prompt(x, lw, shared, l, alpha, nb, seq, pos_prev=None):
    lt = CHUNK
    rows = nb * lt
    nsteps = seq // lt
    gather_in = pos_prev is not None
    weights = tuple(lw[k] for k in MIXER_WEIGHTS) + tuple(shared)
    wspecs = [_layer_spec(lw[k].shape, l) for k in MIXER_WEIGHTS] + [_const_spec(w.shape) for w in shared]
    if gather_in:
        x_spec = pl.BlockSpec(memory_space=pl.ANY)
    else:
        x_spec = pl.BlockSpec((nb, lt, D_MODEL), lambda i, *_: (0, i, 0))
    scratch = [
        pltpu.VMEM((rows, D_MODEL), BF16),
        pltpu.VMEM((W_B // LANES, nb * (lt + SUBLANES), LANES), F32),
        pltpu.VMEM((rows, W_B), F32),
        pltpu.VMEM((W_B, rows), BF16),
        pltpu.VMEM((rows // S5_FOLD, 2 * G_B * N_STATE), F32),
        pltpu.VMEM((W_B, rows), F32),
        pltpu.VMEM((rows, D_MODEL), F32),
        pltpu.VMEM((nb, 2 * G_B * N_STATE), F32),
        pltpu.VMEM((CLASS_ROWS, LANES), F32),
    ]
    if gather_in:
        scratch += [pltpu.VMEM((rows, D_MODEL), F32),
                    pltpu.VMEM((rows * TOK_ROWS, LANES), F32), pltpu.VMEM((rows * TOK_ROWS, LANES), F32),
                    pltpu.SemaphoreType.DMA((2,))]
    grid_spec = pltpu.PrefetchScalarGridSpec(
        num_scalar_prefetch=1 if gather_in else 0,
        grid=(nsteps,),
        in_specs=[x_spec] + wspecs,
        out_specs=[pl.BlockSpec((rows * TOK_ROWS, LANES), lambda i, *_: (i, 0)),
                   pl.BlockSpec((nb, 2 * G_B * N_STATE), lambda i, *_: (0, 0)),
                   pl.BlockSpec((1, 1, rows), lambda i, *_: (i, 0, 0)),
                   pl.BlockSpec((1, 1, rows), lambda i, *_: (i, 0, 0)),
                   pl.BlockSpec((CLASS_ROWS, LANES), lambda i, *_: (0, 0))],
        scratch_shapes=scratch)
    args = ((pos_prev, x) if gather_in else (x,)) + weights
    return pl.pallas_call(
        functools.partial(_mixer_kernel, alpha, nb, gather_in),
        grid_spec=grid_spec,
        out_shape=[jax.ShapeDtypeStruct((nb * seq * TOK_ROWS, LANES), F32),
                   jax.ShapeDtypeStruct((nb, 2 * G_B * N_STATE), F32),
                   jax.ShapeDtypeStruct((nsteps, 1, rows), I32),
                   jax.ShapeDtypeStruct((nsteps, 1, rows), I32),
                   jax.ShapeDtypeStruct((CLASS_ROWS, LANES), F32)],
        compiler_params=pltpu.CompilerParams(dimension_semantics=("arbitrary",),
                                             vmem_limit_bytes=VMEM_LIMIT),
    )(*args)


def _dispatch_kernel(rows, pos8_ref, zstart_ref, zlen_ref, tail_ref, x_hbm, xs_hbm,
                     buf0, buf1, buf2, zero_ref, insem, outsem, zsem):
    step = pl.program_id(0)
    nsteps = pl.num_programs(0)
    ztok = ZERO_TOKENS
    blk = rows * TOK_ROWS
    bufs = (buf0, buf1, buf2)

    def block_in(i, b):
        return pltpu.make_async_copy(x_hbm.at[pl.ds(pl.multiple_of(i * blk, blk), blk), :], bufs[b], insem.at[b])

    def scatter_done(b):
        return pltpu.make_async_copy(bufs[b], xs_hbm.at[pl.ds(0, blk), :], outsem.at[b])

    @pl.when(step == 0)
    def _():
        block_in(0, 0).start()

        @pl.when(nsteps > 1)
        def _():
            block_in(1, 1).start()

        zero_ref[...] = jnp.zeros_like(zero_ref)
        pieces = []
        for c in range(N_CLASSES):
            start = zstart_ref[c]
            zlen = zlen_ref[c]
            p = TILE_M // 2
            while p >= 1:
                hit = (zlen & p) != 0
                pieces.append((hit, pltpu.make_async_copy(
                    zero_ref.at[pl.ds(0, p * TOK_ROWS), :],
                    xs_hbm.at[pl.ds(pl.multiple_of(start * TOK_ROWS, TOK_ROWS), p * TOK_ROWS), :], zsem)))
                start = start + jnp.where(hit, p, 0)
                p //= 2
        for hit, cp in pieces:
            pl.when(hit)(cp.start)
        for hit, cp in pieces:
            pl.when(hit)(cp.wait)

        zrows = ztok * TOK_ROWS
        first = tail_ref[0] // ztok

        def tail_copy(q):
            return pltpu.make_async_copy(
                zero_ref, xs_hbm.at[pl.ds(pl.multiple_of(q * zrows, zrows), zrows), :], zsem)

        def tail_start(q, c):
            tail_copy(q).start()
            return c

        def tail_wait(q, c):
            tail_copy(q).wait()
            return c
        lax.fori_loop(first, xs_hbm.shape[0] // zrows, tail_start, 0)
        lax.fori_loop(first, xs_hbm.shape[0] // zrows, tail_wait, 0)

    base = step * rows
    for b in range(3):
        @pl.when(lax.rem(step, 3) == b)
        def _(b=b):
            block_in(step, b).wait()

            def body(g, c):
                idx0 = base + g * DMA_UNROLL
                row0 = pl.multiple_of(g * (DMA_UNROLL * TOK_ROWS), DMA_UNROLL * TOK_ROWS)
                for i in range(DMA_UNROLL):
                    p8 = pl.multiple_of(pos8_ref[idx0 + i], TOK_ROWS)
                    pltpu.make_async_copy(bufs[b].at[pl.ds(row0 + i * TOK_ROWS, TOK_ROWS), :],
                                          xs_hbm.at[pl.ds(p8, TOK_ROWS), :], outsem.at[b]).start(priority=i % 2)
                return c
            lax.fori_loop(0, rows // DMA_UNROLL, body, 0)

            prev = (b + 2) % 3

            @pl.when(step >= 1)
            def _():
                scatter_done(prev).wait()

            @pl.when(step + 2 < nsteps)
            def _():
                block_in(step + 2, prev).start()

            @pl.when(step == nsteps - 1)
            def _():
                scatter_done(b).wait()


def _dispatch(x1t, pos8, zstart, zlen, tail, rows, ns_tokens):
    blk = (rows * TOK_ROWS, LANES)
    return pl.pallas_call(
        functools.partial(_dispatch_kernel, rows),
        grid_spec=pltpu.PrefetchScalarGridSpec(
            num_scalar_prefetch=4,
            grid=(x1t.shape[0] // (rows * TOK_ROWS),),
            in_specs=[pl.BlockSpec(memory_space=pl.ANY)],
            out_specs=pl.BlockSpec(memory_space=pl.ANY),
            scratch_shapes=[pltpu.VMEM(blk, F32), pltpu.VMEM(blk, F32), pltpu.VMEM(blk, F32),
                            pltpu.VMEM((ZERO_TOKENS * TOK_ROWS, LANES), F32),
                            pltpu.SemaphoreType.DMA((3,)), pltpu.SemaphoreType.DMA((3,)),
                            pltpu.SemaphoreType.DMA(())]),
        out_shape=jax.ShapeDtypeStruct((ns_tokens * TOK_ROWS, LANES), F32),
        compiler_params=pltpu.CompilerParams(dimension_semantics=("arbitrary",),
                                             vmem_limit_bytes=VMEM_LIMIT),
    )(pos8, zstart, zlen, tail, x1t)


def _pair_kernel(alpha, l, tidx_ref, tcls_ref, nused_ref,
                 ea_ref, cha_ref, bufa_ref, nxa_ref, hna_ref, eb_ref, chb_ref, bufb_ref, nxb_ref, hnb_ref,
                 x_ref, wrp_ref, wg_hbm, wu_hbm, wd_hbm,
                 ln2g_ref, ln2b_ref, z_ref, w1_ref, w2_ref, xprev_ref, moe_ref, sg_ref, su_ref, sd_ref, wsem):
    j = pl.program_id(0)
    f = D_FF_EXPERT
    nused = nused_ref[0]
    slots = ((ea_ref, cha_ref, bufa_ref, nxa_ref, hna_ref), (eb_ref, chb_ref, bufb_ref, nxb_ref, hnb_ref))

    def weight_copies(k, e, buf):
        return [pltpu.make_async_copy(wg_hbm.at[l, e], sg_ref.at[k, buf], wsem.at[k, buf]),
                pltpu.make_async_copy(wu_hbm.at[l, e], su_ref.at[k, buf], wsem.at[k, buf]),
                pltpu.make_async_copy(wd_hbm.at[l, e], sd_ref.at[k, buf], wsem.at[k, buf])]

    @pl.when(j == 0)
    def _():
        xprev_ref[...] = jnp.zeros_like(xprev_ref)
        moe_ref[...] = jnp.zeros_like(moe_ref)
        w1_ref[:, 4 * f:] = wrp_ref[...].astype(BF16)
        for k, (e_ref, _, buf_ref, _, _) in enumerate(slots):
            for cp in weight_copies(k, e_ref[0], buf_ref[0]):
                cp.start()

    @pl.when(j <= nused)
    def _():
        class_start = jnp.logical_or(j == 0, tcls_ref[j] != tcls_ref[jnp.maximum(j - 1, 0)])

        for k, (e_ref, ch_ref, buf_ref, nx_ref, hn_ref) in enumerate(slots):
            @pl.when(jnp.logical_and(class_start, ch_ref[j] != 0))
            def _(k=k, e_ref=e_ref, buf_ref=buf_ref, nx_ref=nx_ref, hn_ref=hn_ref):
                buf = buf_ref[j]
                for cp in weight_copies(k, e_ref[j], buf):
                    cp.wait()

                @pl.when(hn_ref[j] != 0)
                def _():
                    for cp in weight_copies(k, nx_ref[j], 1 - buf):
                        cp.start(priority=1)
                w1_ref[:, 2 * k * f:(2 * k + 1) * f] = sg_ref[k, buf].astype(BF16)
                w1_ref[:, (2 * k + 1) * f:(2 * k + 2) * f] = su_ref[k, buf].astype(BF16)
                w2_ref[k * f:(k + 1) * f, :] = sd_ref[k, buf].astype(BF16)

        _to_token_tiles(z_ref, 0, _layer_norm(alpha * xprev_ref[...] + moe_ref[...], ln2g_ref[...], ln2b_ref[...]))

        x = _from_token_tiles(x_ref, 0, TILE_M)
        gu = _dot(x.astype(BF16), w1_ref[...])
        scores = jax.nn.sigmoid(gu[:, 4 * f:])
        lane = lax.broadcasted_iota(I32, scores.shape, 1)
        sa = jnp.sum(jnp.where(lane == ea_ref[j], scores, 0.0), axis=-1, keepdims=True)
        sb = jnp.sum(jnp.where(lane == eb_ref[j], scores, 0.0), axis=-1, keepdims=True)
        tot = sa + sb
        ha = jax.nn.silu(gu[:, 0 * f:1 * f]) * gu[:, 1 * f:2 * f] * (sa / tot)
        hb = jax.nn.silu(gu[:, 2 * f:3 * f]) * gu[:, 3 * f:4 * f] * (sb / tot)
        xprev_ref[...] = x
        moe_ref[...] = _dot(jnp.concatenate([ha, hb], axis=1).astype(BF16), w2_ref[...])


def _pair_experts(xs, l, tables, wrp, w_gate, w_up, w_down, ln2g, ln2b, alpha):
    nsteps = tables[0].shape[0]
    cst = lambda j, *_: (0, 0)
    lsel = lambda j, *_: (l, 0, 0)
    hbm = pl.BlockSpec(memory_space=pl.ANY)
    return pl.pallas_call(
        functools.partial(_pair_kernel, alpha, l),
        grid_spec=pltpu.PrefetchScalarGridSpec(
            num_scalar_prefetch=len(tables),
            grid=(nsteps,),
            in_specs=[pl.BlockSpec((TILE_M * TOK_ROWS, LANES), lambda j, ti, *_: (ti[j], 0)),
                      pl.BlockSpec(wrp.shape, cst), hbm, hbm, hbm,
                      pl.BlockSpec((None,) + ln2g.shape[1:], lsel), pl.BlockSpec((None,) + ln2b.shape[1:], lsel)],
            out_specs=pl.BlockSpec((TILE_M * TOK_ROWS, LANES),
                                   lambda j, ti, tc, nu, *_: (jnp.minimum(jnp.maximum(j - 1, 0), nu[0] - 1), 0)),
            scratch_shapes=[pltpu.VMEM((D_MODEL, 4 * D_FF_EXPERT + LANES), BF16),
                            pltpu.VMEM((2 * D_FF_EXPERT, D_MODEL), BF16),
                            pltpu.VMEM((TILE_M, D_MODEL), F32), pltpu.VMEM((TILE_M, D_MODEL), F32),
                            pltpu.VMEM((2, 2, D_MODEL, D_FF_EXPERT), F32),
                            pltpu.VMEM((2, 2, D_MODEL, D_FF_EXPERT), F32),
                            pltpu.VMEM((2, 2, D_FF_EXPERT, D_MODEL), F32),
                            pltpu.SemaphoreType.DMA((2, 2))]),
        out_shape=jax.ShapeDtypeStruct(xs.shape, F32),
        input_output_aliases={len(tables): 0},
        compiler_params=pltpu.CompilerParams(dimension_semantics=("arbitrary",),
                                             vmem_limit_bytes=VMEM_LIMIT),
    )(*tables, xs, wrp, w_gate, w_up, w_down, ln2g, ln2b)


def _ungather_kernel(nb, pos8_ref, zs_hbm, out_ref, buf0, buf1, sems):
    rows = nb * CHUNK

    def consume(buf):
        out_ref[...] = _from_token_tiles(buf, 0, rows).reshape(nb, CHUNK, D_MODEL)
    _gathered_tokens(pl.program_id(0), pl.num_programs(0), pos8_ref, zs_hbm, (buf0, buf1), sems, rows, consume)


def _ungather(zs, pos8, nb, seq):
    rows = nb * CHUNK
    return pl.pallas_call(
        functools.partial(_ungather_kernel, nb),
        grid_spec=pltpu.PrefetchScalarGridSpec(
            num_scalar_prefetch=1,
            grid=(seq // CHUNK,),
            in_specs=[pl.BlockSpec(memory_space=pl.ANY)],
            out_specs=pl.BlockSpec((nb, CHUNK, D_MODEL), lambda i, *_: (0, i, 0)),
            scratch_shapes=[pltpu.VMEM((rows * TOK_ROWS, LANES), F32), pltpu.VMEM((rows * TOK_ROWS, LANES), F32),
                            pltpu.SemaphoreType.DMA((2,))]),
        out_shape=jax.ShapeDtypeStruct((nb, seq, D_MODEL), F32),
        compiler_params=pltpu.CompilerParams(dimension_semantics=("arbitrary",),
                                             vmem_limit_bytes=VMEM_LIMIT),
    )(pos8, zs)


def _plan(cls, rank, counts, nt):
    cnt = counts[:N_CLASSES, 0].astype(I32)
    ntile = (cnt + TILE_M - 1) // TILE_M
    padded = ntile * TILE_M
    off = jnp.cumsum(padded) - padded
    classes = jnp.arange(N_CLASSES, dtype=I32)
    pos8 = (rank.reshape(-1) + jnp.sum(jnp.where(cls.reshape(-1, 1) == classes, off, 0), axis=1)) * TOK_ROWS
    tile_end = jnp.cumsum(ntile)
    nused = tile_end[-1:].astype(I32)
    tidx = jnp.minimum(jnp.arange(nt + 1, dtype=I32), nused - 1)
    tsel = tile_end[None, :] <= tidx[:, None]
    tcls = jnp.sum(tsel.astype(I32), axis=1)
    used = ntile > 0
    earlier = used[None, :] & (classes[None, :] < classes[:, None])
    later = classes[None, :] > classes[:, None]
    prev_used = jnp.max(jnp.where(earlier, classes[None, :], -1), axis=1)
    per_class = []
    for table in (EA_TABLE, EB_TABLE):
        e_c = jnp.asarray(table)
        change = used & ((prev_used < 0) | (e_c[jnp.maximum(prev_used, 0)] != e_c))
        nxt = jnp.min(jnp.where(change[None, :] & later, classes[None, :], N_CLASSES), axis=1)
        per_class += [e_c, change.astype(I32), (jnp.cumsum(change.astype(I32)) - 1) & 1,
                      e_c[jnp.minimum(nxt, N_CLASSES - 1)], (nxt < N_CLASSES).astype(I32)]
    onehot = tcls[:, None] == classes
    per_tile = jnp.sum(jnp.where(onehot[None], jnp.stack(per_class)[:, None, :], 0), axis=2)
    tables = (tidx, tcls, nused) + tuple(per_tile)
    return pos8.astype(I32), off + cnt, padded - cnt, nused * TILE_M, tables


def _mixer_sample_kernel(alpha, x_ref, h0_ref, win_ref, wout_ref, lnvg_ref, lnvb_ref, ws0_ref, bs0_ref,
                         bdb_ref, bdc_ref, lr_ref, li_ref, dskip_ref, glu_ref, bglu_ref,
                         ln1g_ref, ln1b_ref,
                         x1_ref, hnew_ref, v_ref):
    x = x_ref[...]
    proj = _dot(x.astype(BF16), win_ref[...])
    u = jax.nn.gelu(proj[:, :W_A])
    v = _layer_norm(jax.nn.gelu(proj[:, W_A:2 * W_A]), lnvg_ref[...], lnvb_ref[...])
    v_ref[...] = v
    y_a = u * (ws0_ref[...] * v + bs0_ref[...])
    xs = proj[:, 2 * W_A:]
    zs = []
    for k in range(2):
        sl = slice(k * HALF_W, (k + 1) * HALF_W)
        bu = _dot(xs[:, sl].astype(BF16), bdb_ref[k])
        h0r = h0_ref[k, :, :HALF_STATE]
        h0i = h0_ref[k, :, HALF_STATE:]
        lr = lr_ref[k]
        li = li_ref[k]
        hr = lr * h0r - li * h0i + bu[:, :HALF_STATE]
        hi = lr * h0i + li * h0r + bu[:, HALF_STATE:]
        hnew_ref[k, :, :HALF_STATE] = hr
        hnew_ref[k, :, HALF_STATE:] = hi
        hcat = jnp.concatenate([hr, hi], axis=1).astype(BF16)
        y = jax.nn.gelu(_dot(hcat, bdc_ref[k]) + dskip_ref[:, sl] * xs[:, sl])
        gl = _dot(y.astype(BF16), glu_ref[k]) + bglu_ref[:, sl]
        zs.append(y * jax.nn.sigmoid(gl))
    cat = jnp.concatenate([y_a] + zs, axis=1).astype(BF16)
    mix = _dot(cat, wout_ref[...])
    x1_ref[...] = _layer_norm(alpha * x + mix, ln1g_ref[...], ln1b_ref[...])


def _mixer_sample(x, h0, lw, l, alpha):
    n = x.shape[0]
    full = lambda shape: pl.BlockSpec(shape, lambda i: (0,) * len(shape))
    return pl.pallas_call(
        functools.partial(_mixer_sample_kernel, alpha),
        grid=(1,),
        in_specs=[full(x.shape), _layer_spec(h0.shape, l)] + [_layer_spec(lw[k].shape, l) for k in SAMPLE_WEIGHTS],
        out_specs=[full((n, D_MODEL)), full((2, n, 2 * HALF_STATE)), full((n, W_A))],
        out_shape=[jax.ShapeDtypeStruct((n, D_MODEL), F32),
                   jax.ShapeDtypeStruct((2, n, 2 * HALF_STATE), F32),
                   jax.ShapeDtypeStruct((n, W_A), F32)],
        compiler_params=pltpu.CompilerParams(dimension_semantics=("arbitrary",), vmem_limit_bytes=VMEM_LIMIT),
    )(x, h0, *[lw[k] for k in SAMPLE_WEIGHTS])


def _route(x, wr, rbias):
    logits = jnp.dot(x, wr, preferred_element_type=F32, precision=HIGHEST)
    scores = jax.nn.sigmoid(logits)
    biased = scores + rbias
    lane = lax.broadcasted_iota(I32, biased.shape, 1)
    grp = lane // EXPERTS_PER_GROUP
    neg = jnp.float32(-jnp.inf)

    def top2(vals):
        m1 = jnp.max(vals, axis=-1, keepdims=True)
        i1 = jnp.min(jnp.where(vals == m1, lane, N_EXPERTS), axis=-1, keepdims=True)
        rest = jnp.where(lane == i1, neg, vals)
        m2 = jnp.max(rest, axis=-1, keepdims=True)
        i2 = jnp.min(jnp.where(rest == m2, lane, N_EXPERTS), axis=-1, keepdims=True)
        return m1, i1, m2, i2

    best = sel = None
    for g in range(N_EXPERT_GROUPS):
        m1, _, m2, _ = top2(jnp.where(grp == g, biased, neg))
        gs = m1 + m2
        if g == 0:
            best, sel = gs, jnp.zeros(gs.shape, I32)
        else:
            upd = gs > best
            sel = jnp.where(upd, g, sel)
            best = jnp.where(upd, gs, best)
    _, i1, _, i2 = top2(jnp.where(grp == sel, biased, neg))
    s1 = jnp.sum(jnp.where(lane == i1, scores, 0.0), axis=-1, keepdims=True)
    s2 = jnp.sum(jnp.where(lane == i2, scores, 0.0), axis=-1, keepdims=True)
    tot = s1 + s2
    return jnp.where(lane == i1, s1 / tot, 0.0) + jnp.where(lane == i2, s2 / tot, 0.0)


def _moe_kernel(alpha, x_ref, wr_ref, rb_ref, wg_ref, wu_ref, wd_ref, ln2g_ref, ln2b_ref,
                out_ref, xb_ref, comb_ref, acc_ref):
    step = pl.program_id(1)

    @pl.when(step == 0)
    def _():
        x = x_ref[...]
        xb_ref[...] = x.astype(BF16)
        comb_ref[...] = _route(x, wr_ref[...], rb_ref[...])
        acc_ref[...] = jnp.zeros_like(acc_ref)

    xb = xb_ref[...]
    comb = comb_ref[...]
    lane = lax.broadcasted_iota(I32, comb.shape, 1)
    acc = acc_ref[...]
    for k in range(DENSE_EXPERTS_PER_STEP):
        g = _dot(xb, wg_ref[k].astype(BF16))
        u = _dot(xb, wu_ref[k].astype(BF16))
        ce = jnp.sum(jnp.where(lane == step * DENSE_EXPERTS_PER_STEP + k, comb, 0.0), axis=-1, keepdims=True)
        h = (jax.nn.silu(g) * u * ce).astype(BF16)
        acc = acc + _dot(h, wd_ref[k].astype(BF16))
    acc_ref[...] = acc

    @pl.when(step == pl.num_programs(1) - 1)
    def _():
        out_ref[...] = _layer_norm(alpha * x_ref[...] + acc_ref[...], ln2g_ref[...], ln2b_ref[...])


def _moe_dense(x, l, wr, rb, w_gate, w_up, w_down, ln2g, ln2b, alpha, tm):
    t = x.shape[0]
    cst = lambda i, e: (0, 0)
    wsel = lambda i, e: (l, e, 0, 0)
    return pl.pallas_call(
        functools.partial(_moe_kernel, alpha),
        grid=(t // tm, N_EXPERTS // DENSE_EXPERTS_PER_STEP),
        in_specs=[pl.BlockSpec((tm, D_MODEL), lambda i, e: (i, 0)),
                  pl.BlockSpec(wr.shape, cst), pl.BlockSpec(rb.shape, cst),
                  pl.BlockSpec((None, DENSE_EXPERTS_PER_STEP, D_MODEL, D_FF_EXPERT), wsel),
                  pl.BlockSpec((None, DENSE_EXPERTS_PER_STEP, D_MODEL, D_FF_EXPERT), wsel),
                  pl.BlockSpec((None, DENSE_EXPERTS_PER_STEP, D_FF_EXPERT, D_MODEL), wsel),
                  pl.BlockSpec((None,) + ln2g.shape[1:], lambda i, e: (l, 0, 0)),
                  pl.BlockSpec((None,) + ln2b.shape[1:], lambda i, e: (l, 0, 0))],
        out_specs=pl.BlockSpec((tm, D_MODEL), lambda i, e: (i, 0)),
        out_shape=jax.ShapeDtypeStruct((t, D_MODEL), F32),
        scratch_shapes=[pltpu.VMEM((tm, D_MODEL), BF16),
                        pltpu.VMEM((tm, N_EXPERTS), F32),
                        pltpu.VMEM((tm, D_MODEL), F32)],
        compiler_params=pltpu.CompilerParams(dimension_semantics=("arbitrary", "arbitrary"),
                                             vmem_limit_bytes=VMEM_LIMIT),
    )(x, wr, rb, w_gate, w_up, w_down, ln2g, ln2b)


def _prep_s5_folded(lb_re, lb_im, bb_re, bb_im, c_re, c_im):
    d, g, n = lb_re.shape
    s = S5_FOLD
    pairs = g // 2
    pr, pi = [jnp.ones_like(lb_re)], [jnp.zeros_like(lb_re)]
    for _ in range(s):
        pr, pi = pr + [pr[-1] * lb_re - pi[-1] * lb_im], pi + [pr[-1] * lb_im + pi[-1] * lb_re]
    p_re, p_im = jnp.stack(pr, axis=2), jnp.stack(pi, axis=2)

    def pair_cols(re, im):
        z = jnp.zeros_like(re[:, :, 0])
        g0 = jnp.concatenate([re[:, :, 0], z, im[:, :, 0], z], axis=-1)
        g1 = jnp.concatenate([z, re[:, :, 1], z, im[:, :, 1]], axis=-1)
        return jnp.concatenate([g0, g1], axis=2)

    def pair_diag(a):
        z = jnp.zeros_like(a[:, :, 0])
        return jnp.concatenate([jnp.concatenate([a[:, :, 0], z], axis=-1),
                                jnp.concatenate([z, a[:, :, 1]], axis=-1)], axis=2)

    bt_re, bt_im = bb_re.transpose(0, 1, 3, 2)[:, :, None], bb_im.transpose(0, 1, 3, 2)[:, :, None]
    k_re, k_im = p_re[:, :, s - 1::-1, None, :], p_im[:, :, s - 1::-1, None, :]
    m1t = pair_cols((k_re * bt_re - k_im * bt_im).reshape(d, pairs, 2, -1, n),
                    (k_re * bt_im + k_im * bt_re).reshape(d, pairs, 2, -1, n))
    m1 = jnp.swapaxes(m1t, -1, -2)

    cq_re, cq_im = c_re[:, :, None], c_im[:, :, None]
    j_re, j_im = p_re[:, :, 1:, None, :], p_im[:, :, 1:, None, :]
    hpart = pair_cols((cq_re * j_re - cq_im * j_im).reshape(d, pairs, 2, -1, n),
                      (-(cq_re * j_im + cq_im * j_re)).reshape(d, pairs, 2, -1, n))

    t_re, t_im = p_re[:, :, :s, None, :], p_im[:, :, :s, None, :]
    kern = (jnp.einsum("dgtqn,dgnp->dgtqp", cq_re * t_re - cq_im * t_im, bb_re, precision=HIGHEST)
            - jnp.einsum("dgtqn,dgnp->dgtqp", cq_re * t_im + cq_im * t_re, bb_im, precision=HIGHEST))
    zero = jnp.zeros_like(kern[:, :, 0])
    kx = jnp.stack([jnp.concatenate([kern[:, :, j - q] if q <= j else zero for q in range(s)], axis=-1)
                    for j in range(s)], axis=2)
    apart = pair_diag(kx.reshape(d, pairs, 2, -1, kx.shape[-1]))

    m2 = jnp.concatenate([hpart, apart], axis=-1)
    return (m1.astype(BF16), m2.astype(BF16), p_re[:, :, s].reshape(d, 1, g * n), p_im[:, :, s].reshape(d, 1, g * n))


def _prep_all(w_in, w_out, ln_v_g, ln_v_b, w_s, b_s, a_re, a_im, log_dt, b_re, b_im, c_re, c_im,
              d_skip, w_glu, b_glu, ln1_g, ln1_b):
    d = w_in.shape[0]
    dt = jnp.exp(log_dt)[..., None]
    decay = jnp.exp(a_re * dt)
    lb_re, lb_im = decay * jnp.cos(a_im * dt), decay * jnp.sin(a_im * dt)
    den = a_re * a_re + a_im * a_im
    nr, ni = lb_re - 1.0, lb_im
    zr = (nr * a_re + ni * a_im) / den
    zi = (ni * a_re - nr * a_im) / den
    bb_re = zr[..., None] * b_re - zi[..., None] * b_im
    bb_im = zr[..., None] * b_im + zi[..., None] * b_re

    def bd(a):
        r, c = a.shape[2:]
        rep = jnp.asarray(np.tile(np.eye(c, dtype=np.float32), (1, HALF_GROUPS)))
        diag = jnp.asarray(np.kron(np.eye(HALF_GROUPS), np.ones((r, c))) > 0)
        tiled = jnp.einsum("dkrc,cn->dkrn", a.reshape(d, 2, HALF_GROUPS * r, c), rep)
        return jnp.where(diag, tiled, 0.0).astype(BF16)

    bdb = jnp.concatenate([bd(bb_re.transpose(0, 1, 3, 2)), bd(bb_im.transpose(0, 1, 3, 2))], axis=3)
    bdc = jnp.concatenate([bd(c_re.transpose(0, 1, 3, 2)), bd(-c_im.transpose(0, 1, 3, 2))], axis=2)
    m1, m2, lr8, li8 = _prep_s5_folded(lb_re, lb_im, bb_re, bb_im, c_re, c_im)
    return dict(
        m1=m1, m2=m2, lr8=lr8, li8=li8,
        win=w_in.astype(BF16), wout=w_out.astype(BF16),
        lnvg=ln_v_g[:, None], lnvb=ln_v_b[:, None],
        wtril=jnp.tril(w_s).astype(BF16),
        bsb=jnp.broadcast_to(b_s[..., None], (d, H_A, CHUNK, LANES)),
        ws0=jnp.repeat(w_s[:, :, 0, 0], P_A, axis=1)[:, None], bs0=jnp.repeat(b_s[:, :, 0], P_A, axis=1)[:, None],
        bdb=bdb, bdc=bdc,
        lr=lb_re.reshape(d, 2, 1, HALF_STATE), li=lb_im.reshape(d, 2, 1, HALF_STATE),
        dskip=d_skip.reshape(d, 1, W_B), glu=bd(w_glu),
        bglu=b_glu.reshape(d, 1, W_B), ln1g=ln1_g[:, None], ln1b=ln1_b[:, None])


def _state_to_cols(h_re, h_im):
    d, b = h_re.shape[:2]
    re = h_re.reshape(d, b, 2, HALF_STATE)
    im = h_im.reshape(d, b, 2, HALF_STATE)
    return jnp.concatenate([re, im], axis=3).transpose(0, 2, 1, 3)


def _cols_to_state(h):
    d, _, b, _ = h.shape
    re = h[..., :HALF_STATE].transpose(0, 2, 1, 3).reshape(d, b, G_B, N_STATE)
    im = h[..., HALF_STATE:].transpose(0, 2, 1, 3).reshape(d, b, G_B, N_STATE)
    return re, im


def _pairs_to_state(h):
    d, b, _ = h.shape
    h = h.reshape(d, b, G_B // 2, 2, 2, N_STATE)
    return h[:, :, :, 0].reshape(d, b, G_B, N_STATE), h[:, :, :, 1].reshape(d, b, G_B, N_STATE)


def kernel(x_prompt, x_sample, state_ssm_re, state_ssm_im, w_in, w_out, ln_v_g, ln_v_b, w_s, b_s, ssm_a_re, ssm_a_im, ssm_log_dt, ssm_b_re, ssm_b_im, ssm_c_re, ssm_c_im, ssm_d, w_glu, b_glu, ln1_g, ln1_b, ln2_g, ln2_b, w_router, router_bias, w_gate, w_up, w_down):
    depth = w_in.shape[0]
    alpha = float((2 * depth) ** 0.25)
    nb, seq, _ = x_prompt.shape
    ns = x_sample.shape[0]
    tokens = nb * seq
    nt = tokens // TILE_M + N_CLASSES
    rb = router_bias[None]
    wrt = w_router.T
    wrp = jnp.pad(w_router, ((0, 0), (0, LANES - N_EXPERTS)))
    rbcol = router_bias[:, None]
    tok = np.arange(nb * CHUNK)
    tri = jnp.asarray(tok[:, None] < tok[None, :], BF16)
    lw = _prep_all(w_in, w_out, ln_v_g, ln_v_b, w_s, b_s, ssm_a_re, ssm_a_im, ssm_log_dt,
                   ssm_b_re, ssm_b_im, ssm_c_re, ssm_c_im, ssm_d, w_glu, b_glu, ln1_g, ln1_b)
    shared = (wrt, rbcol, tri)
    ln2g, ln2b = ln2_g[:, None], ln2_b[:, None]
    h0s = _state_to_cols(state_ssm_re, state_ssm_im)
    xp = x_prompt
    pos = None
    xs = x_sample.reshape(ns, D_MODEL)
    pr_h, sm_h, sm_v = [], [], []
    for l in range(depth):
        x1t, hfin, cls, rank, counts = _mixer_prompt(xp, lw, shared, l, alpha, nb, seq, pos)
        pos, zstart, zlen, tail, tables = _plan(cls, rank, counts, nt)
        x_sorted = _dispatch(x1t, pos, zstart, zlen, tail, nb * CHUNK, nt * TILE_M)
        xp = _pair_experts(x_sorted, l, tables, wrp, w_gate, w_up, w_down, ln2g, ln2b, alpha)
        pr_h.append(hfin)

        x1s, hnew, v_new = _mixer_sample(xs, h0s, lw, l, alpha)
        xs = _moe_dense(x1s, l, w_router, rb, w_gate, w_up, w_down, ln2g, ln2b, alpha, tm=ns)
        sm_h.append(hnew)
        sm_v.append(v_new.reshape(ns, 1, W_A))
    y_prompt = _ungather(xp, pos, nb, seq)
    pr_re, pr_im = _pairs_to_state(jnp.stack(pr_h))
    sm_re, sm_im = _cols_to_state(jnp.stack(sm_h))
    return (y_prompt, xs.reshape(ns, 1, D_MODEL), pr_re, pr_im, sm_re, sm_im, jnp.stack(sm_v))
```
